```python
import math
import jax
import jax.numpy as jnp
from jax import lax
import numpy as np

D_MODEL = 2048
BATCH = 1
SEQ = 16384
DEPTH = 2

GRID_W = 64
CTX_LEN = 256
N_MOD = 6
EPS = 1e-6
NEG_INF = -1e30
ROPE_THETA = 10000.0
A_HEADS = 8
A_QK_DIM = 64
A_V_DIM = 2 * A_QK_DIM
B_Q_HEADS = 8
B_KV_HEADS = 2
B_GROUP = B_Q_HEADS // B_KV_HEADS
B_HEAD_DIM = 128
WINDOW = 128
Q_BLOCK = 128
QKV_SIZES = (A_HEADS * 2 * A_QK_DIM, A_HEADS * 2 * A_QK_DIM, A_HEADS * A_V_DIM,
             B_Q_HEADS * B_HEAD_DIM, B_KV_HEADS * B_HEAD_DIM, B_KV_HEADS * B_HEAD_DIM)
QKV_SPLITS = [int(v) for v in np.cumsum(QKV_SIZES)[:-1]]
QKV_DIM = int(sum(QKV_SIZES))
ATTN_OUT_DIM = A_HEADS * A_V_DIM + B_Q_HEADS * B_HEAD_DIM
S5_WIDTH = D_MODEL
S5_GROUP = 16
S5_GROUPS = S5_WIDTH // S5_GROUP
S5_STATE = 64
S5_CHUNK = 128
DT_MIN = 1e-3
DT_MAX = 1e-1
N_EXPERTS = 32
TOP_K = 4
D_EXPERT = D_MODEL
SWIGLU_LIMIT = 7.0
SWIGLU_ALPHA = 1.702
EXPERT_BLOCK = 128

kernel_name = "hybrid_diff_window_s5_moe_block"


def rmsnorm(x, g):
    xf = x.astype(jnp.float32)
    y = xf * lax.rsqrt(jnp.mean(xf * xf, axis=-1, keepdims=True) + EPS)
    return (y * g.astype(jnp.float32)).astype(x.dtype)


def modulate(h, shift, scale):
    return h * (1 + scale) + shift


def lambda_init_fn(layer):
    return 0.8 - 0.6 * math.exp(-0.3 * layer)


def axial_rope_tables(rows, cols, dim):
    quarter = dim // 4
    freqs = ROPE_THETA ** (-jnp.arange(quarter, dtype=jnp.float32) / quarter)
    ang = jnp.stack([rows[:, None] * freqs, cols[:, None] * freqs], axis=1)
    return jnp.cos(ang), jnp.sin(ang)


def apply_rope(x, cos, sin):
    q = x.shape[-1] // 4
    x4 = x.astype(jnp.float32).reshape(x.shape[:-1] + (2, 2, q))
    a, b = x4[..., 0, :], x4[..., 1, :]
    shape = (1, cos.shape[0]) + (1,) * (x.ndim - 3) + (2, q)
    cos, sin = cos.reshape(shape), sin.reshape(shape)
    out = jnp.stack([a * cos - b * sin, b * cos + a * sin], axis=-2)
    return out.reshape(x.shape).astype(x.dtype)


def split_qkv(p):
    bsz, length = p.shape[0], p.shape[1]
    qa, ka, va, qb, kb, vb = jnp.split(p, QKV_SPLITS, axis=-1)
    return (qa.reshape(bsz, length, A_HEADS, 2, A_QK_DIM),
            ka.reshape(bsz, length, A_HEADS, 2, A_QK_DIM),
            va.reshape(bsz, length, A_HEADS, A_V_DIM),
            qb.reshape(bsz, length, B_KV_HEADS, B_GROUP, B_HEAD_DIM),
            kb.reshape(bsz, length, B_KV_HEADS, B_HEAD_DIM),
            vb.reshape(bsz, length, B_KV_HEADS, B_HEAD_DIM))


def diff_weights(q, k, lam):
    s = jnp.einsum('bqhmd,bkhmd->bhmqk', q, k).astype(jnp.float32) * (A_QK_DIM ** -0.5)
    p = jax.nn.softmax(s, axis=-1)
    return p[:, :, 0] - lam * p[:, :, 1]


def diff_attention(q_lat, k_lat, v_lat, q_ctx, k_ctx, v_ctx, lam, lambda_init, subln_g, need_ctx):
    bsz, seq = q_lat.shape[0], q_lat.shape[1]
    nb = seq // Q_BLOCK
    k_all = jnp.concatenate([k_ctx, k_lat], axis=1)
    v_all = jnp.concatenate([v_ctx, v_lat], axis=1)
    q_blocks = jnp.moveaxis(q_lat.reshape(bsz, nb, Q_BLOCK, A_HEADS, 2, A_QK_DIM), 1, 0)

    def block(qb):
        w = diff_weights(qb, k_all, lam)
        return jnp.einsum('bhqk,bkhd->bqhd', w.astype(v_all.dtype), v_all)

    o_lat = jnp.moveaxis(lax.map(block, q_blocks), 0, 1).reshape(bsz, seq, A_HEADS, A_V_DIM)

    def post(o):
        o = rmsnorm(o, subln_g) * (1.0 - lambda_init)
        return o.reshape(o.shape[:2] + (A_HEADS * A_V_DIM,))

    if not need_ctx:
        return post(o_lat), None
    w_c = diff_weights(q_ctx, k_ctx, lam)
    o_ctx = jnp.einsum('bhqk,bkhd->bqhd', w_c.astype(v_ctx.dtype), v_ctx)
    return post(o_lat), post(o_ctx)


def window_attention(q_lat, k_lat, v_lat, q_ctx, k_ctx, v_ctx, sink, need_ctx):
    bsz, seq = q_lat.shape[0], q_lat.shape[1]
    n_ctx = k_ctx.shape[1]
    nb = seq // WINDOW
    scale = B_HEAD_DIM ** -0.5
    sink_l = sink.astype(jnp.float32).reshape(B_KV_HEADS, B_GROUP)

    def windows(t):
        tp = jnp.pad(t, ((0, 0), (WINDOW, WINDOW), (0, 0), (0, 0)))
        tp = tp.reshape(bsz, nb + 2, WINDOW, B_KV_HEADS, B_HEAD_DIM)
        return jnp.moveaxis(jnp.concatenate([tp[:, :-2], tp[:, 1:-1], tp[:, 2:]], axis=2), 1, 0)

    qi = jnp.arange(WINDOW)
    kj = jnp.arange(3 * WINDOW)
    band = jnp.abs(kj[None, :] - WINDOW - qi[:, None]) <= WINDOW
    kpos = jnp.arange(nb)[:, None] * WINDOW - WINDOW + kj[None, :]
    mask = band[None] & ((kpos >= 0) & (kpos < seq))[:, None, :]
    q_blocks = jnp.moveaxis(q_lat.reshape(bsz, nb, WINDOW, B_KV_HEADS, B_GROUP, B_HEAD_DIM), 1, 0)

    def block(args):
        qb, kw, vw, m = args
        s_loc = jnp.einsum('bqgrd,bkgd->bgrqk', qb, kw).astype(jnp.float32) * scale
        s_loc = jnp.where(m, s_loc, NEG_INF)
        s_ctx = jnp.einsum('bqgrd,bcgd->bgrqc', qb, k_ctx).astype(jnp.float32) * scale
        s_sink = jnp.broadcast_to(sink_l[None, :, :, None, None], s_loc.shape[:-1] + (1,))
        p = jax.nn.softmax(jnp.concatenate([s_loc, s_ctx, s_sink], axis=-1), axis=-1).astype(vw.dtype)
        return (jnp.einsum('bgrqk,bkgd->bqgrd', p[..., :3 * WINDOW], vw)
                + jnp.einsum('bgrqc,bcgd->bqgrd', p[..., 3 * WINDOW:3 * WINDOW + n_ctx], v_ctx))

    o = lax.map(block, (q_blocks, windows(k_lat), windows(v_lat), mask))
    o_lat = jnp.moveaxis(o, 0, 1).reshape(bsz, seq, B_Q_HEADS * B_HEAD_DIM)
    if not need_ctx:
        return o_lat, None
    s_c = jnp.einsum('bqgrd,bcgd->bgrqc', q_ctx, k_ctx).astype(jnp.float32) * scale
    s_sink = jnp.broadcast_to(sink_l[None, :, :, None, None], s_c.shape[:-1] + (1,))
    p_c = jax.nn.softmax(jnp.concatenate([s_c, s_sink], axis=-1), axis=-1).astype(v_ctx.dtype)
    o_ctx = jnp.einsum('bgrqc,bcgd->bqgrd', p_c[..., :n_ctx], v_ctx).reshape(bsz, n_ctx, B_Q_HEADS * B_HEAD_DIM)
    return o_lat, o_ctx


def attention_mixer(h_lat, h_ctx, w_qkv, w_o, lam_q1, lam_k1, lam_q2, lam_k2, subln_g, sink,
                    lambda_init, rope_a, rope_b, need_ctx):
    qa, ka, va, qb, kb, vb = split_qkv(h_lat @ w_qkv)
    cqa, cka, cva, cqb, ckb, cvb = split_qkv(h_ctx @ w_qkv)
    qa, ka = apply_rope(qa, *rope_a), apply_rope(ka, *rope_a)
    qb, kb = apply_rope(qb, *rope_b), apply_rope(kb, *rope_b)
    f32 = jnp.float32
    lam = (jnp.exp(jnp.sum(lam_q1.astype(f32) * lam_k1.astype(f32)))
           - jnp.exp(jnp.sum(lam_q2.astype(f32) * lam_k2.astype(f32))) + lambda_init)
    ya_lat, ya_ctx = diff_attention(qa, ka, va, cqa, cka, cva, lam, lambda_init, subln_g, need_ctx)
    yb_lat, yb_ctx = window_attention(qb, kb, vb, cqb, ckb, cvb, sink, need_ctx)
    y_lat = jnp.concatenate([ya_lat, yb_lat], axis=-1) @ w_o
    y_ctx = jnp.concatenate([ya_ctx, yb_ctx], axis=-1) @ w_o if need_ctx else None
    return y_lat, y_ctx


def s5_discretise(a_re, a_im, log_step, b_re, b_im):
    f32 = jnp.float32
    lr = jnp.minimum(a_re.astype(f32), -1e-4)
    li = a_im.astype(f32)
    dt = jnp.exp(log_step.astype(f32))[:, None]
    mag = jnp.exp(lr * dt)
    ar, ai = mag * jnp.cos(li * dt), mag * jnp.sin(li * dt)
    den = lr * lr + li * li
    nr = ar - 1.0
    cr = (nr * lr + ai * li) / den
    ci = (ai * lr - nr * li) / den
    br, bi = b_re.astype(f32), b_im.astype(f32)
    bbr = cr[..., None] * br - ci[..., None] * bi
    bbi = cr[..., None] * bi + ci[..., None] * br
    return ar, ai, bbr, bbi


def complex_affine_combine(e1, e2):
    a1r, a1i, b1r, b1i = e1
    a2r, a2i, b2r, b2i = e2
    return (a2r * a1r - a2i * a1i, a2r * a1i + a2i * a1r,
            a2r * b1r - a2i * b1i + b2r, a2r * b1i + a2i * b1r + b2i)


def s5_scan(u, ar, ai, bbr, bbi, cr, ci, h0, emit):
    bsz, length = u.shape[0], u.shape[1]
    n_chunk = length // S5_CHUNK
    u_ch = jnp.moveaxis(u.reshape(bsz, n_chunk, S5_CHUNK, S5_GROUPS, S5_GROUP), 1, 0)

    def step(carry, ub):
        hr0, hi0 = carry
        bur = jnp.einsum('gpi,blgi->blgp', bbr, ub)
        bui = jnp.einsum('gpi,blgi->blgp', bbi, ub)
        elems = (jnp.broadcast_to(ar, bur.shape), jnp.broadcast_to(ai, bur.shape), bur, bui)
        pr, pi, hr, hi = lax.associative_scan(complex_affine_combine, elems, axis=1)
        hr, hi = (hr + pr * hr0[:, None] - pi * hi0[:, None],
                  hi + pr * hi0[:, None] + pi * hr0[:, None])
        y = (jnp.einsum('gip,blgp->blgi', cr, hr) - jnp.einsum('gip,blgp->blgi', ci, hi)) if emit else None
        return (hr[:, -1], hi[:, -1]), y

    h_end, y = lax.scan(step, h0, u_ch)
    if emit:
        y = jnp.moveaxis(y, 0, 1).reshape(bsz, length, S5_WIDTH)
    return y, h_end


def s5_mixer(h_lat, h_ctx, w_in, a_re, a_im, log_step, b_re, b_im, c_re, c_im, d_skip, w_glu, need_ctx):
    f32 = jnp.float32
    u_lat = (h_lat @ w_in).astype(f32)
    u_ctx = (h_ctx @ w_in).astype(f32)
    bsz = u_lat.shape[0]
    d = d_skip.astype(f32)
    y_lat = d * u_lat
    y_ctx = d * u_ctx if need_ctx else None
    h_zero = (jnp.zeros((bsz, S5_GROUPS, S5_STATE), f32), jnp.zeros((bsz, S5_GROUPS, S5_STATE), f32))
    for dirn in range(2):
        ar, ai, bbr, bbi = s5_discretise(a_re[dirn], a_im[dirn], log_step[dirn], b_re[dirn], b_im[dirn])
        cr, ci = c_re[dirn].astype(f32), c_im[dirn].astype(f32)
        uc = u_ctx if dirn == 0 else jnp.flip(u_ctx, axis=1)
        ul = u_lat if dirn == 0 else jnp.flip(u_lat, axis=1)
        yc, h_ctx_end = s5_scan(uc, ar, ai, bbr, bbi, cr, ci, h_zero, need_ctx)
        yl, _ = s5_scan(ul, ar, ai, bbr, bbi, cr, ci, h_ctx_end, True)
        if dirn == 1:
            yl = jnp.flip(yl, axis=1)
            yc = jnp.flip(yc, axis=1) if need_ctx else None
        y_lat = y_lat + yl
        if need_ctx:
            y_ctx = y_ctx + yc

    def glu_out(y):
        z = jax.nn.gelu(y).astype(h_lat.dtype) @ w_glu
        val, gate = jnp.split(z, 2, axis=-1)
        return val * jax.nn.sigmoid(gate)

    return glu_out(y_lat), (glu_out(y_ctx) if need_ctx else None)


def moe(h, w_r, b_r, w_gu, b_gu, w_dn, b_dn):
    n_tok, d = h.shape
    logits = (h @ w_r + b_r).astype(jnp.float32)
    top_val, top_idx = lax.top_k(logits, TOP_K)
    gates = jax.nn.softmax(top_val, axis=-1)
    n_assign = n_tok * TOP_K
    flat_e = top_idx.reshape(-1).astype(jnp.int32)
    flat_tok = jnp.arange(n_assign, dtype=jnp.int32) // TOP_K
    flat_g = gates.reshape(-1)
    order = jnp.argsort(flat_e)
    e_s, tok_s, g_s = flat_e[order], flat_tok[order], flat_g[order]
    counts = jnp.bincount(flat_e, length=N_EXPERTS).astype(jnp.int32)
    starts = jnp.cumsum(counts) - counts
    padded = (counts + EXPERT_BLOCK - 1) // EXPERT_BLOCK * EXPERT_BLOCK
    pends = jnp.cumsum(padded)
    pstarts = pends - padded
    dest = pstarts[e_s] + jnp.arange(n_assign, dtype=jnp.int32) - starts[e_s]
    n_blocks = -(-n_assign // EXPERT_BLOCK) + N_EXPERTS
    n_rows = n_blocks * EXPERT_BLOCK
    row_tok = jnp.zeros((n_rows,), jnp.int32).at[dest].set(tok_s)
    row_g = jnp.zeros((n_rows,), jnp.float32).at[dest].set(g_s)
    blk_start = jnp.arange(n_blocks, dtype=jnp.int32) * EXPERT_BLOCK
    blk_e = jnp.minimum(jnp.searchsorted(pends, blk_start, side='right'), N_EXPERTS - 1)

    def expert_block(args):
        e, tok = args
        gu = h[tok] @ w_gu[e] + b_gu[e]
        gate, up = jnp.split(gu, 2, axis=-1)
        gate = jnp.minimum(gate, SWIGLU_LIMIT)
        up = jnp.clip(up, -SWIGLU_LIMIT, SWIGLU_LIMIT)
        act = (up + 1) * (gate * jax.nn.sigmoid(SWIGLU_ALPHA * gate))
        return act @ w_dn[e] + b_dn[e]

    y = lax.map(expert_block, (blk_e, row_tok.reshape(n_blocks, EXPERT_BLOCK)))
    y = y.reshape(n_rows, d) * row_g[:, None].astype(h.dtype)
    return jnp.zeros_like(h).at[row_tok].add(y)


def setup_inputs(seed: int = 0) -> dict:
    key = jax.random.key(seed)
    ks = iter(jax.random.split(key, 40))
    f32 = jnp.float32
    n_attn = (DEPTH + 1) // 2
    n_s5 = DEPTH // 2

    def nrm(shape, scale):
        return jax.random.normal(next(ks), shape, f32) * scale

    x = nrm((BATCH, SEQ, D_MODEL), 1.0)
    c = nrm((BATCH, D_MODEL), 1.0)
    ctx = nrm((BATCH, CTX_LEN, D_MODEL), 1.0)
    c_ctx = nrm((D_MODEL,), 1.0)
    w_mod = nrm((DEPTH, D_MODEL, N_MOD * D_MODEL), 0.5 * D_MODEL ** -0.5)
    b_mod = nrm((DEPTH, N_MOD * D_MODEL), 0.02)
    norm1_g = 1.0 + nrm((DEPTH, D_MODEL), 0.02)
    norm2_g = 1.0 + nrm((DEPTH, D_MODEL), 0.02)
    final_g = 1.0 + nrm((D_MODEL,), 0.02)
    attn_w_qkv = nrm((n_attn, D_MODEL, QKV_DIM), D_MODEL ** -0.5)
    attn_w_o = nrm((n_attn, ATTN_OUT_DIM, D_MODEL), ATTN_OUT_DIM ** -0.5)
    lambda_q1 = nrm((n_attn, A_QK_DIM), 0.1)
    lambda_k1 = nrm((n_attn, A_QK_DIM), 0.1)
    lambda_q2 = nrm((n_attn, A_QK_DIM), 0.1)
    lambda_k2 = nrm((n_attn, A_QK_DIM), 0.1)
    subln_g = 1.0 + nrm((n_attn, A_V_DIM), 0.02)
    sink_logit = nrm((n_attn, B_Q_HEADS), 0.5)
    s5_w_in = nrm((n_s5, D_MODEL, S5_WIDTH), D_MODEL ** -0.5)
    s5_a_re = -0.5 + nrm((n_s5, 2, S5_GROUPS, S5_STATE), 0.01)
    s5_a_im = math.pi * jnp.arange(S5_STATE, dtype=f32) + nrm((n_s5, 2, S5_GROUPS, S5_STATE), 0.01)
    s5_log_step = jax.random.uniform(next(ks), (n_s5, 2, S5_GROUPS), f32, math.log(DT_MIN), math.log(DT_MAX))
    s5_b_re = nrm((n_s5, 2, S5_GROUPS, S5_STATE, S5_GROUP), (2 * S5_GROUP) ** -0.5)
    s5_b_im = nrm((n_s5, 2, S5_GROUPS, S5_STATE, S5_GROUP), (2 * S5_GROUP) ** -0.5)
    s5_c_re = nrm((n_s5, 2, S5_GROUPS, S5_GROUP, S5_STATE), 0.5)
    s5_c_im = nrm((n_s5, 2, S5_GROUPS, S5_GROUP, S5_STATE), 0.5)
    s5_d = nrm((n_s5, S5_WIDTH), 0.5)
    s5_w_glu = nrm((n_s5, S5_WIDTH, 2 * D_MODEL), S5_WIDTH ** -0.5)
    router_w = nrm((DEPTH, D_MODEL, N_EXPERTS), D_MODEL ** -0.5)
    router_b = nrm((DEPTH, N_EXPERTS), 0.01)
    expert_w_gu = nrm((DEPTH, N_EXPERTS, D_MODEL, 2 * D_EXPERT), D_MODEL ** -0.5)
    expert_b_gu = nrm((DEPTH, N_EXPERTS, 2 * D_EXPERT), 0.01)
    expert_w_down = nrm((DEPTH, N_EXPERTS, D_EXPERT, D_MODEL), D_EXPERT ** -0.5)
    expert_b_down = nrm((DEPTH, N_EXPERTS, D_MODEL), 0.01)
    return {"x": x, "c": c, "ctx": ctx, "c_ctx": c_ctx, "w_mod": w_mod, "b_mod": b_mod,
            "norm1_g": norm1_g, "norm2_g": norm2_g, "final_g": final_g,
            "attn_w_qkv": attn_w_qkv, "attn_w_o": attn_w_o,
            "lambda_q1": lambda_q1, "lambda_k1": lambda_k1, "lambda_q2": lambda_q2, "lambda_k2": lambda_k2,
            "subln_g": subln_g, "sink_logit": sink_logit,
            "s5_w_in": s5_w_in, "s5_a_re": s5_a_re, "s5_a_im": s5_a_im, "s5_log_step": s5_log_step,
            "s5_b_re": s5_b_re, "s5_b_im": s5_b_im, "s5_c_re": s5_c_re, "s5_c_im": s5_c_im,
            "s5_d": s5_d, "s5_w_glu": s5_w_glu,
            "router_w": router_w, "router_b": router_b,
            "expert_w_gu": expert_w_gu, "expert_b_gu": expert_b_gu,
            "expert_w_down": expert_w_down, "expert_b_down": expert_b_down}


def reference(x, c, ctx, c_ctx, w_mod, b_mod, norm1_g, norm2_g, final_g, attn_w_qkv, attn_w_o,
              lambda_q1, lambda_k1, lambda_q2, lambda_k2, subln_g, sink_logit,
              s5_w_in, s5_a_re, s5_a_im, s5_log_step, s5_b_re, s5_b_im, s5_c_re, s5_c_im, s5_d, s5_w_glu,
              router_w, router_b, expert_w_gu, expert_b_gu, expert_w_down, expert_b_down):
    bsz, seq, d = x.shape
    ROWS = seq // GRID_W
    rows = jnp.repeat(jnp.arange(ROWS, dtype=jnp.float32), GRID_W)
    cols = (jnp.arange(ROWS * GRID_W) % GRID_W).astype(jnp.float32)
    rope_a = axial_rope_tables(rows, cols, A_QK_DIM)
    rope_b = axial_rope_tables(rows, cols, B_HEAD_DIM)
    for i in range(DEPTH):
        last = i == DEPTH - 1
        j = i // 2
        mod = jax.nn.silu(c) @ w_mod[i] + b_mod[i]
        sh1, sc1, g1, sh2, sc2, g2 = jnp.split(mod[:, None, :], N_MOD, axis=-1)
        mod_c = jax.nn.silu(c_ctx) @ w_mod[i] + b_mod[i]
        csh1, csc1, cg1, csh2, csc2, cg2 = jnp.split(mod_c, N_MOD, axis=-1)
        h_lat = modulate(rmsnorm(x, norm1_g[i]), sh1, sc1)
        h_ctx = modulate(rmsnorm(ctx, norm1_g[i]), csh1, csc1)
        if i % 2 == 0:
            y_lat, y_ctx = attention_mixer(h_lat, h_ctx, attn_w_qkv[j], attn_w_o[j], lambda_q1[j], lambda_k1[j],
                                           lambda_q2[j], lambda_k2[j], subln_g[j], sink_logit[j],
                                           lambda_init_fn(i), rope_a, rope_b, not last)
        else:
            y_lat, y_ctx = s5_mixer(h_lat, h_ctx, s5_w_in[j], s5_a_re[j], s5_a_im[j], s5_log_step[j],
                                    s5_b_re[j], s5_b_im[j], s5_c_re[j], s5_c_im[j], s5_d[j], s5_w_glu[j],
                                    not last)
        x = x + g1 * y_lat
        h_lat = modulate(rmsnorm(x, norm2_g[i]), sh2, sc2)
        if last:
            f_lat = moe(h_lat.reshape(-1, d), router_w[i], router_b[i], expert_w_gu[i], expert_b_gu[i],
                        expert_w_down[i], expert_b_down[i]).reshape(x.shape)
        else:
            ctx = ctx + cg1 * y_ctx
            h_ctx = modulate(rmsnorm(ctx, norm2_g[i]), csh2, csc2)
            n_ctx_tok = h_ctx.shape[0] * h_ctx.shape[1]
            tokens = jnp.concatenate([h_ctx.reshape(-1, d), h_lat.reshape(-1, d)], axis=0)
            f = moe(tokens, router_w[i], router_b[i], expert_w_gu[i], expert_b_gu[i],
                    expert_w_down[i], expert_b_down[i])
            ctx = ctx + cg2 * f[:n_ctx_tok].reshape(ctx.shape)
            f_lat = f[n_ctx_tok:].reshape(x.shape)
        x = x + g2 * f_lat
    return rmsnorm(x, final_g)
```

```python
import functools
import math

import jax
import jax.numpy as jnp
import numpy as np
from jax import lax
from jax.experimental import pallas as pl
from jax.experimental.pallas import tpu as pltpu

F32 = jnp.float32
BF16 = jnp.bfloat16
HIGHEST = lax.Precision.HIGHEST

V7X_VMEM_BYTES = 64 * 1024 * 1024
VMEM_LIMIT = V7X_VMEM_BYTES - 8 * 1024 * 1024
LANES = 128
SUBLANES = 8

GRID_W = 64
N_MOD = 6
EPS = 1e-6
NEG_INF = -1e30
ROPE_THETA = 10000.0
A_HEADS = 8
A_QK_DIM = 64
A_V_DIM = 128
B_Q_HEADS = 8
B_KV_HEADS = 2
B_GROUP = 4
B_HEAD_DIM = 128
WINDOW = 128
S5_GROUP = 16
S5_STATE = 64
S5_TC = 16
TOP_K = 4
SWIGLU_LIMIT = 7.0
SWIGLU_ALPHA = 1.702
EXPERT_BLOCK = 512
LOG2E = 1.4426950408889634


def _pick(n, cands):
    for c in cands:
        if n % c == 0:
            return c
    raise ValueError(f"no tile for {n} in {cands}")


def _cparams(*sem):
    return pltpu.CompilerParams(dimension_semantics=sem, vmem_limit_bytes=VMEM_LIMIT)


def _mod_kernel(c_ref, w_ref, b_ref, o_ref):
    cv = c_ref[...]
    s = cv * jax.nn.sigmoid(cv)
    o_ref[...] = jnp.dot(s, w_ref[...], preferred_element_type=F32, precision=HIGHEST) + b_ref[...]


def _adaln_mod(cs, w_mod, b_mod, layer):
    d, n = w_mod.shape[1], w_mod.shape[2]
    tn = _pick(n, (1024, 512, 256, 128))
    return pl.pallas_call(
        _mod_kernel,
        grid=(n // tn,),
        in_specs=[pl.BlockSpec((SUBLANES, d), lambda j: (0, 0)),
                  pl.BlockSpec((None, d, tn), lambda j: (layer, 0, j)),
                  pl.BlockSpec((None, 1, tn), lambda j: (layer, 0, j))],
        out_specs=pl.BlockSpec((SUBLANES, tn), lambda j: (0, j)),
        out_shape=jax.ShapeDtypeStruct((SUBLANES, n), F32),
        compiler_params=_cparams("parallel"),
        name="adaln_mod",
    )(cs, w_mod, b_mod)


def _norm_mod(x, g, sh2, sc2, row0, n_ctx):
    ms = jnp.mean(x * x, axis=-1, keepdims=True)
    y = x * lax.rsqrt(ms + EPS) * g
    row = row0 + lax.broadcasted_iota(jnp.int32, (x.shape[0], 1), 0)
    is_ctx = row < n_ctx
    sc = jnp.where(is_ctx, sc2[0:1, :], sc2[1:2, :])
    sh = jnp.where(is_ctx, sh2[0:1, :], sh2[1:2, :])
    return y * (1.0 + sc) + sh


ROPE_NONE = 4


def _rope_store(acc, cos_ref, sin_ref, o_ref, shift):
    cosv, sinv = cos_ref[...], sin_ref[...]
    for cgrp in range(acc.shape[1] // LANES):
        a = acc[:, cgrp * LANES:(cgrp + 1) * LANES]
        lane = lax.broadcasted_iota(jnp.int32, a.shape, 1)
        in_second = (lane & (2 * shift - 1)) >= shift
        sw = jnp.where(in_second, pltpu.roll(a, shift, 1), pltpu.roll(a, LANES - shift, 1))
        o_ref[:, cgrp * LANES:(cgrp + 1) * LANES] = (a * cosv + sw * sinv).astype(o_ref.dtype)


NORM_ROWS = 256


def _norm_mod_to_scratch(x_ref, g_ref, sh_ref, sc_ref, h_scr, row0, n_ctx):
    def body(r, carry):
        off = pl.multiple_of(r * NORM_ROWS, NORM_ROWS)
        h = _norm_mod(x_ref[pl.ds(off, NORM_ROWS), :], g_ref[...], sh_ref[...], sc_ref[...], row0 + off, n_ctx)
        h_scr[pl.ds(off, NORM_ROWS), :] = h.astype(h_scr.dtype)
        return carry

    lax.fori_loop(0, x_ref.shape[0] // NORM_ROWS, body, 0)


def _nmm_rope_kernel(tt_ref, x_ref, g_ref, sh_ref, sc_ref, w_ref, cos_ref, sin_ref, o_ref, h_scr,
                     *, n_ctx, tm):
    i = pl.program_id(0)
    j = pl.program_id(1)

    @pl.when(j == 0)
    def _():
        _norm_mod_to_scratch(x_ref, g_ref, sh_ref, sc_ref, h_scr, i * tm, n_ctx)

    acc = jnp.dot(h_scr[...], w_ref[...], preferred_element_type=F32)
    t = tt_ref[j]

    @pl.when(t == ROPE_NONE)
    def _():
        o_ref[...] = acc.astype(o_ref.dtype)

    @pl.when(t < 2)
    def _():
        _rope_store(acc, cos_ref, sin_ref, o_ref, A_QK_DIM // 4)

    @pl.when(jnp.logical_and(t >= 2, t < ROPE_NONE))
    def _():
        _rope_store(acc, cos_ref, sin_ref, o_ref, B_HEAD_DIM // 4)


def _nmm_plain_kernel(x_ref, g_ref, sh_ref, sc_ref, w_ref, o_ref, h_scr, *, n_ctx, tm):
    i = pl.program_id(0)
    j = pl.program_id(1)

    @pl.when(j == 0)
    def _():
        _norm_mod_to_scratch(x_ref, g_ref, sh_ref, sc_ref, h_scr, i * tm, n_ctx)

    o_ref[...] = jnp.dot(h_scr[...], w_ref[...], preferred_element_type=F32).astype(o_ref.dtype)


def _norm_mod_matmul(x, g, mod, sh_col, sc_col, w, n_ctx, out_dtype, rope=None):
    t_rows, d = x.shape
    n = w.shape[1]
    tm = _pick(t_rows, (1280, 1024, 512, 256, 128))
    tn = 256
    grid = (t_rows // tm, n // tn)
    kern_kw = dict(n_ctx=n_ctx, tm=tm)
    scratch = [pltpu.VMEM((tm, d), BF16)]
    out_shape = jax.ShapeDtypeStruct((t_rows, n), out_dtype)
    if rope is None:
        return pl.pallas_call(
            functools.partial(_nmm_plain_kernel, **kern_kw),
            grid=grid,
            in_specs=[pl.BlockSpec((tm, d), lambda i, j: (i, 0)),
                      pl.BlockSpec((1, d), lambda i, j: (0, 0)),
                      pl.BlockSpec((SUBLANES, d), lambda i, j: (0, sh_col)),
                      pl.BlockSpec((SUBLANES, d), lambda i, j: (0, sc_col)),
                      pl.BlockSpec((d, tn), lambda i, j: (0, j))],
            out_specs=pl.BlockSpec((tm, tn), lambda i, j: (i, j)),
            out_shape=out_shape,
            scratch_shapes=scratch,
            compiler_params=_cparams("parallel", "arbitrary"),
            name="norm_mod_matmul",
        )(x, g, mod, mod, w)
    ttype, cos_t, sin_t = rope
    return pl.pallas_call(
        functools.partial(_nmm_rope_kernel, **kern_kw),
        grid_spec=pltpu.PrefetchScalarGridSpec(
            num_scalar_prefetch=1,
            grid=grid,
            in_specs=[pl.BlockSpec((tm, d), lambda i, j, tt: (i, 0)),
                      pl.BlockSpec((1, d), lambda i, j, tt: (0, 0)),
                      pl.BlockSpec((SUBLANES, d), lambda i, j, tt: (0, sh_col)),
                      pl.BlockSpec((SUBLANES, d), lambda i, j, tt: (0, sc_col)),
                      pl.BlockSpec((d, tn), lambda i, j, tt: (0, j)),
                      pl.BlockSpec((None, tm, LANES), lambda i, j, tt: (tt[j], i, 0)),
                      pl.BlockSpec((None, tm, LANES), lambda i, j, tt: (tt[j], i, 0))],
            out_specs=pl.BlockSpec((tm, tn), lambda i, j, tt: (i, j)),
            scratch_shapes=scratch),
        out_shape=out_shape,
        compiler_params=_cparams("parallel", "arbitrary"),
        name="norm_mod_qkv_rope",
    )(ttype, x, g, mod, mod, w, cos_t, sin_t)


def _rope_tables(n_ctx, seq):
    pos = jnp.arange(seq)
    rows = (pos // GRID_W).astype(F32)
    cols = (pos % GRID_W).astype(F32)

    def tab(dim):
        quarter = dim // 4
        freqs = ROPE_THETA ** (-jnp.arange(quarter, dtype=F32) / quarter)
        ar, ac = rows[:, None] * freqs, cols[:, None] * freqs
        cosv = jnp.concatenate([jnp.cos(ar), jnp.cos(ar), jnp.cos(ac), jnp.cos(ac)], axis=1)
        sinv = jnp.concatenate([-jnp.sin(ar), jnp.sin(ar), -jnp.sin(ac), jnp.sin(ac)], axis=1)
        reps = LANES // dim
        cosv, sinv = jnp.tile(cosv, (1, reps)), jnp.tile(sinv, (1, reps))
        cosv = jnp.concatenate([jnp.ones((n_ctx, LANES), F32), cosv], axis=0)
        sinv = jnp.concatenate([jnp.zeros((n_ctx, LANES), F32), sinv], axis=0)
        return cosv, sinv

    ca, sa = tab(A_QK_DIM)
    cb, sb = tab(B_HEAD_DIM)
    qa = (A_QK_DIM ** -0.5) * LOG2E
    qb = B_HEAD_DIM ** -0.5
    one, zero = jnp.ones_like(ca), jnp.zeros_like(ca)
    cos_t = jnp.stack([ca * qa, ca, cb * qb, cb, one])
    sin_t = jnp.stack([sa * qa, sa, sb * qb, sb, zero])
    return cos_t, sin_t


def _flash_kernel(lam_ref, qt_ref, k_ref, vt_ref, g_ref, o_ref, acc1, acc2,
                  *, tq, tk, n_ctx, t_rows, out_scale):
    i = pl.program_id(1)
    qt = qt_ref[...]
    row = lax.broadcasted_iota(jnp.int32, qt.shape, 0)
    zero = jnp.zeros_like(qt)
    qm = (jnp.where(row < A_QK_DIM, qt, zero), jnp.where(row >= A_QK_DIM, qt, zero))
    accs = (acc1, acc2)

    def step(j, carry):
        ms, ls = carry
        off = pl.multiple_of(j * tk, tk)
        kt = k_ref[pl.ds(off, tk), :]
        vt = vt_ref[:, pl.ds(off, tk)]
        new_ms, new_ls = [], []
        for m in range(2):
            s = jnp.dot(kt, qm[m], preferred_element_type=F32)
            mx = jnp.maximum(ms[m], jnp.max(s, axis=0, keepdims=True))
            alpha = jnp.exp2(ms[m] - mx)
            p = jnp.exp2(s - mx)
            new_ls.append(alpha * ls[m] + jnp.sum(p, axis=0, keepdims=True))
            accs[m][...] = alpha * accs[m][...] + jnp.dot(vt, p.astype(BF16), preferred_element_type=F32)
            new_ms.append(mx)
        return tuple(new_ms), tuple(new_ls)

    def run(n_steps):
        acc1[...] = jnp.zeros_like(acc1)
        acc2[...] = jnp.zeros_like(acc2)
        m0 = jnp.full((1, tq), NEG_INF, F32)
        l0 = jnp.zeros((1, tq), F32)
        ms, ls = lax.fori_loop(0, n_steps, step, ((m0, m0), (l0, l0)))
        o = acc1[...] / ls[0] - lam_ref[0] * (acc2[...] / ls[1])
        var = jnp.mean(o * o, axis=0, keepdims=True)
        o = o * lax.rsqrt(var + EPS) * (g_ref[...] * out_scale)
        o_ref[...] = o.T.astype(o_ref.dtype)

    n_q_ctx = n_ctx // tq

    @pl.when(i < n_q_ctx)
    def _():
        run(n_ctx // tk)

    @pl.when(i >= n_q_ctx)
    def _():
        run(t_rows // tk)


def _diff_attention(qt, k, vt, lam, subln_col, n_ctx, out_scale):
    t_rows = k.shape[0]
    tq = 256
    tk = 256
    return pl.pallas_call(
        functools.partial(_flash_kernel, tq=tq, tk=tk, n_ctx=n_ctx, t_rows=t_rows, out_scale=out_scale),
        grid=(A_HEADS, t_rows // tq),
        in_specs=[pl.BlockSpec(memory_space=pltpu.SMEM),
                  pl.BlockSpec((LANES, tq), lambda h, i: (h, i)),
                  pl.BlockSpec((t_rows, LANES), lambda h, i: (0, h)),
                  pl.BlockSpec((LANES, t_rows), lambda h, i: (h, 0)),
                  pl.BlockSpec((LANES, 1), lambda h, i: (0, 0))],
        out_specs=pl.BlockSpec((tq, LANES), lambda h, i: (i, h)),
        out_shape=jax.ShapeDtypeStruct((t_rows, A_HEADS * A_V_DIM), BF16),
        scratch_shapes=[pltpu.VMEM((LANES, tq), F32), pltpu.VMEM((LANES, tq), F32)],
        compiler_params=_cparams("parallel", "arbitrary"),
        name="diff_attention",
    )(lam, qt, k, vt, subln_col)


def _window_kernel(sink_ref, q_ref, kp_ref, ko_ref, kn_ref, vp_ref, vo_ref, vn_ref, kc_ref, vc_ref, o_ref,
                   *, nb, nb_ctx):
    g = pl.program_id(0)
    n = pl.program_id(1)
    qi = lax.broadcasted_iota(jnp.int32, (WINDOW, WINDOW), 0)
    kk = lax.broadcasted_iota(jnp.int32, (WINDOW, WINDOW), 1)
    own_ok = n >= nb_ctx
    prev_ok = n >= nb_ctx + 1
    next_ok = jnp.logical_and(own_ok, n <= nb - 2)
    m_prev = jnp.logical_and(kk >= qi, prev_ok)
    m_own = jnp.logical_and(kk >= 0, own_ok)
    m_next = jnp.logical_and(kk <= qi, next_ok)
    dn = (((1,), (1,)), ((), ()))
    for r in range(B_GROUP):
        q = q_ref[:, r * B_HEAD_DIM:(r + 1) * B_HEAD_DIM]
        s_p = jnp.where(m_prev, lax.dot_general(q, kp_ref[...], dn, preferred_element_type=F32), NEG_INF)
        s_o = jnp.where(m_own, lax.dot_general(q, ko_ref[...], dn, preferred_element_type=F32), NEG_INF)
        s_n = jnp.where(m_next, lax.dot_general(q, kn_ref[...], dn, preferred_element_type=F32), NEG_INF)
        s_c = lax.dot_general(q, kc_ref[...], dn, preferred_element_type=F32)
        sink = sink_ref[g * B_GROUP + r]
        mx = jnp.maximum(jnp.maximum(jnp.max(s_p, axis=1, keepdims=True), jnp.max(s_o, axis=1, keepdims=True)),
                         jnp.maximum(jnp.max(s_n, axis=1, keepdims=True), jnp.max(s_c, axis=1, keepdims=True)))
        mx = jnp.maximum(mx, sink)
        p_p, p_o, p_n, p_c = jnp.exp(s_p - mx), jnp.exp(s_o - mx), jnp.exp(s_n - mx), jnp.exp(s_c - mx)
        den = (jnp.sum(p_p, axis=1, keepdims=True) + jnp.sum(p_o, axis=1, keepdims=True)
               + jnp.sum(p_n, axis=1, keepdims=True) + jnp.sum(p_c, axis=1, keepdims=True)
               + jnp.exp(sink - mx))
        o = (jnp.dot(p_p.astype(BF16), vp_ref[...], preferred_element_type=F32)
             + jnp.dot(p_o.astype(BF16), vo_ref[...], preferred_element_type=F32)
             + jnp.dot(p_n.astype(BF16), vn_ref[...], preferred_element_type=F32)
             + jnp.dot(p_c.astype(BF16), vc_ref[...], preferred_element_type=F32))
        o_ref[:, r * B_HEAD_DIM:(r + 1) * B_HEAD_DIM] = (o / den).astype(o_ref.dtype)


def _window_attention(qkv, sink, n_ctx, col_q, col_k, col_v):
    t_rows = qkv.shape[0]
    nb = t_rows // WINDOW
    nb_ctx = n_ctx // WINDOW
    gq = B_GROUP * B_HEAD_DIM // LANES

    def kv_spec(col, shift):
        def imap(g, n):
            return (jnp.clip(n + shift, 0, nb - 1), col + g)
        return pl.BlockSpec((WINDOW, LANES), imap)

    return pl.pallas_call(
        functools.partial(_window_kernel, nb=nb, nb_ctx=nb_ctx),
        grid=(B_KV_HEADS, nb),
        in_specs=[pl.BlockSpec(memory_space=pltpu.SMEM),
                  pl.BlockSpec((WINDOW, B_GROUP * B_HEAD_DIM), lambda g, n: (n, col_q // gq + g)),
                  kv_spec(col_k, -1), kv_spec(col_k, 0), kv_spec(col_k, 1),
                  kv_spec(col_v, -1), kv_spec(col_v, 0), kv_spec(col_v, 1),
                  pl.BlockSpec((n_ctx, LANES), lambda g, n: (0, col_k + g)),
                  pl.BlockSpec((n_ctx, LANES), lambda g, n: (0, col_v + g))],
        out_specs=pl.BlockSpec((WINDOW, B_GROUP * B_HEAD_DIM), lambda g, n: (n, g)),
        out_shape=jax.ShapeDtypeStruct((t_rows, B_Q_HEADS * B_HEAD_DIM), BF16),
        compiler_params=_cparams("parallel", "arbitrary"),
        name="window_attention",
    )(sink, qkv, qkv, qkv, qkv, qkv, qkv, qkv, qkv, qkv)


def _post_mixer(y, x, g1_ref, g_ref, sh_ref, sc_ref, wr_ref, br_ref, xo_ref, h_ref, lg_ref, row0, n_ctx):
    row = row0 + lax.broadcasted_iota(jnp.int32, (x.shape[0], 1), 0)
    g1 = jnp.where(row < n_ctx, g1_ref[0:1, :], g1_ref[1:2, :])
    xn = x + g1 * y
    xo_ref[...] = xn
    h = _norm_mod(xn, g_ref[...], sh_ref[...], sc_ref[...], row0, n_ctx)
    h_ref[...] = h.astype(h_ref.dtype)
    lg_ref[...] = jnp.dot(h, wr_ref[...], preferred_element_type=F32, precision=HIGHEST) + br_ref[...]


def _attn_out_kernel(ya_ref, yb_ref, x_ref, woa_ref, wob_ref, g1_ref, g_ref, sh_ref, sc_ref, wr_ref, br_ref,
                     xo_ref, h_ref, lg_ref, *, n_ctx, tm):
    y = (jnp.dot(ya_ref[...], woa_ref[...], preferred_element_type=F32)
         + jnp.dot(yb_ref[...], wob_ref[...], preferred_element_type=F32))
    _post_mixer(y, x_ref[...], g1_ref, g_ref, sh_ref, sc_ref, wr_ref, br_ref, xo_ref, h_ref, lg_ref,
                pl.program_id(0) * tm, n_ctx)


def _glu_out_kernel(y_ref, x_ref, wv_ref, wg_ref, g1_ref, g_ref, sh_ref, sc_ref, wr_ref, br_ref,
                    xo_ref, h_ref, lg_ref, *, n_ctx, tm):
    a = jax.nn.gelu(y_ref[...], approximate=True).astype(BF16)
    val = jnp.dot(a, wv_ref[...], preferred_element_type=F32)
    gate = jnp.dot(a, wg_ref[...], preferred_element_type=F32)
    _post_mixer(val * jax.nn.sigmoid(gate), x_ref[...], g1_ref, g_ref, sh_ref, sc_ref, wr_ref, br_ref,
                xo_ref, h_ref, lg_ref, pl.program_id(0) * tm, n_ctx)


def _mixer_out(kernel_fn, acts, x, weights, mod, cols, g2row, wr, br, n_ctx, name):
    t_rows, d = x.shape
    tm = 256
    row = lambda i: (i, 0)
    const = lambda i: (0, 0)
    in_specs = ([pl.BlockSpec((tm, a.shape[1]), row) for a in acts]
                + [pl.BlockSpec((tm, d), row)]
                + [pl.BlockSpec(w.shape, const) for w in weights]
                + [pl.BlockSpec((SUBLANES, d), lambda i, c=c: (0, c)) for c in cols[:1]]
                + [pl.BlockSpec((1, d), const)]
                + [pl.BlockSpec((SUBLANES, d), lambda i, c=c: (0, c)) for c in cols[1:]]
                + [pl.BlockSpec(wr.shape, const), pl.BlockSpec(br.shape, const)])
    return pl.pallas_call(
        functools.partial(kernel_fn, n_ctx=n_ctx, tm=tm),
        grid=(t_rows // tm,),
        in_specs=in_specs,
        out_specs=[pl.BlockSpec((tm, d), row), pl.BlockSpec((tm, d), row), pl.BlockSpec((tm, LANES), row)],
        out_shape=[jax.ShapeDtypeStruct((t_rows, d), F32), jax.ShapeDtypeStruct((t_rows, d), BF16),
                   jax.ShapeDtypeStruct((t_rows, LANES), F32)],
        compiler_params=_cparams("parallel"),
        name=name,
    )(*acts, x, *weights, mod, g2row, mod, mod, wr, br)


def _router_kernel(lg_ref, idx_ref, gate_ref, cnt_ref, *, n_exp):
    @pl.when(pl.program_id(0) == 0)
    def _():
        cnt_ref[...] = jnp.zeros_like(cnt_ref)

    lt = lg_ref[...].T[0:n_exp, :]
    eid = lax.broadcasted_iota(jnp.int32, lt.shape, 0).astype(F32)
    vals, idxs = [], []
    hist = jnp.zeros(lt.shape, F32)
    for _ in range(TOP_K):
        mv = jnp.max(lt, axis=0, keepdims=True)
        ix = jnp.min(jnp.where(lt == mv, eid, float(n_exp)), axis=0, keepdims=True)
        sel = eid == ix
        hist = hist + sel.astype(F32)
        lt = jnp.where(sel, -jnp.inf, lt)
        vals.append(mv)
        idxs.append(ix)
    es = [jnp.exp(v - vals[0]) for v in vals]
    den = es[0] + es[1] + es[2] + es[3]
    pad_f = jnp.zeros((SUBLANES - TOP_K, lt.shape[1]), F32)
    idx_ref[...] = jnp.concatenate(idxs + [pad_f], axis=0).astype(jnp.int32)
    gate_ref[...] = jnp.concatenate([e / den for e in es] + [pad_f], axis=0)
    cnt_ref[...] += jnp.sum(hist, axis=1, keepdims=True)


def _router(logits, n_exp):
    t_rows = logits.shape[0]
    tm = 256
    return pl.pallas_call(
        functools.partial(_router_kernel, n_exp=n_exp),
        grid=(t_rows // tm,),
        in_specs=[pl.BlockSpec((tm, LANES), lambda i: (i, 0))],
        out_specs=[pl.BlockSpec((SUBLANES, tm), lambda i: (0, i)),
                   pl.BlockSpec((SUBLANES, tm), lambda i: (0, i)),
                   pl.BlockSpec((n_exp, 1), lambda i: (0, 0))],
        out_shape=[jax.ShapeDtypeStruct((SUBLANES, t_rows), jnp.int32),
                   jax.ShapeDtypeStruct((SUBLANES, t_rows), F32),
                   jax.ShapeDtypeStruct((n_exp, 1), F32)],
        compiler_params=_cparams("arbitrary"),
        name="router_topk",
    )(logits)


def _dest_kernel(idx_ref, start_ref, dest_ref, carry, *, n_exp, tm):
    @pl.when(pl.program_id(0) == 0)
    def _():
        carry[...] = start_ref[...]

    eid = lax.broadcasted_iota(jnp.int32, (n_exp, tm), 0)
    idx = idx_ref[...]
    sels = [eid == idx[k:k + 1, :] for k in range(TOP_K)]
    total = sels[0].astype(F32) + sels[1].astype(F32) + sels[2].astype(F32) + sels[3].astype(F32)
    rr = lax.broadcasted_iota(jnp.int32, (tm, tm), 0)
    cc = lax.broadcasted_iota(jnp.int32, (tm, tm), 1)
    upper = jnp.where(rr < cc, 1.0, 0.0).astype(BF16)
    before = jnp.dot(total.astype(BF16), upper, preferred_element_type=F32) + carry[...]
    rows = [jnp.sum(jnp.where(sels[k], before, 0.0), axis=0, keepdims=True) for k in range(TOP_K)]
    pad = jnp.zeros((SUBLANES - TOP_K, tm), F32)
    dest_ref[...] = jnp.concatenate(rows + [pad], axis=0).astype(jnp.int32)
    carry[...] += jnp.sum(total, axis=1, keepdims=True)


def _dest_rows(idx, starts, n_exp):
    t_rows = idx.shape[1]
    tm = 256
    return pl.pallas_call(
        functools.partial(_dest_kernel, n_exp=n_exp, tm=tm),
        grid=(t_rows // tm,),
        in_specs=[pl.BlockSpec((SUBLANES, tm), lambda i: (0, i)),
                  pl.BlockSpec((n_exp, 1), lambda i: (0, 0))],
        out_specs=pl.BlockSpec((SUBLANES, tm), lambda i: (0, i)),
        out_shape=jax.ShapeDtypeStruct((SUBLANES, t_rows), jnp.int32),
        scratch_shapes=[pltpu.VMEM((n_exp, 1), F32)],
        compiler_params=_cparams("arbitrary"),
        name="moe_dest_rows",
    )(idx, starts)


def _expert_kernel(be_ref, bv_ref, x_ref, wg_ref, wu_ref, bg_ref, bu_ref, wd_ref, bd_ref, o_ref, acc,
                   *, blk, nf):
    b = pl.program_id(0)
    f = pl.program_id(1)
    nvalid = bv_ref[b]

    @pl.when(nvalid > 0)
    def _():
        rows = lax.broadcasted_iota(jnp.int32, (blk, 1), 0)
        x = x_ref[...]
        x = jnp.where(rows < nvalid, x, jnp.zeros_like(x))
        gate = jnp.dot(x, wg_ref[...], preferred_element_type=F32) + bg_ref[...]
        up = jnp.dot(x, wu_ref[...], preferred_element_type=F32) + bu_ref[...]
        gate = jnp.minimum(gate, SWIGLU_LIMIT)
        up = jnp.clip(up, -SWIGLU_LIMIT, SWIGLU_LIMIT)
        act = (up + 1.0) * (gate * jax.nn.sigmoid(SWIGLU_ALPHA * gate))
        part = jnp.dot(act.astype(BF16), wd_ref[...], preferred_element_type=F32)

        @pl.when(f == 0)
        def _():
            acc[...] = part + bd_ref[...]

        @pl.when(f > 0)
        def _():
            acc[...] += part

        @pl.when(f == nf - 1)
        def _():
            o_ref[...] = acc[...].astype(o_ref.dtype)

    @pl.when(jnp.logical_and(nvalid == 0, f == nf - 1))
    def _():
        o_ref[...] = jnp.zeros_like(o_ref)


def _expert_matmul(xs, blk_e, blk_valid, w_gu, b_gu, w_dn, b_dn, layer):
    n_rows, d = xs.shape
    f_dim = w_dn.shape[2]
    blk = EXPERT_BLOCK
    tf = 512
    nf = f_dim // tf
    n_blocks = n_rows // blk
    return pl.pallas_call(
        functools.partial(_expert_kernel, blk=blk, nf=nf),
        grid_spec=pltpu.PrefetchScalarGridSpec(
            num_scalar_prefetch=2,
            grid=(n_blocks, nf),
            in_specs=[pl.BlockSpec((blk, d), lambda b, f, be, bv: (b, 0)),
                      pl.BlockSpec((None, None, d, tf), lambda b, f, be, bv: (layer, be[b], 0, f)),
                      pl.BlockSpec((None, None, d, tf), lambda b, f, be, bv: (layer, be[b], 0, nf + f)),
                      pl.BlockSpec((None, None, 1, tf), lambda b, f, be, bv: (layer, be[b], 0, f)),
                      pl.BlockSpec((None, None, 1, tf), lambda b, f, be, bv: (layer, be[b], 0, nf + f)),
                      pl.BlockSpec((None, None, tf, d), lambda b, f, be, bv: (layer, be[b], f, 0)),
                      pl.BlockSpec((None, None, 1, d), lambda b, f, be, bv: (layer, be[b], 0, 0))],
            out_specs=pl.BlockSpec((blk, d), lambda b, f, be, bv: (b, 0)),
            scratch_shapes=[pltpu.VMEM((blk, d), F32)]),
        out_shape=jax.ShapeDtypeStruct((n_rows, d), BF16),
        compiler_params=_cparams("parallel", "arbitrary"),
        name="expert_matmul",
    )(blk_e, blk_valid, xs, w_gu, w_gu, b_gu, b_gu, w_dn, b_dn)


def _combine_kernel(y_ref, gate_ref, x_ref, g2_ref, fg_ref, o_ref, *, n_ctx, tm, final):
    gates = gate_ref[...]
    f = y_ref[0].astype(F32) * gates[:, 0:1]
    for k in range(1, TOP_K):
        f = f + y_ref[k].astype(F32) * gates[:, k:k + 1]
    row = pl.program_id(0) * tm + lax.broadcasted_iota(jnp.int32, (tm, 1), 0)
    g2 = jnp.where(row < n_ctx, g2_ref[0:1, :], g2_ref[1:2, :])
    xn = x_ref[...] + g2 * f
    if final:
        ms = jnp.mean(xn * xn, axis=-1, keepdims=True)
        xn = xn * lax.rsqrt(ms + EPS) * fg_ref[...]
    o_ref[...] = xn


def _combine(yk, gates_t, x, mod, g2_col, final_g, n_ctx, final):
    t_rows, d = x.shape
    tm = 256
    return pl.pallas_call(
        functools.partial(_combine_kernel, n_ctx=n_ctx, tm=tm, final=final),
        grid=(t_rows // tm,),
        in_specs=[pl.BlockSpec((TOP_K, tm, d), lambda i: (0, i, 0)),
                  pl.BlockSpec((tm, SUBLANES), lambda i: (i, 0)),
                  pl.BlockSpec((tm, d), lambda i: (i, 0)),
                  pl.BlockSpec((SUBLANES, d), lambda i: (0, g2_col)),
                  pl.BlockSpec((1, d), lambda i: (0, 0))],
        out_specs=pl.BlockSpec((tm, d), lambda i: (i, 0)),
        out_shape=jax.ShapeDtypeStruct((t_rows, d), F32),
        compiler_params=_cparams("parallel"),
        name="moe_combine",
    )(yk, gates_t, x, mod, final_g)


def _moe(h, logits, x, mod, g2_col, w_gu, b_gu, w_dn, b_dn, layer, n_ctx, final_g, final):
    t_rows, d = h.shape
    n_exp = w_gu.shape[1]
    idx, gates, counts = _router(logits, n_exp)
    counts = counts[:, 0].astype(jnp.int32)
    blk = EXPERT_BLOCK
    padded = (counts + blk - 1) // blk * blk
    pends = jnp.cumsum(padded)
    pstarts = pends - padded
    dest = _dest_rows(idx, pstarts.astype(F32)[:, None], n_exp)[:TOP_K]
    n_blocks = -(-(t_rows * TOP_K) // blk) + n_exp
    n_rows = n_blocks * blk
    blk_start = jnp.arange(n_blocks, dtype=jnp.int32) * blk
    blk_e = jnp.minimum(jnp.searchsorted(pends, blk_start, side='right'), n_exp - 1).astype(jnp.int32)
    blk_valid = jnp.clip(pstarts[blk_e] + counts[blk_e] - blk_start, 0, blk).astype(jnp.int32)
    blk_valid = jnp.where(blk_start < pends[-1], blk_valid, 0)
    tok = jnp.broadcast_to(jnp.arange(t_rows, dtype=jnp.int32)[None, :], dest.shape)
    row_tok = jnp.zeros((n_rows,), jnp.int32).at[dest.reshape(-1)].set(tok.reshape(-1))
    xs = jnp.take(h, row_tok, axis=0)
    ys = _expert_matmul(xs, blk_e, blk_valid, w_gu, b_gu, w_dn, b_dn, layer)
    yk = jnp.take(ys, dest.reshape(-1), axis=0).reshape(TOP_K, t_rows, d)
    return _combine(yk, gates.T, x, mod, g2_col, final_g, n_ctx, final)


def _s5_tables(a_re, a_im, log_step, b_re, b_im, c_re, c_im, d_skip):
    tc = S5_TC
    n_grp, n_st = a_re.shape[1], a_re.shape[2]
    lr = jnp.minimum(a_re.astype(F32), -1e-4)
    li = a_im.astype(F32)
    dt = jnp.exp(log_step.astype(F32))[..., None]
    dd = jnp.arange(tc + 1, dtype=F32)[:, None, None, None]
    mag = jnp.exp(lr * dt * dd)
    pr, pi = mag * jnp.cos(li * dt * dd), mag * jnp.sin(li * dt * dd)
    ar, ai = pr[1], pi[1]
    den = lr * lr + li * li
    nr = ar - 1.0
    zr = (nr * lr + ai * li) / den
    zi = (ai * lr - nr * li) / den
    br, bi = b_re.astype(F32), b_im.astype(F32)
    bbr = zr[..., None] * br - zi[..., None] * bi
    bbi = zr[..., None] * bi + zi[..., None] * br
    abr = pr[:tc, ..., None] * bbr - pi[:tc, ..., None] * bbi
    abi = pr[:tc, ..., None] * bbi + pi[:tc, ..., None] * bbr
    cr, ci = c_re.astype(F32), c_im.astype(F32)
    kern = (jnp.einsum('xgip,dxgpj->dxgij', cr, abr, precision=HIGHEST)
            - jnp.einsum('xgip,dxgpj->dxgij', ci, abi, precision=HIGHEST))
    s_i = jnp.arange(tc)[:, None]
    t_i = jnp.arange(tc)[None, :]
    lag_f = t_i - s_i
    lag_b = s_i - t_i
    m_f = jnp.where((lag_f >= 0)[..., None, None, None], kern[jnp.clip(lag_f, 0), 0], 0.0)
    m_b = jnp.where((lag_b >= 0)[..., None, None, None], kern[jnp.clip(lag_b, 0), 1], 0.0)
    m_tot = (m_f + m_b).transpose(2, 0, 4, 1, 3)
    eye_t = jnp.eye(tc, dtype=F32)
    eye_i = jnp.eye(S5_GROUP, dtype=F32)
    dsk = d_skip.astype(F32).reshape(n_grp, S5_GROUP)
    m_tot = m_tot + (eye_t[None, :, None, :, None] * eye_i[None, None, :, None, :]
                     * dsk[:, None, :, None, None])
    m_tot = m_tot.reshape(n_grp, tc * S5_GROUP, tc * S5_GROUP)
    pw_f = tc - 1 - jnp.arange(tc)
    pw_b = jnp.arange(tc)

    def b_cols(part, pw, x):
        return part[pw, x].transpose(1, 0, 3, 2).reshape(n_grp, tc * S5_GROUP, n_st)

    b_mat = jnp.concatenate([b_cols(abr, pw_f, 0), b_cols(abi, pw_f, 0),
                             b_cols(abr, pw_b, 1), b_cols(abi, pw_b, 1)], axis=-1)
    pcf = 1 + jnp.arange(tc)
    pcb = tc - jnp.arange(tc)

    def c_rows(pw, x):
        prx, pix = pr[pw, x], pi[pw, x]
        re_c = cr[x][None] * prx[:, :, None, :] - ci[x][None] * pix[:, :, None, :]
        im_c = -(cr[x][None] * pix[:, :, None, :] + ci[x][None] * prx[:, :, None, :])
        to_rows = lambda z: z.transpose(1, 3, 0, 2).reshape(n_grp, n_st, tc * S5_GROUP)
        return to_rows(re_c), to_rows(im_c)

    c_mat = jnp.concatenate(list(c_rows(pcf, 0)) + list(c_rows(pcb, 1)), axis=1)
    prt, pit = pr[tc], pi[tc]
    a1 = jnp.concatenate([prt[0], prt[0], prt[1], prt[1]], axis=-1)
    a2 = jnp.concatenate([-pit[0], pit[0], -pit[1], pit[1]], axis=-1)
    return m_tot, b_mat, c_mat, a1, a2


def _s5_in_kernel(u_ref, b_ref, o_ref, *, gb):
    for g in range(gb):
        o_ref[g] = jnp.dot(u_ref[g], b_ref[g], preferred_element_type=F32, precision=HIGHEST)


def _s5_out_kernel(u_ref, z_ref, m_ref, c_ref, o_ref, *, gb):
    for g in range(gb):
        o_ref[g] = (jnp.dot(u_ref[g], m_ref[g], preferred_element_type=F32, precision=HIGHEST)
                    + jnp.dot(z_ref[g], c_ref[g], preferred_element_type=F32, precision=HIGHEST))


def _s5_scan_kernel(h_ref, a1_ref, a2_ref, z_ref, *, n_chunks, n_ctx_chunks):
    half = 2 * S5_STATE
    a1 = a1_ref[...]
    a2 = a2_ref[...]
    a1f, a1b = a1[:, :half], a1[:, half:]
    a2f, a2b = a2[:, :half], a2[:, half:]
    zero = jnp.zeros((a1.shape[0], half), F32)

    def swap(v):
        return pltpu.roll(v, S5_STATE, 1)

    def step(t, carry):
        rf, rfs, rb, rbs = carry
        cb = jnp.where(t < n_ctx_chunks, n_ctx_chunks - 1 - t, n_chunks - 1 - (t - n_ctx_chunks))
        z_ref[t, :, 0:half] = rf
        z_ref[cb, :, half:2 * half] = rb
        hf = h_ref[t, :, 0:half]
        hb = h_ref[cb, :, half:2 * half]
        nrf = a1f * rf + a2f * rfs + hf
        nrfs = a1f * rfs - a2f * rf + swap(hf)
        nrb = a1b * rb + a2b * rbs + hb
        nrbs = a1b * rbs - a2b * rb + swap(hb)
        return nrf, nrfs, nrb, nrbs

    lax.fori_loop(0, n_chunks, step, (zero, zero, zero, zero))


def _s5_mixer_scan(u, tables, n_ctx):
    m_tot, b_mat, c_mat, a1, a2 = tables
    t_rows, width = u.shape
    tc = S5_TC
    n_grp = width // S5_GROUP
    n_chunks = t_rows // tc
    kdim = tc * S5_GROUP
    sdim = 4 * S5_STATE
    gb = 4
    gs = SUBLANES
    ug = u.reshape(n_chunks, tc, n_grp, S5_GROUP).transpose(2, 0, 1, 3).reshape(n_grp, n_chunks, kdim)
    blk3 = lambda a, b: pl.BlockSpec((gb, a, b), lambda g: (g, 0, 0))
    hin = pl.pallas_call(
        functools.partial(_s5_in_kernel, gb=gb),
        grid=(n_grp // gb,),
        in_specs=[blk3(n_chunks, kdim), blk3(kdim, sdim)],
        out_specs=blk3(n_chunks, sdim),
        out_shape=jax.ShapeDtypeStruct((n_grp, n_chunks, sdim), F32),
        compiler_params=_cparams("parallel"),
        name="s5_chunk_inputs",
    )(ug, b_mat)
    hin_t = hin.transpose(1, 0, 2)
    z_t = pl.pallas_call(
        functools.partial(_s5_scan_kernel, n_chunks=n_chunks, n_ctx_chunks=n_ctx // tc),
        grid=(n_grp // gs,),
        in_specs=[pl.BlockSpec((n_chunks, gs, sdim), lambda g: (0, g, 0)),
                  pl.BlockSpec((gs, sdim), lambda g: (g, 0)),
                  pl.BlockSpec((gs, sdim), lambda g: (g, 0))],
        out_specs=pl.BlockSpec((n_chunks, gs, sdim), lambda g: (0, g, 0)),
        out_shape=jax.ShapeDtypeStruct((n_chunks, n_grp, sdim), F32),
        compiler_params=_cparams("parallel"),
        name="s5_chunk_scan",
    )(hin_t, a1, a2)
    z = z_t.transpose(1, 0, 2)
    yg = pl.pallas_call(
        functools.partial(_s5_out_kernel, gb=gb),
        grid=(n_grp // gb,),
        in_specs=[blk3(n_chunks, kdim), blk3(n_chunks, sdim), blk3(kdim, kdim), blk3(sdim, kdim)],
        out_specs=blk3(n_chunks, kdim),
        out_shape=jax.ShapeDtypeStruct((n_grp, n_chunks, kdim), F32),
        compiler_params=_cparams("parallel"),
        name="s5_chunk_outputs",
    )(ug, z, m_tot, c_mat)
    return yg.reshape(n_grp, n_chunks, tc, S5_GROUP).transpose(1, 2, 0, 3).reshape(t_rows, width)


def _lambda_init(layer):
    return 0.8 - 0.6 * math.exp(-0.3 * layer)


def kernel(x, c, ctx, c_ctx, w_mod, b_mod, norm1_g, norm2_g, final_g, attn_w_qkv, attn_w_o, lambda_q1, lambda_k1, lambda_q2, lambda_k2, subln_g, sink_logit, s5_w_in, s5_a_re, s5_a_im, s5_log_step, s5_b_re, s5_b_im, s5_c_re, s5_c_im, s5_d, s5_w_glu, router_w, router_b, expert_w_gu, expert_b_gu, expert_w_down, expert_b_down):
    bsz, seq, d = x.shape
    assert bsz == 1, "single-sequence block"
    n_ctx = ctx.shape[1]
    depth = w_mod.shape[0]
    n_exp = router_w.shape[2]
    assert n_ctx % 256 == 0 and seq % 256 == 0

    xj = jnp.concatenate([ctx[0], x[0]], axis=0)
    cs = jnp.zeros((SUBLANES, d), F32).at[0].set(c_ctx).at[1].set(c[0])
    b_mod3 = b_mod[:, None, :]
    w_gu_bf = expert_w_gu.astype(BF16)
    w_dn_bf = expert_w_down.astype(BF16)
    b_gu4 = expert_b_gu[:, :, None, :]
    b_dn4 = expert_b_down[:, :, None, :]
    wr_pad = jnp.pad(router_w, ((0, 0), (0, 0), (0, LANES - n_exp)))
    br_pad = jnp.pad(router_b, ((0, 0), (0, LANES - n_exp)), constant_values=NEG_INF)[:, None, :]
    final_row = final_g[None, :]

    for i in range(depth):
        last = i == depth - 1
        j = i // 2
        mod = _adaln_mod(cs, w_mod, b_mod3, i)
        if i % 2 == 0:
            sizes = (A_HEADS * 2 * A_QK_DIM, A_HEADS * 2 * A_QK_DIM, A_HEADS * A_V_DIM,
                     B_Q_HEADS * B_HEAD_DIM, B_KV_HEADS * B_HEAD_DIM, B_KV_HEADS * B_HEAD_DIM)
            offs = np.concatenate([[0], np.cumsum(sizes)])
            types = [0, 1, ROPE_NONE, 2, 3, ROPE_NONE]
            ttype = jnp.asarray(np.concatenate([np.full(s // 256, t) for s, t in zip(sizes, types)]), jnp.int32)
            cos_t, sin_t = _rope_tables(n_ctx, seq)
            qkv = _norm_mod_matmul(xj, norm1_g[i][None, :], mod, 0, 1, attn_w_qkv[j].astype(BF16), n_ctx, BF16,
                                   rope=(ttype, cos_t, sin_t))
            qt = qkv[:, offs[0]:offs[1]].T
            ka = qkv[:, offs[1]:offs[2]]
            vt = qkv[:, offs[2]:offs[3]].T
            f32 = F32
            li = _lambda_init(i)
            lam = (jnp.exp(jnp.sum(lambda_q1[j].astype(f32) * lambda_k1[j].astype(f32)))
                   - jnp.exp(jnp.sum(lambda_q2[j].astype(f32) * lambda_k2[j].astype(f32))) + li)
            ya = _diff_attention(qt, ka, vt, lam.reshape(1), subln_g[j][:, None], n_ctx, 1.0 - li)
            yb = _window_attention(qkv, sink_logit[j], n_ctx, int(offs[3]) // LANES, int(offs[4]) // LANES,
                                   int(offs[5]) // LANES)
            w_o = attn_w_o[j].astype(BF16)
            na = A_HEADS * A_V_DIM
            xj, h2, logits = _mixer_out(_attn_out_kernel, [ya, yb], xj, [w_o[:na], w_o[na:]], mod, (2, 3, 4),
                                        norm2_g[i][None, :], wr_pad[i], br_pad[i], n_ctx, "attn_out_router")
        else:
            u = _norm_mod_matmul(xj, norm1_g[i][None, :], mod, 0, 1, s5_w_in[j].astype(BF16), n_ctx, F32)
            tables = _s5_tables(s5_a_re[j], s5_a_im[j], s5_log_step[j], s5_b_re[j], s5_b_im[j],
                                s5_c_re[j], s5_c_im[j], s5_d[j])
            y = _s5_mixer_scan(u, tables, n_ctx)
            w_glu = s5_w_glu[j].astype(BF16)
            xj, h2, logits = _mixer_out(_glu_out_kernel, [y], xj, [w_glu[:, :d], w_glu[:, d:]], mod, (2, 3, 4),
                                        norm2_g[i][None, :], wr_pad[i], br_pad[i], n_ctx, "glu_out_router")
        xj = _moe(h2, logits, xj, mod, 5, w_gu_bf, b_gu4, w_dn_bf, b_dn4, i, n_ctx, final_row, last)
    return xj[n_ctx:][None]
```

```python
import functools
import math

import jax
import jax.numpy as jnp
import numpy as np
from jax import lax
from jax.experimental import pallas as pl
from jax.experimental.pallas import tpu as pltpu

F32 = jnp.float32
BF16 = jnp.bfloat16
HIGHEST = lax.Precision.HIGHEST

V7X_VMEM_BYTES = 64 * 1024 * 1024
VMEM_LIMIT = V7X_VMEM_BYTES - 8 * 1024 * 1024
LANES = 128
SUBLANES = 8

GRID_W = 64
N_MOD = 6
EPS = 1e-6
NEG_INF = -1e30
ROPE_THETA = 10000.0
A_HEADS = 8
A_QK_DIM = 64
A_V_DIM = 128
B_Q_HEADS = 8
B_KV_HEADS = 2
B_GROUP = 4
B_HEAD_DIM = 128
WINDOW = 128
S5_GROUP = 16
S5_STATE = 64
S5_TC = 16
TOP_K = 4
SWIGLU_LIMIT = 7.0
SWIGLU_ALPHA = 1.702
EXPERT_BLOCK = 512
LOG2E = 1.4426950408889634


def _pick(n, cands):
    for c in cands:
        if n % c == 0:
            return c
    raise ValueError(f"no tile for {n} in {cands}")


def _cparams(*sem):
    return pltpu.CompilerParams(dimension_semantics=sem, vmem_limit_bytes=VMEM_LIMIT)


def _mod_kernel(c_ref, w_ref, b_ref, o_ref):
    cv = c_ref[...]
    s = cv * jax.nn.sigmoid(cv)
    o_ref[...] = jnp.dot(s, w_ref[...], preferred_element_type=F32, precision=HIGHEST) + b_ref[...]


def _adaln_mod(cs, w_mod, b_mod, layer):
    d, n = w_mod.shape[1], w_mod.shape[2]
    tn = _pick(n, (1024, 512, 256, 128))
    return pl.pallas_call(
        _mod_kernel,
        grid=(n // tn,),
        in_specs=[pl.BlockSpec((SUBLANES, d), lambda j: (0, 0)),
                  pl.BlockSpec((None, d, tn), lambda j: (layer, 0, j)),
                  pl.BlockSpec((None, 1, tn), lambda j: (layer, 0, j))],
        out_specs=pl.BlockSpec((SUBLANES, tn), lambda j: (0, j)),
        out_shape=jax.ShapeDtypeStruct((SUBLANES, n), F32),
        compiler_params=_cparams("parallel"),
        name="adaln_mod",
    )(cs, w_mod, b_mod)


def _norm_mod(x, g, sh2, sc2, row0, n_ctx):
    ms = jnp.mean(x * x, axis=-1, keepdims=True)
    y = x * lax.rsqrt(ms + EPS) * g
    row = row0 + lax.broadcasted_iota(jnp.int32, (x.shape[0], 1), 0)
    is_ctx = row < n_ctx
    sc = jnp.where(is_ctx, sc2[0:1, :], sc2[1:2, :])
    sh = jnp.where(is_ctx, sh2[0:1, :], sh2[1:2, :])
    return y * (1.0 + sc) + sh


ROPE_NONE = 4


def _rope_store(acc, cos_ref, sin_ref, o_ref, shift):
    cosv, sinv = cos_ref[...], sin_ref[...]
    for cgrp in range(acc.shape[1] // LANES):
        a = acc[:, cgrp * LANES:(cgrp + 1) * LANES]
        lane = lax.broadcasted_iota(jnp.int32, a.shape, 1)
        in_second = (lane & (2 * shift - 1)) >= shift
        sw = jnp.where(in_second, pltpu.roll(a, shift, 1), pltpu.roll(a, LANES - shift, 1))
        o_ref[:, cgrp * LANES:(cgrp + 1) * LANES] = (a * cosv + sw * sinv).astype(o_ref.dtype)


NORM_ROWS = 256


def _norm_mod_to_scratch(x_ref, g_ref, sh_ref, sc_ref, h_scr, row0, n_ctx):
    def body(r, carry):
        off = pl.multiple_of(r * NORM_ROWS, NORM_ROWS)
        h = _norm_mod(x_ref[pl.ds(off, NORM_ROWS), :], g_ref[...], sh_ref[...], sc_ref[...], row0 + off, n_ctx)
        h_scr[pl.ds(off, NORM_ROWS), :] = h.astype(h_scr.dtype)
        return carry

    lax.fori_loop(0, x_ref.shape[0] // NORM_ROWS, body, 0)


def _nmm_rope_kernel(tt_ref, x_ref, g_ref, sh_ref, sc_ref, w_ref, cos_ref, sin_ref, o_ref, h_scr,
                     *, n_ctx, tm):
    i = pl.program_id(0)
    j = pl.program_id(1)

    @pl.when(j == 0)
    def _():
        _norm_mod_to_scratch(x_ref, g_ref, sh_ref, sc_ref, h_scr, i * tm, n_ctx)

    acc = jnp.dot(h_scr[...], w_ref[...], preferred_element_type=F32)
    t = tt_ref[j]

    @pl.when(t == ROPE_NONE)
    def _():
        o_ref[...] = acc.astype(o_ref.dtype)

    @pl.when(t < 2)
    def _():
        _rope_store(acc, cos_ref, sin_ref, o_ref, A_QK_DIM // 4)

    @pl.when(jnp.logical_and(t >= 2, t < ROPE_NONE))
    def _():
        _rope_store(acc, cos_ref, sin_ref, o_ref, B_HEAD_DIM // 4)


def _nmm_plain_kernel(x_ref, g_ref, sh_ref, sc_ref, w_ref, o_ref, h_scr, *, n_ctx, tm):
    i = pl.program_id(0)
    j = pl.program_id(1)

    @pl.when(j == 0)
    def _():
        _norm_mod_to_scratch(x_ref, g_ref, sh_ref, sc_ref, h_scr, i * tm, n_ctx)

    o_ref[...] = jnp.dot(h_scr[...], w_ref[...], preferred_element_type=F32).astype(o_ref.dtype)


def _norm_mod_matmul(x, g, mod, sh_col, sc_col, w, n_ctx, out_dtype, rope=None):
    t_rows, d = x.shape
    n = w.shape[1]
    tm = _pick(t_rows, (1280, 1024, 512, 256, 128))
    tn = 256
    grid = (t_rows // tm, n // tn)
    kern_kw = dict(n_ctx=n_ctx, tm=tm)
    scratch = [pltpu.VMEM((tm, d), BF16)]
    out_shape = jax.ShapeDtypeStruct((t_rows, n), out_dtype)
    if rope is None:
        return pl.pallas_call(
            functools.partial(_nmm_plain_kernel, **kern_kw),
            grid=grid,
            in_specs=[pl.BlockSpec((tm, d), lambda i, j: (i, 0)),
                      pl.BlockSpec((1, d), lambda i, j: (0, 0)),
                      pl.BlockSpec((SUBLANES, d), lambda i, j: (0, sh_col)),
                      pl.BlockSpec((SUBLANES, d), lambda i, j: (0, sc_col)),
                      pl.BlockSpec((d, tn), lambda i, j: (0, j))],
            out_specs=pl.BlockSpec((tm, tn), lambda i, j: (i, j)),
            out_shape=out_shape,
            scratch_shapes=scratch,
            compiler_params=_cparams("parallel", "arbitrary"),
            name="norm_mod_matmul",
        )(x, g, mod, mod, w)
    ttype, cos_t, sin_t = rope
    return pl.pallas_call(
        functools.partial(_nmm_rope_kernel, **kern_kw),
        grid_spec=pltpu.PrefetchScalarGridSpec(
            num_scalar_prefetch=1,
            grid=grid,
            in_specs=[pl.BlockSpec((tm, d), lambda i, j, tt: (i, 0)),
                      pl.BlockSpec((1, d), lambda i, j, tt: (0, 0)),
                      pl.BlockSpec((SUBLANES, d), lambda i, j, tt: (0, sh_col)),
                      pl.BlockSpec((SUBLANES, d), lambda i, j, tt: (0, sc_col)),
                      pl.BlockSpec((d, tn), lambda i, j, tt: (0, j)),
                      pl.BlockSpec((None, tm, LANES), lambda i, j, tt: (tt[j], i, 0)),
                      pl.BlockSpec((None, tm, LANES), lambda i, j, tt: (tt[j], i, 0))],
            out_specs=pl.BlockSpec((tm, tn), lambda i, j, tt: (i, j)),
            scratch_shapes=scratch),
        out_shape=out_shape,
        compiler_params=_cparams("parallel", "arbitrary"),
        name="norm_mod_qkv_rope",
    )(ttype, x, g, mod, mod, w, cos_t, sin_t)


def _rope_tables(n_ctx, seq):
    pos = jnp.arange(seq)
    rows = (pos // GRID_W).astype(F32)
    cols = (pos % GRID_W).astype(F32)

    def tab(dim):
        quarter = dim // 4
        freqs = ROPE_THETA ** (-jnp.arange(quarter, dtype=F32) / quarter)
        ar, ac = rows[:, None] * freqs, cols[:, None] * freqs
        cosv = jnp.concatenate([jnp.cos(ar), jnp.cos(ar), jnp.cos(ac), jnp.cos(ac)], axis=1)
        sinv = jnp.concatenate([-jnp.sin(ar), jnp.sin(ar), -jnp.sin(ac), jnp.sin(ac)], axis=1)
        reps = LANES // dim
        cosv, sinv = jnp.tile(cosv, (1, reps)), jnp.tile(sinv, (1, reps))
        cosv = jnp.concatenate([jnp.ones((n_ctx, LANES), F32), cosv], axis=0)
        sinv = jnp.concatenate([jnp.zeros((n_ctx, LANES), F32), sinv], axis=0)
        return cosv, sinv

    ca, sa = tab(A_QK_DIM)
    cb, sb = tab(B_HEAD_DIM)
    qa = (A_QK_DIM ** -0.5) * LOG2E
    qb = B_HEAD_DIM ** -0.5
    one, zero = jnp.ones_like(ca), jnp.zeros_like(ca)
    cos_t = jnp.stack([ca * qa, ca, cb * qb, cb, one])
    sin_t = jnp.stack([sa * qa, sa, sb * qb, sb, zero])
    return cos_t, sin_t


V_AUG = A_V_DIM + 16


def _flash_kernel(lam_ref, qt_ref, k_ref, vt_ref, g_ref, o_ref, s_a, s_b, acc,
                  *, tq, tk, n_ctx, t_rows, out_scale):
    i = pl.program_id(1)
    qt = qt_ref[...]
    row = lax.broadcasted_iota(jnp.int32, qt.shape, 0)
    zero = jnp.zeros_like(qt)
    qm = (jnp.where(row < A_QK_DIM, qt, zero), jnp.where(row >= A_QK_DIM, qt, zero))

    def start(tile, size):
        return tile * size if isinstance(tile, int) else pl.multiple_of(tile * size, size)

    def scores(tile, size, dst):
        kt = k_ref[pl.ds(start(tile, size), size), :]
        for m in range(2):
            dst[m, 0:size, :] = jnp.dot(kt, qm[m], preferred_element_type=F32)

    def soft_pv(tile, size, src, ms):
        vt = vt_ref[:, pl.ds(start(tile, size), size)]
        new_ms = []
        for m in range(2):
            s = src[m, 0:size, :]
            mx = jnp.maximum(ms[m], jnp.max(s, axis=0, keepdims=True))
            alpha = jnp.exp2(ms[m] - mx)
            p = jnp.exp2(s - mx).astype(BF16)
            acc[m] = alpha * acc[m] + jnp.dot(vt, p, preferred_element_type=F32)
            new_ms.append(mx)
        return tuple(new_ms)

    def finish():
        o1 = acc[0, 0:A_V_DIM, :] / acc[0, A_V_DIM:A_V_DIM + 1, :]
        o2 = acc[1, 0:A_V_DIM, :] / acc[1, A_V_DIM:A_V_DIM + 1, :]
        o = o1 - lam_ref[0] * o2
        var = jnp.mean(o * o, axis=0, keepdims=True)
        o = o * lax.rsqrt(var + EPS) * (g_ref[...] * out_scale)
        o_ref[...] = o.T.astype(o_ref.dtype)

    m0 = jnp.full((1, tq), NEG_INF, F32)
    n_q_ctx = n_ctx // tq

    @pl.when(i < n_q_ctx)
    def _():
        acc[...] = jnp.zeros_like(acc)
        scores(0, n_ctx, s_a)
        soft_pv(0, n_ctx, s_a, (m0, m0))
        finish()

    @pl.when(i >= n_q_ctx)
    def _():
        nk = t_rows // tk
        n_pairs = (nk - 1) // 2
        acc[...] = jnp.zeros_like(acc)
        scores(0, tk, s_a)

        def pair(pp, ms):
            scores(2 * pp + 1, tk, s_b)
            ms = soft_pv(2 * pp, tk, s_a, ms)
            scores(2 * pp + 2, tk, s_a)
            return soft_pv(2 * pp + 1, tk, s_b, ms)

        ms = lax.fori_loop(0, n_pairs, pair, (m0, m0))
        done = 2 * n_pairs
        if nk - done == 2:
            scores(done + 1, tk, s_b)
            ms = soft_pv(done, tk, s_a, ms)
            soft_pv(done + 1, tk, s_b, ms)
        else:
            soft_pv(done, tk, s_a, ms)
        finish()


def _diff_attention(qt, k, vt_aug, lam, subln_col, n_ctx, out_scale):
    t_rows = k.shape[0]
    tq = 256
    tk = _pick(t_rows, (1280, 1024, 512, 256))
    assert n_ctx <= tk
    return pl.pallas_call(
        functools.partial(_flash_kernel, tq=tq, tk=tk, n_ctx=n_ctx, t_rows=t_rows, out_scale=out_scale),
        grid=(A_HEADS, t_rows // tq),
        in_specs=[pl.BlockSpec(memory_space=pltpu.SMEM),
                  pl.BlockSpec((LANES, tq), lambda h, i: (h, i)),
                  pl.BlockSpec((t_rows, LANES), lambda h, i: (0, h)),
                  pl.BlockSpec((V_AUG, t_rows), lambda h, i: (h, 0)),
                  pl.BlockSpec((LANES, 1), lambda h, i: (0, 0))],
        out_specs=pl.BlockSpec((tq, LANES), lambda h, i: (i, h)),
        out_shape=jax.ShapeDtypeStruct((t_rows, A_HEADS * A_V_DIM), BF16),
        scratch_shapes=[pltpu.VMEM((2, tk, tq), F32), pltpu.VMEM((2, tk, tq), F32),
                        pltpu.VMEM((2, V_AUG, tq), F32)],
        compiler_params=_cparams("parallel", "arbitrary"),
        name="diff_attention",
    )(lam, qt, k, vt_aug, subln_col)


def _window_kernel(sink_ref, q_ref, kp_ref, ko_ref, kn_ref, vp_ref, vo_ref, vn_ref, kc_ref, vc_ref, o_ref,
                   *, nb, nb_ctx):
    g = pl.program_id(0)
    n = pl.program_id(1)
    qi = lax.broadcasted_iota(jnp.int32, (WINDOW, WINDOW), 0)
    kk = lax.broadcasted_iota(jnp.int32, (WINDOW, WINDOW), 1)
    own_ok = n >= nb_ctx
    prev_ok = n >= nb_ctx + 1
    next_ok = jnp.logical_and(own_ok, n <= nb - 2)
    m_prev = jnp.logical_and(kk >= qi, prev_ok)
    m_own = jnp.logical_and(kk >= 0, own_ok)
    m_next = jnp.logical_and(kk <= qi, next_ok)
    dn = (((1,), (1,)), ((), ()))
    for r in range(B_GROUP):
        q = q_ref[:, r * B_HEAD_DIM:(r + 1) * B_HEAD_DIM]
        s_p = jnp.where(m_prev, lax.dot_general(q, kp_ref[...], dn, preferred_element_type=F32), NEG_INF)
        s_o = jnp.where(m_own, lax.dot_general(q, ko_ref[...], dn, preferred_element_type=F32), NEG_INF)
        s_n = jnp.where(m_next, lax.dot_general(q, kn_ref[...], dn, preferred_element_type=F32), NEG_INF)
        s_c = lax.dot_general(q, kc_ref[...], dn, preferred_element_type=F32)
        sink = sink_ref[g * B_GROUP + r]
        mx = jnp.maximum(jnp.maximum(jnp.max(s_p, axis=1, keepdims=True), jnp.max(s_o, axis=1, keepdims=True)),
                         jnp.maximum(jnp.max(s_n, axis=1, keepdims=True), jnp.max(s_c, axis=1, keepdims=True)))
        mx = jnp.maximum(mx, sink)
        p_p, p_o, p_n, p_c = jnp.exp(s_p - mx), jnp.exp(s_o - mx), jnp.exp(s_n - mx), jnp.exp(s_c - mx)
        den = (jnp.sum(p_p, axis=1, keepdims=True) + jnp.sum(p_o, axis=1, keepdims=True)
               + jnp.sum(p_n, axis=1, keepdims=True) + jnp.sum(p_c, axis=1, keepdims=True)
               + jnp.exp(sink - mx))
        o = (jnp.dot(p_p.astype(BF16), vp_ref[...], preferred_element_type=F32)
             + jnp.dot(p_o.astype(BF16), vo_ref[...], preferred_element_type=F32)
             + jnp.dot(p_n.astype(BF16), vn_ref[...], preferred_element_type=F32)
             + jnp.dot(p_c.astype(BF16), vc_ref[...], preferred_element_type=F32))
        o_ref[:, r * B_HEAD_DIM:(r + 1) * B_HEAD_DIM] = (o / den).astype(o_ref.dtype)


def _window_attention(qkv, sink, n_ctx, col_q, col_k, col_v):
    t_rows = qkv.shape[0]
    nb = t_rows // WINDOW
    nb_ctx = n_ctx // WINDOW
    gq = B_GROUP * B_HEAD_DIM // LANES

    def kv_spec(col, shift):
        def imap(g, n):
            return (jnp.clip(n + shift, 0, nb - 1), col + g)
        return pl.BlockSpec((WINDOW, LANES), imap)

    return pl.pallas_call(
        functools.partial(_window_kernel, nb=nb, nb_ctx=nb_ctx),
        grid=(B_KV_HEADS, nb),
        in_specs=[pl.BlockSpec(memory_space=pltpu.SMEM),
                  pl.BlockSpec((WINDOW, B_GROUP * B_HEAD_DIM), lambda g, n: (n, col_q // gq + g)),
                  kv_spec(col_k, -1), kv_spec(col_k, 0), kv_spec(col_k, 1),
                  kv_spec(col_v, -1), kv_spec(col_v, 0), kv_spec(col_v, 1),
                  pl.BlockSpec((n_ctx, LANES), lambda g, n: (0, col_k + g)),
                  pl.BlockSpec((n_ctx, LANES), lambda g, n: (0, col_v + g))],
        out_specs=pl.BlockSpec((WINDOW, B_GROUP * B_HEAD_DIM), lambda g, n: (n, g)),
        out_shape=jax.ShapeDtypeStruct((t_rows, B_Q_HEADS * B_HEAD_DIM), BF16),
        compiler_params=_cparams("parallel", "arbitrary"),
        name="window_attention",
    )(sink, qkv, qkv, qkv, qkv, qkv, qkv, qkv, qkv, qkv)


def _post_mixer(y, x, g1_ref, g_ref, sh_ref, sc_ref, wr_ref, br_ref, xo_ref, h_ref, lg_ref, row0, n_ctx):
    row = row0 + lax.broadcasted_iota(jnp.int32, (x.shape[0], 1), 0)
    g1 = jnp.where(row < n_ctx, g1_ref[0:1, :], g1_ref[1:2, :])
    xn = x + g1 * y
    xo_ref[...] = xn
    h = _norm_mod(xn, g_ref[...], sh_ref[...], sc_ref[...], row0, n_ctx)
    h_ref[...] = h.astype(h_ref.dtype)
    lg_ref[...] = jnp.dot(h, wr_ref[...], preferred_element_type=F32, precision=HIGHEST) + br_ref[...]


def _attn_out_kernel(ya_ref, yb_ref, x_ref, woa_ref, wob_ref, g1_ref, g_ref, sh_ref, sc_ref, wr_ref, br_ref,
                     xo_ref, h_ref, lg_ref, *, n_ctx, tm):
    y = (jnp.dot(ya_ref[...], woa_ref[...], preferred_element_type=F32)
         + jnp.dot(yb_ref[...], wob_ref[...], preferred_element_type=F32))
    _post_mixer(y, x_ref[...], g1_ref, g_ref, sh_ref, sc_ref, wr_ref, br_ref, xo_ref, h_ref, lg_ref,
                pl.program_id(0) * tm, n_ctx)


def _glu_out_kernel(y_ref, x_ref, wv_ref, wg_ref, g1_ref, g_ref, sh_ref, sc_ref, wr_ref, br_ref,
                    xo_ref, h_ref, lg_ref, *, n_ctx, tm):
    a = jax.nn.gelu(y_ref[...], approximate=True).astype(BF16)
    val = jnp.dot(a, wv_ref[...], preferred_element_type=F32)
    gate = jnp.dot(a, wg_ref[...], preferred_element_type=F32)
    _post_mixer(val * jax.nn.sigmoid(gate), x_ref[...], g1_ref, g_ref, sh_ref, sc_ref, wr_ref, br_ref,
                xo_ref, h_ref, lg_ref, pl.program_id(0) * tm, n_ctx)


def _mixer_out(kernel_fn, acts, x, weights, mod, cols, g2row, wr, br, n_ctx, name):
    t_rows, d = x.shape
    tm = 256
    row = lambda i: (i, 0)
    const = lambda i: (0, 0)
    in_specs = ([pl.BlockSpec((tm, a.shape[1]), row) for a in acts]
                + [pl.BlockSpec((tm, d), row)]
                + [pl.BlockSpec(w.shape, const) for w in weights]
                + [pl.BlockSpec((SUBLANES, d), lambda i, c=c: (0, c)) for c in cols[:1]]
                + [pl.BlockSpec((1, d), const)]
                + [pl.BlockSpec((SUBLANES, d), lambda i, c=c: (0, c)) for c in cols[1:]]
                + [pl.BlockSpec(wr.shape, const), pl.BlockSpec(br.shape, const)])
    return pl.pallas_call(
        functools.partial(kernel_fn, n_ctx=n_ctx, tm=tm),
        grid=(t_rows // tm,),
        in_specs=in_specs,
        out_specs=[pl.BlockSpec((tm, d), row), pl.BlockSpec((tm, d), row), pl.BlockSpec((tm, LANES), row)],
        out_shape=[jax.ShapeDtypeStruct((t_rows, d), F32), jax.ShapeDtypeStruct((t_rows, d), BF16),
                   jax.ShapeDtypeStruct((t_rows, LANES), F32)],
        compiler_params=_cparams("parallel"),
        name=name,
    )(*acts, x, *weights, mod, g2row, mod, mod, wr, br)


def _router_kernel(lg_ref, idx_ref, gate_ref, cnt_ref, *, n_exp):
    @pl.when(pl.program_id(0) == 0)
    def _():
        cnt_ref[...] = jnp.zeros_like(cnt_ref)

    lt = lg_ref[...].T[0:n_exp, :]
    eid = lax.broadcasted_iota(jnp.int32, lt.shape, 0).astype(F32)
    vals, idxs = [], []
    hist = jnp.zeros(lt.shape, F32)
    for _ in range(TOP_K):
        mv = jnp.max(lt, axis=0, keepdims=True)
        ix = jnp.min(jnp.where(lt == mv, eid, float(n_exp)), axis=0, keepdims=True)
        sel = eid == ix
        hist = hist + sel.astype(F32)
        lt = jnp.where(sel, -jnp.inf, lt)
        vals.append(mv)
        idxs.append(ix)
    es = [jnp.exp(v - vals[0]) for v in vals]
    den = es[0] + es[1] + es[2] + es[3]
    pad_f = jnp.zeros((SUBLANES - TOP_K, lt.shape[1]), F32)
    idx_ref[...] = jnp.concatenate(idxs + [pad_f], axis=0).astype(jnp.int32)
    gate_ref[...] = jnp.concatenate([e / den for e in es] + [pad_f], axis=0)
    cnt_ref[...] += jnp.sum(hist, axis=1, keepdims=True)


def _router(logits, n_exp):
    t_rows = logits.shape[0]
    tm = 256
    return pl.pallas_call(
        functools.partial(_router_kernel, n_exp=n_exp),
        grid=(t_rows // tm,),
        in_specs=[pl.BlockSpec((tm, LANES), lambda i: (i, 0))],
        out_specs=[pl.BlockSpec((SUBLANES, tm), lambda i: (0, i)),
                   pl.BlockSpec((SUBLANES, tm), lambda i: (0, i)),
                   pl.BlockSpec((n_exp, 1), lambda i: (0, 0))],
        out_shape=[jax.ShapeDtypeStruct((SUBLANES, t_rows), jnp.int32),
                   jax.ShapeDtypeStruct((SUBLANES, t_rows), F32),
                   jax.ShapeDtypeStruct((n_exp, 1), F32)],
        compiler_params=_cparams("arbitrary"),
        name="router_topk",
    )(logits)


def _dest_kernel(idx_ref, start_ref, dest_ref, carry, *, n_exp, tm):
    @pl.when(pl.program_id(0) == 0)
    def _():
        carry[...] = start_ref[...]

    eid = lax.broadcasted_iota(jnp.int32, (n_exp, tm), 0)
    idx = idx_ref[...]
    sels = [eid == idx[k:k + 1, :] for k in range(TOP_K)]
    total = sels[0].astype(F32) + sels[1].astype(F32) + sels[2].astype(F32) + sels[3].astype(F32)
    rr = lax.broadcasted_iota(jnp.int32, (tm, tm), 0)
    cc = lax.broadcasted_iota(jnp.int32, (tm, tm), 1)
    upper = jnp.where(rr < cc, 1.0, 0.0).astype(BF16)
    before = jnp.dot(total.astype(BF16), upper, preferred_element_type=F32) + carry[...]
    rows = [jnp.sum(jnp.where(sels[k], before, 0.0), axis=0, keepdims=True) for k in range(TOP_K)]
    pad = jnp.zeros((SUBLANES - TOP_K, tm), F32)
    dest_ref[...] = jnp.concatenate(rows + [pad], axis=0).astype(jnp.int32)
    carry[...] += jnp.sum(total, axis=1, keepdims=True)


def _dest_rows(idx, starts, n_exp):
    t_rows = idx.shape[1]
    tm = 256
    return pl.pallas_call(
        functools.partial(_dest_kernel, n_exp=n_exp, tm=tm),
        grid=(t_rows // tm,),
        in_specs=[pl.BlockSpec((SUBLANES, tm), lambda i: (0, i)),
                  pl.BlockSpec((n_exp, 1), lambda i: (0, 0))],
        out_specs=pl.BlockSpec((SUBLANES, tm), lambda i: (0, i)),
        out_shape=jax.ShapeDtypeStruct((SUBLANES, t_rows), jnp.int32),
        scratch_shapes=[pltpu.VMEM((n_exp, 1), F32)],
        compiler_params=_cparams("arbitrary"),
        name="moe_dest_rows",
    )(idx, starts)


def _expert_gu_kernel(be_ref, bv_ref, bn_ref, x_ref, wg_ref, wu_ref, bg_ref, bu_ref, o_ref, wg_s, wu_s, *, blk):
    b = pl.program_id(1)
    nvalid = bv_ref[b]

    @pl.when(bn_ref[b] == 1)
    def _():
        wg_s[...] = wg_ref[...].astype(BF16)
        wu_s[...] = wu_ref[...].astype(BF16)

    @pl.when(nvalid > 0)
    def _():
        rows = lax.broadcasted_iota(jnp.int32, (blk, 1), 0)
        x = x_ref[...]
        x = jnp.where(rows < nvalid, x, jnp.zeros_like(x))
        gate = jnp.dot(x, wg_s[...], preferred_element_type=F32) + bg_ref[...]
        up = jnp.dot(x, wu_s[...], preferred_element_type=F32) + bu_ref[...]
        gate = jnp.minimum(gate, SWIGLU_LIMIT)
        up = jnp.clip(up, -SWIGLU_LIMIT, SWIGLU_LIMIT)
        act = (up + 1.0) * (gate * jax.nn.sigmoid(SWIGLU_ALPHA * gate))
        o_ref[...] = act.astype(o_ref.dtype)

    @pl.when(nvalid == 0)
    def _():
        o_ref[...] = jnp.zeros_like(o_ref)


def _expert_dn_kernel(be_ref, bv_ref, bn_ref, a_ref, wd_ref, bd_ref, o_ref, wd_s):
    b = pl.program_id(1)

    @pl.when(bn_ref[b] == 1)
    def _():
        wd_s[...] = wd_ref[...].astype(BF16)

    @pl.when(bv_ref[b] > 0)
    def _():
        y = jnp.dot(a_ref[...], wd_s[...], preferred_element_type=F32) + bd_ref[...]
        o_ref[...] = y.astype(o_ref.dtype)

    @pl.when(bv_ref[b] == 0)
    def _():
        o_ref[...] = jnp.zeros_like(o_ref)


def _expert_matmul(xs, blk_e, blk_valid, blk_new, w_gu, b_gu, w_dn, b_dn, layer):
    n_rows, d = xs.shape
    f_dim = w_dn.shape[2]
    blk = EXPERT_BLOCK
    tf = 512
    nf = f_dim // tf
    tn = 512
    n_blocks = n_rows // blk
    act = pl.pallas_call(
        functools.partial(_expert_gu_kernel, blk=blk),
        grid_spec=pltpu.PrefetchScalarGridSpec(
            num_scalar_prefetch=3,
            grid=(nf, n_blocks),
            in_specs=[pl.BlockSpec((blk, d), lambda f, b, be, bv, bn: (b, 0)),
                      pl.BlockSpec((None, None, d, tf), lambda f, b, be, bv, bn: (layer, be[b], 0, f)),
                      pl.BlockSpec((None, None, d, tf), lambda f, b, be, bv, bn: (layer, be[b], 0, nf + f)),
                      pl.BlockSpec((None, None, 1, tf), lambda f, b, be, bv, bn: (layer, be[b], 0, f)),
                      pl.BlockSpec((None, None, 1, tf), lambda f, b, be, bv, bn: (layer, be[b], 0, nf + f))],
            out_specs=pl.BlockSpec((blk, tf), lambda f, b, be, bv, bn: (b, f)),
            scratch_shapes=[pltpu.VMEM((d, tf), BF16), pltpu.VMEM((d, tf), BF16)]),
        out_shape=jax.ShapeDtypeStruct((n_rows, f_dim), BF16),
        compiler_params=_cparams("arbitrary", "arbitrary"),
        name="expert_gate_up",
    )(blk_e, blk_valid, blk_new, xs, w_gu, w_gu, b_gu, b_gu)
    return pl.pallas_call(
        _expert_dn_kernel,
        grid_spec=pltpu.PrefetchScalarGridSpec(
            num_scalar_prefetch=3,
            grid=(d // tn, n_blocks),
            in_specs=[pl.BlockSpec((blk, f_dim), lambda n, b, be, bv, bn: (b, 0)),
                      pl.BlockSpec((None, None, f_dim, tn), lambda n, b, be, bv, bn: (layer, be[b], 0, n)),
                      pl.BlockSpec((None, None, 1, tn), lambda n, b, be, bv, bn: (layer, be[b], 0, n))],
            out_specs=pl.BlockSpec((blk, tn), lambda n, b, be, bv, bn: (b, n)),
            scratch_shapes=[pltpu.VMEM((f_dim, tn), BF16)]),
        out_shape=jax.ShapeDtypeStruct((n_rows, d), BF16),
        compiler_params=_cparams("arbitrary", "arbitrary"),
        name="expert_down",
    )(blk_e, blk_valid, blk_new, act, w_dn, b_dn)


def _combine_kernel(y_ref, gate_ref, x_ref, g2_ref, fg_ref, o_ref, *, n_ctx, tm, final):
    gates = gate_ref[...]
    f = y_ref[0].astype(F32) * gates[:, 0:1]
    for k in range(1, TOP_K):
        f = f + y_ref[k].astype(F32) * gates[:, k:k + 1]
    row = pl.program_id(0) * tm + lax.broadcasted_iota(jnp.int32, (tm, 1), 0)
    g2 = jnp.where(row < n_ctx, g2_ref[0:1, :], g2_ref[1:2, :])
    xn = x_ref[...] + g2 * f
    if final:
        ms = jnp.mean(xn * xn, axis=-1, keepdims=True)
        xn = xn * lax.rsqrt(ms + EPS) * fg_ref[...]
    o_ref[...] = xn


def _combine(yk, gates_t, x, mod, g2_col, final_g, n_ctx, final):
    t_rows, d = x.shape
    tm = 256
    return pl.pallas_call(
        functools.partial(_combine_kernel, n_ctx=n_ctx, tm=tm, final=final),
        grid=(t_rows // tm,),
        in_specs=[pl.BlockSpec((TOP_K, tm, d), lambda i: (0, i, 0)),
                  pl.BlockSpec((tm, SUBLANES), lambda i: (i, 0)),
                  pl.BlockSpec((tm, d), lambda i: (i, 0)),
                  pl.BlockSpec((SUBLANES, d), lambda i: (0, g2_col)),
                  pl.BlockSpec((1, d), lambda i: (0, 0))],
        out_specs=pl.BlockSpec((tm, d), lambda i: (i, 0)),
        out_shape=jax.ShapeDtypeStruct((t_rows, d), F32),
        compiler_params=_cparams("parallel"),
        name="moe_combine",
    )(yk, gates_t, x, mod, final_g)


def _moe(h, logits, x, mod, g2_col, w_gu, b_gu, w_dn, b_dn, layer, n_ctx, final_g, final):
    t_rows, d = h.shape
    n_exp = w_gu.shape[1]
    idx, gates, counts = _router(logits, n_exp)
    counts = counts[:, 0].astype(jnp.int32)
    blk = EXPERT_BLOCK
    padded = (counts + blk - 1) // blk * blk
    pends = jnp.cumsum(padded)
    pstarts = pends - padded
    dest = _dest_rows(idx, pstarts.astype(F32)[:, None], n_exp)[:TOP_K]
    n_blocks = -(-(t_rows * TOP_K) // blk) + n_exp
    n_rows = n_blocks * blk
    blk_start = jnp.arange(n_blocks, dtype=jnp.int32) * blk
    blk_e = jnp.minimum(jnp.searchsorted(pends, blk_start, side='right'), n_exp - 1).astype(jnp.int32)
    blk_valid = jnp.clip(pstarts[blk_e] + counts[blk_e] - blk_start, 0, blk).astype(jnp.int32)
    blk_valid = jnp.where(blk_start < pends[-1], blk_valid, 0)
    blk_new = jnp.concatenate([jnp.ones((1,), jnp.int32), (blk_e[1:] != blk_e[:-1]).astype(jnp.int32)])
    tok =jnp.broadcast_to(jnp.arange(t_rows, dtype=jnp.int32)[None, :], dest.shape)
    row_tok = jnp.zeros((n_rows,), jnp.int32).at[dest.reshape(-1)].set(tok.reshape(-1))
    xs = jnp.take(h, row_tok, axis=0)
    ys = _expert_matmul(xs, blk_e, blk_valid, blk_new, w_gu, b_gu, w_dn, b_dn, layer)
    yk = jnp.take(ys, dest.reshape(-1), axis=0).reshape(TOP_K, t_rows, d)
    return _combine(yk, gates.T, x, mod, g2_col, final_g, n_ctx, final)


def _s5_tables(a_re, a_im, log_step, b_re, b_im, c_re, c_im, d_skip):
    tc = S5_TC
    n_grp, n_st = a_re.shape[1], a_re.shape[2]
    lr = jnp.minimum(a_re.astype(F32), -1e-4)
    li = a_im.astype(F32)
    dt = jnp.exp(log_step.astype(F32))[..., None]
    dd = jnp.arange(tc + 1, dtype=F32)[:, None, None, None]
    mag = jnp.exp(lr * dt * dd)
    pr, pi = mag * jnp.cos(li * dt * dd), mag * jnp.sin(li * dt * dd)
    ar, ai = pr[1], pi[1]
    den = lr * lr + li * li
    nr = ar - 1.0
    zr = (nr * lr + ai * li) / den
    zi = (ai * lr - nr * li) / den
    br, bi = b_re.astype(F32), b_im.astype(F32)
    bbr = zr[..., None] * br - zi[..., None] * bi
    bbi = zr[..., None] * bi + zi[..., None] * br
    abr = pr[:tc, ..., None] * bbr - pi[:tc, ..., None] * bbi
    abi = pr[:tc, ..., None] * bbi + pi[:tc, ..., None] * bbr
    cr, ci = c_re.astype(F32), c_im.astype(F32)
    kern = (jnp.einsum('xgip,dxgpj->dxgij', cr, abr, precision=HIGHEST)
            - jnp.einsum('xgip,dxgpj->dxgij', ci, abi, precision=HIGHEST))
    s_i = jnp.arange(tc)[:, None]
    t_i = jnp.arange(tc)[None, :]
    lag_f = t_i - s_i
    lag_b = s_i - t_i
    m_f = jnp.where((lag_f >= 0)[..., None, None, None], kern[jnp.clip(lag_f, 0), 0], 0.0)
    m_b = jnp.where((lag_b >= 0)[..., None, None, None], kern[jnp.clip(lag_b, 0), 1], 0.0)
    m_tot = (m_f + m_b).transpose(2, 0, 4, 1, 3)
    eye_t = jnp.eye(tc, dtype=F32)
    eye_i = jnp.eye(S5_GROUP, dtype=F32)
    dsk = d_skip.astype(F32).reshape(n_grp, S5_GROUP)
    m_tot = m_tot + (eye_t[None, :, None, :, None] * eye_i[None, None, :, None, :]
                     * dsk[:, None, :, None, None])
    m_tot = m_tot.reshape(n_grp, tc * S5_GROUP, tc * S5_GROUP)
    pw_f = tc - 1 - jnp.arange(tc)
    pw_b = jnp.arange(tc)

    def b_cols(part, pw, x):
        return part[pw, x].transpose(1, 0, 3, 2).reshape(n_grp, tc * S5_GROUP, n_st)

    b_mat = jnp.concatenate([b_cols(abr, pw_f, 0), b_cols(abi, pw_f, 0),
                             b_cols(abr, pw_b, 1), b_cols(abi, pw_b, 1)], axis=-1)
    pcf = 1 + jnp.arange(tc)
    pcb = tc - jnp.arange(tc)

    def c_rows(pw, x):
        prx, pix = pr[pw, x], pi[pw, x]
        re_c = cr[x][None] * prx[:, :, None, :] - ci[x][None] * pix[:, :, None, :]
        im_c = -(cr[x][None] * pix[:, :, None, :] + ci[x][None] * prx[:, :, None, :])
        to_rows = lambda z: z.transpose(1, 3, 0, 2).reshape(n_grp, n_st, tc * S5_GROUP)
        return to_rows(re_c), to_rows(im_c)

    c_mat = jnp.concatenate(list(c_rows(pcf, 0)) + list(c_rows(pcb, 1)), axis=1)
    prt, pit = pr[tc], pi[tc]
    a1 = jnp.concatenate([prt[0], prt[0], prt[1], prt[1]], axis=-1)
    a2 = jnp.concatenate([-pit[0], pit[0], -pit[1], pit[1]], axis=-1)
    return m_tot, b_mat, c_mat, a1, a2


def _s5_in_kernel(u_ref, b_ref, o_ref, *, gb):
    for g in range(gb):
        o_ref[g] = jnp.dot(u_ref[g], b_ref[g], preferred_element_type=F32, precision=HIGHEST)


def _s5_out_kernel(u_ref, z_ref, m_ref, c_ref, o_ref, *, gb):
    for g in range(gb):
        o_ref[g] = (jnp.dot(u_ref[g], m_ref[g], preferred_element_type=F32, precision=HIGHEST)
                    + jnp.dot(z_ref[g], c_ref[g], preferred_element_type=F32, precision=HIGHEST))


def _s5_scan_kernel(h_ref, a1_ref, a2_ref, z_ref, *, n_chunks, n_ctx_chunks):
    half = 2 * S5_STATE
    a1 = a1_ref[...]
    a2 = a2_ref[...]
    a1f, a1b = a1[:, :half], a1[:, half:]
    a2f, a2b = a2[:, :half], a2[:, half:]
    zero = jnp.zeros((a1.shape[0], half), F32)

    def swap(v):
        return pltpu.roll(v, S5_STATE, 1)

    def step(t, carry):
        rf, rfs, rb, rbs = carry
        cb = jnp.where(t < n_ctx_chunks, n_ctx_chunks - 1 - t, n_chunks - 1 - (t - n_ctx_chunks))
        z_ref[t, :, 0:half] = rf
        z_ref[cb, :, half:2 * half] = rb
        hf = h_ref[t, :, 0:half]
        hb = h_ref[cb, :, half:2 * half]
        nrf = a1f * rf + a2f * rfs + hf
        nrfs = a1f * rfs - a2f * rf + swap(hf)
        nrb = a1b * rb + a2b * rbs + hb
        nrbs = a1b * rbs - a2b * rb + swap(hb)
        return nrf, nrfs, nrb, nrbs

    lax.fori_loop(0, n_chunks, step, (zero, zero, zero, zero), unroll=8 if n_chunks % 8 == 0 else 1)


def _s5_mixer_scan(u, tables, n_ctx):
    m_tot, b_mat, c_mat, a1, a2 = tables
    t_rows, width = u.shape
    tc = S5_TC
    n_grp = width // S5_GROUP
    n_chunks = t_rows // tc
    kdim = tc * S5_GROUP
    sdim = 4 * S5_STATE
    gb = 4
    gs = SUBLANES
    ug = u.reshape(n_chunks, tc, n_grp, S5_GROUP).transpose(2, 0, 1, 3).reshape(n_grp, n_chunks, kdim)
    blk3 = lambda a, b: pl.BlockSpec((gb, a, b), lambda g: (g, 0, 0))
    hin = pl.pallas_call(
        functools.partial(_s5_in_kernel, gb=gb),
        grid=(n_grp // gb,),
        in_specs=[blk3(n_chunks, kdim), blk3(kdim, sdim)],
        out_specs=blk3(n_chunks, sdim),
        out_shape=jax.ShapeDtypeStruct((n_grp, n_chunks, sdim), F32),
        compiler_params=_cparams("parallel"),
        name="s5_chunk_inputs",
    )(ug, b_mat)
    hin_t = hin.transpose(1, 0, 2)
    z_t = pl.pallas_call(
        functools.partial(_s5_scan_kernel, n_chunks=n_chunks, n_ctx_chunks=n_ctx // tc),
        grid=(n_grp // gs,),
        in_specs=[pl.BlockSpec((n_chunks, gs, sdim), lambda g: (0, g, 0)),
                  pl.BlockSpec((gs, sdim), lambda g: (g, 0)),
                  pl.BlockSpec((gs, sdim), lambda g: (g, 0))],
        out_specs=pl.BlockSpec((n_chunks, gs, sdim), lambda g: (0, g, 0)),
        out_shape=jax.ShapeDtypeStruct((n_chunks, n_grp, sdim), F32),
        compiler_params=_cparams("parallel"),
        name="s5_chunk_scan",
    )(hin_t, a1, a2)
    z = z_t.transpose(1, 0, 2)
    yg = pl.pallas_call(
        functools.partial(_s5_out_kernel, gb=gb),
        grid=(n_grp // gb,),
        in_specs=[blk3(n_chunks, kdim), blk3(n_chunks, sdim), blk3(kdim, kdim), blk3(sdim, kdim)],
        out_specs=blk3(n_chunks, kdim),
        out_shape=jax.ShapeDtypeStruct((n_grp, n_chunks, kdim), F32),
        compiler_params=_cparams("parallel"),
        name="s5_chunk_outputs",
    )(ug, z, m_tot, c_mat)
    return yg.reshape(n_grp, n_chunks, tc, S5_GROUP).transpose(1, 2, 0, 3).reshape(t_rows, width)


def _lambda_init(layer):
    return 0.8 - 0.6 * math.exp(-0.3 * layer)


def kernel(x, c, ctx, c_ctx, w_mod, b_mod, norm1_g, norm2_g, final_g, attn_w_qkv, attn_w_o, lambda_q1, lambda_k1, lambda_q2, lambda_k2, subln_g, sink_logit, s5_w_in, s5_a_re, s5_a_im, s5_log_step, s5_b_re, s5_b_im, s5_c_re, s5_c_im, s5_d, s5_w_glu, router_w, router_b, expert_w_gu, expert_b_gu, expert_w_down, expert_b_down):
    bsz, seq, d = x.shape
    assert bsz == 1, "single-sequence block"
    n_ctx = ctx.shape[1]
    depth = w_mod.shape[0]
    n_exp = router_w.shape[2]
    assert n_ctx % 256 == 0 and seq % 256 == 0

    xj = jnp.concatenate([ctx[0], x[0]], axis=0)
    cs = jnp.zeros((SUBLANES, d), F32).at[0].set(c_ctx).at[1].set(c[0])
    b_mod3 = b_mod[:, None, :]
    b_gu4 = expert_b_gu[:, :, None, :]
    b_dn4 = expert_b_down[:, :, None, :]
    wr_pad = jnp.pad(router_w, ((0, 0), (0, 0), (0, LANES - n_exp)))
    br_pad = jnp.pad(router_b, ((0, 0), (0, LANES - n_exp)), constant_values=NEG_INF)[:, None, :]
    final_row = final_g[None, :]

    for i in range(depth):
        last = i == depth - 1
        j = i // 2
        mod = _adaln_mod(cs, w_mod, b_mod3, i)
        if i % 2 == 0:
            sizes = (A_HEADS * 2 * A_QK_DIM, A_HEADS * 2 * A_QK_DIM, A_HEADS * A_V_DIM,
                     B_Q_HEADS * B_HEAD_DIM, B_KV_HEADS * B_HEAD_DIM, B_KV_HEADS * B_HEAD_DIM)
            offs = np.concatenate([[0], np.cumsum(sizes)])
            types = [0, 1, ROPE_NONE, 2, 3, ROPE_NONE]
            ttype = jnp.asarray(np.concatenate([np.full(s // 256, t) for s, t in zip(sizes, types)]), jnp.int32)
            cos_t, sin_t = _rope_tables(n_ctx, seq)
            qkv = _norm_mod_matmul(xj, norm1_g[i][None, :], mod, 0, 1, attn_w_qkv[j].astype(BF16), n_ctx, BF16,
                                   rope=(ttype, cos_t, sin_t))
            qt = qkv[:, offs[0]:offs[1]].T
            ka = qkv[:, offs[1]:offs[2]]
            vt = qkv[:, offs[2]:offs[3]].T.reshape(A_HEADS, A_V_DIM, -1)
            vt = jnp.concatenate([vt, jnp.ones((A_HEADS, V_AUG - A_V_DIM, vt.shape[2]), BF16)], axis=1)
            vt = vt.reshape(A_HEADS * V_AUG, -1)
            f32 = F32
            li = _lambda_init(i)
            lam = (jnp.exp(jnp.sum(lambda_q1[j].astype(f32) * lambda_k1[j].astype(f32)))
                   - jnp.exp(jnp.sum(lambda_q2[j].astype(f32) * lambda_k2[j].astype(f32))) + li)
            ya = _diff_attention(qt, ka, vt, lam.reshape(1), subln_g[j][:, None], n_ctx, 1.0 - li)
            yb = _window_attention(qkv, sink_logit[j], n_ctx, int(offs[3]) // LANES, int(offs[4]) // LANES,
                                   int(offs[5]) // LANES)
            w_o = attn_w_o[j].astype(BF16)
            na = A_HEADS * A_V_DIM
            xj, h2, logits = _mixer_out(_attn_out_kernel, [ya, yb], xj, [w_o[:na], w_o[na:]], mod, (2, 3, 4),
                                        norm2_g[i][None, :], wr_pad[i], br_pad[i], n_ctx, "attn_out_router")
        else:
            u = _norm_mod_matmul(xj, norm1_g[i][None, :], mod, 0, 1, s5_w_in[j].astype(BF16), n_ctx, F32)
            tables = _s5_tables(s5_a_re[j], s5_a_im[j], s5_log_step[j], s5_b_re[j], s5_b_im[j],
                                s5_c_re[j], s5_c_im[j], s5_d[j])
            y = _s5_mixer_scan(u, tables, n_ctx)
            w_glu = s5_w_glu[j].astype(BF16)
            xj, h2, logits = _mixer_out(_glu_out_kernel, [y], xj, [w_glu[:, :d], w_glu[:, d:]], mod, (2, 3, 4),
                                        norm2_g[i][None, :], wr_pad[i], br_pad[i], n_ctx, "glu_out_router")
        xj = _moe(h2, logits, xj, mod, 5, expert_w_gu, b_gu4, expert_w_down, b_dn4, i, n_ctx, final_row, last)
    return xj[n_ctx:][None]
```

```python
import functools
import math

import jax
import jax.numpy as jnp
import numpy as np
from jax import lax
from jax.experimental import pallas as pl
from jax.experimental.pallas import tpu as pltpu

F32 = jnp.float32
BF16 = jnp.bfloat16
HIGHEST = lax.Precision.HIGHEST

V7X_VMEM_BYTES = 64 * 1024 * 1024
VMEM_LIMIT = V7X_VMEM_BYTES - 8 * 1024 * 1024
LANES = 128
SUBLANES = 8

GRID_W = 64
N_MOD = 6
EPS = 1e-6
NEG_INF = -1e30
ROPE_THETA = 10000.0
A_HEADS = 8
A_QK_DIM = 64
A_V_DIM = 128
B_Q_HEADS = 8
B_KV_HEADS = 2
B_GROUP = 4
B_HEAD_DIM = 128
WINDOW = 128
S5_GROUP = 16
S5_STATE = 64
S5_TC = 16
TOP_K = 4
SWIGLU_LIMIT = 7.0
SWIGLU_ALPHA = 1.702
EXPERT_BLOCK = 512
LOG2E = 1.4426950408889634


def _pick(n, cands):
    for c in cands:
        if n % c == 0:
            return c
    raise ValueError(f"no tile for {n} in {cands}")


def _cparams(*sem):
    return pltpu.CompilerParams(dimension_semantics=sem, vmem_limit_bytes=VMEM_LIMIT)


def _mod_kernel(c_ref, w_ref, b_ref, o_ref):
    cv = c_ref[...]
    s = cv * jax.nn.sigmoid(cv)
    o_ref[...] = jnp.dot(s, w_ref[...], preferred_element_type=F32, precision=HIGHEST) + b_ref[...]


def _adaln_mod(cs, w_mod, b_mod, layer):
    d, n = w_mod.shape[1], w_mod.shape[2]
    tn = _pick(n, (1024, 512, 256, 128))
    return pl.pallas_call(
        _mod_kernel,
        grid=(n // tn,),
        in_specs=[pl.BlockSpec((SUBLANES, d), lambda j: (0, 0)),
                  pl.BlockSpec((None, d, tn), lambda j: (layer, 0, j)),
                  pl.BlockSpec((None, 1, tn), lambda j: (layer, 0, j))],
        out_specs=pl.BlockSpec((SUBLANES, tn), lambda j: (0, j)),
        out_shape=jax.ShapeDtypeStruct((SUBLANES, n), F32),
        compiler_params=_cparams("parallel"),
        name="adaln_mod",
    )(cs, w_mod, b_mod)


def _norm_mod(x, g, sh2, sc2, row0, n_ctx):
    ms = jnp.mean(x * x, axis=-1, keepdims=True)
    y = x * lax.rsqrt(ms + EPS) * g
    row = row0 + lax.broadcasted_iota(jnp.int32, (x.shape[0], 1), 0)
    is_ctx = row < n_ctx
    sc = jnp.where(is_ctx, sc2[0:1, :], sc2[1:2, :])
    sh = jnp.where(is_ctx, sh2[0:1, :], sh2[1:2, :])
    return y * (1.0 + sc) + sh


ROPE_NONE = 4


def _rope_store(acc, cos_ref, sin_ref, o_ref, shift):
    cosv, sinv = cos_ref[...], sin_ref[...]
    for cgrp in range(acc.shape[1] // LANES):
        a = acc[:, cgrp * LANES:(cgrp + 1) * LANES]
        lane = lax.broadcasted_iota(jnp.int32, a.shape, 1)
        in_second = (lane & (2 * shift - 1)) >= shift
        sw = jnp.where(in_second, pltpu.roll(a, shift, 1), pltpu.roll(a, LANES - shift, 1))
        o_ref[:, cgrp * LANES:(cgrp + 1) * LANES] = (a * cosv + sw * sinv).astype(o_ref.dtype)


NORM_ROWS = 256


def _norm_mod_to_scratch(x_ref, g_ref, sh_ref, sc_ref, h_scr, row0, n_ctx):
    def body(r, carry):
        off = pl.multiple_of(r * NORM_ROWS, NORM_ROWS)
        h = _norm_mod(x_ref[pl.ds(off, NORM_ROWS), :], g_ref[...], sh_ref[...], sc_ref[...], row0 + off, n_ctx)
        h_scr[pl.ds(off, NORM_ROWS), :] = h.astype(h_scr.dtype)
        return carry

    lax.fori_loop(0, x_ref.shape[0] // NORM_ROWS, body, 0)


def _nmm_rope_kernel(tt_ref, x_ref, g_ref, sh_ref, sc_ref, w_ref, cos_ref, sin_ref, o_ref, h_scr,
                     *, n_ctx, tm):
    i = pl.program_id(0)
    j = pl.program_id(1)

    @pl.when(j == 0)
    def _():
        _norm_mod_to_scratch(x_ref, g_ref, sh_ref, sc_ref, h_scr, i * tm, n_ctx)

    acc = jnp.dot(h_scr[...], w_ref[...], preferred_element_type=F32)
    t = tt_ref[j]

    @pl.when(t == ROPE_NONE)
    def _():
        o_ref[...] = acc.astype(o_ref.dtype)

    @pl.when(t < 2)
    def _():
        _rope_store(acc, cos_ref, sin_ref, o_ref, A_QK_DIM // 4)

    @pl.when(jnp.logical_and(t >= 2, t < ROPE_NONE))
    def _():
        _rope_store(acc, cos_ref, sin_ref, o_ref, B_HEAD_DIM // 4)


def _nmm_plain_kernel(x_ref, g_ref, sh_ref, sc_ref, w_ref, o_ref, h_scr, *, n_ctx, tm):
    i = pl.program_id(0)
    j = pl.program_id(1)

    @pl.when(j == 0)
    def _():
        _norm_mod_to_scratch(x_ref, g_ref, sh_ref, sc_ref, h_scr, i * tm, n_ctx)

    o_ref[...] = jnp.dot(h_scr[...], w_ref[...], preferred_element_type=F32).astype(o_ref.dtype)


def _norm_mod_matmul(x, g, mod, sh_col, sc_col, w, n_ctx, out_dtype, rope=None):
    t_rows, d = x.shape
    n = w.shape[1]
    tm = _pick(t_rows, (1280, 1024, 512, 256, 128))
    tn = 256
    grid = (t_rows // tm, n // tn)
    kern_kw = dict(n_ctx=n_ctx, tm=tm)
    scratch = [pltpu.VMEM((tm, d), BF16)]
    out_shape = jax.ShapeDtypeStruct((t_rows, n), out_dtype)
    if rope is None:
        return pl.pallas_call(
            functools.partial(_nmm_plain_kernel, **kern_kw),
            grid=grid,
            in_specs=[pl.BlockSpec((tm, d), lambda i, j: (i, 0)),
                      pl.BlockSpec((1, d), lambda i, j: (0, 0)),
                      pl.BlockSpec((SUBLANES, d), lambda i, j: (0, sh_col)),
                      pl.BlockSpec((SUBLANES, d), lambda i, j: (0, sc_col)),
                      pl.BlockSpec((d, tn), lambda i, j: (0, j))],
            out_specs=pl.BlockSpec((tm, tn), lambda i, j: (i, j)),
            out_shape=out_shape,
            scratch_shapes=scratch,
            compiler_params=_cparams("parallel", "arbitrary"),
            name="norm_mod_matmul",
        )(x, g, mod, mod, w)
    ttype, cos_t, sin_t = rope
    return pl.pallas_call(
        functools.partial(_nmm_rope_kernel, **kern_kw),
        grid_spec=pltpu.PrefetchScalarGridSpec(
            num_scalar_prefetch=1,
            grid=grid,
            in_specs=[pl.BlockSpec((tm, d), lambda i, j, tt: (i, 0)),
                      pl.BlockSpec((1, d), lambda i, j, tt: (0, 0)),
                      pl.BlockSpec((SUBLANES, d), lambda i, j, tt: (0, sh_col)),
                      pl.BlockSpec((SUBLANES, d), lambda i, j, tt: (0, sc_col)),
                      pl.BlockSpec((d, tn), lambda i, j, tt: (0, j)),
                      pl.BlockSpec((None, tm, LANES), lambda i, j, tt: (tt[j], i, 0)),
                      pl.BlockSpec((None, tm, LANES), lambda i, j, tt: (tt[j], i, 0))],
            out_specs=pl.BlockSpec((tm, tn), lambda i, j, tt: (i, j)),
            scratch_shapes=scratch),
        out_shape=out_shape,
        compiler_params=_cparams("parallel", "arbitrary"),
        name="norm_mod_qkv_rope",
    )(ttype, x, g, mod, mod, w, cos_t, sin_t)


def _rope_tables(n_ctx, seq):
    pos = jnp.arange(seq)
    rows = (pos // GRID_W).astype(F32)
    cols = (pos % GRID_W).astype(F32)

    def tab(dim):
        quarter = dim // 4
        freqs = ROPE_THETA ** (-jnp.arange(quarter, dtype=F32) / quarter)
        ar, ac = rows[:, None] * freqs, cols[:, None] * freqs
        cosv = jnp.concatenate([jnp.cos(ar), jnp.cos(ar), jnp.cos(ac), jnp.cos(ac)], axis=1)
        sinv = jnp.concatenate([-jnp.sin(ar), jnp.sin(ar), -jnp.sin(ac), jnp.sin(ac)], axis=1)
        reps = LANES // dim
        cosv, sinv = jnp.tile(cosv, (1, reps)), jnp.tile(sinv, (1, reps))
        cosv = jnp.concatenate([jnp.ones((n_ctx, LANES), F32), cosv], axis=0)
        sinv = jnp.concatenate([jnp.zeros((n_ctx, LANES), F32), sinv], axis=0)
        return cosv, sinv

    ca, sa = tab(A_QK_DIM)
    cb, sb = tab(B_HEAD_DIM)
    qa = (A_QK_DIM ** -0.5) * LOG2E
    qb = B_HEAD_DIM ** -0.5
    one, zero = jnp.ones_like(ca), jnp.zeros_like(ca)
    cos_t = jnp.stack([ca * qa, ca, cb * qb, cb, one])
    sin_t = jnp.stack([sa * qa, sa, sb * qb, sb, zero])
    return cos_t, sin_t


V_AUG = A_V_DIM + 16


def _flash_kernel(lam_ref, qt_ref, k_ref, vt_ref, g_ref, o_ref, s_a, s_b, acc,
                  *, tq, tk, n_ctx, t_rows, out_scale):
    i = pl.program_id(1)
    qt = qt_ref[...]
    row = lax.broadcasted_iota(jnp.int32, qt.shape, 0)
    zero = jnp.zeros_like(qt)
    qm = (jnp.where(row < A_QK_DIM, qt, zero), jnp.where(row >= A_QK_DIM, qt, zero))

    def start(tile, size):
        return tile * size if isinstance(tile, int) else pl.multiple_of(tile * size, size)

    def scores(tile, size, dst):
        kt = k_ref[pl.ds(start(tile, size), size), :]
        for m in range(2):
            dst[m, 0:size, :] = jnp.dot(kt, qm[m], preferred_element_type=F32)

    def soft_pv(tile, size, src, ms):
        vt = vt_ref[:, pl.ds(start(tile, size), size)]
        new_ms = []
        for m in range(2):
            s = src[m, 0:size, :]
            mx = jnp.maximum(ms[m], jnp.max(s, axis=0, keepdims=True))
            alpha = jnp.exp2(ms[m] - mx)
            p = jnp.exp2((s - mx).astype(BF16))
            acc[m] = alpha * acc[m] + jnp.dot(vt, p, preferred_element_type=F32)
            new_ms.append(mx)
        return tuple(new_ms)

    def finish():
        o1 = acc[0, 0:A_V_DIM, :] / acc[0, A_V_DIM:A_V_DIM + 1, :]
        o2 = acc[1, 0:A_V_DIM, :] / acc[1, A_V_DIM:A_V_DIM + 1, :]
        o = o1 - lam_ref[0] * o2
        var = jnp.mean(o * o, axis=0, keepdims=True)
        o = o * lax.rsqrt(var + EPS) * (g_ref[...] * out_scale)
        o_ref[...] = o.T.astype(o_ref.dtype)

    m0 = jnp.full((1, tq), NEG_INF, F32)
    n_q_ctx = n_ctx // tq

    @pl.when(i < n_q_ctx)
    def _():
        acc[...] = jnp.zeros_like(acc)
        scores(0, n_ctx, s_a)
        soft_pv(0, n_ctx, s_a, (m0, m0))
        finish()

    @pl.when(i >= n_q_ctx)
    def _():
        nk = t_rows // tk
        n_pairs = (nk - 1) // 2
        acc[...] = jnp.zeros_like(acc)
        scores(0, tk, s_a)

        def pair(pp, ms):
            scores(2 * pp + 1, tk, s_b)
            ms = soft_pv(2 * pp, tk, s_a, ms)
            scores(2 * pp + 2, tk, s_a)
            return soft_pv(2 * pp + 1, tk, s_b, ms)

        ms = lax.fori_loop(0, n_pairs, pair, (m0, m0))
        done = 2 * n_pairs
        if nk - done == 2:
            scores(done + 1, tk, s_b)
            ms = soft_pv(done, tk, s_a, ms)
            soft_pv(done + 1, tk, s_b, ms)
        else:
            soft_pv(done, tk, s_a, ms)
        finish()


def _diff_attention(qt, k, vt_aug, lam, subln_col, n_ctx, out_scale):
    t_rows = k.shape[0]
    tq = 256
    tk = _pick(t_rows, (1280, 1024, 512, 256))
    assert n_ctx <= tk
    return pl.pallas_call(
        functools.partial(_flash_kernel, tq=tq, tk=tk, n_ctx=n_ctx, t_rows=t_rows, out_scale=out_scale),
        grid=(A_HEADS, t_rows // tq),
        in_specs=[pl.BlockSpec(memory_space=pltpu.SMEM),
                  pl.BlockSpec((LANES, tq), lambda h, i: (h, i)),
                  pl.BlockSpec((t_rows, LANES), lambda h, i: (0, h)),
                  pl.BlockSpec((V_AUG, t_rows), lambda h, i: (h, 0)),
                  pl.BlockSpec((LANES, 1), lambda h, i: (0, 0))],
        out_specs=pl.BlockSpec((tq, LANES), lambda h, i: (i, h)),
        out_shape=jax.ShapeDtypeStruct((t_rows, A_HEADS * A_V_DIM), BF16),
        scratch_shapes=[pltpu.VMEM((2, tk, tq), F32), pltpu.VMEM((2, tk, tq), F32),
                        pltpu.VMEM((2, V_AUG, tq), F32)],
        compiler_params=_cparams("parallel", "arbitrary"),
        name="diff_attention",
    )(lam, qt, k, vt_aug, subln_col)


def _window_kernel(sink_ref, q_ref, kp_ref, ko_ref, kn_ref, vp_ref, vo_ref, vn_ref, kc_ref, vc_ref, o_ref,
                   *, nb, nb_ctx):
    g = pl.program_id(0)
    n = pl.program_id(1)
    qi = lax.broadcasted_iota(jnp.int32, (WINDOW, WINDOW), 0)
    kk = lax.broadcasted_iota(jnp.int32, (WINDOW, WINDOW), 1)
    own_ok = n >= nb_ctx
    prev_ok = n >= nb_ctx + 1
    next_ok = jnp.logical_and(own_ok, n <= nb - 2)
    m_prev = jnp.logical_and(kk >= qi, prev_ok)
    m_own = jnp.logical_and(kk >= 0, own_ok)
    m_next = jnp.logical_and(kk <= qi, next_ok)
    dn = (((1,), (1,)), ((), ()))
    for r in range(B_GROUP):
        q = q_ref[:, r * B_HEAD_DIM:(r + 1) * B_HEAD_DIM]
        s_p = jnp.where(m_prev, lax.dot_general(q, kp_ref[...], dn, preferred_element_type=F32), NEG_INF)
        s_o = jnp.where(m_own, lax.dot_general(q, ko_ref[...], dn, preferred_element_type=F32), NEG_INF)
        s_n = jnp.where(m_next, lax.dot_general(q, kn_ref[...], dn, preferred_element_type=F32), NEG_INF)
        s_c = lax.dot_general(q, kc_ref[...], dn, preferred_element_type=F32)
        sink = sink_ref[g * B_GROUP + r]
        mx = jnp.maximum(jnp.maximum(jnp.max(s_p, axis=1, keepdims=True), jnp.max(s_o, axis=1, keepdims=True)),
                         jnp.maximum(jnp.max(s_n, axis=1, keepdims=True), jnp.max(s_c, axis=1, keepdims=True)))
        mx = jnp.maximum(mx, sink)
        p_p, p_o, p_n, p_c = jnp.exp(s_p - mx), jnp.exp(s_o - mx), jnp.exp(s_n - mx), jnp.exp(s_c - mx)
        den = (jnp.sum(p_p, axis=1, keepdims=True) + jnp.sum(p_o, axis=1, keepdims=True)
               + jnp.sum(p_n, axis=1, keepdims=True) + jnp.sum(p_c, axis=1, keepdims=True)
               + jnp.exp(sink - mx))
        o = (jnp.dot(p_p.astype(BF16), vp_ref[...], preferred_element_type=F32)
             + jnp.dot(p_o.astype(BF16), vo_ref[...], preferred_element_type=F32)
             + jnp.dot(p_n.astype(BF16), vn_ref[...], preferred_element_type=F32)
             + jnp.dot(p_c.astype(BF16), vc_ref[...], preferred_element_type=F32))
        o_ref[:, r * B_HEAD_DIM:(r + 1) * B_HEAD_DIM] = (o / den).astype(o_ref.dtype)


def _window_attention(qkv, sink, n_ctx, col_q, col_k, col_v):
    t_rows = qkv.shape[0]
    nb = t_rows // WINDOW
    nb_ctx = n_ctx // WINDOW
    gq = B_GROUP * B_HEAD_DIM // LANES

    def kv_spec(col, shift):
        def imap(g, n):
            return (jnp.clip(n + shift, 0, nb - 1), col + g)
        return pl.BlockSpec((WINDOW, LANES), imap)

    return pl.pallas_call(
        functools.partial(_window_kernel, nb=nb, nb_ctx=nb_ctx),
        grid=(B_KV_HEADS, nb),
        in_specs=[pl.BlockSpec(memory_space=pltpu.SMEM),
                  pl.BlockSpec((WINDOW, B_GROUP * B_HEAD_DIM), lambda g, n: (n, col_q // gq + g)),
                  kv_spec(col_k, -1), kv_spec(col_k, 0), kv_spec(col_k, 1),
                  kv_spec(col_v, -1), kv_spec(col_v, 0), kv_spec(col_v, 1),
                  pl.BlockSpec((n_ctx, LANES), lambda g, n: (0, col_k + g)),
                  pl.BlockSpec((n_ctx, LANES), lambda g, n: (0, col_v + g))],
        out_specs=pl.BlockSpec((WINDOW, B_GROUP * B_HEAD_DIM), lambda g, n: (n, g)),
        out_shape=jax.ShapeDtypeStruct((t_rows, B_Q_HEADS * B_HEAD_DIM), BF16),
        compiler_params=_cparams("parallel", "arbitrary"),
        name="window_attention",
    )(sink, qkv, qkv, qkv, qkv, qkv, qkv, qkv, qkv, qkv)


def _post_mixer(y, x, g1_ref, g_ref, sh_ref, sc_ref, wr_ref, br_ref, xo_ref, h_ref, lg_ref, row0, n_ctx):
    row = row0 + lax.broadcasted_iota(jnp.int32, (x.shape[0], 1), 0)
    g1 = jnp.where(row < n_ctx, g1_ref[0:1, :], g1_ref[1:2, :])
    xn = x + g1 * y
    xo_ref[...] = xn
    h = _norm_mod(xn, g_ref[...], sh_ref[...], sc_ref[...], row0, n_ctx)
    h_ref[...] = h.astype(h_ref.dtype)
    lg_ref[...] = jnp.dot(h, wr_ref[...], preferred_element_type=F32, precision=HIGHEST) + br_ref[...]


def _attn_out_kernel(ya_ref, yb_ref, x_ref, woa_ref, wob_ref, g1_ref, g_ref, sh_ref, sc_ref, wr_ref, br_ref,
                     xo_ref, h_ref, lg_ref, *, n_ctx, tm):
    y = (jnp.dot(ya_ref[...], woa_ref[...], preferred_element_type=F32)
         + jnp.dot(yb_ref[...], wob_ref[...], preferred_element_type=F32))
    _post_mixer(y, x_ref[...], g1_ref, g_ref, sh_ref, sc_ref, wr_ref, br_ref, xo_ref, h_ref, lg_ref,
                pl.program_id(0) * tm, n_ctx)


def _glu_out_kernel(y_ref, x_ref, wv_ref, wg_ref, g1_ref, g_ref, sh_ref, sc_ref, wr_ref, br_ref,
                    xo_ref, h_ref, lg_ref, *, n_ctx, tm):
    a = jax.nn.gelu(y_ref[...], approximate=True).astype(BF16)
    val = jnp.dot(a, wv_ref[...], preferred_element_type=F32)
    gate = jnp.dot(a, wg_ref[...], preferred_element_type=F32)
    _post_mixer(val * jax.nn.sigmoid(gate), x_ref[...], g1_ref, g_ref, sh_ref, sc_ref, wr_ref, br_ref,
                xo_ref, h_ref, lg_ref, pl.program_id(0) * tm, n_ctx)


def _mixer_out(kernel_fn, acts, x, weights, mod, cols, g2row, wr, br, n_ctx, name):
    t_rows, d = x.shape
    tm = 256
    row = lambda i: (i, 0)
    const = lambda i: (0, 0)
    in_specs = ([pl.BlockSpec((tm, a.shape[1]), row) for a in acts]
                + [pl.BlockSpec((tm, d), row)]
                + [pl.BlockSpec(w.shape, const) for w in weights]
                + [pl.BlockSpec((SUBLANES, d), lambda i, c=c: (0, c)) for c in cols[:1]]
                + [pl.BlockSpec((1, d), const)]
                + [pl.BlockSpec((SUBLANES, d), lambda i, c=c: (0, c)) for c in cols[1:]]
                + [pl.BlockSpec(wr.shape, const), pl.BlockSpec(br.shape, const)])
    return pl.pallas_call(
        functools.partial(kernel_fn, n_ctx=n_ctx, tm=tm),
        grid=(t_rows // tm,),
        in_specs=in_specs,
        out_specs=[pl.BlockSpec((tm, d), row), pl.BlockSpec((tm, d), row), pl.BlockSpec((tm, LANES), row)],
        out_shape=[jax.ShapeDtypeStruct((t_rows, d), F32), jax.ShapeDtypeStruct((t_rows, d), BF16),
                   jax.ShapeDtypeStruct((t_rows, LANES), F32)],
        compiler_params=_cparams("parallel"),
        name=name,
    )(*acts, x, *weights, mod, g2row, mod, mod, wr, br)


def _router_kernel(lg_ref, idx_ref, gate_ref, cnt_ref, *, n_exp):
    @pl.when(pl.program_id(0) == 0)
    def _():
        cnt_ref[...] = jnp.zeros_like(cnt_ref)

    lt = lg_ref[...].T[0:n_exp, :]
    eid = lax.broadcasted_iota(jnp.int32, lt.shape, 0).astype(F32)
    vals, idxs = [], []
    hist = jnp.zeros(lt.shape, F32)
    for _ in range(TOP_K):
        mv = jnp.max(lt, axis=0, keepdims=True)
        ix = jnp.min(jnp.where(lt == mv, eid, float(n_exp)), axis=0, keepdims=True)
        sel = eid == ix
        hist = hist + sel.astype(F32)
        lt = jnp.where(sel, -jnp.inf, lt)
        vals.append(mv)
        idxs.append(ix)
    es = [jnp.exp(v - vals[0]) for v in vals]
    den = es[0] + es[1] + es[2] + es[3]
    pad_f = jnp.zeros((SUBLANES - TOP_K, lt.shape[1]), F32)
    idx_ref[...] = jnp.concatenate(idxs + [pad_f], axis=0).astype(jnp.int32)
    gate_ref[...] = jnp.concatenate([e / den for e in es] + [pad_f], axis=0)
    cnt_ref[...] += jnp.sum(hist, axis=1, keepdims=True)


def _router(logits, n_exp):
    t_rows = logits.shape[0]
    tm = 256
    return pl.pallas_call(
        functools.partial(_router_kernel, n_exp=n_exp),
        grid=(t_rows // tm,),
        in_specs=[pl.BlockSpec((tm, LANES), lambda i: (i, 0))],
        out_specs=[pl.BlockSpec((SUBLANES, tm), lambda i: (0, i)),
                   pl.BlockSpec((SUBLANES, tm), lambda i: (0, i)),
                   pl.BlockSpec((n_exp, 1), lambda i: (0, 0))],
        out_shape=[jax.ShapeDtypeStruct((SUBLANES, t_rows), jnp.int32),
                   jax.ShapeDtypeStruct((SUBLANES, t_rows), F32),
                   jax.ShapeDtypeStruct((n_exp, 1), F32)],
        compiler_params=_cparams("arbitrary"),
        name="router_topk",
    )(logits)


def _dest_kernel(idx_ref, start_ref, dest_ref, carry, *, n_exp, tm):
    @pl.when(pl.program_id(0) == 0)
    def _():
        carry[...] = start_ref[...]

    eid = lax.broadcasted_iota(jnp.int32, (n_exp, tm), 0)
    idx = idx_ref[...]
    sels = [eid == idx[k:k + 1, :] for k in range(TOP_K)]
    total = sels[0].astype(F32) + sels[1].astype(F32) + sels[2].astype(F32) + sels[3].astype(F32)
    rr = lax.broadcasted_iota(jnp.int32, (tm, tm), 0)
    cc = lax.broadcasted_iota(jnp.int32, (tm, tm), 1)
    upper = jnp.where(rr < cc, 1.0, 0.0).astype(BF16)
    before = jnp.dot(total.astype(BF16), upper, preferred_element_type=F32) + carry[...]
    rows = [jnp.sum(jnp.where(sels[k], before, 0.0), axis=0, keepdims=True) for k in range(TOP_K)]
    pad = jnp.zeros((SUBLANES - TOP_K, tm), F32)
    dest_ref[...] = jnp.concatenate(rows + [pad], axis=0).astype(jnp.int32)
    carry[...] += jnp.sum(total, axis=1, keepdims=True)


def _dest_rows(idx, starts, n_exp):
    t_rows = idx.shape[1]
    tm = 256
    return pl.pallas_call(
        functools.partial(_dest_kernel, n_exp=n_exp, tm=tm),
        grid=(t_rows // tm,),
        in_specs=[pl.BlockSpec((SUBLANES, tm), lambda i: (0, i)),
                  pl.BlockSpec((n_exp, 1), lambda i: (0, 0))],
        out_specs=pl.BlockSpec((SUBLANES, tm), lambda i: (0, i)),
        out_shape=jax.ShapeDtypeStruct((SUBLANES, t_rows), jnp.int32),
        scratch_shapes=[pltpu.VMEM((n_exp, 1), F32)],
        compiler_params=_cparams("arbitrary"),
        name="moe_dest_rows",
    )(idx, starts)


def _expert_gu_kernel(be_ref, bv_ref, bn_ref, x_ref, wg_ref, wu_ref, bg_ref, bu_ref, o_ref, wg_s, wu_s, *, blk):
    b = pl.program_id(1)
    nvalid = bv_ref[b]

    @pl.when(bn_ref[b] == 1)
    def _():
        wg_s[...] = wg_ref[...].astype(BF16)
        wu_s[...] = wu_ref[...].astype(BF16)

    @pl.when(nvalid > 0)
    def _():
        rows = lax.broadcasted_iota(jnp.int32, (blk, 1), 0)
        x = x_ref[...]
        x = jnp.where(rows < nvalid, x, jnp.zeros_like(x))
        gate = jnp.dot(x, wg_s[...], preferred_element_type=F32) + bg_ref[...]
        up = jnp.dot(x, wu_s[...], preferred_element_type=F32) + bu_ref[...]
        gate = jnp.minimum(gate, SWIGLU_LIMIT)
        up = jnp.clip(up, -SWIGLU_LIMIT, SWIGLU_LIMIT)
        act = (up + 1.0) * (gate * jax.nn.sigmoid(SWIGLU_ALPHA * gate))
        o_ref[...] = act.astype(o_ref.dtype)

    @pl.when(nvalid == 0)
    def _():
        o_ref[...] = jnp.zeros_like(o_ref)


def _expert_dn_kernel(be_ref, bv_ref, bn_ref, a_ref, wd_ref, bd_ref, o_ref, wd_s):
    b = pl.program_id(1)

    @pl.when(bn_ref[b] == 1)
    def _():
        wd_s[...] = wd_ref[...].astype(BF16)

    @pl.when(bv_ref[b] > 0)
    def _():
        y = jnp.dot(a_ref[...], wd_s[...], preferred_element_type=F32) + bd_ref[...]
        o_ref[...] = y.astype(o_ref.dtype)

    @pl.when(bv_ref[b] == 0)
    def _():
        o_ref[...] = jnp.zeros_like(o_ref)


def _expert_matmul(xs, blk_e, blk_valid, blk_new, w_gu, b_gu, w_dn, b_dn, layer):
    n_rows, d = xs.shape
    f_dim = w_dn.shape[2]
    blk = EXPERT_BLOCK
    tf = 512
    nf = f_dim // tf
    tn = 512
    n_blocks = n_rows // blk
    act = pl.pallas_call(
        functools.partial(_expert_gu_kernel, blk=blk),
        grid_spec=pltpu.PrefetchScalarGridSpec(
            num_scalar_prefetch=3,
            grid=(nf, n_blocks),
            in_specs=[pl.BlockSpec((blk, d), lambda f, b, be, bv, bn: (b, 0)),
                      pl.BlockSpec((None, None, d, tf), lambda f, b, be, bv, bn: (layer, be[b], 0, f)),
                      pl.BlockSpec((None, None, d, tf), lambda f, b, be, bv, bn: (layer, be[b], 0, nf + f)),
                      pl.BlockSpec((None, None, 1, tf), lambda f, b, be, bv, bn: (layer, be[b], 0, f)),
                      pl.BlockSpec((None, None, 1, tf), lambda f, b, be, bv, bn: (layer, be[b], 0, nf + f))],
            out_specs=pl.BlockSpec((blk, tf), lambda f, b, be, bv, bn: (b, f)),
            scratch_shapes=[pltpu.VMEM((d, tf), BF16), pltpu.VMEM((d, tf), BF16)]),
        out_shape=jax.ShapeDtypeStruct((n_rows, f_dim), BF16),
        compiler_params=_cparams("arbitrary", "arbitrary"),
        name="expert_gate_up",
    )(blk_e, blk_valid, blk_new, xs, w_gu, w_gu, b_gu, b_gu)
    return pl.pallas_call(
        _expert_dn_kernel,
        grid_spec=pltpu.PrefetchScalarGridSpec(
            num_scalar_prefetch=3,
            grid=(d // tn, n_blocks),
            in_specs=[pl.BlockSpec((blk, f_dim), lambda n, b, be, bv, bn: (b, 0)),
                      pl.BlockSpec((None, None, f_dim, tn), lambda n, b, be, bv, bn: (layer, be[b], 0, n)),
                      pl.BlockSpec((None, None, 1, tn), lambda n, b, be, bv, bn: (layer, be[b], 0, n))],
            out_specs=pl.BlockSpec((blk, tn), lambda n, b, be, bv, bn: (b, n)),
            scratch_shapes=[pltpu.VMEM((f_dim, tn), BF16)]),
        out_shape=jax.ShapeDtypeStruct((n_rows, d), BF16),
        compiler_params=_cparams("arbitrary", "arbitrary"),
        name="expert_down",
    )(blk_e, blk_valid, blk_new, act, w_dn, b_dn)


def _combine_kernel(y_ref, gate_ref, x_ref, g2_ref, fg_ref, o_ref, *, n_ctx, tm, final):
    gates = gate_ref[...]
    f = y_ref[0].astype(F32) * gates[:, 0:1]
    for k in range(1, TOP_K):
        f = f + y_ref[k].astype(F32) * gates[:, k:k + 1]
    row = pl.program_id(0) * tm + lax.broadcasted_iota(jnp.int32, (tm, 1), 0)
    g2 = jnp.where(row < n_ctx, g2_ref[0:1, :], g2_ref[1:2, :])
    xn = x_ref[...] + g2 * f
    if final:
        ms = jnp.mean(xn * xn, axis=-1, keepdims=True)
        xn = xn * lax.rsqrt(ms + EPS) * fg_ref[...]
    o_ref[...] = xn


def _combine(yk, gates_t, x, mod, g2_col, final_g, n_ctx, final):
    t_rows, d = x.shape
    tm = 256
    return pl.pallas_call(
        functools.partial(_combine_kernel, n_ctx=n_ctx, tm=tm, final=final),
        grid=(t_rows // tm,),
        in_specs=[pl.BlockSpec((TOP_K, tm, d), lambda i: (0, i, 0)),
                  pl.BlockSpec((tm, SUBLANES), lambda i: (i, 0)),
                  pl.BlockSpec((tm, d), lambda i: (i, 0)),
                  pl.BlockSpec((SUBLANES, d), lambda i: (0, g2_col)),
                  pl.BlockSpec((1, d), lambda i: (0, 0))],
        out_specs=pl.BlockSpec((tm, d), lambda i: (i, 0)),
        out_shape=jax.ShapeDtypeStruct((t_rows, d), F32),
        compiler_params=_cparams("parallel"),
        name="moe_combine",
    )(yk, gates_t, x, mod, final_g)


def _moe(h, logits, x, mod, g2_col, w_gu, b_gu, w_dn, b_dn, layer, n_ctx, final_g, final):
    t_rows, d = h.shape
    n_exp = w_gu.shape[1]
    idx, gates, counts = _router(logits, n_exp)
    counts = counts[:, 0].astype(jnp.int32)
    blk = EXPERT_BLOCK
    padded = (counts + blk - 1) // blk * blk
    pends = jnp.cumsum(padded)
    pstarts = pends - padded
    dest = _dest_rows(idx, pstarts.astype(F32)[:, None], n_exp)[:TOP_K]
    n_blocks = -(-(t_rows * TOP_K) // blk) + n_exp
    n_rows = n_blocks * blk
    blk_start = jnp.arange(n_blocks, dtype=jnp.int32) * blk
    blk_e = jnp.minimum(jnp.sum(blk_start[:, None] >= pends[None, :], axis=1), n_exp - 1).astype(jnp.int32)
    blk_valid = jnp.clip(pstarts[blk_e] + counts[blk_e] - blk_start, 0, blk).astype(jnp.int32)
    blk_valid = jnp.where(blk_start < pends[-1], blk_valid, 0)
    blk_new = jnp.concatenate([jnp.ones((1,), jnp.int32), (blk_e[1:] != blk_e[:-1]).astype(jnp.int32)])
    tok =jnp.broadcast_to(jnp.arange(t_rows, dtype=jnp.int32)[None, :], dest.shape)
    row_tok = (jnp.arange(n_rows, dtype=jnp.int32) % t_rows).at[dest.reshape(-1)].set(tok.reshape(-1))
    xs = h.at[row_tok].get(mode='promise_in_bounds')
    ys = _expert_matmul(xs, blk_e, blk_valid, blk_new, w_gu, b_gu, w_dn, b_dn, layer)
    yk = ys.at[dest.reshape(-1)].get(mode='promise_in_bounds').reshape(TOP_K, t_rows, d)
    return _combine(yk, gates.T, x, mod, g2_col, final_g, n_ctx, final)


def _s5_tables(a_re, a_im, log_step, b_re, b_im, c_re, c_im, d_skip):
    tc = S5_TC
    n_grp, n_st = a_re.shape[1], a_re.shape[2]
    lr = jnp.minimum(a_re.astype(F32), -1e-4)
    li = a_im.astype(F32)
    dt = jnp.exp(log_step.astype(F32))[..., None]
    dd = jnp.arange(tc + 1, dtype=F32)[:, None, None, None]
    mag = jnp.exp(lr * dt * dd)
    pr, pi = mag * jnp.cos(li * dt * dd), mag * jnp.sin(li * dt * dd)
    ar, ai = pr[1], pi[1]
    den = lr * lr + li * li
    nr = ar - 1.0
    zr = (nr * lr + ai * li) / den
    zi = (ai * lr - nr * li) / den
    br, bi = b_re.astype(F32), b_im.astype(F32)
    bbr = zr[..., None] * br - zi[..., None] * bi
    bbi = zr[..., None] * bi + zi[..., None] * br
    abr = pr[:tc, ..., None] * bbr - pi[:tc, ..., None] * bbi
    abi = pr[:tc, ..., None] * bbi + pi[:tc, ..., None] * bbr
    cr, ci = c_re.astype(F32), c_im.astype(F32)
    kern = (jnp.einsum('xgip,dxgpj->dxgij', cr, abr, precision=HIGHEST)
            - jnp.einsum('xgip,dxgpj->dxgij', ci, abi, precision=HIGHEST))
    def toeplitz(kx):
        ext = jnp.concatenate([jnp.zeros_like(kx), kx], axis=0)
        return jnp.stack([ext[tc - a:2 * tc - a] for a in range(tc)], axis=0)

    m_f = toeplitz(kern[:, 0])
    m_b = toeplitz(kern[:, 1]).transpose(1, 0, 2, 3, 4)
    m_tot = (m_f + m_b).transpose(2, 0, 4, 1, 3)
    eye_t = jnp.eye(tc, dtype=F32)
    eye_i = jnp.eye(S5_GROUP, dtype=F32)
    dsk = d_skip.astype(F32).reshape(n_grp, S5_GROUP)
    m_tot = m_tot + (eye_t[None, :, None, :, None] * eye_i[None, None, :, None, :]
                     * dsk[:, None, :, None, None])
    m_tot = m_tot.reshape(n_grp, tc * S5_GROUP, tc * S5_GROUP)
    pw_f = tc - 1 - jnp.arange(tc)
    pw_b = jnp.arange(tc)

    def b_cols(part, pw, x):
        return part[pw, x].transpose(1, 0, 3, 2).reshape(n_grp, tc * S5_GROUP, n_st)

    b_mat = jnp.concatenate([b_cols(abr, pw_f, 0), b_cols(abi, pw_f, 0),
                             b_cols(abr, pw_b, 1), b_cols(abi, pw_b, 1)], axis=-1)
    pcf = 1 + jnp.arange(tc)
    pcb = tc - jnp.arange(tc)

    def c_rows(pw, x):
        prx, pix = pr[pw, x], pi[pw, x]
        re_c = cr[x][None] * prx[:, :, None, :] - ci[x][None] * pix[:, :, None, :]
        im_c = -(cr[x][None] * pix[:, :, None, :] + ci[x][None] * prx[:, :, None, :])
        to_rows = lambda z: z.transpose(1, 3, 0, 2).reshape(n_grp, n_st, tc * S5_GROUP)
        return to_rows(re_c), to_rows(im_c)

    c_mat = jnp.concatenate(list(c_rows(pcf, 0)) + list(c_rows(pcb, 1)), axis=1)
    prt, pit = pr[tc], pi[tc]
    a1 = jnp.concatenate([prt[0], prt[0], prt[1], prt[1]], axis=-1)
    a2 = jnp.concatenate([-pit[0], pit[0], -pit[1], pit[1]], axis=-1)
    return m_tot, b_mat, c_mat, a1, a2


S5_LANE_GROUPS = LANES // S5_GROUP


def _s5_pack(u_ref, c0, cc):
    xs = [u_ref[pl.ds(c0 * S5_TC + tau, cc, stride=S5_TC), :] for tau in range(S5_TC)]
    return [jnp.concatenate([xs[tau][:, g * S5_GROUP:(g + 1) * S5_GROUP] for tau in range(S5_TC)], axis=1)
            for g in range(S5_LANE_GROUPS)]


def _s5_in_kernel(u_ref, b_ref, o_ref, *, cc):
    def body(ci, carry):
        c0 = pl.multiple_of(ci * cc, SUBLANES)
        ugs = _s5_pack(u_ref, c0, cc)
        for g in range(S5_LANE_GROUPS):
            o_ref[g, pl.ds(c0, cc), :] = jnp.dot(ugs[g], b_ref[g], preferred_element_type=F32, precision=HIGHEST)
        return carry

    lax.fori_loop(0, u_ref.shape[0] // (S5_TC * cc), body, 0)


def _s5_out_kernel(u_ref, z_ref, m_ref, c_ref, o_ref, *, cc):
    def body(ci, carry):
        c0 = pl.multiple_of(ci * cc, SUBLANES)
        ugs = _s5_pack(u_ref, c0, cc)
        ys = [jnp.dot(ugs[g], m_ref[g], preferred_element_type=F32, precision=HIGHEST)
              + jnp.dot(z_ref[g, pl.ds(c0, cc), :], c_ref[g], preferred_element_type=F32, precision=HIGHEST)
              for g in range(S5_LANE_GROUPS)]
        for tau in range(S5_TC):
            row = jnp.concatenate([y[:, tau * S5_GROUP:(tau + 1) * S5_GROUP] for y in ys], axis=1)
            o_ref[pl.ds(c0 * S5_TC + tau, cc, stride=S5_TC), :] = row
        return carry

    lax.fori_loop(0, u_ref.shape[0] // (S5_TC * cc), body, 0)


def _s5_scan_kernel(h_ref, a1_ref, a2_ref, z_ref, *, n_chunks, n_ctx_chunks):
    half = 2 * S5_STATE
    a1 = a1_ref[...]
    a2 = a2_ref[...]
    a1f, a1b = a1[:, :half], a1[:, half:]
    a2f, a2b = a2[:, :half], a2[:, half:]
    zero = jnp.zeros((a1.shape[0], half), F32)

    def swap(v):
        return pltpu.roll(v, S5_STATE, 1)

    def step(t, carry):
        rf, rfs, rb, rbs = carry
        cb = jnp.where(t < n_ctx_chunks, n_ctx_chunks - 1 - t, n_chunks - 1 - (t - n_ctx_chunks))
        z_ref[t, :, 0:half] = rf
        z_ref[cb, :, half:2 * half] = rb
        hf = h_ref[t, :, 0:half]
        hb = h_ref[cb, :, half:2 * half]
        nrf = a1f * rf + a2f * rfs + hf
        nrfs = a1f * rfs - a2f * rf + swap(hf)
        nrb = a1b * rb + a2b * rbs + hb
        nrbs = a1b * rbs - a2b * rb + swap(hb)
        return nrf, nrfs, nrb, nrbs

    lax.fori_loop(0, n_chunks, step, (zero, zero, zero, zero), unroll=8 if n_chunks % 8 == 0 else 1)


def _s5_mixer_scan(u, tables, n_ctx):
    m_tot, b_mat, c_mat, a1, a2 = tables
    t_rows, width = u.shape
    tc = S5_TC
    n_grp = width // S5_GROUP
    n_chunks = t_rows // tc
    kdim = tc * S5_GROUP
    sdim = 4 * S5_STATE
    gb = S5_LANE_GROUPS
    gs = SUBLANES
    n_split = 2
    half_chunks = n_chunks // n_split
    cc = _pick(half_chunks, (104, 80, 40, 8))
    u_spec = pl.BlockSpec((t_rows // n_split, LANES), lambda j, r: (r, j))
    tab_spec = lambda a, b: pl.BlockSpec((gb, a, b), lambda j, r: (j, 0, 0))
    st_spec = pl.BlockSpec((gb, half_chunks, sdim), lambda j, r: (j, r, 0))
    grid = (width // LANES, n_split)
    hin = pl.pallas_call(
        functools.partial(_s5_in_kernel, cc=cc),
        grid=grid,
        in_specs=[u_spec, tab_spec(kdim, sdim)],
        out_specs=st_spec,
        out_shape=jax.ShapeDtypeStruct((n_grp, n_chunks, sdim), F32),
        compiler_params=_cparams("parallel", "arbitrary"),
        name="s5_chunk_inputs",
    )(u, b_mat)
    hin_t = hin.transpose(1, 0, 2)
    z_t = pl.pallas_call(
        functools.partial(_s5_scan_kernel, n_chunks=n_chunks, n_ctx_chunks=n_ctx // tc),
        grid=(n_grp // gs,),
        in_specs=[pl.BlockSpec((n_chunks, gs, sdim), lambda g: (0, g, 0)),
                  pl.BlockSpec((gs, sdim), lambda g: (g, 0)),
                  pl.BlockSpec((gs, sdim), lambda g: (g, 0))],
        out_specs=pl.BlockSpec((n_chunks, gs, sdim), lambda g: (0, g, 0)),
        out_shape=jax.ShapeDtypeStruct((n_chunks, n_grp, sdim), F32),
        compiler_params=_cparams("parallel"),
        name="s5_chunk_scan",
    )(hin_t, a1, a2)
    z = z_t.transpose(1, 0, 2)
    return pl.pallas_call(
        functools.partial(_s5_out_kernel, cc=cc),
        grid=grid,
        in_specs=[u_spec, st_spec, tab_spec(kdim, kdim), tab_spec(sdim, kdim)],
        out_specs=u_spec,
        out_shape=jax.ShapeDtypeStruct((t_rows, width), F32),
        compiler_params=_cparams("parallel", "arbitrary"),
        name="s5_chunk_outputs",
    )(u, z, m_tot, c_mat)


def _lambda_init(layer):
    return 0.8 - 0.6 * math.exp(-0.3 * layer)


def kernel(x, c, ctx, c_ctx, w_mod, b_mod, norm1_g, norm2_g, final_g, attn_w_qkv, attn_w_o, lambda_q1, lambda_k1, lambda_q2, lambda_k2, subln_g, sink_logit, s5_w_in, s5_a_re, s5_a_im, s5_log_step, s5_b_re, s5_b_im, s5_c_re, s5_c_im, s5_d, s5_w_glu, router_w, router_b, expert_w_gu, expert_b_gu, expert_w_down, expert_b_down):
    bsz, seq, d = x.shape
    assert bsz == 1, "single-sequence block"
    n_ctx = ctx.shape[1]
    depth = w_mod.shape[0]
    n_exp = router_w.shape[2]
    assert n_ctx % 256 == 0 and seq % 256 == 0

    xj = jnp.concatenate([ctx[0], x[0]], axis=0)
    cs = jnp.zeros((SUBLANES, d), F32).at[0].set(c_ctx).at[1].set(c[0])
    b_mod3 = b_mod[:, None, :]
    b_gu4 = expert_b_gu[:, :, None, :]
    b_dn4 = expert_b_down[:, :, None, :]
    wr_pad = jnp.pad(router_w, ((0, 0), (0, 0), (0, LANES - n_exp)))
    br_pad = jnp.pad(router_b, ((0, 0), (0, LANES - n_exp)), constant_values=NEG_INF)[:, None, :]
    final_row = final_g[None, :]

    for i in range(depth):
        last = i == depth - 1
        j = i // 2
        mod = _adaln_mod(cs, w_mod, b_mod3, i)
        if i % 2 == 0:
            sizes = (A_HEADS * 2 * A_QK_DIM, A_HEADS * 2 * A_QK_DIM, A_HEADS * A_V_DIM,
                     B_Q_HEADS * B_HEAD_DIM, B_KV_HEADS * B_HEAD_DIM, B_KV_HEADS * B_HEAD_DIM)
            offs = np.concatenate([[0], np.cumsum(sizes)])
            types = [0, 1, ROPE_NONE, 2, 3, ROPE_NONE]
            ttype = jnp.asarray(np.concatenate([np.full(s // 256, t) for s, t in zip(sizes, types)]), jnp.int32)
            cos_t, sin_t = _rope_tables(n_ctx, seq)
            qkv = _norm_mod_matmul(xj, norm1_g[i][None, :], mod, 0, 1, attn_w_qkv[j].astype(BF16), n_ctx, BF16,
                                   rope=(ttype, cos_t, sin_t))
            qt = qkv[:, offs[0]:offs[1]].T
            ka = qkv[:, offs[1]:offs[2]]
            vt = qkv[:, offs[2]:offs[3]].T.reshape(A_HEADS, A_V_DIM, -1)
            vt = jnp.concatenate([vt, jnp.ones((A_HEADS, V_AUG - A_V_DIM, vt.shape[2]), BF16)], axis=1)
            vt = vt.reshape(A_HEADS * V_AUG, -1)
            f32 = F32
            li = _lambda_init(i)
            lam = (jnp.exp(jnp.sum(lambda_q1[j].astype(f32) * lambda_k1[j].astype(f32)))
                   - jnp.exp(jnp.sum(lambda_q2[j].astype(f32) * lambda_k2[j].astype(f32))) + li)
            ya = _diff_attention(qt, ka, vt, lam.reshape(1), subln_g[j][:, None], n_ctx, 1.0 - li)
            yb = _window_attention(qkv, sink_logit[j], n_ctx, int(offs[3]) // LANES, int(offs[4]) // LANES,
                                   int(offs[5]) // LANES)
            w_o = attn_w_o[j].astype(BF16)
            na = A_HEADS * A_V_DIM
            xj, h2, logits = _mixer_out(_attn_out_kernel, [ya, yb], xj, [w_o[:na], w_o[na:]], mod, (2, 3, 4),
                                        norm2_g[i][None, :], wr_pad[i], br_pad[i], n_ctx, "attn_out_router")
        else:
            u = _norm_mod_matmul(xj, norm1_g[i][None, :], mod, 0, 1, s5_w_in[j].astype(BF16), n_ctx, F32)
            tables = _s5_tables(s5_a_re[j], s5_a_im[j], s5_log_step[j], s5_b_re[j], s5_b_im[j],
                                s5_c_re[j], s5_c_im[j], s5_d[j])
            y = _s5_mixer_scan(u, tables, n_ctx)
            w_glu = s5_w_glu[j].astype(BF16)
            xj, h2, logits = _mixer_out(_glu_out_kernel, [y], xj, [w_glu[:, :d], w_glu[:, d:]], mod, (2, 3, 4),
                                        norm2_g[i][None, :], wr_pad[i], br_pad[i], n_ctx, "glu_out_router")
        xj = _moe(h2, logits, xj, mod, 5, expert_w_gu, b_gu4, expert_w_down, b_dn4, i, n_ctx, final_row, last)
    return xj[n_ctx:][None]
```

```python
import functools
import math

import jax
import jax.numpy as jnp
import numpy as np
from jax import lax
from jax.experimental import pallas as pl
from jax.experimental.pallas import tpu as pltpu
from jax.experimental.pallas import tpu_sc as plsc

F32 = jnp.float32
BF16 = jnp.bfloat16
HIGHEST = lax.Precision.HIGHEST

V7X_VMEM_BYTES = 64 * 1024 * 1024
VMEM_LIMIT = V7X_VMEM_BYTES - 8 * 1024 * 1024
LANES = 128
SUBLANES = 8

GRID_W = 64
N_MOD = 6
EPS = 1e-6
NEG_INF = -1e30
ROPE_THETA = 10000.0
A_HEADS = 8
A_QK_DIM = 64
A_V_DIM = 128
B_Q_HEADS = 8
B_KV_HEADS = 2
B_GROUP = 4
B_HEAD_DIM = 128
WINDOW = 128
S5_GROUP = 16
S5_STATE = 64
S5_TC = 16
TOP_K = 4
SWIGLU_LIMIT = 7.0
SWIGLU_ALPHA = 1.702
EXPERT_BLOCK = 512
DOWN_BLOCK = 256
LOG2E = 1.4426950408889634


def _pick(n, cands):
    for c in cands:
        if n % c == 0:
            return c
    raise ValueError(f"no tile for {n} in {cands}")


def _cparams(*sem):
    return pltpu.CompilerParams(dimension_semantics=sem, vmem_limit_bytes=VMEM_LIMIT)


def _mod_kernel(c_ref, w_ref, b_ref, o_ref):
    cv = c_ref[...]
    s = cv * jax.nn.sigmoid(cv)
    o_ref[...] = jnp.dot(s, w_ref[...], preferred_element_type=F32, precision=HIGHEST) + b_ref[...]


def _adaln_mod(cs, w_mod, b_mod, layer):
    d, n = w_mod.shape[1], w_mod.shape[2]
    tn = _pick(n, (1024, 512, 256, 128))
    return pl.pallas_call(
        _mod_kernel,
        grid=(n // tn,),
        in_specs=[pl.BlockSpec((SUBLANES, d), lambda j: (0, 0)),
                  pl.BlockSpec((None, d, tn), lambda j: (layer, 0, j)),
                  pl.BlockSpec((None, 1, tn), lambda j: (layer, 0, j))],
        out_specs=pl.BlockSpec((SUBLANES, tn), lambda j: (0, j)),
        out_shape=jax.ShapeDtypeStruct((SUBLANES, n), F32),
        compiler_params=_cparams("parallel"),
        name="adaln_mod",
    )(cs, w_mod, b_mod)


def _norm_mod(x, g, sh2, sc2, row0, n_ctx):
    ms = jnp.mean(x * x, axis=-1, keepdims=True)
    y = x * lax.rsqrt(ms + EPS) * g
    row = row0 + lax.broadcasted_iota(jnp.int32, (x.shape[0], 1), 0)
    is_ctx = row < n_ctx
    sc = jnp.where(is_ctx, sc2[0:1, :], sc2[1:2, :])
    sh = jnp.where(is_ctx, sh2[0:1, :], sh2[1:2, :])
    return y * (1.0 + sc) + sh


ROPE_NONE = 4


def _rope_store(acc, cos_ref, sin_ref, o_ref, shift):
    cosv, sinv = cos_ref[...], sin_ref[...]
    for cgrp in range(acc.shape[1] // LANES):
        a = acc[:, cgrp * LANES:(cgrp + 1) * LANES]
        lane = lax.broadcasted_iota(jnp.int32, a.shape, 1)
        in_second = (lane & (2 * shift - 1)) >= shift
        sw = jnp.where(in_second, pltpu.roll(a, shift, 1), pltpu.roll(a, LANES - shift, 1))
        o_ref[:, cgrp * LANES:(cgrp + 1) * LANES] = (a * cosv + sw * sinv).astype(o_ref.dtype)


NORM_ROWS = 256


def _norm_mod_to_scratch(x_ref, g_ref, sh_ref, sc_ref, h_scr, row0, n_ctx):
    def body(r, carry):
        off = pl.multiple_of(r * NORM_ROWS, NORM_ROWS)
        h = _norm_mod(x_ref[pl.ds(off, NORM_ROWS), :], g_ref[...], sh_ref[...], sc_ref[...], row0 + off, n_ctx)
        h_scr[pl.ds(off, NORM_ROWS), :] = h.astype(h_scr.dtype)
        return carry

    lax.fori_loop(0, x_ref.shape[0] // NORM_ROWS, body, 0)


def _nmm_rope_kernel(tt_ref, x_ref, g_ref, sh_ref, sc_ref, w_ref, cos_ref, sin_ref, o_ref, h_scr,
                     *, n_ctx, tm):
    i = pl.program_id(0)
    j = pl.program_id(1)

    @pl.when(j == 0)
    def _():
        _norm_mod_to_scratch(x_ref, g_ref, sh_ref, sc_ref, h_scr, i * tm, n_ctx)

    acc = jnp.dot(h_scr[...], w_ref[...], preferred_element_type=F32)
    t = tt_ref[j]

    @pl.when(t == ROPE_NONE)
    def _():
        o_ref[...] = acc.astype(o_ref.dtype)

    @pl.when(t < 2)
    def _():
        _rope_store(acc, cos_ref, sin_ref, o_ref, A_QK_DIM // 4)

    @pl.when(jnp.logical_and(t >= 2, t < ROPE_NONE))
    def _():
        _rope_store(acc, cos_ref, sin_ref, o_ref, B_HEAD_DIM // 4)


def _nmm_plain_kernel(x_ref, g_ref, sh_ref, sc_ref, w_ref, o_ref, h_scr, *, n_ctx, tm):
    i = pl.program_id(0)
    j = pl.program_id(1)

    @pl.when(j == 0)
    def _():
        _norm_mod_to_scratch(x_ref, g_ref, sh_ref, sc_ref, h_scr, i * tm, n_ctx)

    o_ref[...] = jnp.dot(h_scr[...], w_ref[...], preferred_element_type=F32).astype(o_ref.dtype)


def _norm_mod_matmul(x, g, mod, sh_col, sc_col, w, n_ctx, out_dtype, rope=None):
    t_rows, d = x.shape
    n = w.shape[1]
    tm = _pick(t_rows, (1280, 1024, 512, 256, 128))
    tn = 256
    grid = (t_rows // tm, n // tn)
    kern_kw = dict(n_ctx=n_ctx, tm=tm)
    scratch = [pltpu.VMEM((tm, d), BF16)]
    out_shape = jax.ShapeDtypeStruct((t_rows, n), out_dtype)
    if rope is None:
        return pl.pallas_call(
            functools.partial(_nmm_plain_kernel, **kern_kw),
            grid=grid,
            in_specs=[pl.BlockSpec((tm, d), lambda i, j: (i, 0)),
                      pl.BlockSpec((1, d), lambda i, j: (0, 0)),
                      pl.BlockSpec((SUBLANES, d), lambda i, j: (0, sh_col)),
                      pl.BlockSpec((SUBLANES, d), lambda i, j: (0, sc_col)),
                      pl.BlockSpec((d, tn), lambda i, j: (0, j))],
            out_specs=pl.BlockSpec((tm, tn), lambda i, j: (i, j)),
            out_shape=out_shape,
            scratch_shapes=scratch,
            compiler_params=_cparams("parallel", "arbitrary"),
            name="norm_mod_matmul",
        )(x, g, mod, mod, w)
    ttype, cos_t, sin_t = rope
    return pl.pallas_call(
        functools.partial(_nmm_rope_kernel, **kern_kw),
        grid_spec=pltpu.PrefetchScalarGridSpec(
            num_scalar_prefetch=1,
            grid=grid,
            in_specs=[pl.BlockSpec((tm, d), lambda i, j, tt: (i, 0)),
                      pl.BlockSpec((1, d), lambda i, j, tt: (0, 0)),
                      pl.BlockSpec((SUBLANES, d), lambda i, j, tt: (0, sh_col)),
                      pl.BlockSpec((SUBLANES, d), lambda i, j, tt: (0, sc_col)),
                      pl.BlockSpec((d, tn), lambda i, j, tt: (0, j)),
                      pl.BlockSpec((None, tm, LANES), lambda i, j, tt: (tt[j], i, 0)),
                      pl.BlockSpec((None, tm, LANES), lambda i, j, tt: (tt[j], i, 0))],
            out_specs=pl.BlockSpec((tm, tn), lambda i, j, tt: (i, j)),
            scratch_shapes=scratch),
        out_shape=out_shape,
        compiler_params=_cparams("parallel", "arbitrary"),
        name="norm_mod_qkv_rope",
    )(ttype, x, g, mod, mod, w, cos_t, sin_t)


def _rope_tables(n_ctx, seq):
    pos = jnp.arange(seq)
    rows = (pos // GRID_W).astype(F32)
    cols = (pos % GRID_W).astype(F32)

    def tab(dim):
        quarter = dim // 4
        freqs = ROPE_THETA ** (-jnp.arange(quarter, dtype=F32) / quarter)
        ar, ac = rows[:, None] * freqs, cols[:, None] * freqs
        cosv = jnp.concatenate([jnp.cos(ar), jnp.cos(ar), jnp.cos(ac), jnp.cos(ac)], axis=1)
        sinv = jnp.concatenate([-jnp.sin(ar), jnp.sin(ar), -jnp.sin(ac), jnp.sin(ac)], axis=1)
        reps = LANES // dim
        cosv, sinv = jnp.tile(cosv, (1, reps)), jnp.tile(sinv, (1, reps))
        cosv = jnp.concatenate([jnp.ones((n_ctx, LANES), F32), cosv], axis=0)
        sinv = jnp.concatenate([jnp.zeros((n_ctx, LANES), F32), sinv], axis=0)
        return cosv, sinv

    ca, sa = tab(A_QK_DIM)
    cb, sb = tab(B_HEAD_DIM)
    qa = (A_QK_DIM ** -0.5) * LOG2E
    qb = B_HEAD_DIM ** -0.5
    one, zero = jnp.ones_like(ca), jnp.zeros_like(ca)
    cos_t = jnp.stack([ca * qa, ca, cb * qb, cb, one])
    sin_t = jnp.stack([sa * qa, sa, sb * qb, sb, zero])
    return cos_t, sin_t


V_AUG = A_V_DIM + 16


def _flash_kernel(lam_ref, qt_ref, k_ref, vt_ref, g_ref, o_ref, s_a, s_b, acc,
                  *, tq, tk, n_ctx, t_rows, out_scale):
    i = pl.program_id(1)
    qt = qt_ref[...]
    row = lax.broadcasted_iota(jnp.int32, qt.shape, 0)
    zero = jnp.zeros_like(qt)
    qm = (jnp.where(row < A_QK_DIM, qt, zero), jnp.where(row >= A_QK_DIM, qt, zero))

    def start(tile, size):
        return tile * size if isinstance(tile, int) else pl.multiple_of(tile * size, size)

    def scores(tile, size, dst):
        kt = k_ref[pl.ds(start(tile, size), size), :]
        for m in range(2):
            dst[m, 0:size, :] = jnp.dot(kt, qm[m], preferred_element_type=F32)

    def soft_pv(tile, size, src, ms):
        vt = vt_ref[:, pl.ds(start(tile, size), size)]
        new_ms = []
        for m in range(2):
            s = src[m, 0:size, :]
            mx = jnp.maximum(ms[m], jnp.max(s, axis=0, keepdims=True))
            alpha = jnp.exp2(ms[m] - mx)
            p = jnp.exp2((s - mx).astype(BF16))
            acc[m] = alpha * acc[m] + jnp.dot(vt, p, preferred_element_type=F32)
            new_ms.append(mx)
        return tuple(new_ms)

    def finish():
        o1 = acc[0, 0:A_V_DIM, :] / acc[0, A_V_DIM:A_V_DIM + 1, :]
        o2 = acc[1, 0:A_V_DIM, :] / acc[1, A_V_DIM:A_V_DIM + 1, :]
        o = o1 - lam_ref[0] * o2
        var = jnp.mean(o * o, axis=0, keepdims=True)
        o = o * lax.rsqrt(var + EPS) * (g_ref[...] * out_scale)
        o_ref[...] = o.T.astype(o_ref.dtype)

    m0 = jnp.full((1, tq), NEG_INF, F32)
    n_q_ctx = n_ctx // tq

    @pl.when(i < n_q_ctx)
    def _():
        acc[...] = jnp.zeros_like(acc)
        scores(0, n_ctx, s_a)
        soft_pv(0, n_ctx, s_a, (m0, m0))
        finish()

    @pl.when(i >= n_q_ctx)
    def _():
        nk = t_rows // tk
        n_pairs = (nk - 1) // 2
        acc[...] = jnp.zeros_like(acc)
        scores(0, tk, s_a)

        def pair(pp, ms):
            scores(2 * pp + 1, tk, s_b)
            ms = soft_pv(2 * pp, tk, s_a, ms)
            scores(2 * pp + 2, tk, s_a)
            return soft_pv(2 * pp + 1, tk, s_b, ms)

        ms = lax.fori_loop(0, n_pairs, pair, (m0, m0))
        done = 2 * n_pairs
        if nk - done == 2:
            scores(done + 1, tk, s_b)
            ms = soft_pv(done, tk, s_a, ms)
            soft_pv(done + 1, tk, s_b, ms)
        else:
            soft_pv(done, tk, s_a, ms)
        finish()


def _diff_attention(qt, k, vt_aug, lam, subln_col, n_ctx, out_scale):
    t_rows = k.shape[0]
    tq = 256
    tk = _pick(t_rows, (1280, 1024, 512, 256))
    assert n_ctx <= tk
    return pl.pallas_call(
        functools.partial(_flash_kernel, tq=tq, tk=tk, n_ctx=n_ctx, t_rows=t_rows, out_scale=out_scale),
        grid=(A_HEADS, t_rows // tq),
        in_specs=[pl.BlockSpec(memory_space=pltpu.SMEM),
                  pl.BlockSpec((LANES, tq), lambda h, i: (h, i)),
                  pl.BlockSpec((t_rows, LANES), lambda h, i: (0, h)),
                  pl.BlockSpec((V_AUG, t_rows), lambda h, i: (h, 0)),
                  pl.BlockSpec((LANES, 1), lambda h, i: (0, 0))],
        out_specs=pl.BlockSpec((tq, LANES), lambda h, i: (i, h)),
        out_shape=jax.ShapeDtypeStruct((t_rows, A_HEADS * A_V_DIM), BF16),
        scratch_shapes=[pltpu.VMEM((2, tk, tq), F32), pltpu.VMEM((2, tk, tq), F32),
                        pltpu.VMEM((2, V_AUG, tq), F32)],
        compiler_params=_cparams("parallel", "arbitrary"),
        name="diff_attention",
    )(lam, qt, k, vt_aug, subln_col)


def _window_kernel(sink_ref, q_ref, kp_ref, ko_ref, kn_ref, vp_ref, vo_ref, vn_ref, kc_ref, vc_ref, o_ref,
                   *, nb, nb_ctx):
    g = pl.program_id(0)
    n = pl.program_id(1)
    qi = lax.broadcasted_iota(jnp.int32, (WINDOW, WINDOW), 0)
    kk = lax.broadcasted_iota(jnp.int32, (WINDOW, WINDOW), 1)
    own_ok = n >= nb_ctx
    prev_ok = n >= nb_ctx + 1
    next_ok = jnp.logical_and(own_ok, n <= nb - 2)
    m_prev = jnp.logical_and(kk >= qi, prev_ok)
    m_own = jnp.logical_and(kk >= 0, own_ok)
    m_next = jnp.logical_and(kk <= qi, next_ok)
    dn = (((1,), (1,)), ((), ()))
    for r in range(B_GROUP):
        q = q_ref[:, r * B_HEAD_DIM:(r + 1) * B_HEAD_DIM]
        s_p = jnp.where(m_prev, lax.dot_general(q, kp_ref[...], dn, preferred_element_type=F32), NEG_INF)
        s_o = jnp.where(m_own, lax.dot_general(q, ko_ref[...], dn, preferred_element_type=F32), NEG_INF)
        s_n = jnp.where(m_next, lax.dot_general(q, kn_ref[...], dn, preferred_element_type=F32), NEG_INF)
        s_c = lax.dot_general(q, kc_ref[...], dn, preferred_element_type=F32)
        sink = sink_ref[g * B_GROUP + r]
        mx = jnp.maximum(jnp.maximum(jnp.max(s_p, axis=1, keepdims=True), jnp.max(s_o, axis=1, keepdims=True)),
                         jnp.maximum(jnp.max(s_n, axis=1, keepdims=True), jnp.max(s_c, axis=1, keepdims=True)))
        mx = jnp.maximum(mx, sink)
        p_p, p_o, p_n, p_c = jnp.exp(s_p - mx), jnp.exp(s_o - mx), jnp.exp(s_n - mx), jnp.exp(s_c - mx)
        den = (jnp.sum(p_p, axis=1, keepdims=True) + jnp.sum(p_o, axis=1, keepdims=True)
               + jnp.sum(p_n, axis=1, keepdims=True) + jnp.sum(p_c, axis=1, keepdims=True)
               + jnp.exp(sink - mx))
        o = (jnp.dot(p_p.astype(BF16), vp_ref[...], preferred_element_type=F32)
             + jnp.dot(p_o.astype(BF16), vo_ref[...], preferred_element_type=F32)
             + jnp.dot(p_n.astype(BF16), vn_ref[...], preferred_element_type=F32)
             + jnp.dot(p_c.astype(BF16), vc_ref[...], preferred_element_type=F32))
        o_ref[:, r * B_HEAD_DIM:(r + 1) * B_HEAD_DIM] = (o / den).astype(o_ref.dtype)


def _window_attention(qkv, sink, n_ctx, col_q, col_k, col_v):
    t_rows = qkv.shape[0]
    nb = t_rows // WINDOW
    nb_ctx = n_ctx // WINDOW
    gq = B_GROUP * B_HEAD_DIM // LANES

    def kv_spec(col, shift):
        def imap(g, n):
            return (jnp.clip(n + shift, 0, nb - 1), col + g)
        return pl.BlockSpec((WINDOW, LANES), imap)

    return pl.pallas_call(
        functools.partial(_window_kernel, nb=nb, nb_ctx=nb_ctx),
        grid=(B_KV_HEADS, nb),
        in_specs=[pl.BlockSpec(memory_space=pltpu.SMEM),
                  pl.BlockSpec((WINDOW, B_GROUP * B_HEAD_DIM), lambda g, n: (n, col_q // gq + g)),
                  kv_spec(col_k, -1), kv_spec(col_k, 0), kv_spec(col_k, 1),
                  kv_spec(col_v, -1), kv_spec(col_v, 0), kv_spec(col_v, 1),
                  pl.BlockSpec((n_ctx, LANES), lambda g, n: (0, col_k + g)),
                  pl.BlockSpec((n_ctx, LANES), lambda g, n: (0, col_v + g))],
        out_specs=pl.BlockSpec((WINDOW, B_GROUP * B_HEAD_DIM), lambda g, n: (n, g)),
        out_shape=jax.ShapeDtypeStruct((t_rows, B_Q_HEADS * B_HEAD_DIM), BF16),
        compiler_params=_cparams("parallel", "arbitrary"),
        name="window_attention",
    )(sink, qkv, qkv, qkv, qkv, qkv, qkv, qkv, qkv, qkv)


ROW_PIECES = 8


def _store_packed_rows(ref, v):
    m, half = v.shape[0], v.shape[1] // 2
    lo = pltpu.bitcast(v[:, :half].astype(BF16).astype(F32), jnp.uint32) >> 16
    hi = pltpu.bitcast(v[:, half:].astype(BF16).astype(F32), jnp.uint32) & jnp.uint32(0xFFFF0000)
    w = lo | hi
    for j in range(ROW_PIECES):
        ref[pl.ds(j, m, stride=ROW_PIECES), :] = w[:, j * LANES:(j + 1) * LANES]


def _load_packed_rows(ref, m):
    w = jnp.concatenate([ref[pl.ds(j, m, stride=ROW_PIECES), :] for j in range(ROW_PIECES)], axis=1)
    lo = pltpu.bitcast(w << 16, F32)
    hi = pltpu.bitcast(w & jnp.uint32(0xFFFF0000), F32)
    return lo, hi


def _post_mixer(y, x, g1_ref, g_ref, sh_ref, sc_ref, wr_ref, br_ref, xo_ref, h_ref, lg_ref, row0, n_ctx):
    row = row0 + lax.broadcasted_iota(jnp.int32, (x.shape[0], 1), 0)
    g1 = jnp.where(row < n_ctx, g1_ref[0:1, :], g1_ref[1:2, :])
    xn = x + g1 * y
    xo_ref[...] = xn
    h = _norm_mod(xn, g_ref[...], sh_ref[...], sc_ref[...], row0, n_ctx)
    _store_packed_rows(h_ref, h)
    lg_ref[...] = jnp.dot(h, wr_ref[...], preferred_element_type=F32, precision=HIGHEST) + br_ref[...]


def _attn_out_kernel(ya_ref, yb_ref, x_ref, woa_ref, wob_ref, g1_ref, g_ref, sh_ref, sc_ref, wr_ref, br_ref,
                     xo_ref, h_ref, lg_ref, *, n_ctx, tm):
    y = (jnp.dot(ya_ref[...], woa_ref[...], preferred_element_type=F32)
         + jnp.dot(yb_ref[...], wob_ref[...], preferred_element_type=F32))
    _post_mixer(y, x_ref[...], g1_ref, g_ref, sh_ref, sc_ref, wr_ref, br_ref, xo_ref, h_ref, lg_ref,
                pl.program_id(0) * tm, n_ctx)


def _glu_out_kernel(y_ref, x_ref, wv_ref, wg_ref, g1_ref, g_ref, sh_ref, sc_ref, wr_ref, br_ref,
                    xo_ref, h_ref, lg_ref, *, n_ctx, tm):
    a = jax.nn.gelu(y_ref[...], approximate=True).astype(BF16)
    val = jnp.dot(a, wv_ref[...], preferred_element_type=F32)
    gate = jnp.dot(a, wg_ref[...], preferred_element_type=F32)
    _post_mixer(val * jax.nn.sigmoid(gate), x_ref[...], g1_ref, g_ref, sh_ref, sc_ref, wr_ref, br_ref,
                xo_ref, h_ref, lg_ref, pl.program_id(0) * tm, n_ctx)


def _mixer_out(kernel_fn, acts, x, weights, mod, cols, g2row, wr, br, n_ctx, name):
    t_rows, d = x.shape
    tm = 256
    row = lambda i: (i, 0)
    const = lambda i: (0, 0)
    in_specs = ([pl.BlockSpec((tm, a.shape[1]), row) for a in acts]
                + [pl.BlockSpec((tm, d), row)]
                + [pl.BlockSpec(w.shape, const) for w in weights]
                + [pl.BlockSpec((SUBLANES, d), lambda i, c=c: (0, c)) for c in cols[:1]]
                + [pl.BlockSpec((1, d), const)]
                + [pl.BlockSpec((SUBLANES, d), lambda i, c=c: (0, c)) for c in cols[1:]]
                + [pl.BlockSpec(wr.shape, const), pl.BlockSpec(br.shape, const)])
    return pl.pallas_call(
        functools.partial(kernel_fn, n_ctx=n_ctx, tm=tm),
        grid=(t_rows // tm,),
        in_specs=in_specs,
        out_specs=[pl.BlockSpec((tm, d), row), pl.BlockSpec((tm * ROW_PIECES, LANES), row),
                   pl.BlockSpec((tm, LANES), row)],
        out_shape=[jax.ShapeDtypeStruct((t_rows, d), F32),
                   jax.ShapeDtypeStruct((t_rows * ROW_PIECES, LANES), jnp.uint32),
                   jax.ShapeDtypeStruct((t_rows, LANES), F32)],
        compiler_params=_cparams("parallel"),
        name=name,
    )(*acts, x, *weights, mod, g2row, mod, mod, wr, br)


def _router_kernel(lg_ref, idx_ref, gate_ref, cnt_ref, *, n_exp):
    @pl.when(pl.program_id(0) == 0)
    def _():
        cnt_ref[...] = jnp.zeros_like(cnt_ref)

    lt = lg_ref[...].T[0:n_exp, :]
    eid = lax.broadcasted_iota(jnp.int32, lt.shape, 0).astype(F32)
    vals, idxs = [], []
    hist = jnp.zeros(lt.shape, F32)
    for _ in range(TOP_K):
        mv = jnp.max(lt, axis=0, keepdims=True)
        ix = jnp.min(jnp.where(lt == mv, eid, float(n_exp)), axis=0, keepdims=True)
        sel = eid == ix
        hist = hist + sel.astype(F32)
        lt = jnp.where(sel, -jnp.inf, lt)
        vals.append(mv)
        idxs.append(ix)
    es = [jnp.exp(v - vals[0]) for v in vals]
    den = es[0] + es[1] + es[2] + es[3]
    pad_f = jnp.zeros((SUBLANES - TOP_K, lt.shape[1]), F32)
    idx_ref[...] = jnp.concatenate(idxs + [pad_f], axis=0).astype(jnp.int32)
    gate_ref[...] = jnp.concatenate([e / den for e in es] + [pad_f], axis=0)
    cnt_ref[...] += jnp.sum(hist, axis=1, keepdims=True)


def _router(logits, n_exp):
    t_rows = logits.shape[0]
    tm = 256
    return pl.pallas_call(
        functools.partial(_router_kernel, n_exp=n_exp),
        grid=(t_rows // tm,),
        in_specs=[pl.BlockSpec((tm, LANES), lambda i: (i, 0))],
        out_specs=[pl.BlockSpec((SUBLANES, tm), lambda i: (0, i)),
                   pl.BlockSpec((SUBLANES, tm), lambda i: (0, i)),
                   pl.BlockSpec((n_exp, 1), lambda i: (0, 0))],
        out_shape=[jax.ShapeDtypeStruct((SUBLANES, t_rows), jnp.int32),
                   jax.ShapeDtypeStruct((SUBLANES, t_rows), F32),
                   jax.ShapeDtypeStruct((n_exp, 1), F32)],
        compiler_params=_cparams("arbitrary"),
        name="router_topk",
    )(logits)


def _dest_kernel(idx_ref, start_ref, dest_ref, carry, *, n_exp, tm):
    @pl.when(pl.program_id(0) == 0)
    def _():
        carry[...] = start_ref[...]

    eid = lax.broadcasted_iota(jnp.int32, (n_exp, tm), 0)
    idx = idx_ref[...]
    sels = [eid == idx[k:k + 1, :] for k in range(TOP_K)]
    total = sels[0].astype(F32) + sels[1].astype(F32) + sels[2].astype(F32) + sels[3].astype(F32)
    rr = lax.broadcasted_iota(jnp.int32, (tm, tm), 0)
    cc = lax.broadcasted_iota(jnp.int32, (tm, tm), 1)
    upper = jnp.where(rr < cc, 1.0, 0.0).astype(BF16)
    before = jnp.dot(total.astype(BF16), upper, preferred_element_type=F32) + carry[...]
    rows = [jnp.sum(jnp.where(sels[k], before, 0.0), axis=0, keepdims=True) for k in range(TOP_K)]
    pad = jnp.zeros((SUBLANES - TOP_K, tm), F32)
    dest_ref[...] = jnp.concatenate(rows + [pad], axis=0).astype(jnp.int32)
    carry[...] += jnp.sum(total, axis=1, keepdims=True)


def _dest_rows(idx, starts, n_exp):
    t_rows = idx.shape[1]
    tm = 256
    return pl.pallas_call(
        functools.partial(_dest_kernel, n_exp=n_exp, tm=tm),
        grid=(t_rows // tm,),
        in_specs=[pl.BlockSpec((SUBLANES, tm), lambda i: (0, i)),
                  pl.BlockSpec((n_exp, 1), lambda i: (0, 0))],
        out_specs=pl.BlockSpec((SUBLANES, tm), lambda i: (0, i)),
        out_shape=jax.ShapeDtypeStruct((SUBLANES, t_rows), jnp.int32),
        scratch_shapes=[pltpu.VMEM((n_exp, 1), F32)],
        compiler_params=_cparams("arbitrary"),
        name="moe_dest_rows",
    )(idx, starts)


def _expert_gu_kernel(be_ref, bv_ref, bn_ref, x_ref, wg_ref, wu_ref, bg_ref, bu_ref, o_ref, wg_s, wu_s, *, blk):
    b = pl.program_id(1)
    nvalid = bv_ref[b]

    @pl.when(bn_ref[b] == 1)
    def _():
        wg_s[...] = wg_ref[...].astype(BF16)
        wu_s[...] = wu_ref[...].astype(BF16)

    @pl.when(nvalid > 0)
    def _():
        rows = lax.broadcasted_iota(jnp.int32, (blk, 1), 0)
        lo, hi = _load_packed_rows(x_ref, blk)
        x = jnp.concatenate([lo.astype(BF16), hi.astype(BF16)], axis=1)
        x = jnp.where(rows < nvalid, x, jnp.zeros_like(x))
        gate = jnp.dot(x, wg_s[...], preferred_element_type=F32) + bg_ref[...]
        up = jnp.dot(x, wu_s[...], preferred_element_type=F32) + bu_ref[...]
        gate = jnp.minimum(gate, SWIGLU_LIMIT)
        up = jnp.clip(up, -SWIGLU_LIMIT, SWIGLU_LIMIT)
        act = (up + 1.0) * (gate * jax.nn.sigmoid(SWIGLU_ALPHA * gate))
        o_ref[...] = act.astype(o_ref.dtype)

    @pl.when(nvalid == 0)
    def _():
        o_ref[...] = jnp.zeros_like(o_ref)


def _expert_dn_kernel(be_ref, bv_ref, bn_ref, a_ref, wd_ref, bd_ref, o_ref, wd_s, *, blk):
    b = pl.program_id(0)

    @pl.when(bn_ref[b] == 1)
    def _():
        wd_s[...] = wd_ref[...].astype(BF16)

    @pl.when(bv_ref[b] > 0)
    def _():
        y = jnp.dot(a_ref[...], wd_s[...], preferred_element_type=F32) + bd_ref[...]
        _store_packed_rows(o_ref, y)

    @pl.when(bv_ref[b] == 0)
    def _():
        o_ref[...] = jnp.zeros_like(o_ref)


def _expert_matmul(xs, blk_e, blk_valid, blk_new, w_gu, b_gu, w_dn, b_dn, layer):
    d, f_dim = w_dn.shape[3], w_dn.shape[2]
    n_rows = xs.shape[0] // ROW_PIECES
    blk = EXPERT_BLOCK
    tf = 512
    nf = f_dim // tf
    n_blocks = n_rows // blk
    packed_rows = pl.BlockSpec((blk * ROW_PIECES, LANES), lambda f, b, be, bv, bn: (b, 0))
    act = pl.pallas_call(
        functools.partial(_expert_gu_kernel, blk=blk),
        grid_spec=pltpu.PrefetchScalarGridSpec(
            num_scalar_prefetch=3,
            grid=(nf, n_blocks),
            in_specs=[packed_rows,
                      pl.BlockSpec((None, None, d, tf), lambda f, b, be, bv, bn: (layer, be[b], 0, f)),
                      pl.BlockSpec((None, None, d, tf), lambda f, b, be, bv, bn: (layer, be[b], 0, nf + f)),
                      pl.BlockSpec((None, None, 1, tf), lambda f, b, be, bv, bn: (layer, be[b], 0, f)),
                      pl.BlockSpec((None, None, 1, tf), lambda f, b, be, bv, bn: (layer, be[b], 0, nf + f))],
            out_specs=pl.BlockSpec((blk, tf), lambda f, b, be, bv, bn: (b, f)),
            scratch_shapes=[pltpu.VMEM((d, tf), BF16), pltpu.VMEM((d, tf), BF16)]),
        out_shape=jax.ShapeDtypeStruct((n_rows, f_dim), BF16),
        compiler_params=_cparams("arbitrary", "arbitrary"),
        name="expert_gate_up",
    )(blk_e, blk_valid, blk_new, xs, w_gu, w_gu, b_gu, b_gu)
    assert d == 2 * ROW_PIECES * LANES
    sub = blk // DOWN_BLOCK
    part = jnp.arange(n_blocks * sub, dtype=jnp.int32) % sub
    dn_e = jnp.repeat(blk_e, sub)
    dn_valid = jnp.clip(jnp.repeat(blk_valid, sub) - part * DOWN_BLOCK, 0, DOWN_BLOCK)
    dn_new = jnp.where(part == 0, jnp.repeat(blk_new, sub), 0)
    return pl.pallas_call(
        functools.partial(_expert_dn_kernel, blk=DOWN_BLOCK),
        grid_spec=pltpu.PrefetchScalarGridSpec(
            num_scalar_prefetch=3,
            grid=(n_blocks * sub,),
            in_specs=[pl.BlockSpec((DOWN_BLOCK, f_dim), lambda b, be, bv, bn: (b, 0)),
                      pl.BlockSpec((None, None, f_dim, d), lambda b, be, bv, bn: (layer, be[b], 0, 0)),
                      pl.BlockSpec((None, None, 1, d), lambda b, be, bv, bn: (layer, be[b], 0, 0))],
            out_specs=pl.BlockSpec((DOWN_BLOCK * ROW_PIECES, LANES), lambda b, be, bv, bn: (b, 0)),
            scratch_shapes=[pltpu.VMEM((f_dim, d), BF16)]),
        out_shape=jax.ShapeDtypeStruct((n_rows * ROW_PIECES, LANES), jnp.uint32),
        compiler_params=_cparams("arbitrary"),
        name="expert_down",
    )(dn_e, dn_valid, dn_new, act, w_dn, b_dn)


def _combine_kernel(y0_ref, y1_ref, y2_ref, y3_ref, gate_ref, x_ref, g2_ref, fg_ref, o_ref, *, n_ctx, tm, final):
    gates = gate_ref[...]
    f_lo, f_hi = None, None
    for k, y_ref in enumerate((y0_ref, y1_ref, y2_ref, y3_ref)):
        lo, hi = _load_packed_rows(y_ref, tm)
        gk = gates[:, k:k + 1]
        f_lo = lo * gk if f_lo is None else f_lo + lo * gk
        f_hi = hi * gk if f_hi is None else f_hi + hi * gk
    f = jnp.concatenate([f_lo, f_hi], axis=1)
    row = pl.program_id(0) * tm + lax.broadcasted_iota(jnp.int32, (tm, 1), 0)
    g2 = jnp.where(row < n_ctx, g2_ref[0:1, :], g2_ref[1:2, :])
    xn = x_ref[...] + g2 * f
    if final:
        ms = jnp.mean(xn * xn, axis=-1, keepdims=True)
        xn = xn * lax.rsqrt(ms + EPS) * fg_ref[...]
    o_ref[...] = xn


def _combine(yk, gates_t, x, mod, g2_col, final_g, n_ctx, final):
    t_rows, d = x.shape
    tm = 256
    nt = t_rows // tm
    return pl.pallas_call(
        functools.partial(_combine_kernel, n_ctx=n_ctx, tm=tm, final=final),
        grid=(nt,),
        in_specs=[pl.BlockSpec((tm * ROW_PIECES, LANES), lambda i, k=k: (k * nt + i, 0)) for k in range(TOP_K)]
                 + [pl.BlockSpec((tm, SUBLANES), lambda i: (i, 0)),
                  pl.BlockSpec((tm, d), lambda i: (i, 0)),
                  pl.BlockSpec((SUBLANES, d), lambda i: (0, g2_col)),
                  pl.BlockSpec((1, d), lambda i: (0, 0))],
        out_specs=pl.BlockSpec((tm, d), lambda i: (i, 0)),
        out_shape=jax.ShapeDtypeStruct((t_rows, d), F32),
        compiler_params=_cparams("parallel"),
        name="moe_combine",
    )(yk, yk, yk, yk, gates_t, x, mod, final_g)


SC_WINDOW = 128


def _sc_mesh():
    return plsc.VectorSubcoreMesh(core_axis_name="core", subcore_axis_name="subcore")


def _sc_scatter_rows(x, dest, n_out):
    n_src_blocks = x.shape[0] // SC_WINDOW

    @functools.partial(pl.kernel, out_type=jax.ShapeDtypeStruct((n_out, LANES), x.dtype), mesh=_sc_mesh(),
                       scratch_types=[], name="sc_dispatch_rows")
    def run(x_hbm, i_hbm, o_hbm):
        def body(x_vmem, i_vmem):
            pltpu.sync_copy(x_vmem, o_hbm.at[i_vmem.at[0]])

        pltpu.emit_pipeline(
            body, grid=(dest.shape[1] // SC_WINDOW,),
            in_specs=[pl.BlockSpec((SC_WINDOW, LANES), lambda i: (i % n_src_blocks, 0)),
                      pl.BlockSpec((1, SC_WINDOW), lambda i: (0, i))],
            out_specs=[], core_axis_name=("core", "subcore"),
            dimension_semantics=(pltpu.PARALLEL,))(x_hbm, i_hbm)

    return run(x, dest)


def _sc_gather_rows(y, idx):
    n = idx.shape[1]

    @functools.partial(pl.kernel, out_type=jax.ShapeDtypeStruct((n, LANES), y.dtype), mesh=_sc_mesh(),
                       scratch_types=[], name="sc_combine_rows")
    def run(y_hbm, i_hbm, o_hbm):
        def body(i_vmem, o_vmem):
            pltpu.sync_copy(y_hbm.at[i_vmem.at[0]], o_vmem)

        pltpu.emit_pipeline(
            body, grid=(n // SC_WINDOW,),
            in_specs=[pl.BlockSpec((1, SC_WINDOW), lambda i: (0, i))],
            out_specs=[pl.BlockSpec((SC_WINDOW, LANES), lambda i: (i, 0))],
            core_axis_name=("core", "subcore"),
            dimension_semantics=(pltpu.PARALLEL,))(i_hbm, o_hbm)

    return run(y, idx)


def _moe(h, logits, x, mod, g2_col, w_gu, b_gu, w_dn, b_dn, layer, n_ctx, final_g, final):
    t_rows = x.shape[0]
    n_exp = w_gu.shape[1]
    idx, gates, counts = _router(logits, n_exp)
    counts = counts[:, 0].astype(jnp.int32)
    blk = EXPERT_BLOCK
    padded = (counts + blk - 1) // blk * blk
    pends = jnp.cumsum(padded)
    pstarts = pends - padded
    dest = _dest_rows(idx, pstarts.astype(F32)[:, None], n_exp)[:TOP_K]
    n_blocks = -(-(t_rows * TOP_K) // blk) + n_exp
    n_rows = n_blocks * blk
    blk_start = jnp.arange(n_blocks, dtype=jnp.int32) * blk
    blk_e = jnp.minimum(jnp.sum(blk_start[:, None] >= pends[None, :], axis=1), n_exp - 1).astype(jnp.int32)
    blk_valid = jnp.clip(pstarts[blk_e] + counts[blk_e] - blk_start, 0, blk).astype(jnp.int32)
    blk_valid = jnp.where(blk_start < pends[-1], blk_valid, 0)
    blk_new = jnp.concatenate([jnp.ones((1,), jnp.int32), (blk_e[1:] != blk_e[:-1]).astype(jnp.int32)])
    dest8 = (dest[:, :, None] * ROW_PIECES + jnp.arange(ROW_PIECES, dtype=jnp.int32)).reshape(1, -1)
    xs = _sc_scatter_rows(h, dest8, n_rows * ROW_PIECES)
    ys = _expert_matmul(xs, blk_e, blk_valid, blk_new, w_gu, b_gu, w_dn, b_dn, layer)
    yk = _sc_gather_rows(ys, dest8)
    return _combine(yk, gates.T, x, mod, g2_col, final_g, n_ctx, final)


def _s5_tables(a_re, a_im, log_step, b_re, b_im, c_re, c_im, d_skip):
    tc = S5_TC
    n_grp, n_st = a_re.shape[1], a_re.shape[2]
    lr = jnp.minimum(a_re.astype(F32), -1e-4)
    li = a_im.astype(F32)
    dt = jnp.exp(log_step.astype(F32))[..., None]
    dd = jnp.arange(tc + 1, dtype=F32)[:, None, None, None]
    mag = jnp.exp(lr * dt * dd)
    pr, pi = mag * jnp.cos(li * dt * dd), mag * jnp.sin(li * dt * dd)
    ar, ai = pr[1], pi[1]
    den = lr * lr + li * li
    nr = ar - 1.0
    zr = (nr * lr + ai * li) / den
    zi = (ai * lr - nr * li) / den
    br, bi = b_re.astype(F32), b_im.astype(F32)
    bbr = zr[..., None] * br - zi[..., None] * bi
    bbi = zr[..., None] * bi + zi[..., None] * br
    abr = pr[:tc, ..., None] * bbr - pi[:tc, ..., None] * bbi
    abi = pr[:tc, ..., None] * bbi + pi[:tc, ..., None] * bbr
    cr, ci = c_re.astype(F32), c_im.astype(F32)
    kern = (jnp.einsum('xgip,dxgpj->dxgij', cr, abr, precision=HIGHEST)
            - jnp.einsum('xgip,dxgpj->dxgij', ci, abi, precision=HIGHEST))
    def toeplitz(kx):
        ext = jnp.concatenate([jnp.zeros_like(kx), kx], axis=0)
        return jnp.stack([ext[tc - a:2 * tc - a] for a in range(tc)], axis=0)

    m_f = toeplitz(kern[:, 0])
    m_b = toeplitz(kern[:, 1]).transpose(1, 0, 2, 3, 4)
    m_tot = (m_f + m_b).transpose(2, 0, 4, 1, 3)
    eye_t = jnp.eye(tc, dtype=F32)
    eye_i = jnp.eye(S5_GROUP, dtype=F32)
    dsk = d_skip.astype(F32).reshape(n_grp, S5_GROUP)
    m_tot = m_tot + (eye_t[None, :, None, :, None] * eye_i[None, None, :, None, :]
                     * dsk[:, None, :, None, None])
    m_tot = m_tot.reshape(n_grp, tc * S5_GROUP, tc * S5_GROUP)
    pw_f = tc - 1 - jnp.arange(tc)
    pw_b = jnp.arange(tc)

    def b_cols(part, pw, x):
        return part[pw, x].transpose(1, 0, 3, 2).reshape(n_grp, tc * S5_GROUP, n_st)

    b_mat = jnp.concatenate([b_cols(abr, pw_f, 0), b_cols(abi, pw_f, 0),
                             b_cols(abr, pw_b, 1), b_cols(abi, pw_b, 1)], axis=-1)
    pcf = 1 + jnp.arange(tc)
    pcb = tc - jnp.arange(tc)

    def c_rows(pw, x):
        prx, pix = pr[pw, x], pi[pw, x]
        re_c = cr[x][None] * prx[:, :, None, :] - ci[x][None] * pix[:, :, None, :]
        im_c = -(cr[x][None] * pix[:, :, None, :] + ci[x][None] * prx[:, :, None, :])
        to_rows = lambda z: z.transpose(1, 3, 0, 2).reshape(n_grp, n_st, tc * S5_GROUP)
        return to_rows(re_c), to_rows(im_c)

    c_mat = jnp.concatenate(list(c_rows(pcf, 0)) + list(c_rows(pcb, 1)), axis=1)
    prt, pit = pr[tc], pi[tc]
    a1 = jnp.concatenate([prt[0], prt[0], prt[1], prt[1]], axis=-1)
    a2 = jnp.concatenate([-pit[0], pit[0], -pit[1], pit[1]], axis=-1)
    return m_tot, b_mat, c_mat, a1, a2


S5_LANE_GROUPS = LANES // S5_GROUP


def _s5_pack(u_ref, c0, cc):
    xs = [u_ref[pl.ds(c0 * S5_TC + tau, cc, stride=S5_TC), :] for tau in range(S5_TC)]
    return [jnp.concatenate([xs[tau][:, g * S5_GROUP:(g + 1) * S5_GROUP] for tau in range(S5_TC)], axis=1)
            for g in range(S5_LANE_GROUPS)]


def _s5_in_kernel(u_ref, b_ref, o_ref, *, cc):
    def body(ci, carry):
        c0 = pl.multiple_of(ci * cc, SUBLANES)
        ugs = _s5_pack(u_ref, c0, cc)
        for g in range(S5_LANE_GROUPS):
            o_ref[g, pl.ds(c0, cc), :] = jnp.dot(ugs[g], b_ref[g], preferred_element_type=F32, precision=HIGHEST)
        return carry

    lax.fori_loop(0, u_ref.shape[0] // (S5_TC * cc), body, 0)


def _s5_out_kernel(u_ref, z_ref, m_ref, c_ref, o_ref, *, cc):
    def body(ci, carry):
        c0 = pl.multiple_of(ci * cc, SUBLANES)
        ugs = _s5_pack(u_ref, c0, cc)
        ys = [jnp.dot(ugs[g], m_ref[g], preferred_element_type=F32, precision=HIGHEST)
              + jnp.dot(z_ref[g, pl.ds(c0, cc), :], c_ref[g], preferred_element_type=F32, precision=HIGHEST)
              for g in range(S5_LANE_GROUPS)]
        for tau in range(S5_TC):
            row = jnp.concatenate([y[:, tau * S5_GROUP:(tau + 1) * S5_GROUP] for y in ys], axis=1)
            o_ref[pl.ds(c0 * S5_TC + tau, cc, stride=S5_TC), :] = row
        return carry

    lax.fori_loop(0, u_ref.shape[0] // (S5_TC * cc), body, 0)


def _s5_scan_kernel(h_ref, a1_ref, a2_ref, z_ref, *, n_chunks, n_ctx_chunks):
    half = 2 * S5_STATE
    a1 = a1_ref[...]
    a2 = a2_ref[...]
    a1f, a1b = a1[:, :half], a1[:, half:]
    a2f, a2b = a2[:, :half], a2[:, half:]
    zero = jnp.zeros((a1.shape[0], half), F32)

    def swap(v):
        return pltpu.roll(v, S5_STATE, 1)

    def step(t, carry):
        rf, rfs, rb, rbs = carry
        cb = jnp.where(t < n_ctx_chunks, n_ctx_chunks - 1 - t, n_chunks - 1 - (t - n_ctx_chunks))
        z_ref[t, :, 0:half] = rf
        z_ref[cb, :, half:2 * half] = rb
        hf = h_ref[t, :, 0:half]
        hb = h_ref[cb, :, half:2 * half]
        nrf = a1f * rf + a2f * rfs + hf
        nrfs = a1f * rfs - a2f * rf + swap(hf)
        nrb = a1b * rb + a2b * rbs + hb
        nrbs = a1b * rbs - a2b * rb + swap(hb)
        return nrf, nrfs, nrb, nrbs

    lax.fori_loop(0, n_chunks, step, (zero, zero, zero, zero), unroll=8 if n_chunks % 8 == 0 else 1)


def _s5_mixer_scan(u, tables, n_ctx):
    m_tot, b_mat, c_mat, a1, a2 = tables
    t_rows, width = u.shape
    tc = S5_TC
    n_grp = width // S5_GROUP
    n_chunks = t_rows // tc
    kdim = tc * S5_GROUP
    sdim = 4 * S5_STATE
    gb = S5_LANE_GROUPS
    gs = SUBLANES
    n_split = 2
    half_chunks = n_chunks // n_split
    cc = _pick(half_chunks, (104, 80, 40, 8))
    u_spec = pl.BlockSpec((t_rows // n_split, LANES), lambda j, r: (r, j))
    tab_spec = lambda a, b: pl.BlockSpec((gb, a, b), lambda j, r: (j, 0, 0))
    st_spec = pl.BlockSpec((gb, half_chunks, sdim), lambda j, r: (j, r, 0))
    grid = (width // LANES, n_split)
    hin = pl.pallas_call(
        functools.partial(_s5_in_kernel, cc=cc),
        grid=grid,
        in_specs=[u_spec, tab_spec(kdim, sdim)],
        out_specs=st_spec,
        out_shape=jax.ShapeDtypeStruct((n_grp, n_chunks, sdim), F32),
        compiler_params=_cparams("parallel", "arbitrary"),
        name="s5_chunk_inputs",
    )(u, b_mat)
    hin_t = hin.transpose(1, 0, 2)
    z_t = pl.pallas_call(
        functools.partial(_s5_scan_kernel, n_chunks=n_chunks, n_ctx_chunks=n_ctx // tc),
        grid=(n_grp // gs,),
        in_specs=[pl.BlockSpec((n_chunks, gs, sdim), lambda g: (0, g, 0)),
                  pl.BlockSpec((gs, sdim), lambda g: (g, 0)),
                  pl.BlockSpec((gs, sdim), lambda g: (g, 0))],
        out_specs=pl.BlockSpec((n_chunks, gs, sdim), lambda g: (0, g, 0)),
        out_shape=jax.ShapeDtypeStruct((n_chunks, n_grp, sdim), F32),
        compiler_params=_cparams("parallel"),
        name="s5_chunk_scan",
    )(hin_t, a1, a2)
    z = z_t.transpose(1, 0, 2)
    return pl.pallas_call(
        functools.partial(_s5_out_kernel, cc=cc),
        grid=grid,
        in_specs=[u_spec, st_spec, tab_spec(kdim, kdim), tab_spec(sdim, kdim)],
        out_specs=u_spec,
        out_shape=jax.ShapeDtypeStruct((t_rows, width), F32),
        compiler_params=_cparams("parallel", "arbitrary"),
        name="s5_chunk_outputs",
    )(u, z, m_tot, c_mat)


def _lambda_init(layer):
    return 0.8 - 0.6 * math.exp(-0.3 * layer)


def kernel(x, c, ctx, c_ctx, w_mod, b_mod, norm1_g, norm2_g, final_g, attn_w_qkv, attn_w_o, lambda_q1, lambda_k1, lambda_q2, lambda_k2, subln_g, sink_logit, s5_w_in, s5_a_re, s5_a_im, s5_log_step, s5_b_re, s5_b_im, s5_c_re, s5_c_im, s5_d, s5_w_glu, router_w, router_b, expert_w_gu, expert_b_gu, expert_w_down, expert_b_down):
    bsz, seq, d = x.shape
    assert bsz == 1, "single-sequence block"
    n_ctx = ctx.shape[1]
    depth = w_mod.shape[0]
    n_exp = router_w.shape[2]
    assert n_ctx % 256 == 0 and seq % 256 == 0

    xj = jnp.concatenate([ctx[0], x[0]], axis=0)
    cs = jnp.zeros((SUBLANES, d), F32).at[0].set(c_ctx).at[1].set(c[0])
    b_mod3 = b_mod[:, None, :]
    b_gu4 = expert_b_gu[:, :, None, :]
    b_dn4 = expert_b_down[:, :, None, :]
    wr_pad = jnp.pad(router_w, ((0, 0), (0, 0), (0, LANES - n_exp)))
    br_pad = jnp.pad(router_b, ((0, 0), (0, LANES - n_exp)), constant_values=NEG_INF)[:, None, :]
    final_row = final_g[None, :]

    for i in range(depth):
        last = i == depth - 1
        j = i // 2
        mod = _adaln_mod(cs, w_mod, b_mod3, i)
        if i % 2 == 0:
            sizes = (A_HEADS * 2 * A_QK_DIM, A_HEADS * 2 * A_QK_DIM, A_HEADS * A_V_DIM,
                     B_Q_HEADS * B_HEAD_DIM, B_KV_HEADS * B_HEAD_DIM, B_KV_HEADS * B_HEAD_DIM)
            offs = np.concatenate([[0], np.cumsum(sizes)])
            types = [0, 1, ROPE_NONE, 2, 3, ROPE_NONE]
            ttype = jnp.asarray(np.concatenate([np.full(s // 256, t) for s, t in zip(sizes, types)]), jnp.int32)
            cos_t, sin_t = _rope_tables(n_ctx, seq)
            qkv = _norm_mod_matmul(xj, norm1_g[i][None, :], mod, 0, 1, attn_w_qkv[j].astype(BF16), n_ctx, BF16,
                                   rope=(ttype, cos_t, sin_t))
            qt = qkv[:, offs[0]:offs[1]].T
            ka = qkv[:, offs[1]:offs[2]]
            vt = qkv[:, offs[2]:offs[3]].T.reshape(A_HEADS, A_V_DIM, -1)
            vt = jnp.concatenate([vt, jnp.ones((A_HEADS, V_AUG - A_V_DIM, vt.shape[2]), BF16)], axis=1)
            vt = vt.reshape(A_HEADS * V_AUG, -1)
            f32 = F32
            li = _lambda_init(i)
            lam = (jnp.exp(jnp.sum(lambda_q1[j].astype(f32) * lambda_k1[j].astype(f32)))
                   - jnp.exp(jnp.sum(lambda_q2[j].astype(f32) * lambda_k2[j].astype(f32))) + li)
            ya = _diff_attention(qt, ka, vt, lam.reshape(1), subln_g[j][:, None], n_ctx, 1.0 - li)
            yb = _window_attention(qkv, sink_logit[j], n_ctx, int(offs[3]) // LANES, int(offs[4]) // LANES,
                                   int(offs[5]) // LANES)
            w_o = attn_w_o[j].astype(BF16)
            na = A_HEADS * A_V_DIM
            xj, h2, logits = _mixer_out(_attn_out_kernel, [ya, yb], xj, [w_o[:na], w_o[na:]], mod, (2, 3, 4),
                                        norm2_g[i][None, :], wr_pad[i], br_pad[i], n_ctx, "attn_out_router")
        else:
            u = _norm_mod_matmul(xj, norm1_g[i][None, :], mod, 0, 1, s5_w_in[j].astype(BF16), n_ctx, F32)
            tables = _s5_tables(s5_a_re[j], s5_a_im[j], s5_log_step[j], s5_b_re[j], s5_b_im[j],
                                s5_c_re[j], s5_c_im[j], s5_d[j])
            y = _s5_mixer_scan(u, tables, n_ctx)
            w_glu = s5_w_glu[j].astype(BF16)
            xj, h2, logits = _mixer_out(_glu_out_kernel, [y], xj, [w_glu[:, :d], w_glu[:, d:]], mod, (2, 3, 4),
                                        norm2_g[i][None, :], wr_pad[i], br_pad[i], n_ctx, "glu_out_router")
        xj = _moe(h2, logits, xj, mod, 5, expert_w_gu, b_gu4, expert_w_down, b_dn4, i, n_ctx, final_row, last)
    return xj[n_ctx:][None]
```

```python
import functools
import math

import jax
import jax.numpy as jnp
import numpy as np
from jax import lax
from jax.experimental import pallas as pl
from jax.experimental.pallas import tpu as pltpu
from jax.experimental.pallas import tpu_sc as plsc

F32 = jnp.float32
BF16 = jnp.bfloat16
HIGHEST = lax.Precision.HIGHEST

V7X_VMEM_BYTES = 64 * 1024 * 1024
VMEM_LIMIT = V7X_VMEM_BYTES - 8 * 1024 * 1024
LANES = 128
SUBLANES = 8

GRID_W = 64
N_MOD = 6
EPS = 1e-6
NEG_INF = -1e30
ROPE_THETA = 10000.0
A_HEADS = 8
A_QK_DIM = 64
A_V_DIM = 128
B_Q_HEADS = 8
B_KV_HEADS = 2
B_GROUP = 4
B_HEAD_DIM = 128
WINDOW = 128
S5_GROUP = 16
S5_STATE = 64
S5_TC = 16
TOP_K = 4
SWIGLU_LIMIT = 7.0
SWIGLU_ALPHA = 1.702
EXPERT_BLOCK = 512
DOWN_BLOCK = 256
LOG2E = 1.4426950408889634


def _pick(n, cands):
    for c in cands:
        if n % c == 0:
            return c
    raise ValueError(f"no tile for {n} in {cands}")


def _cparams(*sem):
    return pltpu.CompilerParams(dimension_semantics=sem, vmem_limit_bytes=VMEM_LIMIT)


def _mod_kernel(c_ref, w_ref, b_ref, o_ref):
    cv = c_ref[...]
    s = cv * jax.nn.sigmoid(cv)
    o_ref[...] = jnp.dot(s, w_ref[...], preferred_element_type=F32, precision=HIGHEST) + b_ref[...]


def _adaln_mod(cs, w_mod, b_mod, layer):
    d, n = w_mod.shape[1], w_mod.shape[2]
    tn = _pick(n, (1024, 512, 256, 128))
    return pl.pallas_call(
        _mod_kernel,
        grid=(n // tn,),
        in_specs=[pl.BlockSpec((SUBLANES, d), lambda j: (0, 0)),
                  pl.BlockSpec((None, d, tn), lambda j: (layer, 0, j)),
                  pl.BlockSpec((None, 1, tn), lambda j: (layer, 0, j))],
        out_specs=pl.BlockSpec((SUBLANES, tn), lambda j: (0, j)),
        out_shape=jax.ShapeDtypeStruct((SUBLANES, n), F32),
        compiler_params=_cparams("parallel"),
        name="adaln_mod",
    )(cs, w_mod, b_mod)


def _norm_mod(x, g, sh2, sc2, row0, n_ctx):
    ms = jnp.mean(x * x, axis=-1, keepdims=True)
    y = x * lax.rsqrt(ms + EPS) * g
    row = row0 + lax.broadcasted_iota(jnp.int32, (x.shape[0], 1), 0)
    is_ctx = row < n_ctx
    sc = jnp.where(is_ctx, sc2[0:1, :], sc2[1:2, :])
    sh = jnp.where(is_ctx, sh2[0:1, :], sh2[1:2, :])
    return y * (1.0 + sc) + sh


ROPE_NONE = 4


def _rope_store(acc, cos_ref, sin_ref, o_ref, shift):
    cosv, sinv = cos_ref[...], sin_ref[...]
    for cgrp in range(acc.shape[1] // LANES):
        a = acc[:, cgrp * LANES:(cgrp + 1) * LANES]
        lane = lax.broadcasted_iota(jnp.int32, a.shape, 1)
        in_second = (lane & (2 * shift - 1)) >= shift
        sw = jnp.where(in_second, pltpu.roll(a, shift, 1), pltpu.roll(a, LANES - shift, 1))
        o_ref[:, cgrp * LANES:(cgrp + 1) * LANES] = (a * cosv + sw * sinv).astype(o_ref.dtype)


NORM_ROWS = 256


def _norm_mod_to_scratch(x_ref, g_ref, sh_ref, sc_ref, h_scr, row0, n_ctx):
    def body(r, carry):
        off = pl.multiple_of(r * NORM_ROWS, NORM_ROWS)
        h = _norm_mod(x_ref[pl.ds(off, NORM_ROWS), :], g_ref[...], sh_ref[...], sc_ref[...], row0 + off, n_ctx)
        h_scr[pl.ds(off, NORM_ROWS), :] = h.astype(h_scr.dtype)
        return carry

    lax.fori_loop(0, x_ref.shape[0] // NORM_ROWS, body, 0)


def _nmm_rope_kernel(tt_ref, x_ref, g_ref, sh_ref, sc_ref, w_ref, cos_ref, sin_ref, o_ref, h_scr,
                     *, n_ctx, tm):
    i = pl.program_id(0)
    j = pl.program_id(1)

    @pl.when(j == 0)
    def _():
        _norm_mod_to_scratch(x_ref, g_ref, sh_ref, sc_ref, h_scr, i * tm, n_ctx)

    acc = jnp.dot(h_scr[...], w_ref[...], preferred_element_type=F32)
    t = tt_ref[j]

    @pl.when(t == ROPE_NONE)
    def _():
        o_ref[...] = acc.astype(o_ref.dtype)

    @pl.when(t < 2)
    def _():
        _rope_store(acc, cos_ref, sin_ref, o_ref, A_QK_DIM // 4)

    @pl.when(jnp.logical_and(t >= 2, t < ROPE_NONE))
    def _():
        _rope_store(acc, cos_ref, sin_ref, o_ref, B_HEAD_DIM // 4)


def _nmm_plain_kernel(x_ref, g_ref, sh_ref, sc_ref, w_ref, o_ref, h_scr, *, n_ctx, tm):
    i = pl.program_id(0)
    j = pl.program_id(1)

    @pl.when(j == 0)
    def _():
        _norm_mod_to_scratch(x_ref, g_ref, sh_ref, sc_ref, h_scr, i * tm, n_ctx)

    o_ref[...] = jnp.dot(h_scr[...], w_ref[...], preferred_element_type=F32).astype(o_ref.dtype)


def _norm_mod_matmul(x, g, mod, sh_col, sc_col, w, n_ctx, out_dtype, rope=None):
    t_rows, d = x.shape
    n = w.shape[1]
    tm = _pick(t_rows, (1280, 1024, 512, 256, 128))
    tn = 256
    grid = (t_rows // tm, n // tn)
    kern_kw = dict(n_ctx=n_ctx, tm=tm)
    scratch = [pltpu.VMEM((tm, d), BF16)]
    out_shape = jax.ShapeDtypeStruct((t_rows, n), out_dtype)
    if rope is None:
        return pl.pallas_call(
            functools.partial(_nmm_plain_kernel, **kern_kw),
            grid=grid,
            in_specs=[pl.BlockSpec((tm, d), lambda i, j: (i, 0)),
                      pl.BlockSpec((1, d), lambda i, j: (0, 0)),
                      pl.BlockSpec((SUBLANES, d), lambda i, j: (0, sh_col)),
                      pl.BlockSpec((SUBLANES, d), lambda i, j: (0, sc_col)),
                      pl.BlockSpec((d, tn), lambda i, j: (0, j))],
            out_specs=pl.BlockSpec((tm, tn), lambda i, j: (i, j)),
            out_shape=out_shape,
            scratch_shapes=scratch,
            compiler_params=_cparams("parallel", "arbitrary"),
            name="norm_mod_matmul",
        )(x, g, mod, mod, w)
    ttype, cos_t, sin_t = rope
    return pl.pallas_call(
        functools.partial(_nmm_rope_kernel, **kern_kw),
        grid_spec=pltpu.PrefetchScalarGridSpec(
            num_scalar_prefetch=1,
            grid=grid,
            in_specs=[pl.BlockSpec((tm, d), lambda i, j, tt: (i, 0)),
                      pl.BlockSpec((1, d), lambda i, j, tt: (0, 0)),
                      pl.BlockSpec((SUBLANES, d), lambda i, j, tt: (0, sh_col)),
                      pl.BlockSpec((SUBLANES, d), lambda i, j, tt: (0, sc_col)),
                      pl.BlockSpec((d, tn), lambda i, j, tt: (0, j)),
                      pl.BlockSpec((None, tm, LANES), lambda i, j, tt: (tt[j], i, 0)),
                      pl.BlockSpec((None, tm, LANES), lambda i, j, tt: (tt[j], i, 0))],
            out_specs=pl.BlockSpec((tm, tn), lambda i, j, tt: (i, j)),
            scratch_shapes=scratch),
        out_shape=out_shape,
        compiler_params=_cparams("parallel", "arbitrary"),
        name="norm_mod_qkv_rope",
    )(ttype, x, g, mod, mod, w, cos_t, sin_t)


def _rope_tables(n_ctx, seq):
    pos = jnp.arange(seq)
    rows = (pos // GRID_W).astype(F32)
    cols = (pos % GRID_W).astype(F32)

    def tab(dim):
        quarter = dim // 4
        freqs = ROPE_THETA ** (-jnp.arange(quarter, dtype=F32) / quarter)
        ar, ac = rows[:, None] * freqs, cols[:, None] * freqs
        cosv = jnp.concatenate([jnp.cos(ar), jnp.cos(ar), jnp.cos(ac), jnp.cos(ac)], axis=1)
        sinv = jnp.concatenate([-jnp.sin(ar), jnp.sin(ar), -jnp.sin(ac), jnp.sin(ac)], axis=1)
        reps = LANES // dim
        cosv, sinv = jnp.tile(cosv, (1, reps)), jnp.tile(sinv, (1, reps))
        cosv = jnp.concatenate([jnp.ones((n_ctx, LANES), F32), cosv], axis=0)
        sinv = jnp.concatenate([jnp.zeros((n_ctx, LANES), F32), sinv], axis=0)
        return cosv, sinv

    ca, sa = tab(A_QK_DIM)
    cb, sb = tab(B_HEAD_DIM)
    qa = (A_QK_DIM ** -0.5) * LOG2E
    qb = B_HEAD_DIM ** -0.5
    one, zero = jnp.ones_like(ca), jnp.zeros_like(ca)
    cos_t = jnp.stack([ca * qa, ca, cb * qb, cb, one])
    sin_t = jnp.stack([sa * qa, sa, sb * qb, sb, zero])
    return cos_t, sin_t


V_AUG = A_V_DIM + 16


def _flash_kernel(lam_ref, qt_ref, k_ref, vt_ref, g_ref, o_ref, s_a, s_b, acc,
                  *, tq, tk, n_ctx, t_rows, out_scale):
    i = pl.program_id(1)
    qt = qt_ref[...]
    row = lax.broadcasted_iota(jnp.int32, qt.shape, 0)
    zero = jnp.zeros_like(qt)
    qm = (jnp.where(row < A_QK_DIM, qt, zero), jnp.where(row >= A_QK_DIM, qt, zero))

    def start(tile, size):
        return tile * size if isinstance(tile, int) else pl.multiple_of(tile * size, size)

    def scores(tile, size, dst):
        kt = k_ref[pl.ds(start(tile, size), size), :]
        for m in range(2):
            dst[m, 0:size, :] = jnp.dot(kt, qm[m], preferred_element_type=F32).astype(BF16)

    def soft_pv(tile, size, src, ms):
        vt = vt_ref[:, pl.ds(start(tile, size), size)]
        new_ms = []
        for m in range(2):
            s = src[m, 0:size, :]
            mx = jnp.maximum(ms[m], jnp.max(s, axis=0, keepdims=True).astype(F32))
            alpha = jnp.exp2(ms[m] - mx)
            p = jnp.exp2(s - mx.astype(BF16))
            acc[m] = alpha * acc[m] + jnp.dot(vt, p, preferred_element_type=F32)
            new_ms.append(mx)
        return tuple(new_ms)

    def finish():
        o1 = acc[0, 0:A_V_DIM, :] / acc[0, A_V_DIM:A_V_DIM + 1, :]
        o2 = acc[1, 0:A_V_DIM, :] / acc[1, A_V_DIM:A_V_DIM + 1, :]
        o = o1 - lam_ref[0] * o2
        var = jnp.mean(o * o, axis=0, keepdims=True)
        o = o * lax.rsqrt(var + EPS) * (g_ref[...] * out_scale)
        o_ref[...] = o.T.astype(o_ref.dtype)

    m0 = jnp.full((1, tq), NEG_INF, F32)
    n_q_ctx = n_ctx // tq

    @pl.when(i < n_q_ctx)
    def _():
        acc[...] = jnp.zeros_like(acc)
        scores(0, n_ctx, s_a)
        soft_pv(0, n_ctx, s_a, (m0, m0))
        finish()

    @pl.when(i >= n_q_ctx)
    def _():
        nk = t_rows // tk
        n_pairs = (nk - 1) // 2
        acc[...] = jnp.zeros_like(acc)
        scores(0, tk, s_a)

        def pair(pp, ms):
            scores(2 * pp + 1, tk, s_b)
            ms = soft_pv(2 * pp, tk, s_a, ms)
            scores(2 * pp + 2, tk, s_a)
            return soft_pv(2 * pp + 1, tk, s_b, ms)

        ms = lax.fori_loop(0, n_pairs, pair, (m0, m0))
        done = 2 * n_pairs
        if nk - done == 2:
            scores(done + 1, tk, s_b)
            ms = soft_pv(done, tk, s_a, ms)
            soft_pv(done + 1, tk, s_b, ms)
        else:
            soft_pv(done, tk, s_a, ms)
        finish()


def _diff_attention(qt, k, vt_aug, lam, subln_col, n_ctx, out_scale):
    t_rows = k.shape[0]
    tq = 256
    tk = _pick(t_rows, (1280, 1024, 512, 256))
    assert n_ctx <= tk
    return pl.pallas_call(
        functools.partial(_flash_kernel, tq=tq, tk=tk, n_ctx=n_ctx, t_rows=t_rows, out_scale=out_scale),
        grid=(A_HEADS, t_rows // tq),
        in_specs=[pl.BlockSpec(memory_space=pltpu.SMEM),
                  pl.BlockSpec((LANES, tq), lambda h, i: (h, i)),
                  pl.BlockSpec((t_rows, LANES), lambda h, i: (0, h)),
                  pl.BlockSpec((V_AUG, t_rows), lambda h, i: (h, 0)),
                  pl.BlockSpec((LANES, 1), lambda h, i: (0, 0))],
        out_specs=pl.BlockSpec((tq, LANES), lambda h, i: (i, h)),
        out_shape=jax.ShapeDtypeStruct((t_rows, A_HEADS * A_V_DIM), BF16),
        scratch_shapes=[pltpu.VMEM((2, tk, tq), BF16), pltpu.VMEM((2, tk, tq), BF16),
                        pltpu.VMEM((2, V_AUG, tq), F32)],
        compiler_params=_cparams("parallel", "arbitrary"),
        name="diff_attention",
    )(lam, qt, k, vt_aug, subln_col)


def _window_kernel(sink_ref, q_ref, kp_ref, ko_ref, kn_ref, vp_ref, vo_ref, vn_ref, kc_ref, vc_ref, o_ref,
                   *, nb, nb_ctx):
    g = pl.program_id(0)
    n = pl.program_id(1)
    qi = lax.broadcasted_iota(jnp.int32, (WINDOW, WINDOW), 0)
    kk = lax.broadcasted_iota(jnp.int32, (WINDOW, WINDOW), 1)
    own_ok = n >= nb_ctx
    prev_ok = n >= nb_ctx + 1
    next_ok = jnp.logical_and(own_ok, n <= nb - 2)
    m_prev = jnp.logical_and(kk >= qi, prev_ok)
    m_own = jnp.logical_and(kk >= 0, own_ok)
    m_next = jnp.logical_and(kk <= qi, next_ok)
    dn = (((1,), (1,)), ((), ()))
    for r in range(B_GROUP):
        q = q_ref[:, r * B_HEAD_DIM:(r + 1) * B_HEAD_DIM]
        s_p = jnp.where(m_prev, lax.dot_general(q, kp_ref[...], dn, preferred_element_type=F32), NEG_INF)
        s_o = jnp.where(m_own, lax.dot_general(q, ko_ref[...], dn, preferred_element_type=F32), NEG_INF)
        s_n = jnp.where(m_next, lax.dot_general(q, kn_ref[...], dn, preferred_element_type=F32), NEG_INF)
        s_c = lax.dot_general(q, kc_ref[...], dn, preferred_element_type=F32)
        sink = sink_ref[g * B_GROUP + r]
        mx = jnp.maximum(jnp.maximum(jnp.max(s_p, axis=1, keepdims=True), jnp.max(s_o, axis=1, keepdims=True)),
                         jnp.maximum(jnp.max(s_n, axis=1, keepdims=True), jnp.max(s_c, axis=1, keepdims=True)))
        mx = jnp.maximum(mx, sink)
        p_p, p_o, p_n, p_c = jnp.exp(s_p - mx), jnp.exp(s_o - mx), jnp.exp(s_n - mx), jnp.exp(s_c - mx)
        den = (jnp.sum(p_p, axis=1, keepdims=True) + jnp.sum(p_o, axis=1, keepdims=True)
               + jnp.sum(p_n, axis=1, keepdims=True) + jnp.sum(p_c, axis=1, keepdims=True)
               + jnp.exp(sink - mx))
        o = (jnp.dot(p_p.astype(BF16), vp_ref[...], preferred_element_type=F32)
             + jnp.dot(p_o.astype(BF16), vo_ref[...], preferred_element_type=F32)
             + jnp.dot(p_n.astype(BF16), vn_ref[...], preferred_element_type=F32)
             + jnp.dot(p_c.astype(BF16), vc_ref[...], preferred_element_type=F32))
        o_ref[:, r * B_HEAD_DIM:(r + 1) * B_HEAD_DIM] = (o / den).astype(o_ref.dtype)


def _window_attention(qkv, sink, n_ctx, col_q, col_k, col_v):
    t_rows = qkv.shape[0]
    nb = t_rows // WINDOW
    nb_ctx = n_ctx // WINDOW
    gq = B_GROUP * B_HEAD_DIM // LANES

    def kv_spec(col, shift):
        def imap(g, n):
            return (jnp.clip(n + shift, 0, nb - 1), col + g)
        return pl.BlockSpec((WINDOW, LANES), imap)

    return pl.pallas_call(
        functools.partial(_window_kernel, nb=nb, nb_ctx=nb_ctx),
        grid=(B_KV_HEADS, nb),
        in_specs=[pl.BlockSpec(memory_space=pltpu.SMEM),
                  pl.BlockSpec((WINDOW, B_GROUP * B_HEAD_DIM), lambda g, n: (n, col_q // gq + g)),
                  kv_spec(col_k, -1), kv_spec(col_k, 0), kv_spec(col_k, 1),
                  kv_spec(col_v, -1), kv_spec(col_v, 0), kv_spec(col_v, 1),
                  pl.BlockSpec((n_ctx, LANES), lambda g, n: (0, col_k + g)),
                  pl.BlockSpec((n_ctx, LANES), lambda g, n: (0, col_v + g))],
        out_specs=pl.BlockSpec((WINDOW, B_GROUP * B_HEAD_DIM), lambda g, n: (n, g)),
        out_shape=jax.ShapeDtypeStruct((t_rows, B_Q_HEADS * B_HEAD_DIM), BF16),
        compiler_params=_cparams("parallel", "arbitrary"),
        name="window_attention",
    )(sink, qkv, qkv, qkv, qkv, qkv, qkv, qkv, qkv, qkv)


ROW_PIECES = 8


def _store_packed_rows(ref, v):
    m, half = v.shape[0], v.shape[1] // 2
    lo = pltpu.bitcast(v[:, :half].astype(BF16).astype(F32), jnp.uint32) >> 16
    hi = pltpu.bitcast(v[:, half:].astype(BF16).astype(F32), jnp.uint32) & jnp.uint32(0xFFFF0000)
    w = lo | hi
    for j in range(ROW_PIECES):
        ref[pl.ds(j, m, stride=ROW_PIECES), :] = w[:, j * LANES:(j + 1) * LANES]


def _load_packed_rows(ref, m):
    w = jnp.concatenate([ref[pl.ds(j, m, stride=ROW_PIECES), :] for j in range(ROW_PIECES)], axis=1)
    lo = pltpu.bitcast(w << 16, F32)
    hi = pltpu.bitcast(w & jnp.uint32(0xFFFF0000), F32)
    return lo, hi


def _post_mixer(y, x, g1_ref, g_ref, sh_ref, sc_ref, wr_ref, br_ref, xo_ref, h_ref, lg_ref, row0, n_ctx):
    row = row0 + lax.broadcasted_iota(jnp.int32, (x.shape[0], 1), 0)
    g1 = jnp.where(row < n_ctx, g1_ref[0:1, :], g1_ref[1:2, :])
    xn = x + g1 * y
    xo_ref[...] = xn
    h = _norm_mod(xn, g_ref[...], sh_ref[...], sc_ref[...], row0, n_ctx)
    _store_packed_rows(h_ref, h)
    h_hi = h.astype(BF16)
    h_lo = (h - h_hi.astype(F32)).astype(BF16)
    lg_ref[...] = (jnp.dot(h_hi, wr_ref[0], preferred_element_type=F32)
                   + jnp.dot(h_lo, wr_ref[0], preferred_element_type=F32)
                   + jnp.dot(h_hi, wr_ref[1], preferred_element_type=F32) + br_ref[...])


def _attn_out_kernel(ya_ref, yb_ref, x_ref, woa_ref, wob_ref, g1_ref, g_ref, sh_ref, sc_ref, wr_ref, br_ref,
                     xo_ref, h_ref, lg_ref, *, n_ctx, tm):
    y = (jnp.dot(ya_ref[...], woa_ref[...], preferred_element_type=F32)
         + jnp.dot(yb_ref[...], wob_ref[...], preferred_element_type=F32))
    _post_mixer(y, x_ref[...], g1_ref, g_ref, sh_ref, sc_ref, wr_ref, br_ref, xo_ref, h_ref, lg_ref,
                pl.program_id(0) * tm, n_ctx)


def _glu_out_kernel(y_ref, x_ref, wv_ref, wg_ref, g1_ref, g_ref, sh_ref, sc_ref, wr_ref, br_ref,
                    xo_ref, h_ref, lg_ref, *, n_ctx, tm):
    a = jax.nn.gelu(y_ref[...], approximate=True).astype(BF16)
    val = jnp.dot(a, wv_ref[...], preferred_element_type=F32)
    gate = jnp.dot(a, wg_ref[...], preferred_element_type=F32)
    _post_mixer(val * jax.nn.sigmoid(gate), x_ref[...], g1_ref, g_ref, sh_ref, sc_ref, wr_ref, br_ref,
                xo_ref, h_ref, lg_ref, pl.program_id(0) * tm, n_ctx)


def _mixer_out(kernel_fn, acts, x, weights, mod, cols, g2row, wr, br, n_ctx, name):
    t_rows, d = x.shape
    tm = 256
    row = lambda i: (i, 0)
    const = lambda i: (0, 0)
    in_specs = ([pl.BlockSpec((tm, a.shape[1]), row) for a in acts]
                + [pl.BlockSpec((tm, d), row)]
                + [pl.BlockSpec(w.shape, const) for w in weights]
                + [pl.BlockSpec((SUBLANES, d), lambda i, c=c: (0, c)) for c in cols[:1]]
                + [pl.BlockSpec((1, d), const)]
                + [pl.BlockSpec((SUBLANES, d), lambda i, c=c: (0, c)) for c in cols[1:]]
                + [pl.BlockSpec(wr.shape, lambda i: (0, 0, 0)), pl.BlockSpec(br.shape, const)])
    return pl.pallas_call(
        functools.partial(kernel_fn, n_ctx=n_ctx, tm=tm),
        grid=(t_rows // tm,),
        in_specs=in_specs,
        out_specs=[pl.BlockSpec((tm, d), row), pl.BlockSpec((tm * ROW_PIECES, LANES), row),
                   pl.BlockSpec((tm, LANES), row)],
        out_shape=[jax.ShapeDtypeStruct((t_rows, d), F32),
                   jax.ShapeDtypeStruct((t_rows * ROW_PIECES, LANES), jnp.uint32),
                   jax.ShapeDtypeStruct((t_rows, LANES), F32)],
        compiler_params=_cparams("parallel"),
        name=name,
    )(*acts, x, *weights, mod, g2row, mod, mod, wr, br)


def _router_kernel(lg_ref, idx_ref, gate_ref, cnt_ref, *, n_exp):
    @pl.when(pl.program_id(0) == 0)
    def _():
        cnt_ref[...] = jnp.zeros_like(cnt_ref)

    lt = lg_ref[...].T[0:n_exp, :]
    eid = lax.broadcasted_iota(jnp.int32, lt.shape, 0).astype(F32)
    vals, idxs = [], []
    hist = jnp.zeros(lt.shape, F32)
    for _ in range(TOP_K):
        mv = jnp.max(lt, axis=0, keepdims=True)
        ix = jnp.min(jnp.where(lt == mv, eid, float(n_exp)), axis=0, keepdims=True)
        sel = eid == ix
        hist = hist + sel.astype(F32)
        lt = jnp.where(sel, -jnp.inf, lt)
        vals.append(mv)
        idxs.append(ix)
    es = [jnp.exp(v - vals[0]) for v in vals]
    den = es[0] + es[1] + es[2] + es[3]
    pad_f = jnp.zeros((SUBLANES - TOP_K, lt.shape[1]), F32)
    idx_ref[...] = jnp.concatenate(idxs + [pad_f], axis=0).astype(jnp.int32)
    gate_ref[...] = jnp.concatenate([e / den for e in es] + [pad_f], axis=0)
    cnt_ref[...] += jnp.sum(hist, axis=1, keepdims=True)


def _router(logits, n_exp):
    t_rows = logits.shape[0]
    tm = 256
    return pl.pallas_call(
        functools.partial(_router_kernel, n_exp=n_exp),
        grid=(t_rows // tm,),
        in_specs=[pl.BlockSpec((tm, LANES), lambda i: (i, 0))],
        out_specs=[pl.BlockSpec((SUBLANES, tm), lambda i: (0, i)),
                   pl.BlockSpec((SUBLANES, tm), lambda i: (0, i)),
                   pl.BlockSpec((n_exp, 1), lambda i: (0, 0))],
        out_shape=[jax.ShapeDtypeStruct((SUBLANES, t_rows), jnp.int32),
                   jax.ShapeDtypeStruct((SUBLANES, t_rows), F32),
                   jax.ShapeDtypeStruct((n_exp, 1), F32)],
        compiler_params=_cparams("arbitrary"),
        name="router_topk",
    )(logits)


def _dest_kernel(idx_ref, start_ref, dest_ref, carry, *, n_exp, tm):
    @pl.when(pl.program_id(0) == 0)
    def _():
        carry[...] = start_ref[...]

    eid = lax.broadcasted_iota(jnp.int32, (n_exp, tm), 0)
    idx = idx_ref[...]
    sels = [eid == idx[k:k + 1, :] for k in range(TOP_K)]
    total = sels[0].astype(F32) + sels[1].astype(F32) + sels[2].astype(F32) + sels[3].astype(F32)
    rr = lax.broadcasted_iota(jnp.int32, (tm, tm), 0)
    cc = lax.broadcasted_iota(jnp.int32, (tm, tm), 1)
    upper = jnp.where(rr < cc, 1.0, 0.0).astype(BF16)
    before = jnp.dot(total.astype(BF16), upper, preferred_element_type=F32) + carry[...]
    rows = [jnp.sum(jnp.where(sels[k], before, 0.0), axis=0, keepdims=True) for k in range(TOP_K)]
    pad = jnp.zeros((SUBLANES - TOP_K, tm), F32)
    dest_ref[...] = jnp.concatenate(rows + [pad], axis=0).astype(jnp.int32)
    carry[...] += jnp.sum(total, axis=1, keepdims=True)


def _dest_rows(idx, starts, n_exp):
    t_rows = idx.shape[1]
    tm = 256
    return pl.pallas_call(
        functools.partial(_dest_kernel, n_exp=n_exp, tm=tm),
        grid=(t_rows // tm,),
        in_specs=[pl.BlockSpec((SUBLANES, tm), lambda i: (0, i)),
                  pl.BlockSpec((n_exp, 1), lambda i: (0, 0))],
        out_specs=pl.BlockSpec((SUBLANES, tm), lambda i: (0, i)),
        out_shape=jax.ShapeDtypeStruct((SUBLANES, t_rows), jnp.int32),
        scratch_shapes=[pltpu.VMEM((n_exp, 1), F32)],
        compiler_params=_cparams("arbitrary"),
        name="moe_dest_rows",
    )(idx, starts)


def _expert_gu_kernel(be_ref, bv_ref, bn_ref, x_ref, wg_ref, wu_ref, bg_ref, bu_ref, o_ref, wg_s, wu_s, *, blk):
    b = pl.program_id(1)
    nvalid = bv_ref[b]

    @pl.when(bn_ref[b] == 1)
    def _():
        wg_s[...] = wg_ref[...].astype(BF16)
        wu_s[...] = wu_ref[...].astype(BF16)

    @pl.when(nvalid > 0)
    def _():
        rows = lax.broadcasted_iota(jnp.int32, (blk, 1), 0)
        lo, hi = _load_packed_rows(x_ref, blk)
        x = jnp.concatenate([lo.astype(BF16), hi.astype(BF16)], axis=1)
        x = jnp.where(rows < nvalid, x, jnp.zeros_like(x))
        gate = jnp.dot(x, wg_s[...], preferred_element_type=F32) + bg_ref[...]
        up = jnp.dot(x, wu_s[...], preferred_element_type=F32) + bu_ref[...]
        gate = jnp.minimum(gate, SWIGLU_LIMIT)
        up = jnp.clip(up, -SWIGLU_LIMIT, SWIGLU_LIMIT)
        act = (up + 1.0) * (gate * jax.nn.sigmoid(SWIGLU_ALPHA * gate))
        o_ref[...] = act.astype(o_ref.dtype)

    @pl.when(nvalid == 0)
    def _():
        o_ref[...] = jnp.zeros_like(o_ref)


def _expert_dn_kernel(be_ref, bv_ref, bn_ref, a_ref, wd_ref, bd_ref, o_ref, wd_s, *, blk):
    b = pl.program_id(0)

    @pl.when(bn_ref[b] == 1)
    def _():
        wd_s[...] = wd_ref[...].astype(BF16)

    @pl.when(bv_ref[b] > 0)
    def _():
        y = jnp.dot(a_ref[...], wd_s[...], preferred_element_type=F32) + bd_ref[...]
        _store_packed_rows(o_ref, y)

    @pl.when(bv_ref[b] == 0)
    def _():
        o_ref[...] = jnp.zeros_like(o_ref)


def _expert_matmul(xs, blk_e, blk_valid, blk_new, w_gu, b_gu, w_dn, b_dn, layer):
    d, f_dim = w_dn.shape[3], w_dn.shape[2]
    n_rows = xs.shape[0] // ROW_PIECES
    blk = EXPERT_BLOCK
    tf = 512
    nf = f_dim // tf
    n_blocks = n_rows // blk
    packed_rows = pl.BlockSpec((blk * ROW_PIECES, LANES), lambda f, b, be, bv, bn: (b, 0))
    act = pl.pallas_call(
        functools.partial(_expert_gu_kernel, blk=blk),
        grid_spec=pltpu.PrefetchScalarGridSpec(
            num_scalar_prefetch=3,
            grid=(nf, n_blocks),
            in_specs=[packed_rows,
                      pl.BlockSpec((None, None, d, tf), lambda f, b, be, bv, bn: (layer, be[b], 0, f)),
                      pl.BlockSpec((None, None, d, tf), lambda f, b, be, bv, bn: (layer, be[b], 0, nf + f)),
                      pl.BlockSpec((None, None, 1, tf), lambda f, b, be, bv, bn: (layer, be[b], 0, f)),
                      pl.BlockSpec((None, None, 1, tf), lambda f, b, be, bv, bn: (layer, be[b], 0, nf + f))],
            out_specs=pl.BlockSpec((blk, tf), lambda f, b, be, bv, bn: (b, f)),
            scratch_shapes=[pltpu.VMEM((d, tf), BF16), pltpu.VMEM((d, tf), BF16)]),
        out_shape=jax.ShapeDtypeStruct((n_rows, f_dim), BF16),
        compiler_params=_cparams("arbitrary", "arbitrary"),
        name="expert_gate_up",
    )(blk_e, blk_valid, blk_new, xs, w_gu, w_gu, b_gu, b_gu)
    assert d == 2 * ROW_PIECES * LANES
    sub = blk // DOWN_BLOCK
    part = jnp.arange(n_blocks * sub, dtype=jnp.int32) % sub
    dn_e = jnp.repeat(blk_e, sub)
    dn_valid = jnp.clip(jnp.repeat(blk_valid, sub) - part * DOWN_BLOCK, 0, DOWN_BLOCK)
    dn_new = jnp.where(part == 0, jnp.repeat(blk_new, sub), 0)
    return pl.pallas_call(
        functools.partial(_expert_dn_kernel, blk=DOWN_BLOCK),
        grid_spec=pltpu.PrefetchScalarGridSpec(
            num_scalar_prefetch=3,
            grid=(n_blocks * sub,),
            in_specs=[pl.BlockSpec((DOWN_BLOCK, f_dim), lambda b, be, bv, bn: (b, 0)),
                      pl.BlockSpec((None, None, f_dim, d), lambda b, be, bv, bn: (layer, be[b], 0, 0)),
                      pl.BlockSpec((None, None, 1, d), lambda b, be, bv, bn: (layer, be[b], 0, 0))],
            out_specs=pl.BlockSpec((DOWN_BLOCK * ROW_PIECES, LANES), lambda b, be, bv, bn: (b, 0)),
            scratch_shapes=[pltpu.VMEM((f_dim, d), BF16)]),
        out_shape=jax.ShapeDtypeStruct((n_rows * ROW_PIECES, LANES), jnp.uint32),
        compiler_params=_cparams("arbitrary"),
        name="expert_down",
    )(dn_e, dn_valid, dn_new, act, w_dn, b_dn)


def _combine_kernel(y0_ref, y1_ref, y2_ref, y3_ref, gate_ref, x_ref, g2_ref, fg_ref, o_ref, *, n_ctx, tm, final):
    gates = gate_ref[...]
    f_lo, f_hi = None, None
    for k, y_ref in enumerate((y0_ref, y1_ref, y2_ref, y3_ref)):
        lo, hi = _load_packed_rows(y_ref, tm)
        gk = gates[:, k:k + 1]
        f_lo = lo * gk if f_lo is None else f_lo + lo * gk
        f_hi = hi * gk if f_hi is None else f_hi + hi * gk
    f = jnp.concatenate([f_lo, f_hi], axis=1)
    row = pl.program_id(0) * tm + lax.broadcasted_iota(jnp.int32, (tm, 1), 0)
    g2 = jnp.where(row < n_ctx, g2_ref[0:1, :], g2_ref[1:2, :])
    xn = x_ref[...] + g2 * f
    if final:
        ms = jnp.mean(xn * xn, axis=-1, keepdims=True)
        xn = xn * lax.rsqrt(ms + EPS) * fg_ref[...]
    o_ref[...] = xn


def _combine(yk, gates_t, x, mod, g2_col, final_g, n_ctx, final):
    t_rows, d = x.shape
    tm = 256
    nt = t_rows // tm
    return pl.pallas_call(
        functools.partial(_combine_kernel, n_ctx=n_ctx, tm=tm, final=final),
        grid=(nt,),
        in_specs=[pl.BlockSpec((tm * ROW_PIECES, LANES), lambda i, k=k: (k * nt + i, 0)) for k in range(TOP_K)]
                 + [pl.BlockSpec((tm, SUBLANES), lambda i: (i, 0)),
                  pl.BlockSpec((tm, d), lambda i: (i, 0)),
                  pl.BlockSpec((SUBLANES, d), lambda i: (0, g2_col)),
                  pl.BlockSpec((1, d), lambda i: (0, 0))],
        out_specs=pl.BlockSpec((tm, d), lambda i: (i, 0)),
        out_shape=jax.ShapeDtypeStruct((t_rows, d), F32),
        compiler_params=_cparams("parallel"),
        name="moe_combine",
    )(yk, yk, yk, yk, gates_t, x, mod, final_g)


SC_WINDOW = 128


def _sc_mesh():
    return plsc.VectorSubcoreMesh(core_axis_name="core", subcore_axis_name="subcore")


def _sc_scatter_rows(x, dest, n_out):
    n_src_blocks = x.shape[0] // SC_WINDOW

    @functools.partial(pl.kernel, out_type=jax.ShapeDtypeStruct((n_out, LANES), x.dtype), mesh=_sc_mesh(),
                       scratch_types=[], name="sc_dispatch_rows")
    def run(x_hbm, i_hbm, o_hbm):
        def body(x_vmem, i_vmem):
            pltpu.sync_copy(x_vmem, o_hbm.at[i_vmem.at[0]])

        pltpu.emit_pipeline(
            body, grid=(dest.shape[1] // SC_WINDOW,),
            in_specs=[pl.BlockSpec((SC_WINDOW, LANES), lambda i: (i % n_src_blocks, 0)),
                      pl.BlockSpec((1, SC_WINDOW), lambda i: (0, i))],
            out_specs=[], core_axis_name=("core", "subcore"),
            dimension_semantics=(pltpu.PARALLEL,))(x_hbm, i_hbm)

    return run(x, dest)


def _sc_gather_rows(y, idx):
    n = idx.shape[1]

    @functools.partial(pl.kernel, out_type=jax.ShapeDtypeStruct((n, LANES), y.dtype), mesh=_sc_mesh(),
                       scratch_types=[], name="sc_combine_rows")
    def run(y_hbm, i_hbm, o_hbm):
        def body(i_vmem, o_vmem):
            pltpu.sync_copy(y_hbm.at[i_vmem.at[0]], o_vmem)

        pltpu.emit_pipeline(
            body, grid=(n // SC_WINDOW,),
            in_specs=[pl.BlockSpec((1, SC_WINDOW), lambda i: (0, i))],
            out_specs=[pl.BlockSpec((SC_WINDOW, LANES), lambda i: (i, 0))],
            core_axis_name=("core", "subcore"),
            dimension_semantics=(pltpu.PARALLEL,))(i_hbm, o_hbm)

    return run(y, idx)


def _moe(h, logits, x, mod, g2_col, w_gu, b_gu, w_dn, b_dn, layer, n_ctx, final_g, final):
    t_rows = x.shape[0]
    n_exp = w_gu.shape[1]
    idx, gates, counts = _router(logits, n_exp)
    counts = counts[:, 0].astype(jnp.int32)
    blk = EXPERT_BLOCK
    padded = (counts + blk - 1) // blk * blk
    pends = jnp.cumsum(padded)
    pstarts = pends - padded
    dest = _dest_rows(idx, pstarts.astype(F32)[:, None], n_exp)[:TOP_K]
    n_blocks = -(-(t_rows * TOP_K) // blk) + n_exp
    n_rows = n_blocks * blk
    blk_start = jnp.arange(n_blocks, dtype=jnp.int32) * blk
    blk_e = jnp.minimum(jnp.sum(blk_start[:, None] >= pends[None, :], axis=1), n_exp - 1).astype(jnp.int32)
    blk_valid = jnp.clip(pstarts[blk_e] + counts[blk_e] - blk_start, 0, blk).astype(jnp.int32)
    blk_valid = jnp.where(blk_start < pends[-1], blk_valid, 0)
    blk_new = jnp.concatenate([jnp.ones((1,), jnp.int32), (blk_e[1:] != blk_e[:-1]).astype(jnp.int32)])
    dest8 = (dest[:, :, None] * ROW_PIECES + jnp.arange(ROW_PIECES, dtype=jnp.int32)).reshape(1, -1)
    xs = _sc_scatter_rows(h, dest8, n_rows * ROW_PIECES)
    ys = _expert_matmul(xs, blk_e, blk_valid, blk_new, w_gu, b_gu, w_dn, b_dn, layer)
    yk = _sc_gather_rows(ys, dest8)
    return _combine(yk, gates.T, x, mod, g2_col, final_g, n_ctx, final)


def _s5_tables(a_re, a_im, log_step, b_re, b_im, c_re, c_im, d_skip):
    tc = S5_TC
    n_grp, n_st = a_re.shape[1], a_re.shape[2]
    lr = jnp.minimum(a_re.astype(F32), -1e-4)
    li = a_im.astype(F32)
    dt = jnp.exp(log_step.astype(F32))[..., None]
    dd = jnp.arange(tc + 1, dtype=F32)[:, None, None, None]
    mag = jnp.exp(lr * dt * dd)
    pr, pi = mag * jnp.cos(li * dt * dd), mag * jnp.sin(li * dt * dd)
    ar, ai = pr[1], pi[1]
    den = lr * lr + li * li
    nr = ar - 1.0
    zr = (nr * lr + ai * li) / den
    zi = (ai * lr - nr * li) / den
    br, bi = b_re.astype(F32), b_im.astype(F32)
    bbr = zr[..., None] * br - zi[..., None] * bi
    bbi = zr[..., None] * bi + zi[..., None] * br
    abr = pr[:tc, ..., None] * bbr - pi[:tc, ..., None] * bbi
    abi = pr[:tc, ..., None] * bbi + pi[:tc, ..., None] * bbr
    cr, ci = c_re.astype(F32), c_im.astype(F32)
    kern = (jnp.einsum('xgip,dxgpj->dxgij', cr, abr, precision=HIGHEST)
            - jnp.einsum('xgip,dxgpj->dxgij', ci, abi, precision=HIGHEST))
    def toeplitz(kx):
        ext = jnp.concatenate([jnp.zeros_like(kx), kx], axis=0)
        return jnp.stack([ext[tc - a:2 * tc - a] for a in range(tc)], axis=0)

    m_f = toeplitz(kern[:, 0])
    m_b = toeplitz(kern[:, 1]).transpose(1, 0, 2, 3, 4)
    m_tot = (m_f + m_b).transpose(2, 0, 4, 1, 3)
    eye_t = jnp.eye(tc, dtype=F32)
    eye_i = jnp.eye(S5_GROUP, dtype=F32)
    dsk = d_skip.astype(F32).reshape(n_grp, S5_GROUP)
    m_tot = m_tot + (eye_t[None, :, None, :, None] * eye_i[None, None, :, None, :]
                     * dsk[:, None, :, None, None])
    m_tot = m_tot.reshape(n_grp, tc * S5_GROUP, tc * S5_GROUP)
    pw_f = tc - 1 - jnp.arange(tc)
    pw_b = jnp.arange(tc)

    def b_cols(part, pw, x):
        return part[pw, x].transpose(1, 0, 3, 2).reshape(n_grp, tc * S5_GROUP, n_st)

    b_mat = jnp.concatenate([b_cols(abr, pw_f, 0), b_cols(abi, pw_f, 0),
                             b_cols(abr, pw_b, 1), b_cols(abi, pw_b, 1)], axis=-1)
    pcf = 1 + jnp.arange(tc)
    pcb = tc - jnp.arange(tc)

    def c_rows(pw, x):
        prx, pix = pr[pw, x], pi[pw, x]
        re_c = cr[x][None] * prx[:, :, None, :] - ci[x][None] * pix[:, :, None, :]
        im_c = -(cr[x][None] * pix[:, :, None, :] + ci[x][None] * prx[:, :, None, :])
        to_rows = lambda z: z.transpose(1, 3, 0, 2).reshape(n_grp, n_st, tc * S5_GROUP)
        return to_rows(re_c), to_rows(im_c)

    c_mat = jnp.concatenate(list(c_rows(pcf, 0)) + list(c_rows(pcb, 1)), axis=1)
    prt, pit = pr[tc], pi[tc]
    a1 = jnp.concatenate([prt[0], prt[0], prt[1], prt[1]], axis=-1)
    a2 = jnp.concatenate([-pit[0], pit[0], -pit[1], pit[1]], axis=-1)
    return m_tot, b_mat, c_mat, a1, a2


S5_LANE_GROUPS = LANES // S5_GROUP


def _s5_pack(u_ref, c0, cc):
    xs = [u_ref[pl.ds(c0 * S5_TC + tau, cc, stride=S5_TC), :] for tau in range(S5_TC)]
    return [jnp.concatenate([xs[tau][:, g * S5_GROUP:(g + 1) * S5_GROUP] for tau in range(S5_TC)], axis=1)
            for g in range(S5_LANE_GROUPS)]


def _s5_in_kernel(u_ref, b_ref, o_ref, *, cc):
    def body(ci, carry):
        c0 = pl.multiple_of(ci * cc, SUBLANES)
        ugs = _s5_pack(u_ref, c0, cc)
        for g in range(S5_LANE_GROUPS):
            o_ref[g, pl.ds(c0, cc), :] = jnp.dot(ugs[g], b_ref[g], preferred_element_type=F32, precision=HIGHEST)
        return carry

    lax.fori_loop(0, u_ref.shape[0] // (S5_TC * cc), body, 0)


def _s5_out_kernel(u_ref, z_ref, m_ref, c_ref, o_ref, *, cc):
    def body(ci, carry):
        c0 = pl.multiple_of(ci * cc, SUBLANES)
        ugs = _s5_pack(u_ref, c0, cc)
        ys = [jnp.dot(ugs[g], m_ref[g], preferred_element_type=F32, precision=HIGHEST)
              + jnp.dot(z_ref[g, pl.ds(c0, cc), :], c_ref[g], preferred_element_type=F32, precision=HIGHEST)
              for g in range(S5_LANE_GROUPS)]
        for tau in range(S5_TC):
            row = jnp.concatenate([y[:, tau * S5_GROUP:(tau + 1) * S5_GROUP] for y in ys], axis=1)
            o_ref[pl.ds(c0 * S5_TC + tau, cc, stride=S5_TC), :] = row
        return carry

    lax.fori_loop(0, u_ref.shape[0] // (S5_TC * cc), body, 0)


def _s5_scan_kernel(h_ref, a1_ref, a2_ref, z_ref, *, n_chunks, n_ctx_chunks):
    half = 2 * S5_STATE
    a1 = a1_ref[...]
    a2 = a2_ref[...]
    a1f, a1b = a1[:, :half], a1[:, half:]
    a2f, a2b = a2[:, :half], a2[:, half:]
    zero = jnp.zeros((a1.shape[0], half), F32)

    def swap(v):
        return pltpu.roll(v, S5_STATE, 1)

    def step(t, carry):
        rf, rfs, rb, rbs = carry
        cb = jnp.where(t < n_ctx_chunks, n_ctx_chunks - 1 - t, n_chunks - 1 - (t - n_ctx_chunks))
        z_ref[t, :, 0:half] = rf
        z_ref[cb, :, half:2 * half] = rb
        hf = h_ref[t, :, 0:half]
        hb = h_ref[cb, :, half:2 * half]
        nrf = a1f * rf + a2f * rfs + hf
        nrfs = a1f * rfs - a2f * rf + swap(hf)
        nrb = a1b * rb + a2b * rbs + hb
        nrbs = a1b * rbs - a2b * rb + swap(hb)
        return nrf, nrfs, nrb, nrbs

    lax.fori_loop(0, n_chunks, step, (zero, zero, zero, zero), unroll=8 if n_chunks % 8 == 0 else 1)


def _s5_mixer_scan(u, tables, n_ctx):
    m_tot, b_mat, c_mat, a1, a2 = tables
    t_rows, width = u.shape
    tc = S5_TC
    n_grp = width // S5_GROUP
    n_chunks = t_rows // tc
    kdim = tc * S5_GROUP
    sdim = 4 * S5_STATE
    gb = S5_LANE_GROUPS
    gs = SUBLANES
    n_split = 2
    half_chunks = n_chunks // n_split
    cc = _pick(half_chunks, (104, 80, 40, 8))
    u_spec = pl.BlockSpec((t_rows // n_split, LANES), lambda j, r: (r, j))
    tab_spec = lambda a, b: pl.BlockSpec((gb, a, b), lambda j, r: (j, 0, 0))
    st_spec = pl.BlockSpec((gb, half_chunks, sdim), lambda j, r: (j, r, 0))
    grid = (width // LANES, n_split)
    hin = pl.pallas_call(
        functools.partial(_s5_in_kernel, cc=cc),
        grid=grid,
        in_specs=[u_spec, tab_spec(kdim, sdim)],
        out_specs=st_spec,
        out_shape=jax.ShapeDtypeStruct((n_grp, n_chunks, sdim), F32),
        compiler_params=_cparams("parallel", "arbitrary"),
        name="s5_chunk_inputs",
    )(u, b_mat)
    hin_t = hin.transpose(1, 0, 2)
    z_t = pl.pallas_call(
        functools.partial(_s5_scan_kernel, n_chunks=n_chunks, n_ctx_chunks=n_ctx // tc),
        grid=(n_grp // gs,),
        in_specs=[pl.BlockSpec((n_chunks, gs, sdim), lambda g: (0, g, 0)),
                  pl.BlockSpec((gs, sdim), lambda g: (g, 0)),
                  pl.BlockSpec((gs, sdim), lambda g: (g, 0))],
        out_specs=pl.BlockSpec((n_chunks, gs, sdim), lambda g: (0, g, 0)),
        out_shape=jax.ShapeDtypeStruct((n_chunks, n_grp, sdim), F32),
        compiler_params=_cparams("parallel"),
        name="s5_chunk_scan",
    )(hin_t, a1, a2)
    z = z_t.transpose(1, 0, 2)
    return pl.pallas_call(
        functools.partial(_s5_out_kernel, cc=cc),
        grid=grid,
        in_specs=[u_spec, st_spec, tab_spec(kdim, kdim), tab_spec(sdim, kdim)],
        out_specs=u_spec,
        out_shape=jax.ShapeDtypeStruct((t_rows, width), F32),
        compiler_params=_cparams("parallel", "arbitrary"),
        name="s5_chunk_outputs",
    )(u, z, m_tot, c_mat)


def _lambda_init(layer):
    return 0.8 - 0.6 * math.exp(-0.3 * layer)


def kernel(x, c, ctx, c_ctx, w_mod, b_mod, norm1_g, norm2_g, final_g, attn_w_qkv, attn_w_o, lambda_q1, lambda_k1, lambda_q2, lambda_k2, subln_g, sink_logit, s5_w_in, s5_a_re, s5_a_im, s5_log_step, s5_b_re, s5_b_im, s5_c_re, s5_c_im, s5_d, s5_w_glu, router_w, router_b, expert_w_gu, expert_b_gu, expert_w_down, expert_b_down):
    bsz, seq, d = x.shape
    assert bsz == 1, "single-sequence block"
    n_ctx = ctx.shape[1]
    depth = w_mod.shape[0]
    n_exp = router_w.shape[2]
    assert n_ctx % 256 == 0 and seq % 256 == 0

    xj = jnp.concatenate([ctx[0], x[0]], axis=0)
    cs = jnp.zeros((SUBLANES, d), F32).at[0].set(c_ctx).at[1].set(c[0])
    b_mod3 = b_mod[:, None, :]
    b_gu4 = expert_b_gu[:, :, None, :]
    b_dn4 = expert_b_down[:, :, None, :]
    wr_pad = jnp.pad(router_w, ((0, 0), (0, 0), (0, LANES - n_exp)))
    wr_hi = wr_pad.astype(BF16)
    wr_pad = jnp.stack([wr_hi, (wr_pad - wr_hi.astype(F32)).astype(BF16)], axis=1)
    br_pad = jnp.pad(router_b, ((0, 0), (0, LANES - n_exp)), constant_values=NEG_INF)[:, None, :]
    final_row = final_g[None, :]

    for i in range(depth):
        last = i == depth - 1
        j = i // 2
        mod = _adaln_mod(cs, w_mod, b_mod3, i)
        if i % 2 == 0:
            sizes = (A_HEADS * 2 * A_QK_DIM, A_HEADS * 2 * A_QK_DIM, A_HEADS * A_V_DIM,
                     B_Q_HEADS * B_HEAD_DIM, B_KV_HEADS * B_HEAD_DIM, B_KV_HEADS * B_HEAD_DIM)
            offs = np.concatenate([[0], np.cumsum(sizes)])
            types = [0, 1, ROPE_NONE, 2, 3, ROPE_NONE]
            ttype = jnp.asarray(np.concatenate([np.full(s // 256, t) for s, t in zip(sizes, types)]), jnp.int32)
            cos_t, sin_t = _rope_tables(n_ctx, seq)
            qkv = _norm_mod_matmul(xj, norm1_g[i][None, :], mod, 0, 1, attn_w_qkv[j].astype(BF16), n_ctx, BF16,
                                   rope=(ttype, cos_t, sin_t))
            qt = qkv[:, offs[0]:offs[1]].T
            ka = qkv[:, offs[1]:offs[2]]
            vt = qkv[:, offs[2]:offs[3]].T.reshape(A_HEADS, A_V_DIM, -1)
            vt = jnp.concatenate([vt, jnp.ones((A_HEADS, V_AUG - A_V_DIM, vt.shape[2]), BF16)], axis=1)
            vt = vt.reshape(A_HEADS * V_AUG, -1)
            f32 = F32
            li = _lambda_init(i)
            lam = (jnp.exp(jnp.sum(lambda_q1[j].astype(f32) * lambda_k1[j].astype(f32)))
                   - jnp.exp(jnp.sum(lambda_q2[j].astype(f32) * lambda_k2[j].astype(f32))) + li)
            ya = _diff_attention(qt, ka, vt, lam.reshape(1), subln_g[j][:, None], n_ctx, 1.0 - li)
            yb = _window_attention(qkv, sink_logit[j], n_ctx, int(offs[3]) // LANES, int(offs[4]) // LANES,
                                   int(offs[5]) // LANES)
            w_o = attn_w_o[j].astype(BF16)
            na = A_HEADS * A_V_DIM
            xj, h2, logits = _mixer_out(_attn_out_kernel, [ya, yb], xj, [w_o[:na], w_o[na:]], mod, (2, 3, 4),
                                        norm2_g[i][None, :], wr_pad[i], br_pad[i], n_ctx, "attn_out_router")
        else:
            u = _norm_mod_matmul(xj, norm1_g[i][None, :], mod, 0, 1, s5_w_in[j].astype(BF16), n_ctx, F32)
            tables = _s5_tables(s5_a_re[j], s5_a_im[j], s5_log_step[j], s5_b_re[j], s5_b_im[j],
                                s5_c_re[j], s5_c_im[j], s5_d[j])
            y = _s5_mixer_scan(u, tables, n_ctx)
            w_glu = s5_w_glu[j].astype(BF16)
            xj, h2, logits = _mixer_out(_glu_out_kernel, [y], xj, [w_glu[:, :d], w_glu[:, d:]], mod, (2, 3, 4),
                                        norm2_g[i][None, :], wr_pad[i], br_pad[i], n_ctx, "glu_out_router")
        xj = _moe(h2, logits, xj, mod, 5, expert_w_gu, b_gu4, expert_w_down, b_dn4, i, n_ctx, final_row, last)
    return xj[n_ctx:][None]
```

```python
import functools
import math

import jax
import jax.numpy as jnp
import numpy as np
from jax import lax
from jax.experimental import pallas as pl
from jax.experimental.pallas import tpu as pltpu
from jax.experimental.pallas import tpu_sc as plsc

F32 = jnp.float32
BF16 = jnp.bfloat16
HIGHEST = lax.Precision.HIGHEST

V7X_VMEM_BYTES = 64 * 1024 * 1024
VMEM_LIMIT = V7X_VMEM_BYTES - 8 * 1024 * 1024
LANES = 128
SUBLANES = 8

GRID_W = 64
N_MOD = 6
EPS = 1e-6
NEG_INF = -1e30
ROPE_THETA = 10000.0
A_HEADS = 8
A_QK_DIM = 64
A_V_DIM = 128
B_Q_HEADS = 8
B_KV_HEADS = 2
B_GROUP = 4
B_HEAD_DIM = 128
WINDOW = 128
S5_GROUP = 16
S5_STATE = 64
S5_TC = 16
TOP_K = 4
SWIGLU_LIMIT = 7.0
SWIGLU_ALPHA = 1.702
EXPERT_BLOCK = 512
DOWN_BLOCK = 256
LOG2E = 1.4426950408889634


def _pick(n, cands):
    for c in cands:
        if n % c == 0:
            return c
    raise ValueError(f"no tile for {n} in {cands}")


def _cparams(*sem):
    return pltpu.CompilerParams(dimension_semantics=sem, vmem_limit_bytes=VMEM_LIMIT)


def _mod_kernel(c_ref, w_ref, b_ref, o_ref):
    cv = c_ref[...]
    s = cv * jax.nn.sigmoid(cv)
    o_ref[...] = jnp.dot(s, w_ref[...], preferred_element_type=F32, precision=HIGHEST) + b_ref[...]


def _adaln_mod(cs, w_mod, b_mod, layer):
    d, n = w_mod.shape[1], w_mod.shape[2]
    tn = _pick(n, (1024, 512, 256, 128))
    return pl.pallas_call(
        _mod_kernel,
        grid=(n // tn,),
        in_specs=[pl.BlockSpec((SUBLANES, d), lambda j: (0, 0)),
                  pl.BlockSpec((None, d, tn), lambda j: (layer, 0, j)),
                  pl.BlockSpec((None, 1, tn), lambda j: (layer, 0, j))],
        out_specs=pl.BlockSpec((SUBLANES, tn), lambda j: (0, j)),
        out_shape=jax.ShapeDtypeStruct((SUBLANES, n), F32),
        compiler_params=_cparams("parallel"),
        name="adaln_mod",
    )(cs, w_mod, b_mod)


def _norm_mod(x, g, sh2, sc2, row0, n_ctx):
    ms = jnp.mean(x * x, axis=-1, keepdims=True)
    y = x * lax.rsqrt(ms + EPS) * g
    row = row0 + lax.broadcasted_iota(jnp.int32, (x.shape[0], 1), 0)
    is_ctx = row < n_ctx
    sc = jnp.where(is_ctx, sc2[0:1, :], sc2[1:2, :])
    sh = jnp.where(is_ctx, sh2[0:1, :], sh2[1:2, :])
    return y * (1.0 + sc) + sh


ROPE_NONE = 4


def _rope_store(acc, cos_ref, sin_ref, o_ref, shift):
    cosv, sinv = cos_ref[...], sin_ref[...]
    for cgrp in range(acc.shape[1] // LANES):
        a = acc[:, cgrp * LANES:(cgrp + 1) * LANES]
        lane = lax.broadcasted_iota(jnp.int32, a.shape, 1)
        in_second = (lane & (2 * shift - 1)) >= shift
        sw = jnp.where(in_second, pltpu.roll(a, shift, 1), pltpu.roll(a, LANES - shift, 1))
        o_ref[:, cgrp * LANES:(cgrp + 1) * LANES] = (a * cosv + sw * sinv).astype(o_ref.dtype)


NORM_ROWS = 256


def _norm_mod_to_scratch(x_ref, g_ref, sh_ref, sc_ref, h_scr, row0, n_ctx):
    def body(r, carry):
        off = pl.multiple_of(r * NORM_ROWS, NORM_ROWS)
        h = _norm_mod(x_ref[pl.ds(off, NORM_ROWS), :], g_ref[...], sh_ref[...], sc_ref[...], row0 + off, n_ctx)
        h_scr[pl.ds(off, NORM_ROWS), :] = h.astype(h_scr.dtype)
        return carry

    lax.fori_loop(0, x_ref.shape[0] // NORM_ROWS, body, 0)


def _nmm_rope_kernel(tt_ref, x_ref, g_ref, sh_ref, sc_ref, w_ref, cos_ref, sin_ref, o_ref, h_scr,
                     *, n_ctx, tm):
    i = pl.program_id(0)
    j = pl.program_id(1)

    @pl.when(j == 0)
    def _():
        _norm_mod_to_scratch(x_ref, g_ref, sh_ref, sc_ref, h_scr, i * tm, n_ctx)

    acc = jnp.dot(h_scr[...], w_ref[...], preferred_element_type=F32)
    t = tt_ref[j]

    @pl.when(t == ROPE_NONE)
    def _():
        o_ref[...] = acc.astype(o_ref.dtype)

    @pl.when(t < 2)
    def _():
        _rope_store(acc, cos_ref, sin_ref, o_ref, A_QK_DIM // 4)

    @pl.when(jnp.logical_and(t >= 2, t < ROPE_NONE))
    def _():
        _rope_store(acc, cos_ref, sin_ref, o_ref, B_HEAD_DIM // 4)


def _nmm_plain_kernel(x_ref, g_ref, sh_ref, sc_ref, w_ref, o_ref, h_scr, *, n_ctx, tm):
    i = pl.program_id(0)
    j = pl.program_id(1)

    @pl.when(j == 0)
    def _():
        _norm_mod_to_scratch(x_ref, g_ref, sh_ref, sc_ref, h_scr, i * tm, n_ctx)

    o_ref[...] = jnp.dot(h_scr[...], w_ref[...], preferred_element_type=F32).astype(o_ref.dtype)


def _norm_mod_matmul(x, g, mod, sh_col, sc_col, w, n_ctx, out_dtype, rope=None):
    t_rows, d = x.shape
    n = w.shape[1]
    tm = _pick(t_rows, (1280, 1024, 512, 256, 128))
    tn = 256
    grid = (t_rows // tm, n // tn)
    kern_kw = dict(n_ctx=n_ctx, tm=tm)
    scratch = [pltpu.VMEM((tm, d), BF16)]
    out_shape = jax.ShapeDtypeStruct((t_rows, n), out_dtype)
    if rope is None:
        return pl.pallas_call(
            functools.partial(_nmm_plain_kernel, **kern_kw),
            grid=grid,
            in_specs=[pl.BlockSpec((tm, d), lambda i, j: (i, 0)),
                      pl.BlockSpec((1, d), lambda i, j: (0, 0)),
                      pl.BlockSpec((SUBLANES, d), lambda i, j: (0, sh_col)),
                      pl.BlockSpec((SUBLANES, d), lambda i, j: (0, sc_col)),
                      pl.BlockSpec((d, tn), lambda i, j: (0, j))],
            out_specs=pl.BlockSpec((tm, tn), lambda i, j: (i, j)),
            out_shape=out_shape,
            scratch_shapes=scratch,
            compiler_params=_cparams("parallel", "arbitrary"),
            name="norm_mod_matmul",
        )(x, g, mod, mod, w)
    ttype, cos_t, sin_t = rope
    return pl.pallas_call(
        functools.partial(_nmm_rope_kernel, **kern_kw),
        grid_spec=pltpu.PrefetchScalarGridSpec(
            num_scalar_prefetch=1,
            grid=grid,
            in_specs=[pl.BlockSpec((tm, d), lambda i, j, tt: (i, 0)),
                      pl.BlockSpec((1, d), lambda i, j, tt: (0, 0)),
                      pl.BlockSpec((SUBLANES, d), lambda i, j, tt: (0, sh_col)),
                      pl.BlockSpec((SUBLANES, d), lambda i, j, tt: (0, sc_col)),
                      pl.BlockSpec((d, tn), lambda i, j, tt: (0, j)),
                      pl.BlockSpec((None, tm, LANES), lambda i, j, tt: (tt[j], i, 0)),
                      pl.BlockSpec((None, tm, LANES), lambda i, j, tt: (tt[j], i, 0))],
            out_specs=pl.BlockSpec((tm, tn), lambda i, j, tt: (i, j)),
            scratch_shapes=scratch),
        out_shape=out_shape,
        compiler_params=_cparams("parallel", "arbitrary"),
        name="norm_mod_qkv_rope",
    )(ttype, x, g, mod, mod, w, cos_t, sin_t)


def _rope_tables(n_ctx, seq):
    pos = jnp.arange(seq)
    rows = (pos // GRID_W).astype(F32)
    cols = (pos % GRID_W).astype(F32)

    def tab(dim):
        quarter = dim // 4
        freqs = ROPE_THETA ** (-jnp.arange(quarter, dtype=F32) / quarter)
        ar, ac = rows[:, None] * freqs, cols[:, None] * freqs
        cosv = jnp.concatenate([jnp.cos(ar), jnp.cos(ar), jnp.cos(ac), jnp.cos(ac)], axis=1)
        sinv = jnp.concatenate([-jnp.sin(ar), jnp.sin(ar), -jnp.sin(ac), jnp.sin(ac)], axis=1)
        reps = LANES // dim
        cosv, sinv = jnp.tile(cosv, (1, reps)), jnp.tile(sinv, (1, reps))
        cosv = jnp.concatenate([jnp.ones((n_ctx, LANES), F32), cosv], axis=0)
        sinv = jnp.concatenate([jnp.zeros((n_ctx, LANES), F32), sinv], axis=0)
        return cosv, sinv

    ca, sa = tab(A_QK_DIM)
    cb, sb = tab(B_HEAD_DIM)
    qa = (A_QK_DIM ** -0.5) * LOG2E
    qb = B_HEAD_DIM ** -0.5
    one, zero = jnp.ones_like(ca), jnp.zeros_like(ca)
    cos_t = jnp.stack([ca * qa, ca, cb * qb, cb, one])
    sin_t = jnp.stack([sa * qa, sa, sb * qb, sb, zero])
    return cos_t, sin_t


def _flash_kernel(lam_ref, qt_ref, k_ref, vt_ref, g_ref, o_ref, s_a, s_b, acc,
                  *, tq, tk, n_ctx, t_rows, out_scale):
    i = pl.program_id(1)
    qt = qt_ref[...]
    row = lax.broadcasted_iota(jnp.int32, qt.shape, 0)
    zero = jnp.zeros_like(qt)
    qm = (jnp.where(row < A_QK_DIM, qt, zero), jnp.where(row >= A_QK_DIM, qt, zero))

    def start(tile, size):
        return tile * size if isinstance(tile, int) else pl.multiple_of(tile * size, size)

    def scores(tile, size, dst):
        kt = k_ref[pl.ds(start(tile, size), size), :]
        for m in range(2):
            dst[m, 0:size, :] = jnp.dot(kt, qm[m], preferred_element_type=F32).astype(BF16)

    def soft_pv(tile, size, src, carry):
        ms, ls = carry
        vt = vt_ref[:, pl.ds(start(tile, size), size)]
        new_ms, new_ls = [], []
        for m in range(2):
            s = src[m, 0:size, :]
            mx = jnp.maximum(ms[m], jnp.max(s, axis=0, keepdims=True).astype(F32))
            alpha = jnp.exp2(ms[m] - mx)
            p = jnp.exp2(s - mx.astype(BF16))
            new_ls.append(alpha * ls[m] + jnp.sum(p.astype(F32), axis=0, keepdims=True))
            acc[m] = alpha * acc[m] + jnp.dot(vt, p, preferred_element_type=F32)
            new_ms.append(mx)
        return tuple(new_ms), tuple(new_ls)

    def finish(carry):
        ls = carry[1]
        o = acc[0] / ls[0] - lam_ref[0] * (acc[1] / ls[1])
        var = jnp.mean(o * o, axis=0, keepdims=True)
        o = o * lax.rsqrt(var + EPS) * (g_ref[...] * out_scale)
        o_ref[...] = o.T.astype(o_ref.dtype)

    m0 = jnp.full((1, tq), NEG_INF, F32)
    l0 = jnp.zeros((1, tq), F32)
    init = ((m0, m0), (l0, l0))
    n_q_ctx = n_ctx // tq

    @pl.when(i < n_q_ctx)
    def _():
        acc[...] = jnp.zeros_like(acc)
        scores(0, n_ctx, s_a)
        finish(soft_pv(0, n_ctx, s_a, init))

    @pl.when(i >= n_q_ctx)
    def _():
        nk = t_rows // tk
        n_pairs = (nk - 1) // 2
        acc[...] = jnp.zeros_like(acc)
        scores(0, tk, s_a)

        def pair(pp, carry):
            scores(2 * pp + 1, tk, s_b)
            carry = soft_pv(2 * pp, tk, s_a, carry)
            scores(2 * pp + 2, tk, s_a)
            return soft_pv(2 * pp + 1, tk, s_b, carry)

        carry = lax.fori_loop(0, n_pairs, pair, init)
        done = 2 * n_pairs
        if nk - done == 2:
            scores(done + 1, tk, s_b)
            carry = soft_pv(done, tk, s_a, carry)
            carry = soft_pv(done + 1, tk, s_b, carry)
        else:
            carry = soft_pv(done, tk, s_a, carry)
        finish(carry)


def _diff_attention(qt, k, vt, lam, subln_col, n_ctx, out_scale):
    t_rows = k.shape[0]
    tq = 256
    tk = _pick(t_rows, (1280, 1024, 512, 256))
    assert n_ctx <= tk
    return pl.pallas_call(
        functools.partial(_flash_kernel, tq=tq, tk=tk, n_ctx=n_ctx, t_rows=t_rows, out_scale=out_scale),
        grid=(A_HEADS, t_rows // tq),
        in_specs=[pl.BlockSpec(memory_space=pltpu.SMEM),
                  pl.BlockSpec((LANES, tq), lambda h, i: (h, i)),
                  pl.BlockSpec((t_rows, LANES), lambda h, i: (0, h)),
                  pl.BlockSpec((A_V_DIM, t_rows), lambda h, i: (h, 0)),
                  pl.BlockSpec((LANES, 1), lambda h, i: (0, 0))],
        out_specs=pl.BlockSpec((tq, LANES), lambda h, i: (i, h)),
        out_shape=jax.ShapeDtypeStruct((t_rows, A_HEADS * A_V_DIM), BF16),
        scratch_shapes=[pltpu.VMEM((2, tk, tq), BF16), pltpu.VMEM((2, tk, tq), BF16),
                        pltpu.VMEM((2, A_V_DIM, tq), F32)],
        compiler_params=_cparams("parallel", "arbitrary"),
        name="diff_attention",
    )(lam, qt, k, vt, subln_col)


def _window_kernel(sink_ref, q_ref, kp_ref, ko_ref, kn_ref, vp_ref, vo_ref, vn_ref, kc_ref, vc_ref, o_ref,
                   *, nb, nb_ctx):
    g = pl.program_id(0)
    n = pl.program_id(1)
    qi = lax.broadcasted_iota(jnp.int32, (WINDOW, WINDOW), 0)
    kk = lax.broadcasted_iota(jnp.int32, (WINDOW, WINDOW), 1)
    own_ok = n >= nb_ctx
    prev_ok = n >= nb_ctx + 1
    next_ok = jnp.logical_and(own_ok, n <= nb - 2)
    m_prev = jnp.logical_and(kk >= qi, prev_ok)
    m_own = jnp.logical_and(kk >= 0, own_ok)
    m_next = jnp.logical_and(kk <= qi, next_ok)
    dn = (((1,), (1,)), ((), ()))
    for r in range(B_GROUP):
        q = q_ref[:, r * B_HEAD_DIM:(r + 1) * B_HEAD_DIM]
        s_p = jnp.where(m_prev, lax.dot_general(q, kp_ref[...], dn, preferred_element_type=F32), NEG_INF)
        s_o = jnp.where(m_own, lax.dot_general(q, ko_ref[...], dn, preferred_element_type=F32), NEG_INF)
        s_n = jnp.where(m_next, lax.dot_general(q, kn_ref[...], dn, preferred_element_type=F32), NEG_INF)
        s_c = lax.dot_general(q, kc_ref[...], dn, preferred_element_type=F32)
        sink = sink_ref[g * B_GROUP + r]
        mx = jnp.maximum(jnp.maximum(jnp.max(s_p, axis=1, keepdims=True), jnp.max(s_o, axis=1, keepdims=True)),
                         jnp.maximum(jnp.max(s_n, axis=1, keepdims=True), jnp.max(s_c, axis=1, keepdims=True)))
        mx = jnp.maximum(mx, sink)
        p_p, p_o, p_n, p_c = jnp.exp(s_p - mx), jnp.exp(s_o - mx), jnp.exp(s_n - mx), jnp.exp(s_c - mx)
        den = (jnp.sum(p_p, axis=1, keepdims=True) + jnp.sum(p_o, axis=1, keepdims=True)
               + jnp.sum(p_n, axis=1, keepdims=True) + jnp.sum(p_c, axis=1, keepdims=True)
               + jnp.exp(sink - mx))
        o = (jnp.dot(p_p.astype(BF16), vp_ref[...], preferred_element_type=F32)
             + jnp.dot(p_o.astype(BF16), vo_ref[...], preferred_element_type=F32)
             + jnp.dot(p_n.astype(BF16), vn_ref[...], preferred_element_type=F32)
             + jnp.dot(p_c.astype(BF16), vc_ref[...], preferred_element_type=F32))
        o_ref[:, r * B_HEAD_DIM:(r + 1) * B_HEAD_DIM] = (o / den).astype(o_ref.dtype)


def _window_attention(qkv, sink, n_ctx, col_q, col_k, col_v):
    t_rows = qkv.shape[0]
    nb = t_rows // WINDOW
    nb_ctx = n_ctx // WINDOW
    gq = B_GROUP * B_HEAD_DIM // LANES

    def kv_spec(col, shift):
        def imap(g, n):
            return (jnp.clip(n + shift, 0, nb - 1), col + g)
        return pl.BlockSpec((WINDOW, LANES), imap)

    return pl.pallas_call(
        functools.partial(_window_kernel, nb=nb, nb_ctx=nb_ctx),
        grid=(B_KV_HEADS, nb),
        in_specs=[pl.BlockSpec(memory_space=pltpu.SMEM),
                  pl.BlockSpec((WINDOW, B_GROUP * B_HEAD_DIM), lambda g, n: (n, col_q // gq + g)),
                  kv_spec(col_k, -1), kv_spec(col_k, 0), kv_spec(col_k, 1),
                  kv_spec(col_v, -1), kv_spec(col_v, 0), kv_spec(col_v, 1),
                  pl.BlockSpec((n_ctx, LANES), lambda g, n: (0, col_k + g)),
                  pl.BlockSpec((n_ctx, LANES), lambda g, n: (0, col_v + g))],
        out_specs=pl.BlockSpec((WINDOW, B_GROUP * B_HEAD_DIM), lambda g, n: (n, g)),
        out_shape=jax.ShapeDtypeStruct((t_rows, B_Q_HEADS * B_HEAD_DIM), BF16),
        compiler_params=_cparams("parallel", "arbitrary"),
        name="window_attention",
    )(sink, qkv, qkv, qkv, qkv, qkv, qkv, qkv, qkv, qkv)


ROW_PIECES = 8


def _store_packed_rows(ref, v):
    m, half = v.shape[0], v.shape[1] // 2
    lo = pltpu.bitcast(v[:, :half].astype(BF16).astype(F32), jnp.uint32) >> 16
    hi = pltpu.bitcast(v[:, half:].astype(BF16).astype(F32), jnp.uint32) & jnp.uint32(0xFFFF0000)
    w = lo | hi
    for j in range(ROW_PIECES):
        ref[pl.ds(j, m, stride=ROW_PIECES), :] = w[:, j * LANES:(j + 1) * LANES]


def _load_packed_rows(ref, m, r0=0):
    w = jnp.concatenate([ref[pl.ds(r0 * ROW_PIECES + j, m, stride=ROW_PIECES), :] for j in range(ROW_PIECES)],
                        axis=1)
    lo = pltpu.bitcast(w << 16, F32)
    hi = pltpu.bitcast(w & jnp.uint32(0xFFFF0000), F32)
    return lo, hi


def _post_mixer(y, x, g1_ref, g_ref, sh_ref, sc_ref, wr_ref, br_ref, xo_ref, h_ref, lg_ref, row0, n_ctx):
    row = row0 + lax.broadcasted_iota(jnp.int32, (x.shape[0], 1), 0)
    g1 = jnp.where(row < n_ctx, g1_ref[0:1, :], g1_ref[1:2, :])
    xn = x + g1 * y
    xo_ref[...] = xn
    h = _norm_mod(xn, g_ref[...], sh_ref[...], sc_ref[...], row0, n_ctx)
    _store_packed_rows(h_ref, h)
    h_hi = h.astype(BF16)
    h_lo = (h - h_hi.astype(F32)).astype(BF16)
    lg_ref[...] = (jnp.dot(h_hi, wr_ref[0], preferred_element_type=F32)
                   + jnp.dot(h_lo, wr_ref[0], preferred_element_type=F32)
                   + jnp.dot(h_hi, wr_ref[1], preferred_element_type=F32) + br_ref[...])


def _attn_out_kernel(ya_ref, yb_ref, x_ref, woa_ref, wob_ref, g1_ref, g_ref, sh_ref, sc_ref, wr_ref, br_ref,
                     xo_ref, h_ref, lg_ref, *, n_ctx, tm):
    y = (jnp.dot(ya_ref[...], woa_ref[...], preferred_element_type=F32)
         + jnp.dot(yb_ref[...], wob_ref[...], preferred_element_type=F32))
    _post_mixer(y, x_ref[...], g1_ref, g_ref, sh_ref, sc_ref, wr_ref, br_ref, xo_ref, h_ref, lg_ref,
                pl.program_id(0) * tm, n_ctx)


def _glu_out_kernel(y_ref, x_ref, wv_ref, wg_ref, g1_ref, g_ref, sh_ref, sc_ref, wr_ref, br_ref,
                    xo_ref, h_ref, lg_ref, *, n_ctx, tm):
    a = jax.nn.gelu(y_ref[...], approximate=True).astype(BF16)
    val = jnp.dot(a, wv_ref[...], preferred_element_type=F32)
    gate = jnp.dot(a, wg_ref[...], preferred_element_type=F32)
    _post_mixer(val * jax.nn.sigmoid(gate), x_ref[...], g1_ref, g_ref, sh_ref, sc_ref, wr_ref, br_ref,
                xo_ref, h_ref, lg_ref, pl.program_id(0) * tm, n_ctx)


def _mixer_out(kernel_fn, acts, x, weights, mod, cols, g2row, wr, br, n_ctx, name):
    t_rows, d = x.shape
    tm = 256
    row = lambda i: (i, 0)
    const = lambda i: (0, 0)
    in_specs = ([pl.BlockSpec((tm, a.shape[1]), row) for a in acts]
                + [pl.BlockSpec((tm, d), row)]
                + [pl.BlockSpec(w.shape, const) for w in weights]
                + [pl.BlockSpec((SUBLANES, d), lambda i, c=c: (0, c)) for c in cols[:1]]
                + [pl.BlockSpec((1, d), const)]
                + [pl.BlockSpec((SUBLANES, d), lambda i, c=c: (0, c)) for c in cols[1:]]
                + [pl.BlockSpec(wr.shape, lambda i: (0, 0, 0)), pl.BlockSpec(br.shape, const)])
    return pl.pallas_call(
        functools.partial(kernel_fn, n_ctx=n_ctx, tm=tm),
        grid=(t_rows // tm,),
        in_specs=in_specs,
        out_specs=[pl.BlockSpec((tm, d), row), pl.BlockSpec((tm * ROW_PIECES, LANES), row),
                   pl.BlockSpec((tm, LANES), row)],
        out_shape=[jax.ShapeDtypeStruct((t_rows, d), F32),
                   jax.ShapeDtypeStruct((t_rows * ROW_PIECES, LANES), jnp.uint32),
                   jax.ShapeDtypeStruct((t_rows, LANES), F32)],
        compiler_params=_cparams("parallel"),
        name=name,
    )(*acts, x, *weights, mod, g2row, mod, mod, wr, br)


def _router_kernel(lg_ref, idx_ref, gate_ref, cnt_ref, *, n_exp):
    @pl.when(pl.program_id(0) == 0)
    def _():
        cnt_ref[...] = jnp.zeros_like(cnt_ref)

    lt = lg_ref[...].T[0:n_exp, :]
    eid = lax.broadcasted_iota(jnp.int32, lt.shape, 0).astype(F32)
    vals, idxs = [], []
    hist = jnp.zeros(lt.shape, F32)
    for _ in range(TOP_K):
        mv = jnp.max(lt, axis=0, keepdims=True)
        ix = jnp.min(jnp.where(lt == mv, eid, float(n_exp)), axis=0, keepdims=True)
        sel = eid == ix
        hist = hist + sel.astype(F32)
        lt = jnp.where(sel, -jnp.inf, lt)
        vals.append(mv)
        idxs.append(ix)
    es = [jnp.exp(v - vals[0]) for v in vals]
    den = es[0] + es[1] + es[2] + es[3]
    pad_f = jnp.zeros((SUBLANES - TOP_K, lt.shape[1]), F32)
    idx_ref[...] = jnp.concatenate(idxs + [pad_f], axis=0).astype(jnp.int32)
    gate_ref[...] = jnp.concatenate([e / den for e in es] + [pad_f], axis=0)
    cnt_ref[...] += jnp.sum(hist, axis=1, keepdims=True)


def _router(logits, n_exp):
    t_rows = logits.shape[0]
    tm = 256
    return pl.pallas_call(
        functools.partial(_router_kernel, n_exp=n_exp),
        grid=(t_rows // tm,),
        in_specs=[pl.BlockSpec((tm, LANES), lambda i: (i, 0))],
        out_specs=[pl.BlockSpec((SUBLANES, tm), lambda i: (0, i)),
                   pl.BlockSpec((SUBLANES, tm), lambda i: (0, i)),
                   pl.BlockSpec((n_exp, 1), lambda i: (0, 0))],
        out_shape=[jax.ShapeDtypeStruct((SUBLANES, t_rows), jnp.int32),
                   jax.ShapeDtypeStruct((SUBLANES, t_rows), F32),
                   jax.ShapeDtypeStruct((n_exp, 1), F32)],
        compiler_params=_cparams("arbitrary"),
        name="router_topk",
    )(logits)


def _dest_kernel(idx_ref, start_ref, dest_ref, carry, *, n_exp, tm):
    @pl.when(pl.program_id(0) == 0)
    def _():
        carry[...] = start_ref[...]

    eid = lax.broadcasted_iota(jnp.int32, (n_exp, tm), 0)
    idx = idx_ref[...]
    sels = [eid == idx[k:k + 1, :] for k in range(TOP_K)]
    total = sels[0].astype(F32) + sels[1].astype(F32) + sels[2].astype(F32) + sels[3].astype(F32)
    rr = lax.broadcasted_iota(jnp.int32, (tm, tm), 0)
    cc = lax.broadcasted_iota(jnp.int32, (tm, tm), 1)
    upper = jnp.where(rr < cc, 1.0, 0.0).astype(BF16)
    before = jnp.dot(total.astype(BF16), upper, preferred_element_type=F32) + carry[...]
    rows = [jnp.sum(jnp.where(sels[k], before, 0.0), axis=0, keepdims=True) for k in range(TOP_K)]
    pad = jnp.zeros((SUBLANES - TOP_K, tm), F32)
    dest_ref[...] = jnp.concatenate(rows + [pad], axis=0).astype(jnp.int32)
    carry[...] += jnp.sum(total, axis=1, keepdims=True)


def _dest_rows(idx, starts, n_exp):
    t_rows = idx.shape[1]
    tm = 256
    return pl.pallas_call(
        functools.partial(_dest_kernel, n_exp=n_exp, tm=tm),
        grid=(t_rows // tm,),
        in_specs=[pl.BlockSpec((SUBLANES, tm), lambda i: (0, i)),
                  pl.BlockSpec((n_exp, 1), lambda i: (0, 0))],
        out_specs=pl.BlockSpec((SUBLANES, tm), lambda i: (0, i)),
        out_shape=jax.ShapeDtypeStruct((SUBLANES, t_rows), jnp.int32),
        scratch_shapes=[pltpu.VMEM((n_exp, 1), F32)],
        compiler_params=_cparams("arbitrary"),
        name="moe_dest_rows",
    )(idx, starts)


def _expert_gu_kernel(be_ref, bv_ref, bn_ref, x_ref, wg_ref, wu_ref, bg_ref, bu_ref, o_ref, wg_s, wu_s, *, blk):
    b = pl.program_id(1)
    nvalid = bv_ref[b]

    @pl.when(bn_ref[b] == 1)
    def _():
        wg_s[...] = wg_ref[...].astype(BF16)
        wu_s[...] = wu_ref[...].astype(BF16)

    half = blk // 2

    def run_half(r0):
        rows = r0 + lax.broadcasted_iota(jnp.int32, (half, 1), 0)
        lo, hi = _load_packed_rows(x_ref, half, r0)
        x = jnp.concatenate([lo.astype(BF16), hi.astype(BF16)], axis=1)
        x = jnp.where(rows < nvalid, x, jnp.zeros_like(x))
        gate = jnp.dot(x, wg_s[...], preferred_element_type=F32) + bg_ref[...]
        up = jnp.dot(x, wu_s[...], preferred_element_type=F32) + bu_ref[...]
        gate = jnp.minimum(gate, SWIGLU_LIMIT)
        up = jnp.clip(up, -SWIGLU_LIMIT, SWIGLU_LIMIT)
        act = (up + 1.0) * (gate * jax.nn.sigmoid(SWIGLU_ALPHA * gate))
        o_ref[r0:r0 + half, :] = act.astype(o_ref.dtype)

    for r0 in (0, half):
        @pl.when(nvalid > r0)
        def _(r0=r0):
            run_half(r0)

        @pl.when(nvalid <= r0)
        def _(r0=r0):
            o_ref[r0:r0 + half, :] = jnp.zeros((half, o_ref.shape[1]), o_ref.dtype)


def _expert_dn_kernel(be_ref, bv_ref, bn_ref, a_ref, wd_ref, bd_ref, o_ref, wd_s, *, blk):
    b = pl.program_id(0)

    @pl.when(bn_ref[b] == 1)
    def _():
        wd_s[...] = wd_ref[...].astype(BF16)

    @pl.when(bv_ref[b] > 0)
    def _():
        y = jnp.dot(a_ref[...], wd_s[...], preferred_element_type=F32) + bd_ref[...]
        _store_packed_rows(o_ref, y)

    @pl.when(bv_ref[b] == 0)
    def _():
        o_ref[...] = jnp.zeros_like(o_ref)


def _expert_matmul(xs, blk_e, blk_valid, blk_new, w_gu, b_gu, w_dn, b_dn, layer):
    d, f_dim = w_dn.shape[3], w_dn.shape[2]
    n_rows = xs.shape[0] // ROW_PIECES
    blk = EXPERT_BLOCK
    tf = 512
    nf = f_dim // tf
    n_blocks = n_rows // blk
    packed_rows = pl.BlockSpec((blk * ROW_PIECES, LANES), lambda f, b, be, bv, bn: (b, 0))
    act = pl.pallas_call(
        functools.partial(_expert_gu_kernel, blk=blk),
        grid_spec=pltpu.PrefetchScalarGridSpec(
            num_scalar_prefetch=3,
            grid=(nf, n_blocks),
            in_specs=[packed_rows,
                      pl.BlockSpec((None, None, d, tf), lambda f, b, be, bv, bn: (layer, be[b], 0, f)),
                      pl.BlockSpec((None, None, d, tf), lambda f, b, be, bv, bn: (layer, be[b], 0, nf + f)),
                      pl.BlockSpec((None, None, 1, tf), lambda f, b, be, bv, bn: (layer, be[b], 0, f)),
                      pl.BlockSpec((None, None, 1, tf), lambda f, b, be, bv, bn: (layer, be[b], 0, nf + f))],
            out_specs=pl.BlockSpec((blk, tf), lambda f, b, be, bv, bn: (b, f)),
            scratch_shapes=[pltpu.VMEM((d, tf), BF16), pltpu.VMEM((d, tf), BF16)]),
        out_shape=jax.ShapeDtypeStruct((n_rows, f_dim), BF16),
        compiler_params=_cparams("arbitrary", "arbitrary"),
        name="expert_gate_up",
    )(blk_e, blk_valid, blk_new, xs, w_gu, w_gu, b_gu, b_gu)
    assert d == 2 * ROW_PIECES * LANES
    sub = blk // DOWN_BLOCK
    part = jnp.arange(n_blocks * sub, dtype=jnp.int32) % sub
    dn_e = jnp.repeat(blk_e, sub)
    dn_valid = jnp.clip(jnp.repeat(blk_valid, sub) - part * DOWN_BLOCK, 0, DOWN_BLOCK)
    dn_new = jnp.where(part == 0, jnp.repeat(blk_new, sub), 0)
    return pl.pallas_call(
        functools.partial(_expert_dn_kernel, blk=DOWN_BLOCK),
        grid_spec=pltpu.PrefetchScalarGridSpec(
            num_scalar_prefetch=3,
            grid=(n_blocks * sub,),
            in_specs=[pl.BlockSpec((DOWN_BLOCK, f_dim), lambda b, be, bv, bn: (b, 0)),
                      pl.BlockSpec((None, None, f_dim, d), lambda b, be, bv, bn: (layer, be[b], 0, 0)),
                      pl.BlockSpec((None, None, 1, d), lambda b, be, bv, bn: (layer, be[b], 0, 0))],
            out_specs=pl.BlockSpec((DOWN_BLOCK * ROW_PIECES, LANES), lambda b, be, bv, bn: (b, 0)),
            scratch_shapes=[pltpu.VMEM((f_dim, d), BF16)]),
        out_shape=jax.ShapeDtypeStruct((n_rows * ROW_PIECES, LANES), jnp.uint32),
        compiler_params=_cparams("arbitrary"),
        name="expert_down",
    )(dn_e, dn_valid, dn_new, act, w_dn, b_dn)


def _combine_kernel(y0_ref, y1_ref, y2_ref, y3_ref, gate_ref, x_ref, g2_ref, fg_ref, o_ref, *, n_ctx, tm, final):
    gates = gate_ref[...]
    f_lo, f_hi = None, None
    for k, y_ref in enumerate((y0_ref, y1_ref, y2_ref, y3_ref)):
        lo, hi = _load_packed_rows(y_ref, tm)
        gk = gates[:, k:k + 1]
        f_lo = lo * gk if f_lo is None else f_lo + lo * gk
        f_hi = hi * gk if f_hi is None else f_hi + hi * gk
    f = jnp.concatenate([f_lo, f_hi], axis=1)
    row = pl.program_id(0) * tm + lax.broadcasted_iota(jnp.int32, (tm, 1), 0)
    g2 = jnp.where(row < n_ctx, g2_ref[0:1, :], g2_ref[1:2, :])
    xn = x_ref[...] + g2 * f
    if final:
        ms = jnp.mean(xn * xn, axis=-1, keepdims=True)
        xn = xn * lax.rsqrt(ms + EPS) * fg_ref[...]
    o_ref[...] = xn


def _combine(yk, gates_t, x, mod, g2_col, final_g, n_ctx, final):
    t_rows, d = x.shape
    tm = 256
    nt = t_rows // tm
    return pl.pallas_call(
        functools.partial(_combine_kernel, n_ctx=n_ctx, tm=tm, final=final),
        grid=(nt,),
        in_specs=[pl.BlockSpec((tm * ROW_PIECES, LANES), lambda i, k=k: (k * nt + i, 0)) for k in range(TOP_K)]
                 + [pl.BlockSpec((tm, SUBLANES), lambda i: (i, 0)),
                  pl.BlockSpec((tm, d), lambda i: (i, 0)),
                  pl.BlockSpec((SUBLANES, d), lambda i: (0, g2_col)),
                  pl.BlockSpec((1, d), lambda i: (0, 0))],
        out_specs=pl.BlockSpec((tm, d), lambda i: (i, 0)),
        out_shape=jax.ShapeDtypeStruct((t_rows, d), F32),
        compiler_params=_cparams("parallel"),
        name="moe_combine",
    )(yk, yk, yk, yk, gates_t, x, mod, final_g)


SC_WINDOW = 128


def _sc_mesh():
    return plsc.VectorSubcoreMesh(core_axis_name="core", subcore_axis_name="subcore")


def _sc_scatter_rows(x, dest, n_out):
    n_src_blocks = x.shape[0] // SC_WINDOW

    @functools.partial(pl.kernel, out_type=jax.ShapeDtypeStruct((n_out, LANES), x.dtype), mesh=_sc_mesh(),
                       scratch_types=[], name="sc_dispatch_rows")
    def run(x_hbm, i_hbm, o_hbm):
        def body(x_vmem, i_vmem):
            pltpu.sync_copy(x_vmem, o_hbm.at[i_vmem.at[0]])

        pltpu.emit_pipeline(
            body, grid=(dest.shape[1] // SC_WINDOW,),
            in_specs=[pl.BlockSpec((SC_WINDOW, LANES), lambda i: (i % n_src_blocks, 0)),
                      pl.BlockSpec((1, SC_WINDOW), lambda i: (0, i))],
            out_specs=[], core_axis_name=("core", "subcore"),
            dimension_semantics=(pltpu.PARALLEL,))(x_hbm, i_hbm)

    return run(x, dest)


def _sc_gather_rows(y, idx):
    n = idx.shape[1]

    @functools.partial(pl.kernel, out_type=jax.ShapeDtypeStruct((n, LANES), y.dtype), mesh=_sc_mesh(),
                       scratch_types=[], name="sc_combine_rows")
    def run(y_hbm, i_hbm, o_hbm):
        def body(i_vmem, o_vmem):
            pltpu.sync_copy(y_hbm.at[i_vmem.at[0]], o_vmem)

        pltpu.emit_pipeline(
            body, grid=(n // SC_WINDOW,),
            in_specs=[pl.BlockSpec((1, SC_WINDOW), lambda i: (0, i))],
            out_specs=[pl.BlockSpec((SC_WINDOW, LANES), lambda i: (i, 0))],
            core_axis_name=("core", "subcore"),
            dimension_semantics=(pltpu.PARALLEL,))(i_hbm, o_hbm)

    return run(y, idx)


def _moe(h, logits, x, mod, g2_col, w_gu, b_gu, w_dn, b_dn, layer, n_ctx, final_g, final):
    t_rows = x.shape[0]
    n_exp = w_gu.shape[1]
    idx, gates, counts = _router(logits, n_exp)
    counts = counts[:, 0].astype(jnp.int32)
    blk = EXPERT_BLOCK
    padded = (counts + blk - 1) // blk * blk
    pends = jnp.cumsum(padded)
    pstarts = pends - padded
    dest = _dest_rows(idx, pstarts.astype(F32)[:, None], n_exp)[:TOP_K]
    n_blocks = -(-(t_rows * TOP_K) // blk) + n_exp
    n_rows = n_blocks * blk
    blk_start = jnp.arange(n_blocks, dtype=jnp.int32) * blk
    blk_e = jnp.minimum(jnp.sum(blk_start[:, None] >= pends[None, :], axis=1), n_exp - 1).astype(jnp.int32)
    blk_valid = jnp.clip(pstarts[blk_e] + counts[blk_e] - blk_start, 0, blk).astype(jnp.int32)
    blk_valid = jnp.where(blk_start < pends[-1], blk_valid, 0)
    blk_new = jnp.concatenate([jnp.ones((1,), jnp.int32), (blk_e[1:] != blk_e[:-1]).astype(jnp.int32)])
    dest8 = (dest[:, :, None] * ROW_PIECES + jnp.arange(ROW_PIECES, dtype=jnp.int32)).reshape(1, -1)
    xs = _sc_scatter_rows(h, dest8, n_rows * ROW_PIECES)
    ys = _expert_matmul(xs, blk_e, blk_valid, blk_new, w_gu, b_gu, w_dn, b_dn, layer)
    yk = _sc_gather_rows(ys, dest8)
    return _combine(yk, gates.T, x, mod, g2_col, final_g, n_ctx, final)


def _s5_tables(a_re, a_im, log_step, b_re, b_im, c_re, c_im, d_skip):
    tc = S5_TC
    n_grp, n_st = a_re.shape[1], a_re.shape[2]
    lr = jnp.minimum(a_re.astype(F32), -1e-4)
    li = a_im.astype(F32)
    dt = jnp.exp(log_step.astype(F32))[..., None]
    dd = jnp.arange(tc + 1, dtype=F32)[:, None, None, None]
    mag = jnp.exp(lr * dt * dd)
    pr, pi = mag * jnp.cos(li * dt * dd), mag * jnp.sin(li * dt * dd)
    ar, ai = pr[1], pi[1]
    den = lr * lr + li * li
    nr = ar - 1.0
    zr = (nr * lr + ai * li) / den
    zi = (ai * lr - nr * li) / den
    br, bi = b_re.astype(F32), b_im.astype(F32)
    bbr = zr[..., None] * br - zi[..., None] * bi
    bbi = zr[..., None] * bi + zi[..., None] * br
    abr = pr[:tc, ..., None] * bbr - pi[:tc, ..., None] * bbi
    abi = pr[:tc, ..., None] * bbi + pi[:tc, ..., None] * bbr
    cr, ci = c_re.astype(F32), c_im.astype(F32)
    kern = (jnp.einsum('xgip,dxgpj->dxgij', cr, abr, precision=HIGHEST)
            - jnp.einsum('xgip,dxgpj->dxgij', ci, abi, precision=HIGHEST))
    def toeplitz(kx):
        ext = jnp.concatenate([jnp.zeros_like(kx), kx], axis=0)
        return jnp.stack([ext[tc - a:2 * tc - a] for a in range(tc)], axis=0)

    m_f = toeplitz(kern[:, 0])
    m_b = toeplitz(kern[:, 1]).transpose(1, 0, 2, 3, 4)
    m_tot = (m_f + m_b).transpose(2, 0, 4, 1, 3)
    eye_t = jnp.eye(tc, dtype=F32)
    eye_i = jnp.eye(S5_GROUP, dtype=F32)
    dsk = d_skip.astype(F32).reshape(n_grp, S5_GROUP)
    m_tot = m_tot + (eye_t[None, :, None, :, None] * eye_i[None, None, :, None, :]
                     * dsk[:, None, :, None, None])
    m_tot = m_tot.reshape(n_grp, tc * S5_GROUP, tc * S5_GROUP)
    pw_f = tc - 1 - jnp.arange(tc)
    pw_b = jnp.arange(tc)

    def b_cols(part, pw, x):
        return part[pw, x].transpose(1, 0, 3, 2).reshape(n_grp, tc * S5_GROUP, n_st)

    b_mat = jnp.concatenate([b_cols(abr, pw_f, 0), b_cols(abi, pw_f, 0),
                             b_cols(abr, pw_b, 1), b_cols(abi, pw_b, 1)], axis=-1)
    pcf = 1 + jnp.arange(tc)
    pcb = tc - jnp.arange(tc)

    def c_rows(pw, x):
        prx, pix = pr[pw, x], pi[pw, x]
        re_c = cr[x][None] * prx[:, :, None, :] - ci[x][None] * pix[:, :, None, :]
        im_c = -(cr[x][None] * pix[:, :, None, :] + ci[x][None] * prx[:, :, None, :])
        to_rows = lambda z: z.transpose(1, 3, 0, 2).reshape(n_grp, n_st, tc * S5_GROUP)
        return to_rows(re_c), to_rows(im_c)

    c_mat = jnp.concatenate(list(c_rows(pcf, 0)) + list(c_rows(pcb, 1)), axis=1)
    prt, pit = pr[tc], pi[tc]
    a1 = jnp.concatenate([prt[0], prt[0], prt[1], prt[1]], axis=-1)
    a2 = jnp.concatenate([-pit[0], pit[0], -pit[1], pit[1]], axis=-1)
    return m_tot.astype(BF16), b_mat.astype(BF16), c_mat.astype(BF16), a1, a2


S5_LANE_GROUPS = LANES // S5_GROUP


def _s5_pack(u_ref, c0, cc):
    xs = [u_ref[pl.ds(c0 * S5_TC + tau, cc, stride=S5_TC), :] for tau in range(S5_TC)]
    return [jnp.concatenate([xs[tau][:, g * S5_GROUP:(g + 1) * S5_GROUP] for tau in range(S5_TC)], axis=1)
            for g in range(S5_LANE_GROUPS)]


def _s5_in_kernel(u_ref, b_ref, o_ref, *, cc):
    def body(ci, carry):
        c0 = pl.multiple_of(ci * cc, SUBLANES)
        ugs = _s5_pack(u_ref, c0, cc)
        for g in range(S5_LANE_GROUPS):
            o_ref[g, pl.ds(c0, cc), :] = jnp.dot(ugs[g].astype(BF16), b_ref[g], preferred_element_type=F32)
        return carry

    lax.fori_loop(0, u_ref.shape[0] // (S5_TC * cc), body, 0)


def _s5_out_kernel(u_ref, z_ref, m_ref, c_ref, o_ref, *, cc):
    def body(ci, carry):
        c0 = pl.multiple_of(ci * cc, SUBLANES)
        ugs = _s5_pack(u_ref, c0, cc)
        ys = [jnp.dot(ugs[g].astype(BF16), m_ref[g], preferred_element_type=F32)
              + jnp.dot(z_ref[g, pl.ds(c0, cc), :].astype(BF16), c_ref[g], preferred_element_type=F32)
              for g in range(S5_LANE_GROUPS)]
        for tau in range(S5_TC):
            row = jnp.concatenate([y[:, tau * S5_GROUP:(tau + 1) * S5_GROUP] for y in ys], axis=1)
            o_ref[pl.ds(c0 * S5_TC + tau, cc, stride=S5_TC), :] = row
        return carry

    lax.fori_loop(0, u_ref.shape[0] // (S5_TC * cc), body, 0)


def _s5_scan_kernel(h_ref, a1_ref, a2_ref, z_ref, *, n_chunks, n_ctx_chunks):
    half = 2 * S5_STATE
    a1 = a1_ref[...]
    a2 = a2_ref[...]
    a1f, a1b = a1[:, :half], a1[:, half:]
    a2f, a2b = a2[:, :half], a2[:, half:]
    zero = jnp.zeros((a1.shape[0], half), F32)

    def swap(v):
        return pltpu.roll(v, S5_STATE, 1)

    def step(t, carry):
        rf, rfs, rb, rbs = carry
        cb = jnp.where(t < n_ctx_chunks, n_ctx_chunks - 1 - t, n_chunks - 1 - (t - n_ctx_chunks))
        z_ref[t, :, 0:half] = rf
        z_ref[cb, :, half:2 * half] = rb
        hf = h_ref[t, :, 0:half]
        hb = h_ref[cb, :, half:2 * half]
        nrf = a1f * rf + a2f * rfs + hf
        nrfs = a1f * rfs - a2f * rf + swap(hf)
        nrb = a1b * rb + a2b * rbs + hb
        nrbs = a1b * rbs - a2b * rb + swap(hb)
        return nrf, nrfs, nrb, nrbs

    lax.fori_loop(0, n_chunks, step, (zero, zero, zero, zero), unroll=8 if n_chunks % 8 == 0 else 1)


def _s5_mixer_scan(u, tables, n_ctx):
    m_tot, b_mat, c_mat, a1, a2 = tables
    t_rows, width = u.shape
    tc = S5_TC
    n_grp = width // S5_GROUP
    n_chunks = t_rows // tc
    kdim = tc * S5_GROUP
    sdim = 4 * S5_STATE
    gb = S5_LANE_GROUPS
    gs = SUBLANES
    n_split = 2
    half_chunks = n_chunks // n_split
    cc = _pick(half_chunks, (104, 80, 40, 8))
    u_spec = pl.BlockSpec((t_rows // n_split, LANES), lambda j, r: (r, j))
    tab_spec = lambda a, b: pl.BlockSpec((gb, a, b), lambda j, r: (j, 0, 0))
    st_spec = pl.BlockSpec((gb, half_chunks, sdim), lambda j, r: (j, r, 0))
    grid = (width // LANES, n_split)
    hin = pl.pallas_call(
        functools.partial(_s5_in_kernel, cc=cc),
        grid=grid,
        in_specs=[u_spec, tab_spec(kdim, sdim)],
        out_specs=st_spec,
        out_shape=jax.ShapeDtypeStruct((n_grp, n_chunks, sdim), F32),
        compiler_params=_cparams("parallel", "arbitrary"),
        name="s5_chunk_inputs",
    )(u, b_mat)
    hin_t = hin.transpose(1, 0, 2)
    z_t = pl.pallas_call(
        functools.partial(_s5_scan_kernel, n_chunks=n_chunks, n_ctx_chunks=n_ctx // tc),
        grid=(n_grp // gs,),
        in_specs=[pl.BlockSpec((n_chunks, gs, sdim), lambda g: (0, g, 0)),
                  pl.BlockSpec((gs, sdim), lambda g: (g, 0)),
                  pl.BlockSpec((gs, sdim), lambda g: (g, 0))],
        out_specs=pl.BlockSpec((n_chunks, gs, sdim), lambda g: (0, g, 0)),
        out_shape=jax.ShapeDtypeStruct((n_chunks, n_grp, sdim), F32),
        compiler_params=_cparams("parallel"),
        name="s5_chunk_scan",
    )(hin_t, a1, a2)
    z = z_t.transpose(1, 0, 2)
    return pl.pallas_call(
        functools.partial(_s5_out_kernel, cc=cc),
        grid=grid,
        in_specs=[u_spec, st_spec, tab_spec(kdim, kdim), tab_spec(sdim, kdim)],
        out_specs=u_spec,
        out_shape=jax.ShapeDtypeStruct((t_rows, width), F32),
        compiler_params=_cparams("parallel", "arbitrary"),
        name="s5_chunk_outputs",
    )(u, z, m_tot, c_mat)


def _lambda_init(layer):
    return 0.8 - 0.6 * math.exp(-0.3 * layer)


def kernel(x, c, ctx, c_ctx, w_mod, b_mod, norm1_g, norm2_g, final_g, attn_w_qkv, attn_w_o, lambda_q1, lambda_k1, lambda_q2, lambda_k2, subln_g, sink_logit, s5_w_in, s5_a_re, s5_a_im, s5_log_step, s5_b_re, s5_b_im, s5_c_re, s5_c_im, s5_d, s5_w_glu, router_w, router_b, expert_w_gu, expert_b_gu, expert_w_down, expert_b_down):
    bsz, seq, d = x.shape
    assert bsz == 1, "single-sequence block"
    n_ctx = ctx.shape[1]
    depth = w_mod.shape[0]
    n_exp = router_w.shape[2]
    assert n_ctx % 256 == 0 and seq % 256 == 0

    xj = jnp.concatenate([ctx[0], x[0]], axis=0)
    cs = jnp.zeros((SUBLANES, d), F32).at[0].set(c_ctx).at[1].set(c[0])
    b_mod3 = b_mod[:, None, :]
    b_gu4 = expert_b_gu[:, :, None, :]
    b_dn4 = expert_b_down[:, :, None, :]
    wr_pad = jnp.pad(router_w, ((0, 0), (0, 0), (0, LANES - n_exp)))
    wr_hi = wr_pad.astype(BF16)
    wr_pad = jnp.stack([wr_hi, (wr_pad - wr_hi.astype(F32)).astype(BF16)], axis=1)
    br_pad = jnp.pad(router_b, ((0, 0), (0, LANES - n_exp)), constant_values=NEG_INF)[:, None, :]
    final_row = final_g[None, :]

    for i in range(depth):
        last = i == depth - 1
        j = i // 2
        mod = _adaln_mod(cs, w_mod, b_mod3, i)
        if i % 2 == 0:
            sizes = (A_HEADS * 2 * A_QK_DIM, A_HEADS * 2 * A_QK_DIM, A_HEADS * A_V_DIM,
                     B_Q_HEADS * B_HEAD_DIM, B_KV_HEADS * B_HEAD_DIM, B_KV_HEADS * B_HEAD_DIM)
            offs = np.concatenate([[0], np.cumsum(sizes)])
            types = [0, 1, ROPE_NONE, 2, 3, ROPE_NONE]
            ttype = jnp.asarray(np.concatenate([np.full(s // 256, t) for s, t in zip(sizes, types)]), jnp.int32)
            cos_t, sin_t = _rope_tables(n_ctx, seq)
            qkv = _norm_mod_matmul(xj, norm1_g[i][None, :], mod, 0, 1, attn_w_qkv[j].astype(BF16), n_ctx, BF16,
                                   rope=(ttype, cos_t, sin_t))
            qt = qkv[:, offs[0]:offs[1]].T
            ka = qkv[:, offs[1]:offs[2]]
            vt = qkv[:, offs[2]:offs[3]].T
            f32 = F32
            li = _lambda_init(i)
            lam = (jnp.exp(jnp.sum(lambda_q1[j].astype(f32) * lambda_k1[j].astype(f32)))
                   - jnp.exp(jnp.sum(lambda_q2[j].astype(f32) * lambda_k2[j].astype(f32))) + li)
            ya = _diff_attention(qt, ka, vt, lam.reshape(1), subln_g[j][:, None], n_ctx, 1.0 - li)
            yb = _window_attention(qkv, sink_logit[j], n_ctx, int(offs[3]) // LANES, int(offs[4]) // LANES,
                                   int(offs[5]) // LANES)
            w_o = attn_w_o[j].astype(BF16)
            na = A_HEADS * A_V_DIM
            xj, h2, logits = _mixer_out(_attn_out_kernel, [ya, yb], xj, [w_o[:na], w_o[na:]], mod, (2, 3, 4),
                                        norm2_g[i][None, :], wr_pad[i], br_pad[i], n_ctx, "attn_out_router")
        else:
            u = _norm_mod_matmul(xj, norm1_g[i][None, :], mod, 0, 1, s5_w_in[j].astype(BF16), n_ctx, F32)
            tables = _s5_tables(s5_a_re[j], s5_a_im[j], s5_log_step[j], s5_b_re[j], s5_b_im[j],
                                s5_c_re[j], s5_c_im[j], s5_d[j])
            y = _s5_mixer_scan(u, tables, n_ctx)
            w_glu = s5_w_glu[j].astype(BF16)
            xj, h2, logits = _mixer_out(_glu_out_kernel, [y], xj, [w_glu[:, :d], w_glu[:, d:]], mod, (2, 3, 4),
                                        norm2_g[i][None, :], wr_pad[i], br_pad[i], n_ctx, "glu_out_router")
        xj = _moe(h2, logits, xj, mod, 5, expert_w_gu, b_gu4, expert_w_down, b_dn4, i, n_ctx, final_row, last)
    return xj[n_ctx:][None]
```

```python
import functools
import math

import jax
import jax.numpy as jnp
import numpy as np
from jax import lax
from jax.experimental import pallas as pl
from jax.experimental.pallas import tpu as pltpu
from jax.experimental.pallas import tpu_sc as plsc

F32 = jnp.float32
BF16 = jnp.bfloat16
HIGHEST = lax.Precision.HIGHEST

V7X_VMEM_BYTES = 64 * 1024 * 1024
VMEM_LIMIT = V7X_VMEM_BYTES - 8 * 1024 * 1024
LANES = 128
SUBLANES = 8

GRID_W = 64
N_MOD = 6
EPS = 1e-6
NEG_INF = -1e30
ROPE_THETA = 10000.0
A_HEADS = 8
A_QK_DIM = 64
A_V_DIM = 128
B_Q_HEADS = 8
B_KV_HEADS = 2
B_GROUP = 4
B_HEAD_DIM = 128
WINDOW = 128
S5_GROUP = 16
S5_STATE = 64
S5_TC = 16
TOP_K = 4
SWIGLU_LIMIT = 7.0
SWIGLU_ALPHA = 1.702
EXPERT_BLOCK = 512
DOWN_BLOCK = 256
LOG2E = 1.4426950408889634


def _pick(n, cands):
    for c in cands:
        if n % c == 0:
            return c
    raise ValueError(f"no tile for {n} in {cands}")


def _cparams(*sem):
    return pltpu.CompilerParams(dimension_semantics=sem, vmem_limit_bytes=VMEM_LIMIT)


def _mod_kernel(c_ref, w_ref, b_ref, o_ref):
    cv = c_ref[...]
    s = cv * jax.nn.sigmoid(cv)
    o_ref[...] = jnp.dot(s, w_ref[...], preferred_element_type=F32, precision=HIGHEST) + b_ref[...]


def _adaln_mod(cs, w_mod, b_mod, layer):
    d, n = w_mod.shape[1], w_mod.shape[2]
    tn = _pick(n, (1024, 512, 256, 128))
    return pl.pallas_call(
        _mod_kernel,
        grid=(n // tn,),
        in_specs=[pl.BlockSpec((SUBLANES, d), lambda j: (0, 0)),
                  pl.BlockSpec((None, d, tn), lambda j: (layer, 0, j)),
                  pl.BlockSpec((None, 1, tn), lambda j: (layer, 0, j))],
        out_specs=pl.BlockSpec((SUBLANES, tn), lambda j: (0, j)),
        out_shape=jax.ShapeDtypeStruct((SUBLANES, n), F32),
        compiler_params=_cparams("parallel"),
        name="adaln_mod",
    )(cs, w_mod, b_mod)


def _norm_mod(x, g, sh2, sc2, row0, n_ctx):
    ms = jnp.mean(x * x, axis=-1, keepdims=True)
    y = x * lax.rsqrt(ms + EPS) * g
    row = row0 + lax.broadcasted_iota(jnp.int32, (x.shape[0], 1), 0)
    is_ctx = row < n_ctx
    sc = jnp.where(is_ctx, sc2[0:1, :], sc2[1:2, :])
    sh = jnp.where(is_ctx, sh2[0:1, :], sh2[1:2, :])
    return y * (1.0 + sc) + sh


ROPE_NONE = 4


def _rope_store(acc, cos_ref, sin_ref, o_ref, shift):
    cosv, sinv = cos_ref[...], sin_ref[...]
    for cgrp in range(acc.shape[1] // LANES):
        a = acc[:, cgrp * LANES:(cgrp + 1) * LANES]
        lane = lax.broadcasted_iota(jnp.int32, a.shape, 1)
        in_second = (lane & (2 * shift - 1)) >= shift
        sw = jnp.where(in_second, pltpu.roll(a, shift, 1), pltpu.roll(a, LANES - shift, 1))
        o_ref[:, cgrp * LANES:(cgrp + 1) * LANES] = (a * cosv + sw * sinv).astype(o_ref.dtype)


NORM_ROWS = 256


def _norm_mod_to_scratch(x_ref, g_ref, sh_ref, sc_ref, h_scr, row0, n_ctx):
    def body(r, carry):
        off = pl.multiple_of(r * NORM_ROWS, NORM_ROWS)
        h = _norm_mod(x_ref[pl.ds(off, NORM_ROWS), :], g_ref[...], sh_ref[...], sc_ref[...], row0 + off, n_ctx)
        h_scr[pl.ds(off, NORM_ROWS), :] = h.astype(h_scr.dtype)
        return carry

    lax.fori_loop(0, x_ref.shape[0] // NORM_ROWS, body, 0)


def _nmm_rope_kernel(tt_ref, x_ref, g_ref, sh_ref, sc_ref, w_ref, cos_ref, sin_ref, o_ref, h_scr,
                     *, n_ctx, tm):
    i = pl.program_id(0)
    j = pl.program_id(1)

    @pl.when(j == 0)
    def _():
        _norm_mod_to_scratch(x_ref, g_ref, sh_ref, sc_ref, h_scr, i * tm, n_ctx)

    acc = jnp.dot(h_scr[...], w_ref[...], preferred_element_type=F32)
    t = tt_ref[j]

    @pl.when(t == ROPE_NONE)
    def _():
        o_ref[...] = acc.astype(o_ref.dtype)

    @pl.when(t < 2)
    def _():
        _rope_store(acc, cos_ref, sin_ref, o_ref, A_QK_DIM // 4)

    @pl.when(jnp.logical_and(t >= 2, t < ROPE_NONE))
    def _():
        _rope_store(acc, cos_ref, sin_ref, o_ref, B_HEAD_DIM // 4)


def _nmm_plain_kernel(x_ref, g_ref, sh_ref, sc_ref, w_ref, o_ref, h_scr, *, n_ctx, tm):
    i = pl.program_id(0)
    j = pl.program_id(1)

    @pl.when(j == 0)
    def _():
        _norm_mod_to_scratch(x_ref, g_ref, sh_ref, sc_ref, h_scr, i * tm, n_ctx)

    o_ref[...] = jnp.dot(h_scr[...], w_ref[...], preferred_element_type=F32).astype(o_ref.dtype)


def _norm_mod_matmul(x, g, mod, sh_col, sc_col, w, n_ctx, out_dtype, rope=None):
    t_rows, d = x.shape
    n = w.shape[1]
    tm = _pick(t_rows, (1280, 1024, 512, 256, 128))
    tn = 256
    grid = (t_rows // tm, n // tn)
    kern_kw = dict(n_ctx=n_ctx, tm=tm)
    scratch = [pltpu.VMEM((tm, d), BF16)]
    out_shape = jax.ShapeDtypeStruct((t_rows, n), out_dtype)
    if rope is None:
        return pl.pallas_call(
            functools.partial(_nmm_plain_kernel, **kern_kw),
            grid=grid,
            in_specs=[pl.BlockSpec((tm, d), lambda i, j: (i, 0)),
                      pl.BlockSpec((1, d), lambda i, j: (0, 0)),
                      pl.BlockSpec((SUBLANES, d), lambda i, j: (0, sh_col)),
                      pl.BlockSpec((SUBLANES, d), lambda i, j: (0, sc_col)),
                      pl.BlockSpec((d, tn), lambda i, j: (0, j))],
            out_specs=pl.BlockSpec((tm, tn), lambda i, j: (i, j)),
            out_shape=out_shape,
            scratch_shapes=scratch,
            compiler_params=_cparams("parallel", "arbitrary"),
            name="norm_mod_matmul",
        )(x, g, mod, mod, w)
    ttype, cos_t, sin_t = rope
    return pl.pallas_call(
        functools.partial(_nmm_rope_kernel, **kern_kw),
        grid_spec=pltpu.PrefetchScalarGridSpec(
            num_scalar_prefetch=1,
            grid=grid,
            in_specs=[pl.BlockSpec((tm, d), lambda i, j, tt: (i, 0)),
                      pl.BlockSpec((1, d), lambda i, j, tt: (0, 0)),
                      pl.BlockSpec((SUBLANES, d), lambda i, j, tt: (0, sh_col)),
                      pl.BlockSpec((SUBLANES, d), lambda i, j, tt: (0, sc_col)),
                      pl.BlockSpec((d, tn), lambda i, j, tt: (0, j)),
                      pl.BlockSpec((None, tm, LANES), lambda i, j, tt: (tt[j], i, 0)),
                      pl.BlockSpec((None, tm, LANES), lambda i, j, tt: (tt[j], i, 0))],
            out_specs=pl.BlockSpec((tm, tn), lambda i, j, tt: (i, j)),
            scratch_shapes=scratch),
        out_shape=out_shape,
        compiler_params=_cparams("parallel", "arbitrary"),
        name="norm_mod_qkv_rope",
    )(ttype, x, g, mod, mod, w, cos_t, sin_t)


def _rope_tables(n_ctx, seq):
    pos = jnp.arange(seq)
    rows = (pos // GRID_W).astype(F32)
    cols = (pos % GRID_W).astype(F32)

    def tab(dim):
        quarter = dim // 4
        freqs = ROPE_THETA ** (-jnp.arange(quarter, dtype=F32) / quarter)
        ar, ac = rows[:, None] * freqs, cols[:, None] * freqs
        cosv = jnp.concatenate([jnp.cos(ar), jnp.cos(ar), jnp.cos(ac), jnp.cos(ac)], axis=1)
        sinv = jnp.concatenate([-jnp.sin(ar), jnp.sin(ar), -jnp.sin(ac), jnp.sin(ac)], axis=1)
        reps = LANES // dim
        cosv, sinv = jnp.tile(cosv, (1, reps)), jnp.tile(sinv, (1, reps))
        cosv = jnp.concatenate([jnp.ones((n_ctx, LANES), F32), cosv], axis=0)
        sinv = jnp.concatenate([jnp.zeros((n_ctx, LANES), F32), sinv], axis=0)
        return cosv, sinv

    ca, sa = tab(A_QK_DIM)
    cb, sb = tab(B_HEAD_DIM)
    qa = (A_QK_DIM ** -0.5) * LOG2E
    qb = B_HEAD_DIM ** -0.5
    one, zero = jnp.ones_like(ca), jnp.zeros_like(ca)
    cos_t = jnp.stack([ca * qa, ca, cb * qb, cb, one])
    sin_t = jnp.stack([sa * qa, sa, sb * qb, sb, zero])
    return cos_t, sin_t


V_AUG = A_V_DIM + 16


def _flash_kernel(lam_ref, qt_ref, k_ref, vt_ref, g_ref, o_ref, s_a, s_b, acc,
                  *, tq, tk, t_rows, out_scale):
    qt = qt_ref[...]
    row = lax.broadcasted_iota(jnp.int32, qt.shape, 0)
    zero = jnp.zeros_like(qt)
    qm = (jnp.where(row < A_QK_DIM, qt, zero), jnp.where(row >= A_QK_DIM, qt, zero))

    def start(tile, size):
        return tile * size if isinstance(tile, int) else pl.multiple_of(tile * size, size)

    def scores(tile, size, dst):
        kt = k_ref[pl.ds(start(tile, size), size), :]
        for m in range(2):
            dst[m, 0:size, :] = jnp.dot(kt, qm[m], preferred_element_type=F32).astype(BF16)

    def soft_pv(tile, size, src, ms):
        vt = vt_ref[:, pl.ds(start(tile, size), size)]
        new_ms = []
        for m in range(2):
            s = src[m, 0:size, :]
            mx = jnp.maximum(ms[m], jnp.max(s, axis=0, keepdims=True).astype(F32))
            alpha = jnp.exp2(ms[m] - mx)
            p = jnp.exp2(s - mx.astype(BF16))
            acc[m] = alpha * acc[m] + jnp.dot(vt, p, preferred_element_type=F32)
            new_ms.append(mx)
        return tuple(new_ms)

    def finish(ms):
        o1 = acc[0, 0:A_V_DIM, :] / acc[0, A_V_DIM:A_V_DIM + 1, :]
        o2 = acc[1, 0:A_V_DIM, :] / acc[1, A_V_DIM:A_V_DIM + 1, :]
        o = o1 - lam_ref[0] * o2
        var = jnp.mean(o * o, axis=0, keepdims=True)
        o = o * lax.rsqrt(var + EPS) * (g_ref[...] * out_scale)
        o_ref[...] = o.T.astype(o_ref.dtype)

    m0 = jnp.full((1, tq), NEG_INF, F32)
    nk = t_rows // tk
    n_pairs = (nk - 1) // 2
    acc[...] = jnp.zeros_like(acc)
    scores(0, tk, s_a)

    def pair(pp, carry):
        scores(2 * pp + 1, tk, s_b)
        carry = soft_pv(2 * pp, tk, s_a, carry)
        scores(2 * pp + 2, tk, s_a)
        return soft_pv(2 * pp + 1, tk, s_b, carry)

    carry = lax.fori_loop(0, n_pairs, pair, (m0, m0))
    done = 2 * n_pairs
    if nk - done == 2:
        scores(done + 1, tk, s_b)
        carry = soft_pv(done, tk, s_a, carry)
        carry = soft_pv(done + 1, tk, s_b, carry)
    else:
        carry = soft_pv(done, tk, s_a, carry)
    finish(carry)


def _diff_attention(qt, k, vt, lam, subln_col, out_scale):
    t_rows = k.shape[0]
    n_q = qt.shape[1]
    tq = _pick(n_q, (1024, 512, 256))
    tk = _pick(t_rows, (1664, 1280, 1024, 512, 256))
    return pl.pallas_call(
        functools.partial(_flash_kernel, tq=tq, tk=tk, t_rows=t_rows, out_scale=out_scale),
        grid=(A_HEADS, n_q // tq),
        in_specs=[pl.BlockSpec(memory_space=pltpu.SMEM),
                  pl.BlockSpec((LANES, tq), lambda h, i: (h, i)),
                  pl.BlockSpec((t_rows, LANES), lambda h, i: (0, h)),
                  pl.BlockSpec((V_AUG, t_rows), lambda h, i: (h, 0)),
                  pl.BlockSpec((LANES, 1), lambda h, i: (0, 0))],
        out_specs=pl.BlockSpec((tq, LANES), lambda h, i: (i, h)),
        out_shape=jax.ShapeDtypeStruct((n_q, A_HEADS * A_V_DIM), BF16),
        scratch_shapes=[pltpu.VMEM((2, tk, tq), BF16), pltpu.VMEM((2, tk, tq), BF16),
                        pltpu.VMEM((2, V_AUG, tq), F32)],
        compiler_params=_cparams("parallel", "arbitrary"),
        name="diff_attention",
    )(lam, qt, k, vt, subln_col)


def _window_kernel(sink_ref, q_ref, kp_ref, ko_ref, kn_ref, vp_ref, vo_ref, vn_ref, kc_ref, vc_ref, o_ref,
                   *, nb, nb_ctx):
    g = pl.program_id(0)
    n = pl.program_id(1)
    qi = lax.broadcasted_iota(jnp.int32, (WINDOW, WINDOW), 0)
    kk = lax.broadcasted_iota(jnp.int32, (WINDOW, WINDOW), 1)
    own_ok = n >= nb_ctx
    prev_ok = n >= nb_ctx + 1
    next_ok = jnp.logical_and(own_ok, n <= nb - 2)
    m_prev = jnp.logical_and(kk >= qi, prev_ok)
    m_own = jnp.logical_and(kk >= 0, own_ok)
    m_next = jnp.logical_and(kk <= qi, next_ok)
    dn = (((1,), (1,)), ((), ()))
    for r in range(B_GROUP):
        q = q_ref[:, r * B_HEAD_DIM:(r + 1) * B_HEAD_DIM]
        s_p = jnp.where(m_prev, lax.dot_general(q, kp_ref[...], dn, preferred_element_type=F32), NEG_INF)
        s_o = jnp.where(m_own, lax.dot_general(q, ko_ref[...], dn, preferred_element_type=F32), NEG_INF)
        s_n = jnp.where(m_next, lax.dot_general(q, kn_ref[...], dn, preferred_element_type=F32), NEG_INF)
        s_c = lax.dot_general(q, kc_ref[...], dn, preferred_element_type=F32)
        sink = sink_ref[g * B_GROUP + r]
        mx = jnp.maximum(jnp.maximum(jnp.max(s_p, axis=1, keepdims=True), jnp.max(s_o, axis=1, keepdims=True)),
                         jnp.maximum(jnp.max(s_n, axis=1, keepdims=True), jnp.max(s_c, axis=1, keepdims=True)))
        mx = jnp.maximum(mx, sink)
        p_p, p_o, p_n, p_c = jnp.exp(s_p - mx), jnp.exp(s_o - mx), jnp.exp(s_n - mx), jnp.exp(s_c - mx)
        den = (jnp.sum(p_p, axis=1, keepdims=True) + jnp.sum(p_o, axis=1, keepdims=True)
               + jnp.sum(p_n, axis=1, keepdims=True) + jnp.sum(p_c, axis=1, keepdims=True)
               + jnp.exp(sink - mx))
        o = (jnp.dot(p_p.astype(BF16), vp_ref[...], preferred_element_type=F32)
             + jnp.dot(p_o.astype(BF16), vo_ref[...], preferred_element_type=F32)
             + jnp.dot(p_n.astype(BF16), vn_ref[...], preferred_element_type=F32)
             + jnp.dot(p_c.astype(BF16), vc_ref[...], preferred_element_type=F32))
        o_ref[:, r * B_HEAD_DIM:(r + 1) * B_HEAD_DIM] = (o / den).astype(o_ref.dtype)


def _window_attention(qkv, sink, n_ctx, col_q, col_k, col_v):
    t_rows = qkv.shape[0]
    nb = t_rows // WINDOW
    nb_ctx = n_ctx // WINDOW
    gq = B_GROUP * B_HEAD_DIM // LANES

    def kv_spec(col, shift):
        def imap(g, n):
            return (jnp.clip(n + shift, 0, nb - 1), col + g)
        return pl.BlockSpec((WINDOW, LANES), imap)

    return pl.pallas_call(
        functools.partial(_window_kernel, nb=nb, nb_ctx=nb_ctx),
        grid=(B_KV_HEADS, nb),
        in_specs=[pl.BlockSpec(memory_space=pltpu.SMEM),
                  pl.BlockSpec((WINDOW, B_GROUP * B_HEAD_DIM), lambda g, n: (n, col_q // gq + g)),
                  kv_spec(col_k, -1), kv_spec(col_k, 0), kv_spec(col_k, 1),
                  kv_spec(col_v, -1), kv_spec(col_v, 0), kv_spec(col_v, 1),
                  pl.BlockSpec((n_ctx, LANES), lambda g, n: (0, col_k + g)),
                  pl.BlockSpec((n_ctx, LANES), lambda g, n: (0, col_v + g))],
        out_specs=pl.BlockSpec((WINDOW, B_GROUP * B_HEAD_DIM), lambda g, n: (n, g)),
        out_shape=jax.ShapeDtypeStruct((t_rows, B_Q_HEADS * B_HEAD_DIM), BF16),
        compiler_params=_cparams("parallel", "arbitrary"),
        name="window_attention",
    )(sink, qkv, qkv, qkv, qkv, qkv, qkv, qkv, qkv, qkv)


ROW_PIECES = 8


def _store_packed_rows(ref, v):
    m, half = v.shape[0], v.shape[1] // 2
    lo = pltpu.bitcast(v[:, :half].astype(BF16).astype(F32), jnp.uint32) >> 16
    hi = pltpu.bitcast(v[:, half:].astype(BF16).astype(F32), jnp.uint32) & jnp.uint32(0xFFFF0000)
    w = lo | hi
    for j in range(ROW_PIECES):
        ref[pl.ds(j, m, stride=ROW_PIECES), :] = w[:, j * LANES:(j + 1) * LANES]


def _load_packed_rows(ref, m, r0=0):
    w = jnp.concatenate([ref[pl.ds(r0 * ROW_PIECES + j, m, stride=ROW_PIECES), :] for j in range(ROW_PIECES)],
                        axis=1)
    lo = pltpu.bitcast(w << 16, F32)
    hi = pltpu.bitcast(w & jnp.uint32(0xFFFF0000), F32)
    return lo, hi


def _post_mixer(y, x, g1_ref, g_ref, sh_ref, sc_ref, wr_ref, br_ref, xo_ref, h_ref, lg_ref, row0, n_ctx):
    row = row0 + lax.broadcasted_iota(jnp.int32, (x.shape[0], 1), 0)
    g1 = jnp.where(row < n_ctx, g1_ref[0:1, :], g1_ref[1:2, :])
    xn = x + g1 * y
    xo_ref[...] = xn
    h = _norm_mod(xn, g_ref[...], sh_ref[...], sc_ref[...], row0, n_ctx)
    _store_packed_rows(h_ref, h)
    h_hi = h.astype(BF16)
    h_lo = (h - h_hi.astype(F32)).astype(BF16)
    lg_ref[...] = (jnp.dot(h_hi, wr_ref[0], preferred_element_type=F32)
                   + jnp.dot(h_lo, wr_ref[0], preferred_element_type=F32)
                   + jnp.dot(h_hi, wr_ref[1], preferred_element_type=F32) + br_ref[...])


def _attn_out_kernel(ya_ref, yb_ref, x_ref, woa_ref, wob_ref, g1_ref, g_ref, sh_ref, sc_ref, wr_ref, br_ref,
                     xo_ref, h_ref, lg_ref, *, n_ctx, tm):
    y = (jnp.dot(ya_ref[...], woa_ref[...], preferred_element_type=F32)
         + jnp.dot(yb_ref[...], wob_ref[...], preferred_element_type=F32))
    _post_mixer(y, x_ref[...], g1_ref, g_ref, sh_ref, sc_ref, wr_ref, br_ref, xo_ref, h_ref, lg_ref,
                pl.program_id(0) * tm, n_ctx)


def _glu_out_kernel(y_ref, x_ref, wv_ref, wg_ref, g1_ref, g_ref, sh_ref, sc_ref, wr_ref, br_ref,
                    xo_ref, h_ref, lg_ref, *, n_ctx, tm):
    a = jax.nn.gelu(y_ref[...], approximate=True).astype(BF16)
    val = jnp.dot(a, wv_ref[...], preferred_element_type=F32)
    gate = jnp.dot(a, wg_ref[...], preferred_element_type=F32)
    _post_mixer(val * jax.nn.sigmoid(gate), x_ref[...], g1_ref, g_ref, sh_ref, sc_ref, wr_ref, br_ref,
                xo_ref, h_ref, lg_ref, pl.program_id(0) * tm, n_ctx)


def _mixer_out(kernel_fn, acts, x, weights, mod, cols, g2row, wr, br, n_ctx, name):
    t_rows, d = x.shape
    tm = 256
    row = lambda i: (i, 0)
    const = lambda i: (0, 0)
    in_specs = ([pl.BlockSpec((tm, a.shape[1]), row) for a in acts]
                + [pl.BlockSpec((tm, d), row)]
                + [pl.BlockSpec(w.shape, const) for w in weights]
                + [pl.BlockSpec((SUBLANES, d), lambda i, c=c: (0, c)) for c in cols[:1]]
                + [pl.BlockSpec((1, d), const)]
                + [pl.BlockSpec((SUBLANES, d), lambda i, c=c: (0, c)) for c in cols[1:]]
                + [pl.BlockSpec(wr.shape, lambda i: (0, 0, 0)), pl.BlockSpec(br.shape, const)])
    return pl.pallas_call(
        functools.partial(kernel_fn, n_ctx=n_ctx, tm=tm),
        grid=(t_rows // tm,),
        in_specs=in_specs,
        out_specs=[pl.BlockSpec((tm, d), row), pl.BlockSpec((tm * ROW_PIECES, LANES), row),
                   pl.BlockSpec((tm, LANES), row)],
        out_shape=[jax.ShapeDtypeStruct((t_rows, d), F32),
                   jax.ShapeDtypeStruct((t_rows * ROW_PIECES, LANES), jnp.uint32),
                   jax.ShapeDtypeStruct((t_rows, LANES), F32)],
        compiler_params=_cparams("parallel"),
        name=name,
    )(*acts, x, *weights, mod, g2row, mod, mod, wr, br)


def _router_kernel(lg_ref, idx_ref, gate_ref, cnt_ref, *, n_exp):
    @pl.when(pl.program_id(0) == 0)
    def _():
        cnt_ref[...] = jnp.zeros_like(cnt_ref)

    lt = lg_ref[...].T[0:n_exp, :]
    eid = lax.broadcasted_iota(jnp.int32, lt.shape, 0).astype(F32)
    vals, idxs = [], []
    hist = jnp.zeros(lt.shape, F32)
    for _ in range(TOP_K):
        mv = jnp.max(lt, axis=0, keepdims=True)
        ix = jnp.min(jnp.where(lt == mv, eid, float(n_exp)), axis=0, keepdims=True)
        sel = eid == ix
        hist = hist + sel.astype(F32)
        lt = jnp.where(sel, -jnp.inf, lt)
        vals.append(mv)
        idxs.append(ix)
    es = [jnp.exp(v - vals[0]) for v in vals]
    den = es[0] + es[1] + es[2] + es[3]
    pad_f = jnp.zeros((SUBLANES - TOP_K, lt.shape[1]), F32)
    idx_ref[...] = jnp.concatenate(idxs + [pad_f], axis=0).astype(jnp.int32)
    gate_ref[...] = jnp.concatenate([e / den for e in es] + [pad_f], axis=0)
    cnt_ref[...] += jnp.sum(hist, axis=1, keepdims=True)


def _router(logits, n_exp):
    t_rows = logits.shape[0]
    tm = 256
    return pl.pallas_call(
        functools.partial(_router_kernel, n_exp=n_exp),
        grid=(t_rows // tm,),
        in_specs=[pl.BlockSpec((tm, LANES), lambda i: (i, 0))],
        out_specs=[pl.BlockSpec((SUBLANES, tm), lambda i: (0, i)),
                   pl.BlockSpec((SUBLANES, tm), lambda i: (0, i)),
                   pl.BlockSpec((n_exp, 1), lambda i: (0, 0))],
        out_shape=[jax.ShapeDtypeStruct((SUBLANES, t_rows), jnp.int32),
                   jax.ShapeDtypeStruct((SUBLANES, t_rows), F32),
                   jax.ShapeDtypeStruct((n_exp, 1), F32)],
        compiler_params=_cparams("arbitrary"),
        name="router_topk",
    )(logits)


def _dest_kernel(idx_ref, start_ref, dest_ref, carry, *, n_exp, tm):
    @pl.when(pl.program_id(0) == 0)
    def _():
        carry[...] = start_ref[...]

    eid = lax.broadcasted_iota(jnp.int32, (n_exp, tm), 0)
    idx = idx_ref[...]
    sels = [eid == idx[k:k + 1, :] for k in range(TOP_K)]
    total = sels[0].astype(F32) + sels[1].astype(F32) + sels[2].astype(F32) + sels[3].astype(F32)
    rr = lax.broadcasted_iota(jnp.int32, (tm, tm), 0)
    cc = lax.broadcasted_iota(jnp.int32, (tm, tm), 1)
    upper = jnp.where(rr < cc, 1.0, 0.0).astype(BF16)
    before = jnp.dot(total.astype(BF16), upper, preferred_element_type=F32) + carry[...]
    rows = [jnp.sum(jnp.where(sels[k], before, 0.0), axis=0, keepdims=True) for k in range(TOP_K)]
    pad = jnp.zeros((SUBLANES - TOP_K, tm), F32)
    dest_ref[...] = jnp.concatenate(rows + [pad], axis=0).astype(jnp.int32)
    carry[...] += jnp.sum(total, axis=1, keepdims=True)


def _dest_rows(idx, starts, n_exp):
    t_rows = idx.shape[1]
    tm = 256
    return pl.pallas_call(
        functools.partial(_dest_kernel, n_exp=n_exp, tm=tm),
        grid=(t_rows // tm,),
        in_specs=[pl.BlockSpec((SUBLANES, tm), lambda i: (0, i)),
                  pl.BlockSpec((n_exp, 1), lambda i: (0, 0))],
        out_specs=pl.BlockSpec((SUBLANES, tm), lambda i: (0, i)),
        out_shape=jax.ShapeDtypeStruct((SUBLANES, t_rows), jnp.int32),
        scratch_shapes=[pltpu.VMEM((n_exp, 1), F32)],
        compiler_params=_cparams("arbitrary"),
        name="moe_dest_rows",
    )(idx, starts)


def _expert_gu_kernel(be_ref, bv_ref, bn_ref, x_ref, wg_ref, wu_ref, bg_ref, bu_ref, o_ref, wg_s, wu_s, *, blk):
    b = pl.program_id(1)
    nvalid = bv_ref[b]

    @pl.when(bn_ref[b] == 1)
    def _():
        wg_s[...] = wg_ref[...].astype(BF16)
        wu_s[...] = wu_ref[...].astype(BF16)

    @pl.when(nvalid > 0)
    def _():
        rows = lax.broadcasted_iota(jnp.int32, (blk, 1), 0)
        lo, hi = _load_packed_rows(x_ref, blk)
        x = jnp.concatenate([lo.astype(BF16), hi.astype(BF16)], axis=1)
        x = jnp.where(rows < nvalid, x, jnp.zeros_like(x))
        gate = jnp.dot(x, wg_s[...], preferred_element_type=F32) + bg_ref[...]
        up = jnp.dot(x, wu_s[...], preferred_element_type=F32) + bu_ref[...]
        gate = jnp.minimum(gate, SWIGLU_LIMIT)
        up = jnp.clip(up, -SWIGLU_LIMIT, SWIGLU_LIMIT)
        act = (up + 1.0) * (gate * jax.nn.sigmoid(SWIGLU_ALPHA * gate))
        o_ref[...] = act.astype(o_ref.dtype)

    @pl.when(nvalid == 0)
    def _():
        o_ref[...] = jnp.zeros_like(o_ref)


def _expert_dn_kernel(be_ref, bv_ref, bn_ref, a_ref, wd_ref, bd_ref, o_ref, wd_s, *, blk):
    b = pl.program_id(0)

    @pl.when(bn_ref[b] == 1)
    def _():
        wd_s[...] = wd_ref[...].astype(BF16)

    @pl.when(bv_ref[b] > 0)
    def _():
        y = jnp.dot(a_ref[...], wd_s[...], preferred_element_type=F32) + bd_ref[...]
        _store_packed_rows(o_ref, y)

    @pl.when(bv_ref[b] == 0)
    def _():
        o_ref[...] = jnp.zeros_like(o_ref)


def _expert_matmul(xs, blk_e, blk_valid, blk_new, w_gu, b_gu, w_dn, b_dn, layer):
    d, f_dim = w_dn.shape[3], w_dn.shape[2]
    n_rows = xs.shape[0] // ROW_PIECES
    blk = EXPERT_BLOCK
    tf = 512
    nf = f_dim // tf
    n_blocks = n_rows // blk
    packed_rows = pl.BlockSpec((blk * ROW_PIECES, LANES), lambda f, b, be, bv, bn: (b, 0))
    act = pl.pallas_call(
        functools.partial(_expert_gu_kernel, blk=blk),
        grid_spec=pltpu.PrefetchScalarGridSpec(
            num_scalar_prefetch=3,
            grid=(nf, n_blocks),
            in_specs=[packed_rows,
                      pl.BlockSpec((None, None, d, tf), lambda f, b, be, bv, bn: (layer, be[b], 0, f)),
                      pl.BlockSpec((None, None, d, tf), lambda f, b, be, bv, bn: (layer, be[b], 0, nf + f)),
                      pl.BlockSpec((None, None, 1, tf), lambda f, b, be, bv, bn: (layer, be[b], 0, f)),
                      pl.BlockSpec((None, None, 1, tf), lambda f, b, be, bv, bn: (layer, be[b], 0, nf + f))],
            out_specs=pl.BlockSpec((blk, tf), lambda f, b, be, bv, bn: (b, f)),
            scratch_shapes=[pltpu.VMEM((d, tf), BF16), pltpu.VMEM((d, tf), BF16)]),
        out_shape=jax.ShapeDtypeStruct((n_rows, f_dim), BF16),
        compiler_params=_cparams("arbitrary", "arbitrary"),
        name="expert_gate_up",
    )(blk_e, blk_valid, blk_new, xs, w_gu, w_gu, b_gu, b_gu)
    assert d == 2 * ROW_PIECES * LANES
    sub = blk // DOWN_BLOCK
    part = jnp.arange(n_blocks * sub, dtype=jnp.int32) % sub
    dn_e = jnp.repeat(blk_e, sub)
    dn_valid = jnp.clip(jnp.repeat(blk_valid, sub) - part * DOWN_BLOCK, 0, DOWN_BLOCK)
    dn_new = jnp.where(part == 0, jnp.repeat(blk_new, sub), 0)
    return pl.pallas_call(
        functools.partial(_expert_dn_kernel, blk=DOWN_BLOCK),
        grid_spec=pltpu.PrefetchScalarGridSpec(
            num_scalar_prefetch=3,
            grid=(n_blocks * sub,),
            in_specs=[pl.BlockSpec((DOWN_BLOCK, f_dim), lambda b, be, bv, bn: (b, 0)),
                      pl.BlockSpec((None, None, f_dim, d), lambda b, be, bv, bn: (layer, be[b], 0, 0)),
                      pl.BlockSpec((None, None, 1, d), lambda b, be, bv, bn: (layer, be[b], 0, 0))],
            out_specs=pl.BlockSpec((DOWN_BLOCK * ROW_PIECES, LANES), lambda b, be, bv, bn: (b, 0)),
            scratch_shapes=[pltpu.VMEM((f_dim, d), BF16)]),
        out_shape=jax.ShapeDtypeStruct((n_rows * ROW_PIECES, LANES), jnp.uint32),
        compiler_params=_cparams("arbitrary"),
        name="expert_down",
    )(dn_e, dn_valid, dn_new, act, w_dn, b_dn)


def _combine_kernel(y0_ref, y1_ref, y2_ref, y3_ref, gate_ref, x_ref, g2_ref, fg_ref, o_ref, *, n_ctx, tm, final):
    gates = gate_ref[...]
    f_lo, f_hi = None, None
    for k, y_ref in enumerate((y0_ref, y1_ref, y2_ref, y3_ref)):
        lo, hi = _load_packed_rows(y_ref, tm)
        gk = gates[:, k:k + 1]
        f_lo = lo * gk if f_lo is None else f_lo + lo * gk
        f_hi = hi * gk if f_hi is None else f_hi + hi * gk
    f = jnp.concatenate([f_lo, f_hi], axis=1)
    row = pl.program_id(0) * tm + lax.broadcasted_iota(jnp.int32, (tm, 1), 0)
    g2 = jnp.where(row < n_ctx, g2_ref[0:1, :], g2_ref[1:2, :])
    xn = x_ref[...] + g2 * f
    if final:
        ms = jnp.mean(xn * xn, axis=-1, keepdims=True)
        xn = xn * lax.rsqrt(ms + EPS) * fg_ref[...]
    o_ref[...] = xn


def _combine(yk, gates_t, x, mod, g2_col, final_g, n_ctx, final):
    t_rows, d = x.shape
    tm = 256
    nt = t_rows // tm
    return pl.pallas_call(
        functools.partial(_combine_kernel, n_ctx=n_ctx, tm=tm, final=final),
        grid=(nt,),
        in_specs=[pl.BlockSpec((tm * ROW_PIECES, LANES), lambda i, k=k: (k * nt + i, 0)) for k in range(TOP_K)]
                 + [pl.BlockSpec((tm, SUBLANES), lambda i: (i, 0)),
                  pl.BlockSpec((tm, d), lambda i: (i, 0)),
                  pl.BlockSpec((SUBLANES, d), lambda i: (0, g2_col)),
                  pl.BlockSpec((1, d), lambda i: (0, 0))],
        out_specs=pl.BlockSpec((tm, d), lambda i: (i, 0)),
        out_shape=jax.ShapeDtypeStruct((t_rows, d), F32),
        compiler_params=_cparams("parallel"),
        name="moe_combine",
    )(yk, yk, yk, yk, gates_t, x, mod, final_g)


SC_WINDOW = 128


def _sc_mesh():
    return plsc.VectorSubcoreMesh(core_axis_name="core", subcore_axis_name="subcore")


def _sc_scatter_rows(x, dest, n_out):
    n_src_blocks = x.shape[0] // SC_WINDOW

    @functools.partial(pl.kernel, out_type=jax.ShapeDtypeStruct((n_out, LANES), x.dtype), mesh=_sc_mesh(),
                       scratch_types=[], name="sc_dispatch_rows")
    def run(x_hbm, i_hbm, o_hbm):
        def body(x_vmem, i_vmem):
            pltpu.sync_copy(x_vmem, o_hbm.at[i_vmem.at[0]])

        pltpu.emit_pipeline(
            body, grid=(dest.shape[1] // SC_WINDOW,),
            in_specs=[pl.BlockSpec((SC_WINDOW, LANES), lambda i: (i % n_src_blocks, 0)),
                      pl.BlockSpec((1, SC_WINDOW), lambda i: (0, i))],
            out_specs=[], core_axis_name=("core", "subcore"),
            dimension_semantics=(pltpu.PARALLEL,))(x_hbm, i_hbm)

    return run(x, dest)


def _sc_gather_rows(y, idx):
    n = idx.shape[1]

    @functools.partial(pl.kernel, out_type=jax.ShapeDtypeStruct((n, LANES), y.dtype), mesh=_sc_mesh(),
                       scratch_types=[], name="sc_combine_rows")
    def run(y_hbm, i_hbm, o_hbm):
        def body(i_vmem, o_vmem):
            pltpu.sync_copy(y_hbm.at[i_vmem.at[0]], o_vmem)

        pltpu.emit_pipeline(
            body, grid=(n // SC_WINDOW,),
            in_specs=[pl.BlockSpec((1, SC_WINDOW), lambda i: (0, i))],
            out_specs=[pl.BlockSpec((SC_WINDOW, LANES), lambda i: (i, 0))],
            core_axis_name=("core", "subcore"),
            dimension_semantics=(pltpu.PARALLEL,))(i_hbm, o_hbm)

    return run(y, idx)


def _moe(h, logits, x, mod, g2_col, w_gu, b_gu, w_dn, b_dn, layer, n_ctx, final_g, final):
    t_rows = x.shape[0]
    n_exp = w_gu.shape[1]
    idx, gates, counts = _router(logits, n_exp)
    counts = counts[:, 0].astype(jnp.int32)
    blk = EXPERT_BLOCK
    padded = (counts + blk - 1) // blk * blk
    pends = jnp.cumsum(padded)
    pstarts = pends - padded
    dest = _dest_rows(idx, pstarts.astype(F32)[:, None], n_exp)[:TOP_K]
    n_blocks = -(-(t_rows * TOP_K) // blk) + n_exp
    n_rows = n_blocks * blk
    blk_start = jnp.arange(n_blocks, dtype=jnp.int32) * blk
    blk_e = jnp.minimum(jnp.sum(blk_start[:, None] >= pends[None, :], axis=1), n_exp - 1).astype(jnp.int32)
    blk_valid = jnp.clip(pstarts[blk_e] + counts[blk_e] - blk_start, 0, blk).astype(jnp.int32)
    blk_valid = jnp.where(blk_start < pends[-1], blk_valid, 0)
    blk_new = jnp.concatenate([jnp.ones((1,), jnp.int32), (blk_e[1:] != blk_e[:-1]).astype(jnp.int32)])
    dest8 = (dest[:, :, None] * ROW_PIECES + jnp.arange(ROW_PIECES, dtype=jnp.int32)).reshape(1, -1)
    xs = _sc_scatter_rows(h, dest8, n_rows * ROW_PIECES)
    ys = _expert_matmul(xs, blk_e, blk_valid, blk_new, w_gu, b_gu, w_dn, b_dn, layer)
    yk = _sc_gather_rows(ys, dest8)
    return _combine(yk, gates.T, x, mod, g2_col, final_g, n_ctx, final)


def _s5_tables(a_re, a_im, log_step, b_re, b_im, c_re, c_im, d_skip):
    tc = S5_TC
    n_grp, n_st = a_re.shape[1], a_re.shape[2]
    lr = jnp.minimum(a_re.astype(F32), -1e-4)
    li = a_im.astype(F32)
    dt = jnp.exp(log_step.astype(F32))[..., None]
    dd = jnp.arange(tc + 1, dtype=F32)[:, None, None, None]
    mag = jnp.exp(lr * dt * dd)
    pr, pi = mag * jnp.cos(li * dt * dd), mag * jnp.sin(li * dt * dd)
    ar, ai = pr[1], pi[1]
    den = lr * lr + li * li
    nr = ar - 1.0
    zr = (nr * lr + ai * li) / den
    zi = (ai * lr - nr * li) / den
    br, bi = b_re.astype(F32), b_im.astype(F32)
    bbr = zr[..., None] * br - zi[..., None] * bi
    bbi = zr[..., None] * bi + zi[..., None] * br
    abr = pr[:tc, ..., None] * bbr - pi[:tc, ..., None] * bbi
    abi = pr[:tc, ..., None] * bbi + pi[:tc, ..., None] * bbr
    cr, ci = c_re.astype(F32), c_im.astype(F32)
    kern = (jnp.einsum('xgip,dxgpj->dxgij', cr, abr, precision=HIGHEST)
            - jnp.einsum('xgip,dxgpj->dxgij', ci, abi, precision=HIGHEST))
    def toeplitz(kx):
        ext = jnp.concatenate([jnp.zeros_like(kx), kx], axis=0)
        return jnp.stack([ext[tc - a:2 * tc - a] for a in range(tc)], axis=0)

    m_f = toeplitz(kern[:, 0])
    m_b = toeplitz(kern[:, 1]).transpose(1, 0, 2, 3, 4)
    m_tot = (m_f + m_b).transpose(2, 0, 4, 1, 3)
    eye_t = jnp.eye(tc, dtype=F32)
    eye_i = jnp.eye(S5_GROUP, dtype=F32)
    dsk = d_skip.astype(F32).reshape(n_grp, S5_GROUP)
    m_tot = m_tot + (eye_t[None, :, None, :, None] * eye_i[None, None, :, None, :]
                     * dsk[:, None, :, None, None])
    m_tot = m_tot.reshape(n_grp, tc * S5_GROUP, tc * S5_GROUP)
    pw_f = tc - 1 - jnp.arange(tc)
    pw_b = jnp.arange(tc)

    def b_cols(part, pw, x):
        return part[pw, x].transpose(1, 0, 3, 2).reshape(n_grp, tc * S5_GROUP, n_st)

    b_mat = jnp.concatenate([b_cols(abr, pw_f, 0), b_cols(abi, pw_f, 0),
                             b_cols(abr, pw_b, 1), b_cols(abi, pw_b, 1)], axis=-1)
    pcf = 1 + jnp.arange(tc)
    pcb = tc - jnp.arange(tc)

    def c_rows(pw, x):
        prx, pix = pr[pw, x], pi[pw, x]
        re_c = cr[x][None] * prx[:, :, None, :] - ci[x][None] * pix[:, :, None, :]
        im_c = -(cr[x][None] * pix[:, :, None, :] + ci[x][None] * prx[:, :, None, :])
        to_rows = lambda z: z.transpose(1, 3, 0, 2).reshape(n_grp, n_st, tc * S5_GROUP)
        return to_rows(re_c), to_rows(im_c)

    c_mat = jnp.concatenate(list(c_rows(pcf, 0)) + list(c_rows(pcb, 1)), axis=1)
    prt, pit = pr[tc], pi[tc]
    a1 = jnp.concatenate([prt[0], prt[0], prt[1], prt[1]], axis=-1)
    a2 = jnp.concatenate([-pit[0], pit[0], -pit[1], pit[1]], axis=-1)
    return m_tot.astype(BF16), b_mat.astype(BF16), c_mat.astype(BF16), a1, a2


S5_LANE_GROUPS = LANES // S5_GROUP


def _s5_pack(u_ref, c0, cc):
    xs = [u_ref[pl.ds(c0 * S5_TC + tau, cc, stride=S5_TC), :] for tau in range(S5_TC)]
    return [jnp.concatenate([xs[tau][:, g * S5_GROUP:(g + 1) * S5_GROUP] for tau in range(S5_TC)], axis=1)
            for g in range(S5_LANE_GROUPS)]


def _s5_in_kernel(u_ref, b_ref, o_ref, *, cc):
    def body(ci, carry):
        c0 = pl.multiple_of(ci * cc, SUBLANES)
        ugs = _s5_pack(u_ref, c0, cc)
        for g in range(S5_LANE_GROUPS):
            o_ref[g, pl.ds(c0, cc), :] = jnp.dot(ugs[g].astype(BF16), b_ref[g], preferred_element_type=F32)
        return carry

    lax.fori_loop(0, u_ref.shape[0] // (S5_TC * cc), body, 0)


def _s5_out_kernel(u_ref, z_ref, m_ref, c_ref, o_ref, *, cc):
    def body(ci, carry):
        c0 = pl.multiple_of(ci * cc, SUBLANES)
        ugs = _s5_pack(u_ref, c0, cc)
        ys = [jnp.dot(ugs[g].astype(BF16), m_ref[g], preferred_element_type=F32)
              + jnp.dot(z_ref[g, pl.ds(c0, cc), :].astype(BF16), c_ref[g], preferred_element_type=F32)
              for g in range(S5_LANE_GROUPS)]
        for tau in range(S5_TC):
            row = jnp.concatenate([y[:, tau * S5_GROUP:(tau + 1) * S5_GROUP] for y in ys], axis=1)
            o_ref[pl.ds(c0 * S5_TC + tau, cc, stride=S5_TC), :] = row
        return carry

    lax.fori_loop(0, u_ref.shape[0] // (S5_TC * cc), body, 0)


def _s5_scan_kernel(h_ref, a1_ref, a2_ref, z_ref, *, n_chunks, n_ctx_chunks):
    half = 2 * S5_STATE
    a1 = a1_ref[...]
    a2 = a2_ref[...]
    a1f, a1b = a1[:, :half], a1[:, half:]
    a2f, a2b = a2[:, :half], a2[:, half:]
    zero = jnp.zeros((a1.shape[0], half), F32)

    def swap(v):
        return pltpu.roll(v, S5_STATE, 1)

    def step(t, carry):
        rf, rfs, rb, rbs = carry
        cb = jnp.where(t < n_ctx_chunks, n_ctx_chunks - 1 - t, n_chunks - 1 - (t - n_ctx_chunks))
        z_ref[t, :, 0:half] = rf
        z_ref[cb, :, half:2 * half] = rb
        hf = h_ref[t, :, 0:half]
        hb = h_ref[cb, :, half:2 * half]
        nrf = a1f * rf + a2f * rfs + hf
        nrfs = a1f * rfs - a2f * rf + swap(hf)
        nrb = a1b * rb + a2b * rbs + hb
        nrbs = a1b * rbs - a2b * rb + swap(hb)
        return nrf, nrfs, nrb, nrbs

    lax.fori_loop(0, n_chunks, step, (zero, zero, zero, zero), unroll=8 if n_chunks % 8 == 0 else 1)


def _s5_mixer_scan(u, tables, n_ctx):
    m_tot, b_mat, c_mat, a1, a2 = tables
    t_rows, width = u.shape
    tc = S5_TC
    n_grp = width // S5_GROUP
    n_chunks = t_rows // tc
    kdim = tc * S5_GROUP
    sdim = 4 * S5_STATE
    gb = S5_LANE_GROUPS
    gs = SUBLANES
    n_split = 2
    half_chunks = n_chunks // n_split
    cc = _pick(half_chunks, (104, 80, 40, 8))
    u_spec = pl.BlockSpec((t_rows // n_split, LANES), lambda j, r: (r, j))
    tab_spec = lambda a, b: pl.BlockSpec((gb, a, b), lambda j, r: (j, 0, 0))
    st_spec = pl.BlockSpec((gb, half_chunks, sdim), lambda j, r: (j, r, 0))
    grid = (width // LANES, n_split)
    hin = pl.pallas_call(
        functools.partial(_s5_in_kernel, cc=cc),
        grid=grid,
        in_specs=[u_spec, tab_spec(kdim, sdim)],
        out_specs=st_spec,
        out_shape=jax.ShapeDtypeStruct((n_grp, n_chunks, sdim), F32),
        compiler_params=_cparams("parallel", "arbitrary"),
        name="s5_chunk_inputs",
    )(u, b_mat)
    hin_t = hin.transpose(1, 0, 2)
    z_t = pl.pallas_call(
        functools.partial(_s5_scan_kernel, n_chunks=n_chunks, n_ctx_chunks=n_ctx // tc),
        grid=(n_grp // gs,),
        in_specs=[pl.BlockSpec((n_chunks, gs, sdim), lambda g: (0, g, 0)),
                  pl.BlockSpec((gs, sdim), lambda g: (g, 0)),
                  pl.BlockSpec((gs, sdim), lambda g: (g, 0))],
        out_specs=pl.BlockSpec((n_chunks, gs, sdim), lambda g: (0, g, 0)),
        out_shape=jax.ShapeDtypeStruct((n_chunks, n_grp, sdim), F32),
        compiler_params=_cparams("parallel"),
        name="s5_chunk_scan",
    )(hin_t, a1, a2)
    z = z_t.transpose(1, 0, 2)
    return pl.pallas_call(
        functools.partial(_s5_out_kernel, cc=cc),
        grid=grid,
        in_specs=[u_spec, st_spec, tab_spec(kdim, kdim), tab_spec(sdim, kdim)],
        out_specs=u_spec,
        out_shape=jax.ShapeDtypeStruct((t_rows, width), F32),
        compiler_params=_cparams("parallel", "arbitrary"),
        name="s5_chunk_outputs",
    )(u, z, m_tot, c_mat)


def _lambda_init(layer):
    return 0.8 - 0.6 * math.exp(-0.3 * layer)


def kernel(x, c, ctx, c_ctx, w_mod, b_mod, norm1_g, norm2_g, final_g, attn_w_qkv, attn_w_o, lambda_q1, lambda_k1, lambda_q2, lambda_k2, subln_g, sink_logit, s5_w_in, s5_a_re, s5_a_im, s5_log_step, s5_b_re, s5_b_im, s5_c_re, s5_c_im, s5_d, s5_w_glu, router_w, router_b, expert_w_gu, expert_b_gu, expert_w_down, expert_b_down):
    bsz, seq, d = x.shape
    assert bsz == 1, "single-sequence block"
    n_ctx = ctx.shape[1]
    depth = w_mod.shape[0]
    n_exp = router_w.shape[2]
    assert n_ctx % 256 == 0 and seq % 256 == 0

    xj = jnp.concatenate([ctx[0], x[0]], axis=0)
    cs = jnp.zeros((SUBLANES, d), F32).at[0].set(c_ctx).at[1].set(c[0])
    b_mod3 = b_mod[:, None, :]
    b_gu4 = expert_b_gu[:, :, None, :]
    b_dn4 = expert_b_down[:, :, None, :]
    wr_pad = jnp.pad(router_w, ((0, 0), (0, 0), (0, LANES - n_exp)))
    wr_hi = wr_pad.astype(BF16)
    wr_pad = jnp.stack([wr_hi, (wr_pad - wr_hi.astype(F32)).astype(BF16)], axis=1)
    br_pad = jnp.pad(router_b, ((0, 0), (0, LANES - n_exp)), constant_values=NEG_INF)[:, None, :]
    final_row = final_g[None, :]

    for i in range(depth):
        last = i == depth - 1
        j = i // 2
        mod = _adaln_mod(cs, w_mod, b_mod3, i)
        if i % 2 == 0:
            sizes = (A_HEADS * 2 * A_QK_DIM, A_HEADS * 2 * A_QK_DIM, A_HEADS * A_V_DIM,
                     B_Q_HEADS * B_HEAD_DIM, B_KV_HEADS * B_HEAD_DIM, B_KV_HEADS * B_HEAD_DIM)
            offs = np.concatenate([[0], np.cumsum(sizes)])
            types = [0, 1, ROPE_NONE, 2, 3, ROPE_NONE]
            ttype = jnp.asarray(np.concatenate([np.full(s // 256, t) for s, t in zip(sizes, types)]), jnp.int32)
            cos_t, sin_t = _rope_tables(n_ctx, seq)
            qkv = _norm_mod_matmul(xj, norm1_g[i][None, :], mod, 0, 1, attn_w_qkv[j].astype(BF16), n_ctx, BF16,
                                   rope=(ttype, cos_t, sin_t))
            qt = qkv[:, offs[0]:offs[1]].T
            ka = qkv[:, offs[1]:offs[2]]
            vt = qkv[:, offs[2]:offs[3]].T.reshape(A_HEADS, A_V_DIM, -1)
            vt = jnp.concatenate([vt, jnp.ones((A_HEADS, V_AUG - A_V_DIM, vt.shape[2]), BF16)], axis=1)
            vt = vt.reshape(A_HEADS * V_AUG, -1)
            f32 = F32
            li = _lambda_init(i)
            lam = (jnp.exp(jnp.sum(lambda_q1[j].astype(f32) * lambda_k1[j].astype(f32)))
                   - jnp.exp(jnp.sum(lambda_q2[j].astype(f32) * lambda_k2[j].astype(f32))) + li)
            attn_args = (lam.reshape(1), subln_g[j][:, None], 1.0 - li)
            ya = jnp.concatenate([_diff_attention(qt[:, :n_ctx], ka[:n_ctx], vt[:, :n_ctx], *attn_args),
                                  _diff_attention(qt[:, n_ctx:], ka, vt, *attn_args)], axis=0)
            yb = _window_attention(qkv, sink_logit[j], n_ctx, int(offs[3]) // LANES, int(offs[4]) // LANES,
                                   int(offs[5]) // LANES)
            w_o = attn_w_o[j].astype(BF16)
            na = A_HEADS * A_V_DIM
            xj, h2, logits = _mixer_out(_attn_out_kernel, [ya, yb], xj, [w_o[:na], w_o[na:]], mod, (2, 3, 4),
                                        norm2_g[i][None, :], wr_pad[i], br_pad[i], n_ctx, "attn_out_router")
        else:
            u = _norm_mod_matmul(xj, norm1_g[i][None, :], mod, 0, 1, s5_w_in[j].astype(BF16), n_ctx, F32)
            tables = _s5_tables(s5_a_re[j], s5_a_im[j], s5_log_step[j], s5_b_re[j], s5_b_im[j],
                                s5_c_re[j], s5_c_im[j], s5_d[j])
            y = _s5_mixer_scan(u, tables, n_ctx)
            w_glu = s5_w_glu[j].astype(BF16)
            xj, h2, logits = _mixer_out(_glu_out_kernel, [y], xj, [w_glu[:, :d], w_glu[:, d:]], mod, (2, 3, 4),
                                        norm2_g[i][None, :], wr_pad[i], br_pad[i], n_ctx, "glu_out_router")
        xj = _moe(h2, logits, xj, mod, 5, expert_w_gu, b_gu4, expert_w_down, b_dn4, i, n_ctx, final_row, last)
    return xj[n_ctx:][None]
```

```python
import functools
import math

import jax
import jax.numpy as jnp
import numpy as np
from jax import lax
from jax.experimental import pallas as pl
from jax.experimental.pallas import tpu as pltpu
from jax.experimental.pallas import tpu_sc as plsc

F32 = jnp.float32
BF16 = jnp.bfloat16
HIGHEST = lax.Precision.HIGHEST

V7X_VMEM_BYTES = 64 * 1024 * 1024
VMEM_LIMIT = V7X_VMEM_BYTES - 8 * 1024 * 1024
LANES = 128
SUBLANES = 8

GRID_W = 64
N_MOD = 6
EPS = 1e-6
NEG_INF = -1e30
ROPE_THETA = 10000.0
A_HEADS = 8
A_QK_DIM = 64
A_V_DIM = 128
B_Q_HEADS = 8
B_KV_HEADS = 2
B_GROUP = 4
B_HEAD_DIM = 128
WINDOW = 128
S5_GROUP = 16
S5_STATE = 64
S5_TC = 16
TOP_K = 4
SWIGLU_LIMIT = 7.0
SWIGLU_ALPHA = 1.702
EXPERT_BLOCK = 512
DOWN_BLOCK = 256
GATE_UP_TILE = 1024
LOG2E = 1.4426950408889634


def _pick(n, cands):
    for c in cands:
        if n % c == 0:
            return c
    raise ValueError(f"no tile for {n} in {cands}")


def _cparams(*sem):
    return pltpu.CompilerParams(dimension_semantics=sem, vmem_limit_bytes=VMEM_LIMIT)


def _mod_kernel(c_ref, w_ref, b_ref, o_ref):
    cv = c_ref[...]
    s = cv * jax.nn.sigmoid(cv)
    o_ref[...] = jnp.dot(s, w_ref[...], preferred_element_type=F32, precision=HIGHEST) + b_ref[...]


def _adaln_mod(cs, w_mod, b_mod, layer):
    d, n = w_mod.shape[1], w_mod.shape[2]
    tn = _pick(n, (1024, 512, 256, 128))
    return pl.pallas_call(
        _mod_kernel,
        grid=(n // tn,),
        in_specs=[pl.BlockSpec((SUBLANES, d), lambda j: (0, 0)),
                  pl.BlockSpec((None, d, tn), lambda j: (layer, 0, j)),
                  pl.BlockSpec((None, 1, tn), lambda j: (layer, 0, j))],
        out_specs=pl.BlockSpec((SUBLANES, tn), lambda j: (0, j)),
        out_shape=jax.ShapeDtypeStruct((SUBLANES, n), F32),
        compiler_params=_cparams("parallel"),
        name="adaln_mod",
    )(cs, w_mod, b_mod)


def _norm_mod(x, g, sh2, sc2, row0, n_ctx):
    ms = jnp.mean(x * x, axis=-1, keepdims=True)
    y = x * lax.rsqrt(ms + EPS) * g
    row = row0 + lax.broadcasted_iota(jnp.int32, (x.shape[0], 1), 0)
    is_ctx = row < n_ctx
    sc = jnp.where(is_ctx, sc2[0:1, :], sc2[1:2, :])
    sh = jnp.where(is_ctx, sh2[0:1, :], sh2[1:2, :])
    return y * (1.0 + sc) + sh


ROPE_NONE = 4


def _rope_store(acc, cos_ref, sin_ref, o_ref, shift):
    cosv, sinv = cos_ref[...], sin_ref[...]
    for cgrp in range(acc.shape[1] // LANES):
        a = acc[:, cgrp * LANES:(cgrp + 1) * LANES]
        lane = lax.broadcasted_iota(jnp.int32, a.shape, 1)
        in_second = (lane & (2 * shift - 1)) >= shift
        sw = jnp.where(in_second, pltpu.roll(a, shift, 1), pltpu.roll(a, LANES - shift, 1))
        o_ref[:, cgrp * LANES:(cgrp + 1) * LANES] = (a * cosv + sw * sinv).astype(o_ref.dtype)


NORM_ROWS = 256


def _norm_mod_to_scratch(x_ref, g_ref, sh_ref, sc_ref, h_scr, row0, n_ctx):
    def body(r, carry):
        off = pl.multiple_of(r * NORM_ROWS, NORM_ROWS)
        h = _norm_mod(x_ref[pl.ds(off, NORM_ROWS), :], g_ref[...], sh_ref[...], sc_ref[...], row0 + off, n_ctx)
        h_scr[pl.ds(off, NORM_ROWS), :] = h.astype(h_scr.dtype)
        return carry

    lax.fori_loop(0, x_ref.shape[0] // NORM_ROWS, body, 0)


def _nmm_rope_kernel(tt_ref, x_ref, g_ref, sh_ref, sc_ref, w_ref, cos_ref, sin_ref, o_ref, h_scr,
                     *, n_ctx, tm):
    i = pl.program_id(0)
    j = pl.program_id(1)

    @pl.when(j == 0)
    def _():
        _norm_mod_to_scratch(x_ref, g_ref, sh_ref, sc_ref, h_scr, i * tm, n_ctx)

    acc = jnp.dot(h_scr[...], w_ref[...], preferred_element_type=F32)
    t = tt_ref[j]

    @pl.when(t == ROPE_NONE)
    def _():
        o_ref[...] = acc.astype(o_ref.dtype)

    @pl.when(t < 2)
    def _():
        _rope_store(acc, cos_ref, sin_ref, o_ref, A_QK_DIM // 4)

    @pl.when(jnp.logical_and(t >= 2, t < ROPE_NONE))
    def _():
        _rope_store(acc, cos_ref, sin_ref, o_ref, B_HEAD_DIM // 4)


def _nmm_plain_kernel(x_ref, g_ref, sh_ref, sc_ref, w_ref, o_ref, h_scr, *, n_ctx, tm):
    i = pl.program_id(0)
    j = pl.program_id(1)

    @pl.when(j == 0)
    def _():
        _norm_mod_to_scratch(x_ref, g_ref, sh_ref, sc_ref, h_scr, i * tm, n_ctx)

    o_ref[...] = jnp.dot(h_scr[...], w_ref[...], preferred_element_type=F32).astype(o_ref.dtype)


def _norm_mod_matmul(x, g, mod, sh_col, sc_col, w, n_ctx, out_dtype, rope=None):
    t_rows, d = x.shape
    n = w.shape[1]
    tm = _pick(t_rows, (1280, 1024, 512, 256, 128))
    tn = 256
    grid = (t_rows // tm, n // tn)
    kern_kw = dict(n_ctx=n_ctx, tm=tm)
    scratch = [pltpu.VMEM((tm, d), BF16)]
    out_shape = jax.ShapeDtypeStruct((t_rows, n), out_dtype)
    if rope is None:
        return pl.pallas_call(
            functools.partial(_nmm_plain_kernel, **kern_kw),
            grid=grid,
            in_specs=[pl.BlockSpec((tm, d), lambda i, j: (i, 0)),
                      pl.BlockSpec((1, d), lambda i, j: (0, 0)),
                      pl.BlockSpec((SUBLANES, d), lambda i, j: (0, sh_col)),
                      pl.BlockSpec((SUBLANES, d), lambda i, j: (0, sc_col)),
                      pl.BlockSpec((d, tn), lambda i, j: (0, j))],
            out_specs=pl.BlockSpec((tm, tn), lambda i, j: (i, j)),
            out_shape=out_shape,
            scratch_shapes=scratch,
            compiler_params=_cparams("parallel", "arbitrary"),
            name="norm_mod_matmul",
        )(x, g, mod, mod, w)
    ttype, cos_t, sin_t = rope
    return pl.pallas_call(
        functools.partial(_nmm_rope_kernel, **kern_kw),
        grid_spec=pltpu.PrefetchScalarGridSpec(
            num_scalar_prefetch=1,
            grid=grid,
            in_specs=[pl.BlockSpec((tm, d), lambda i, j, tt: (i, 0)),
                      pl.BlockSpec((1, d), lambda i, j, tt: (0, 0)),
                      pl.BlockSpec((SUBLANES, d), lambda i, j, tt: (0, sh_col)),
                      pl.BlockSpec((SUBLANES, d), lambda i, j, tt: (0, sc_col)),
                      pl.BlockSpec((d, tn), lambda i, j, tt: (0, j)),
                      pl.BlockSpec((None, tm, LANES), lambda i, j, tt: (tt[j], i, 0)),
                      pl.BlockSpec((None, tm, LANES), lambda i, j, tt: (tt[j], i, 0))],
            out_specs=pl.BlockSpec((tm, tn), lambda i, j, tt: (i, j)),
            scratch_shapes=scratch),
        out_shape=out_shape,
        compiler_params=_cparams("parallel", "arbitrary"),
        name="norm_mod_qkv_rope",
    )(ttype, x, g, mod, mod, w, cos_t, sin_t)


def _rope_tables(n_ctx, seq):
    pos = jnp.arange(seq)
    rows = (pos // GRID_W).astype(F32)
    cols = (pos % GRID_W).astype(F32)

    def tab(dim):
        quarter = dim // 4
        freqs = ROPE_THETA ** (-jnp.arange(quarter, dtype=F32) / quarter)
        ar, ac = rows[:, None] * freqs, cols[:, None] * freqs
        cosv = jnp.concatenate([jnp.cos(ar), jnp.cos(ar), jnp.cos(ac), jnp.cos(ac)], axis=1)
        sinv = jnp.concatenate([-jnp.sin(ar), jnp.sin(ar), -jnp.sin(ac), jnp.sin(ac)], axis=1)
        reps = LANES // dim
        cosv, sinv = jnp.tile(cosv, (1, reps)), jnp.tile(sinv, (1, reps))
        cosv = jnp.concatenate([jnp.ones((n_ctx, LANES), F32), cosv], axis=0)
        sinv = jnp.concatenate([jnp.zeros((n_ctx, LANES), F32), sinv], axis=0)
        return cosv, sinv

    ca, sa = tab(A_QK_DIM)
    cb, sb = tab(B_HEAD_DIM)
    qa = (A_QK_DIM ** -0.5) * LOG2E
    qb = B_HEAD_DIM ** -0.5
    one, zero = jnp.ones_like(ca), jnp.zeros_like(ca)
    cos_t = jnp.stack([ca * qa, ca, cb * qb, cb, one])
    sin_t = jnp.stack([sa * qa, sa, sb * qb, sb, zero])
    return cos_t, sin_t


V_AUG = A_V_DIM + 16


def _flash_kernel(lam_ref, qt_ref, k_ref, vt_ref, g_ref, o_ref, s_a, s_b, acc,
                  *, tq, tk, t_rows, out_scale):
    qt = qt_ref[...]
    row = lax.broadcasted_iota(jnp.int32, qt.shape, 0)
    zero = jnp.zeros_like(qt)
    qm = (jnp.where(row < A_QK_DIM, qt, zero), jnp.where(row >= A_QK_DIM, qt, zero))

    def start(tile, size):
        return tile * size if isinstance(tile, int) else pl.multiple_of(tile * size, size)

    def scores(tile, size, dst):
        kt = k_ref[pl.ds(start(tile, size), size), :]
        for m in range(2):
            dst[m, 0:size, :] = jnp.dot(kt, qm[m], preferred_element_type=F32).astype(BF16)

    def soft_pv(tile, size, src, ms):
        vt = vt_ref[:, pl.ds(start(tile, size), size)]
        new_ms = []
        for m in range(2):
            s = src[m, 0:size, :]
            mx = jnp.maximum(ms[m], jnp.max(s, axis=0, keepdims=True).astype(F32))
            alpha = jnp.exp2(ms[m] - mx)
            p = jnp.exp2(s - mx.astype(BF16))
            acc[m] = alpha * acc[m] + jnp.dot(vt, p, preferred_element_type=F32)
            new_ms.append(mx)
        return tuple(new_ms)

    def finish(ms):
        o1 = acc[0, 0:A_V_DIM, :] / acc[0, A_V_DIM:A_V_DIM + 1, :]
        o2 = acc[1, 0:A_V_DIM, :] / acc[1, A_V_DIM:A_V_DIM + 1, :]
        o = o1 - lam_ref[0] * o2
        var = jnp.mean(o * o, axis=0, keepdims=True)
        o = o * lax.rsqrt(var + EPS) * (g_ref[...] * out_scale)
        o_ref[...] = o.T.astype(o_ref.dtype)

    m0 = jnp.full((1, tq), NEG_INF, F32)
    nk = t_rows // tk
    n_pairs = (nk - 1) // 2
    acc[...] = jnp.zeros_like(acc)
    scores(0, tk, s_a)

    def pair(pp, carry):
        scores(2 * pp + 1, tk, s_b)
        carry = soft_pv(2 * pp, tk, s_a, carry)
        scores(2 * pp + 2, tk, s_a)
        return soft_pv(2 * pp + 1, tk, s_b, carry)

    carry = lax.fori_loop(0, n_pairs, pair, (m0, m0))
    done = 2 * n_pairs
    if nk - done == 2:
        scores(done + 1, tk, s_b)
        carry = soft_pv(done, tk, s_a, carry)
        carry = soft_pv(done + 1, tk, s_b, carry)
    else:
        carry = soft_pv(done, tk, s_a, carry)
    finish(carry)


def _diff_attention(qt, k, vt, lam, subln_col, out_scale):
    t_rows = k.shape[0]
    n_q = qt.shape[1]
    tq = _pick(n_q, (1024, 512, 256))
    tk = _pick(t_rows, (1664, 1280, 1024, 512, 256))
    return pl.pallas_call(
        functools.partial(_flash_kernel, tq=tq, tk=tk, t_rows=t_rows, out_scale=out_scale),
        grid=(A_HEADS, n_q // tq),
        in_specs=[pl.BlockSpec(memory_space=pltpu.SMEM),
                  pl.BlockSpec((LANES, tq), lambda h, i: (h, i)),
                  pl.BlockSpec((t_rows, LANES), lambda h, i: (0, h)),
                  pl.BlockSpec((V_AUG, t_rows), lambda h, i: (h, 0)),
                  pl.BlockSpec((LANES, 1), lambda h, i: (0, 0))],
        out_specs=pl.BlockSpec((tq, LANES), lambda h, i: (i, h)),
        out_shape=jax.ShapeDtypeStruct((n_q, A_HEADS * A_V_DIM), BF16),
        scratch_shapes=[pltpu.VMEM((2, tk, tq), BF16), pltpu.VMEM((2, tk, tq), BF16),
                        pltpu.VMEM((2, V_AUG, tq), F32)],
        compiler_params=_cparams("parallel", "arbitrary"),
        name="diff_attention",
    )(lam, qt, k, vt, subln_col)


def _window_kernel(sink_ref, q_ref, kp_ref, ko_ref, kn_ref, vp_ref, vo_ref, vn_ref, kc_ref, vc_ref, o_ref,
                   *, nb, nb_ctx):
    g = pl.program_id(0)
    n = pl.program_id(1)
    qi = lax.broadcasted_iota(jnp.int32, (WINDOW, WINDOW), 0)
    kk = lax.broadcasted_iota(jnp.int32, (WINDOW, WINDOW), 1)
    own_ok = n >= nb_ctx
    prev_ok = n >= nb_ctx + 1
    next_ok = jnp.logical_and(own_ok, n <= nb - 2)
    m_prev = jnp.logical_and(kk >= qi, prev_ok)
    m_own = jnp.logical_and(kk >= 0, own_ok)
    m_next = jnp.logical_and(kk <= qi, next_ok)
    dn = (((1,), (1,)), ((), ()))
    for r in range(B_GROUP):
        q = q_ref[:, r * B_HEAD_DIM:(r + 1) * B_HEAD_DIM]
        s_p = jnp.where(m_prev, lax.dot_general(q, kp_ref[...], dn, preferred_element_type=F32), NEG_INF)
        s_o = jnp.where(m_own, lax.dot_general(q, ko_ref[...], dn, preferred_element_type=F32), NEG_INF)
        s_n = jnp.where(m_next, lax.dot_general(q, kn_ref[...], dn, preferred_element_type=F32), NEG_INF)
        s_c = lax.dot_general(q, kc_ref[...], dn, preferred_element_type=F32)
        sink = sink_ref[g * B_GROUP + r]
        mx = jnp.maximum(jnp.maximum(jnp.max(s_p, axis=1, keepdims=True), jnp.max(s_o, axis=1, keepdims=True)),
                         jnp.maximum(jnp.max(s_n, axis=1, keepdims=True), jnp.max(s_c, axis=1, keepdims=True)))
        mx = jnp.maximum(mx, sink)
        p_p, p_o, p_n, p_c = jnp.exp(s_p - mx), jnp.exp(s_o - mx), jnp.exp(s_n - mx), jnp.exp(s_c - mx)
        den = (jnp.sum(p_p, axis=1, keepdims=True) + jnp.sum(p_o, axis=1, keepdims=True)
               + jnp.sum(p_n, axis=1, keepdims=True) + jnp.sum(p_c, axis=1, keepdims=True)
               + jnp.exp(sink - mx))
        o = (jnp.dot(p_p.astype(BF16), vp_ref[...], preferred_element_type=F32)
             + jnp.dot(p_o.astype(BF16), vo_ref[...], preferred_element_type=F32)
             + jnp.dot(p_n.astype(BF16), vn_ref[...], preferred_element_type=F32)
             + jnp.dot(p_c.astype(BF16), vc_ref[...], preferred_element_type=F32))
        o_ref[:, r * B_HEAD_DIM:(r + 1) * B_HEAD_DIM] = (o / den).astype(o_ref.dtype)


def _window_attention(qkv, sink, n_ctx, col_q, col_k, col_v):
    t_rows = qkv.shape[0]
    nb = t_rows // WINDOW
    nb_ctx = n_ctx // WINDOW
    gq = B_GROUP * B_HEAD_DIM // LANES

    def kv_spec(col, shift):
        def imap(g, n):
            return (jnp.clip(n + shift, 0, nb - 1), col + g)
        return pl.BlockSpec((WINDOW, LANES), imap)

    return pl.pallas_call(
        functools.partial(_window_kernel, nb=nb, nb_ctx=nb_ctx),
        grid=(B_KV_HEADS, nb),
        in_specs=[pl.BlockSpec(memory_space=pltpu.SMEM),
                  pl.BlockSpec((WINDOW, B_GROUP * B_HEAD_DIM), lambda g, n: (n, col_q // gq + g)),
                  kv_spec(col_k, -1), kv_spec(col_k, 0), kv_spec(col_k, 1),
                  kv_spec(col_v, -1), kv_spec(col_v, 0), kv_spec(col_v, 1),
                  pl.BlockSpec((n_ctx, LANES), lambda g, n: (0, col_k + g)),
                  pl.BlockSpec((n_ctx, LANES), lambda g, n: (0, col_v + g))],
        out_specs=pl.BlockSpec((WINDOW, B_GROUP * B_HEAD_DIM), lambda g, n: (n, g)),
        out_shape=jax.ShapeDtypeStruct((t_rows, B_Q_HEADS * B_HEAD_DIM), BF16),
        compiler_params=_cparams("parallel", "arbitrary"),
        name="window_attention",
    )(sink, qkv, qkv, qkv, qkv, qkv, qkv, qkv, qkv, qkv)


ROW_PIECES = 8


def _store_packed_rows(ref, v):
    m, half = v.shape[0], v.shape[1] // 2
    lo = pltpu.bitcast(v[:, :half].astype(BF16).astype(F32), jnp.uint32) >> 16
    hi = pltpu.bitcast(v[:, half:].astype(BF16).astype(F32), jnp.uint32) & jnp.uint32(0xFFFF0000)
    w = lo | hi
    for j in range(ROW_PIECES):
        ref[pl.ds(j, m, stride=ROW_PIECES), :] = w[:, j * LANES:(j + 1) * LANES]


def _load_packed_rows(ref, m, r0=0):
    w = jnp.concatenate([ref[pl.ds(r0 * ROW_PIECES + j, m, stride=ROW_PIECES), :] for j in range(ROW_PIECES)],
                        axis=1)
    lo = pltpu.bitcast(w << 16, F32)
    hi = pltpu.bitcast(w & jnp.uint32(0xFFFF0000), F32)
    return lo, hi


def _post_mixer(y, x, g1_ref, g_ref, sh_ref, sc_ref, wr_ref, br_ref, xo_ref, h_ref, lg_ref, row0, n_ctx):
    row = row0 + lax.broadcasted_iota(jnp.int32, (x.shape[0], 1), 0)
    g1 = jnp.where(row < n_ctx, g1_ref[0:1, :], g1_ref[1:2, :])
    xn = x + g1 * y
    xo_ref[...] = xn
    h = _norm_mod(xn, g_ref[...], sh_ref[...], sc_ref[...], row0, n_ctx)
    _store_packed_rows(h_ref, h)
    h_hi = h.astype(BF16)
    h_lo = (h - h_hi.astype(F32)).astype(BF16)
    lg_ref[...] = (jnp.dot(h_hi, wr_ref[0], preferred_element_type=F32)
                   + jnp.dot(h_lo, wr_ref[0], preferred_element_type=F32)
                   + jnp.dot(h_hi, wr_ref[1], preferred_element_type=F32) + br_ref[...])


def _attn_out_kernel(ya_ref, yb_ref, x_ref, woa_ref, wob_ref, g1_ref, g_ref, sh_ref, sc_ref, wr_ref, br_ref,
                     xo_ref, h_ref, lg_ref, *, n_ctx, tm):
    y = (jnp.dot(ya_ref[...], woa_ref[...], preferred_element_type=F32)
         + jnp.dot(yb_ref[...], wob_ref[...], preferred_element_type=F32))
    _post_mixer(y, x_ref[...], g1_ref, g_ref, sh_ref, sc_ref, wr_ref, br_ref, xo_ref, h_ref, lg_ref,
                pl.program_id(0) * tm, n_ctx)


def _glu_out_kernel(y_ref, x_ref, wv_ref, wg_ref, g1_ref, g_ref, sh_ref, sc_ref, wr_ref, br_ref,
                    xo_ref, h_ref, lg_ref, *, n_ctx, tm):
    a = jax.nn.gelu(y_ref[...], approximate=True).astype(BF16)
    val = jnp.dot(a, wv_ref[...], preferred_element_type=F32)
    gate = jnp.dot(a, wg_ref[...], preferred_element_type=F32)
    _post_mixer(val * jax.nn.sigmoid(gate), x_ref[...], g1_ref, g_ref, sh_ref, sc_ref, wr_ref, br_ref,
                xo_ref, h_ref, lg_ref, pl.program_id(0) * tm, n_ctx)


def _mixer_out(kernel_fn, acts, x, weights, mod, cols, g2row, wr, br, n_ctx, name):
    t_rows, d = x.shape
    tm = 256
    row = lambda i: (i, 0)
    const = lambda i: (0, 0)
    in_specs = ([pl.BlockSpec((tm, a.shape[1]), row) for a in acts]
                + [pl.BlockSpec((tm, d), row)]
                + [pl.BlockSpec(w.shape, const) for w in weights]
                + [pl.BlockSpec((SUBLANES, d), lambda i, c=c: (0, c)) for c in cols[:1]]
                + [pl.BlockSpec((1, d), const)]
                + [pl.BlockSpec((SUBLANES, d), lambda i, c=c: (0, c)) for c in cols[1:]]
                + [pl.BlockSpec(wr.shape, lambda i: (0, 0, 0)), pl.BlockSpec(br.shape, const)])
    return pl.pallas_call(
        functools.partial(kernel_fn, n_ctx=n_ctx, tm=tm),
        grid=(t_rows // tm,),
        in_specs=in_specs,
        out_specs=[pl.BlockSpec((tm, d), row), pl.BlockSpec((tm * ROW_PIECES, LANES), row),
                   pl.BlockSpec((tm, LANES), row)],
        out_shape=[jax.ShapeDtypeStruct((t_rows, d), F32),
                   jax.ShapeDtypeStruct((t_rows * ROW_PIECES, LANES), jnp.uint32),
                   jax.ShapeDtypeStruct((t_rows, LANES), F32)],
        compiler_params=_cparams("parallel"),
        name=name,
    )(*acts, x, *weights, mod, g2row, mod, mod, wr, br)


def _router_kernel(lg_ref, idx_ref, gate_ref, cnt_ref, *, n_exp):
    @pl.when(pl.program_id(0) == 0)
    def _():
        cnt_ref[...] = jnp.zeros_like(cnt_ref)

    lt = lg_ref[...].T[0:n_exp, :]
    eid = lax.broadcasted_iota(jnp.int32, lt.shape, 0).astype(F32)
    vals, idxs = [], []
    hist = jnp.zeros(lt.shape, F32)
    for _ in range(TOP_K):
        mv = jnp.max(lt, axis=0, keepdims=True)
        ix = jnp.min(jnp.where(lt == mv, eid, float(n_exp)), axis=0, keepdims=True)
        sel = eid == ix
        hist = hist + sel.astype(F32)
        lt = jnp.where(sel, -jnp.inf, lt)
        vals.append(mv)
        idxs.append(ix)
    es = [jnp.exp(v - vals[0]) for v in vals]
    den = es[0] + es[1] + es[2] + es[3]
    pad_f = jnp.zeros((SUBLANES - TOP_K, lt.shape[1]), F32)
    idx_ref[...] = jnp.concatenate(idxs + [pad_f], axis=0).astype(jnp.int32)
    gate_ref[...] = jnp.concatenate([e / den for e in es] + [pad_f], axis=0)
    cnt_ref[...] += jnp.sum(hist, axis=1, keepdims=True)


def _router(logits, n_exp):
    t_rows = logits.shape[0]
    tm = 256
    return pl.pallas_call(
        functools.partial(_router_kernel, n_exp=n_exp),
        grid=(t_rows // tm,),
        in_specs=[pl.BlockSpec((tm, LANES), lambda i: (i, 0))],
        out_specs=[pl.BlockSpec((SUBLANES, tm), lambda i: (0, i)),
                   pl.BlockSpec((SUBLANES, tm), lambda i: (0, i)),
                   pl.BlockSpec((n_exp, 1), lambda i: (0, 0))],
        out_shape=[jax.ShapeDtypeStruct((SUBLANES, t_rows), jnp.int32),
                   jax.ShapeDtypeStruct((SUBLANES, t_rows), F32),
                   jax.ShapeDtypeStruct((n_exp, 1), F32)],
        compiler_params=_cparams("arbitrary"),
        name="router_topk",
    )(logits)


def _dest_kernel(idx_ref, start_ref, dest_ref, carry, *, n_exp, tm):
    @pl.when(pl.program_id(0) == 0)
    def _():
        carry[...] = start_ref[...]

    eid = lax.broadcasted_iota(jnp.int32, (n_exp, tm), 0)
    idx = idx_ref[...]
    sels = [eid == idx[k:k + 1, :] for k in range(TOP_K)]
    total = sels[0].astype(F32) + sels[1].astype(F32) + sels[2].astype(F32) + sels[3].astype(F32)
    rr = lax.broadcasted_iota(jnp.int32, (tm, tm), 0)
    cc = lax.broadcasted_iota(jnp.int32, (tm, tm), 1)
    upper = jnp.where(rr < cc, 1.0, 0.0).astype(BF16)
    before = jnp.dot(total.astype(BF16), upper, preferred_element_type=F32) + carry[...]
    rows = [jnp.sum(jnp.where(sels[k], before, 0.0), axis=0, keepdims=True) for k in range(TOP_K)]
    pad = jnp.zeros((SUBLANES - TOP_K, tm), F32)
    dest_ref[...] = jnp.concatenate(rows + [pad], axis=0).astype(jnp.int32)
    carry[...] += jnp.sum(total, axis=1, keepdims=True)


def _dest_rows(idx, starts, n_exp):
    t_rows = idx.shape[1]
    tm = 256
    return pl.pallas_call(
        functools.partial(_dest_kernel, n_exp=n_exp, tm=tm),
        grid=(t_rows // tm,),
        in_specs=[pl.BlockSpec((SUBLANES, tm), lambda i: (0, i)),
                  pl.BlockSpec((n_exp, 1), lambda i: (0, 0))],
        out_specs=pl.BlockSpec((SUBLANES, tm), lambda i: (0, i)),
        out_shape=jax.ShapeDtypeStruct((SUBLANES, t_rows), jnp.int32),
        scratch_shapes=[pltpu.VMEM((n_exp, 1), F32)],
        compiler_params=_cparams("arbitrary"),
        name="moe_dest_rows",
    )(idx, starts)


def _expert_gu_kernel(be_ref, bv_ref, bn_ref, x_ref, wg_ref, wu_ref, bg_ref, bu_ref, o_ref, wg_s, wu_s, *, blk):
    b = pl.program_id(1)
    nvalid = bv_ref[b]

    @pl.when(bn_ref[b] == 1)
    def _():
        wg_s[...] = wg_ref[...].astype(BF16)
        wu_s[...] = wu_ref[...].astype(BF16)

    @pl.when(nvalid > 0)
    def _():
        rows = lax.broadcasted_iota(jnp.int32, (blk, 1), 0)
        lo, hi = _load_packed_rows(x_ref, blk)
        x = jnp.concatenate([lo.astype(BF16), hi.astype(BF16)], axis=1)
        x = jnp.where(rows < nvalid, x, jnp.zeros_like(x))
        gate = jnp.dot(x, wg_s[...], preferred_element_type=F32) + bg_ref[...]
        up = jnp.dot(x, wu_s[...], preferred_element_type=F32) + bu_ref[...]
        gate = jnp.minimum(gate, SWIGLU_LIMIT)
        up = jnp.clip(up, -SWIGLU_LIMIT, SWIGLU_LIMIT)
        act = (up + 1.0) * (gate * jax.nn.sigmoid(SWIGLU_ALPHA * gate))
        o_ref[...] = act.astype(o_ref.dtype)

    @pl.when(nvalid == 0)
    def _():
        o_ref[...] = jnp.zeros_like(o_ref)


def _expert_dn_kernel(be_ref, bv_ref, bn_ref, a_ref, wd_ref, bd_ref, o_ref, wd_s, *, blk):
    b = pl.program_id(0)

    @pl.when(bn_ref[b] == 1)
    def _():
        wd_s[...] = wd_ref[...].astype(BF16)

    @pl.when(bv_ref[b] > 0)
    def _():
        y = jnp.dot(a_ref[...], wd_s[...], preferred_element_type=F32) + bd_ref[...]
        _store_packed_rows(o_ref, y)

    @pl.when(bv_ref[b] == 0)
    def _():
        o_ref[...] = jnp.zeros_like(o_ref)


def _expert_matmul(xs, blk_e, blk_valid, blk_new, w_gu, b_gu, w_dn, b_dn, layer):
    d, f_dim = w_dn.shape[3], w_dn.shape[2]
    n_rows = xs.shape[0] // ROW_PIECES
    blk = EXPERT_BLOCK
    tf = GATE_UP_TILE
    nf = f_dim // tf
    n_blocks = n_rows // blk
    packed_rows = pl.BlockSpec((blk * ROW_PIECES, LANES), lambda f, b, be, bv, bn: (b, 0))
    act = pl.pallas_call(
        functools.partial(_expert_gu_kernel, blk=blk),
        grid_spec=pltpu.PrefetchScalarGridSpec(
            num_scalar_prefetch=3,
            grid=(nf, n_blocks),
            in_specs=[packed_rows,
                      pl.BlockSpec((None, None, d, tf), lambda f, b, be, bv, bn: (layer, be[b], 0, f)),
                      pl.BlockSpec((None, None, d, tf), lambda f, b, be, bv, bn: (layer, be[b], 0, nf + f)),
                      pl.BlockSpec((None, None, 1, tf), lambda f, b, be, bv, bn: (layer, be[b], 0, f)),
                      pl.BlockSpec((None, None, 1, tf), lambda f, b, be, bv, bn: (layer, be[b], 0, nf + f))],
            out_specs=pl.BlockSpec((blk, tf), lambda f, b, be, bv, bn: (b, f)),
            scratch_shapes=[pltpu.VMEM((d, tf), BF16), pltpu.VMEM((d, tf), BF16)]),
        out_shape=jax.ShapeDtypeStruct((n_rows, f_dim), BF16),
        compiler_params=_cparams("arbitrary", "arbitrary"),
        name="expert_gate_up",
    )(blk_e, blk_valid, blk_new, xs, w_gu, w_gu, b_gu, b_gu)
    assert d == 2 * ROW_PIECES * LANES
    sub = blk // DOWN_BLOCK
    part = jnp.arange(n_blocks * sub, dtype=jnp.int32) % sub
    dn_e = jnp.repeat(blk_e, sub)
    dn_valid = jnp.clip(jnp.repeat(blk_valid, sub) - part * DOWN_BLOCK, 0, DOWN_BLOCK)
    dn_new = jnp.where(part == 0, jnp.repeat(blk_new, sub), 0)
    return pl.pallas_call(
        functools.partial(_expert_dn_kernel, blk=DOWN_BLOCK),
        grid_spec=pltpu.PrefetchScalarGridSpec(
            num_scalar_prefetch=3,
            grid=(n_blocks * sub,),
            in_specs=[pl.BlockSpec((DOWN_BLOCK, f_dim), lambda b, be, bv, bn: (b, 0)),
                      pl.BlockSpec((None, None, f_dim, d), lambda b, be, bv, bn: (layer, be[b], 0, 0)),
                      pl.BlockSpec((None, None, 1, d), lambda b, be, bv, bn: (layer, be[b], 0, 0))],
            out_specs=pl.BlockSpec((DOWN_BLOCK * ROW_PIECES, LANES), lambda b, be, bv, bn: (b, 0)),
            scratch_shapes=[pltpu.VMEM((f_dim, d), BF16)]),
        out_shape=jax.ShapeDtypeStruct((n_rows * ROW_PIECES, LANES), jnp.uint32),
        compiler_params=_cparams("arbitrary"),
        name="expert_down",
    )(dn_e, dn_valid, dn_new, act, w_dn, b_dn)


def _combine_kernel(y0_ref, y1_ref, y2_ref, y3_ref, gate_ref, x_ref, g2_ref, fg_ref, o_ref, *, n_ctx, tm, final):
    gates = gate_ref[...]
    f_lo, f_hi = None, None
    for k, y_ref in enumerate((y0_ref, y1_ref, y2_ref, y3_ref)):
        lo, hi = _load_packed_rows(y_ref, tm)
        gk = gates[:, k:k + 1]
        f_lo = lo * gk if f_lo is None else f_lo + lo * gk
        f_hi = hi * gk if f_hi is None else f_hi + hi * gk
    f = jnp.concatenate([f_lo, f_hi], axis=1)
    row = pl.program_id(0) * tm + lax.broadcasted_iota(jnp.int32, (tm, 1), 0)
    g2 = jnp.where(row < n_ctx, g2_ref[0:1, :], g2_ref[1:2, :])
    xn = x_ref[...] + g2 * f
    if final:
        ms = jnp.mean(xn * xn, axis=-1, keepdims=True)
        xn = xn * lax.rsqrt(ms + EPS) * fg_ref[...]
    o_ref[...] = xn


def _combine(yk, gates_t, x, mod, g2_col, final_g, n_ctx, final):
    t_rows, d = x.shape
    tm = 256
    nt = t_rows // tm
    return pl.pallas_call(
        functools.partial(_combine_kernel, n_ctx=n_ctx, tm=tm, final=final),
        grid=(nt,),
        in_specs=[pl.BlockSpec((tm * ROW_PIECES, LANES), lambda i, k=k: (k * nt + i, 0)) for k in range(TOP_K)]
                 + [pl.BlockSpec((tm, SUBLANES), lambda i: (i, 0)),
                  pl.BlockSpec((tm, d), lambda i: (i, 0)),
                  pl.BlockSpec((SUBLANES, d), lambda i: (0, g2_col)),
                  pl.BlockSpec((1, d), lambda i: (0, 0))],
        out_specs=pl.BlockSpec((tm, d), lambda i: (i, 0)),
        out_shape=jax.ShapeDtypeStruct((t_rows, d), F32),
        compiler_params=_cparams("parallel"),
        name="moe_combine",
    )(yk, yk, yk, yk, gates_t, x, mod, final_g)


SC_WINDOW = 128


def _sc_mesh():
    return plsc.VectorSubcoreMesh(core_axis_name="core", subcore_axis_name="subcore")


def _sc_scatter_rows(x, dest, n_out):
    n_src_blocks = x.shape[0] // SC_WINDOW

    @functools.partial(pl.kernel, out_type=jax.ShapeDtypeStruct((n_out, LANES), x.dtype), mesh=_sc_mesh(),
                       scratch_types=[], name="sc_dispatch_rows")
    def run(x_hbm, i_hbm, o_hbm):
        def body(x_vmem, i_vmem):
            pltpu.sync_copy(x_vmem, o_hbm.at[i_vmem.at[0]])

        pltpu.emit_pipeline(
            body, grid=(dest.shape[1] // SC_WINDOW,),
            in_specs=[pl.BlockSpec((SC_WINDOW, LANES), lambda i: (i % n_src_blocks, 0)),
                      pl.BlockSpec((1, SC_WINDOW), lambda i: (0, i))],
            out_specs=[], core_axis_name=("core", "subcore"),
            dimension_semantics=(pltpu.PARALLEL,))(x_hbm, i_hbm)

    return run(x, dest)


def _sc_gather_rows(y, idx):
    n = idx.shape[1]

    @functools.partial(pl.kernel, out_type=jax.ShapeDtypeStruct((n, LANES), y.dtype), mesh=_sc_mesh(),
                       scratch_types=[], name="sc_combine_rows")
    def run(y_hbm, i_hbm, o_hbm):
        def body(i_vmem, o_vmem):
            pltpu.sync_copy(y_hbm.at[i_vmem.at[0]], o_vmem)

        pltpu.emit_pipeline(
            body, grid=(n // SC_WINDOW,),
            in_specs=[pl.BlockSpec((1, SC_WINDOW), lambda i: (0, i))],
            out_specs=[pl.BlockSpec((SC_WINDOW, LANES), lambda i: (i, 0))],
            core_axis_name=("core", "subcore"),
            dimension_semantics=(pltpu.PARALLEL,))(i_hbm, o_hbm)

    return run(y, idx)


def _moe(h, logits, x, mod, g2_col, w_gu, b_gu, w_dn, b_dn, layer, n_ctx, final_g, final):
    t_rows = x.shape[0]
    n_exp = w_gu.shape[1]
    idx, gates, counts = _router(logits, n_exp)
    counts = counts[:, 0].astype(jnp.int32)
    blk = EXPERT_BLOCK
    padded = (counts + blk - 1) // blk * blk
    pends = jnp.cumsum(padded)
    pstarts = pends - padded
    dest = _dest_rows(idx, pstarts.astype(F32)[:, None], n_exp)[:TOP_K]
    n_blocks = -(-(t_rows * TOP_K) // blk) + n_exp
    n_rows = n_blocks * blk
    blk_start = jnp.arange(n_blocks, dtype=jnp.int32) * blk
    blk_e = jnp.minimum(jnp.sum(blk_start[:, None] >= pends[None, :], axis=1), n_exp - 1).astype(jnp.int32)
    blk_valid = jnp.clip(pstarts[blk_e] + counts[blk_e] - blk_start, 0, blk).astype(jnp.int32)
    blk_valid = jnp.where(blk_start < pends[-1], blk_valid, 0)
    blk_new = jnp.concatenate([jnp.ones((1,), jnp.int32), (blk_e[1:] != blk_e[:-1]).astype(jnp.int32)])
    dest8 = (dest[:, :, None] * ROW_PIECES + jnp.arange(ROW_PIECES, dtype=jnp.int32)).reshape(1, -1)
    xs = _sc_scatter_rows(h, dest8, n_rows * ROW_PIECES)
    ys = _expert_matmul(xs, blk_e, blk_valid, blk_new, w_gu, b_gu, w_dn, b_dn, layer)
    yk = _sc_gather_rows(ys, dest8)
    return _combine(yk, gates.T, x, mod, g2_col, final_g, n_ctx, final)


def _s5_tables(a_re, a_im, log_step, b_re, b_im, c_re, c_im, d_skip):
    tc = S5_TC
    n_grp, n_st = a_re.shape[1], a_re.shape[2]
    lr = jnp.minimum(a_re.astype(F32), -1e-4)
    li = a_im.astype(F32)
    dt = jnp.exp(log_step.astype(F32))[..., None]
    dd = jnp.arange(tc + 1, dtype=F32)[:, None, None, None]
    mag = jnp.exp(lr * dt * dd)
    pr, pi = mag * jnp.cos(li * dt * dd), mag * jnp.sin(li * dt * dd)
    ar, ai = pr[1], pi[1]
    den = lr * lr + li * li
    nr = ar - 1.0
    zr = (nr * lr + ai * li) / den
    zi = (ai * lr - nr * li) / den
    br, bi = b_re.astype(F32), b_im.astype(F32)
    bbr = zr[..., None] * br - zi[..., None] * bi
    bbi = zr[..., None] * bi + zi[..., None] * br
    abr = pr[:tc, ..., None] * bbr - pi[:tc, ..., None] * bbi
    abi = pr[:tc, ..., None] * bbi + pi[:tc, ..., None] * bbr
    cr, ci = c_re.astype(F32), c_im.astype(F32)
    kern = (jnp.einsum('xgip,dxgpj->dxgij', cr, abr, precision=HIGHEST)
            - jnp.einsum('xgip,dxgpj->dxgij', ci, abi, precision=HIGHEST))
    kdim = tc * S5_GROUP
    k_t = kern.transpose(1, 2, 4, 0, 3)
    zeros = jnp.zeros((n_grp, S5_GROUP, kdim), F32)
    ext_f = jnp.concatenate([zeros, k_t[0].reshape(n_grp, S5_GROUP, kdim)], axis=-1)
    ext_b = jnp.concatenate([k_t[1, :, :, ::-1].reshape(n_grp, S5_GROUP, kdim), zeros], axis=-1)
    m_tot = jnp.stack([ext_f[:, :, (tc - s) * S5_GROUP:(tc - s) * S5_GROUP + kdim]
                       + ext_b[:, :, (tc - 1 - s) * S5_GROUP:(tc - 1 - s) * S5_GROUP + kdim]
                       for s in range(tc)], axis=1).reshape(n_grp, kdim, kdim)
    dsk = jnp.tile(d_skip.astype(F32).reshape(n_grp, S5_GROUP), (1, tc))
    m_tot = m_tot + jnp.eye(kdim, dtype=F32)[None] * dsk[:, None, :]
    pw_f = tc - 1 - jnp.arange(tc)
    pw_b = jnp.arange(tc)

    def b_cols(part, pw, x):
        return part[pw, x].transpose(1, 0, 3, 2).reshape(n_grp, tc * S5_GROUP, n_st)

    b_mat = jnp.concatenate([b_cols(abr, pw_f, 0), b_cols(abi, pw_f, 0),
                             b_cols(abr, pw_b, 1), b_cols(abi, pw_b, 1)], axis=-1)
    pcf = 1 + jnp.arange(tc)
    pcb = tc - jnp.arange(tc)

    def c_rows(pw, x):
        prx, pix = pr[pw, x], pi[pw, x]
        re_c = cr[x][None] * prx[:, :, None, :] - ci[x][None] * pix[:, :, None, :]
        im_c = -(cr[x][None] * pix[:, :, None, :] + ci[x][None] * prx[:, :, None, :])
        to_rows = lambda z: z.transpose(1, 3, 0, 2).reshape(n_grp, n_st, tc * S5_GROUP)
        return to_rows(re_c), to_rows(im_c)

    c_mat = jnp.concatenate(list(c_rows(pcf, 0)) + list(c_rows(pcb, 1)), axis=1)
    prt, pit = pr[tc], pi[tc]
    a1 = jnp.concatenate([prt[0], prt[0], prt[1], prt[1]], axis=-1)
    a2 = jnp.concatenate([-pit[0], pit[0], -pit[1], pit[1]], axis=-1)
    return m_tot.astype(BF16), b_mat.astype(BF16), c_mat.astype(BF16), a1, a2


S5_LANE_GROUPS = LANES // S5_GROUP


def _s5_pack(u_ref, c0, cc):
    xs = [u_ref[pl.ds(c0 * S5_TC + tau, cc, stride=S5_TC), :] for tau in range(S5_TC)]
    return [jnp.concatenate([xs[tau][:, g * S5_GROUP:(g + 1) * S5_GROUP] for tau in range(S5_TC)], axis=1)
            for g in range(S5_LANE_GROUPS)]


def _s5_in_kernel(u_ref, b_ref, o_ref, *, cc):
    def body(ci, carry):
        c0 = pl.multiple_of(ci * cc, SUBLANES)
        ugs = _s5_pack(u_ref, c0, cc)
        for g in range(S5_LANE_GROUPS):
            o_ref[pl.ds(c0, cc), g, :] = jnp.dot(ugs[g].astype(BF16), b_ref[g], preferred_element_type=F32)
        return carry

    lax.fori_loop(0, u_ref.shape[0] // (S5_TC * cc), body, 0)


def _s5_out_kernel(u_ref, z_ref, m_ref, c_ref, o_ref, *, cc):
    def body(ci, carry):
        c0 = pl.multiple_of(ci * cc, SUBLANES)
        ugs = _s5_pack(u_ref, c0, cc)
        ys = [jnp.dot(ugs[g].astype(BF16), m_ref[g], preferred_element_type=F32)
              + jnp.dot(z_ref[pl.ds(c0, cc), g, :].astype(BF16), c_ref[g], preferred_element_type=F32)
              for g in range(S5_LANE_GROUPS)]
        for tau in range(S5_TC):
            row = jnp.concatenate([y[:, tau * S5_GROUP:(tau + 1) * S5_GROUP] for y in ys], axis=1)
            o_ref[pl.ds(c0 * S5_TC + tau, cc, stride=S5_TC), :] = row
        return carry

    lax.fori_loop(0, u_ref.shape[0] // (S5_TC * cc), body, 0)


def _s5_scan_kernel(h_ref, a1_ref, a2_ref, z_ref, *, n_chunks, n_ctx_chunks):
    half = 2 * S5_STATE
    a1 = a1_ref[...]
    a2 = a2_ref[...]
    a1f, a1b = a1[:, :half], a1[:, half:]
    a2f, a2b = a2[:, :half], a2[:, half:]
    zero = jnp.zeros((a1.shape[0], half), F32)

    def swap(v):
        return pltpu.roll(v, S5_STATE, 1)

    def step(t, carry):
        rf, rfs, rb, rbs = carry
        cb = jnp.where(t < n_ctx_chunks, n_ctx_chunks - 1 - t, n_chunks - 1 - (t - n_ctx_chunks))
        z_ref[t, :, 0:half] = rf
        z_ref[cb, :, half:2 * half] = rb
        hf = h_ref[t, :, 0:half]
        hb = h_ref[cb, :, half:2 * half]
        nrf = a1f * rf + a2f * rfs + hf
        nrfs = a1f * rfs - a2f * rf + swap(hf)
        nrb = a1b * rb + a2b * rbs + hb
        nrbs = a1b * rbs - a2b * rb + swap(hb)
        return nrf, nrfs, nrb, nrbs

    lax.fori_loop(0, n_chunks, step, (zero, zero, zero, zero), unroll=8 if n_chunks % 8 == 0 else 1)


def _s5_mixer_scan(u, tables, n_ctx):
    m_tot, b_mat, c_mat, a1, a2 = tables
    t_rows, width = u.shape
    tc = S5_TC
    n_grp = width // S5_GROUP
    n_chunks = t_rows // tc
    kdim = tc * S5_GROUP
    sdim = 4 * S5_STATE
    gb = S5_LANE_GROUPS
    gs = SUBLANES
    n_split = 2
    half_chunks = n_chunks // n_split
    cc = _pick(half_chunks, (104, 80, 40, 8))
    u_spec = pl.BlockSpec((t_rows // n_split, LANES), lambda j, r: (r, j))
    tab_spec = lambda a, b: pl.BlockSpec((gb, a, b), lambda j, r: (j, 0, 0))
    st_spec = pl.BlockSpec((half_chunks, gb, sdim), lambda j, r: (r, j, 0))
    grid = (width // LANES, n_split)
    hin = pl.pallas_call(
        functools.partial(_s5_in_kernel, cc=cc),
        grid=grid,
        in_specs=[u_spec, tab_spec(kdim, sdim)],
        out_specs=st_spec,
        out_shape=jax.ShapeDtypeStruct((n_chunks, n_grp, sdim), F32),
        compiler_params=_cparams("parallel", "arbitrary"),
        name="s5_chunk_inputs",
    )(u, b_mat)
    z = pl.pallas_call(
        functools.partial(_s5_scan_kernel, n_chunks=n_chunks, n_ctx_chunks=n_ctx // tc),
        grid=(n_grp // gs,),
        in_specs=[pl.BlockSpec((n_chunks, gs, sdim), lambda g: (0, g, 0)),
                  pl.BlockSpec((gs, sdim), lambda g: (g, 0)),
                  pl.BlockSpec((gs, sdim), lambda g: (g, 0))],
        out_specs=pl.BlockSpec((n_chunks, gs, sdim), lambda g: (0, g, 0)),
        out_shape=jax.ShapeDtypeStruct((n_chunks, n_grp, sdim), F32),
        compiler_params=_cparams("parallel"),
        name="s5_chunk_scan",
    )(hin, a1, a2)
    return pl.pallas_call(
        functools.partial(_s5_out_kernel, cc=cc),
        grid=grid,
        in_specs=[u_spec, st_spec, tab_spec(kdim, kdim), tab_spec(sdim, kdim)],
        out_specs=u_spec,
        out_shape=jax.ShapeDtypeStruct((t_rows, width), F32),
        compiler_params=_cparams("parallel", "arbitrary"),
        name="s5_chunk_outputs",
    )(u, z, m_tot, c_mat)


def _lambda_init(layer):
    return 0.8 - 0.6 * math.exp(-0.3 * layer)


def kernel(x, c, ctx, c_ctx, w_mod, b_mod, norm1_g, norm2_g, final_g, attn_w_qkv, attn_w_o, lambda_q1, lambda_k1, lambda_q2, lambda_k2, subln_g, sink_logit, s5_w_in, s5_a_re, s5_a_im, s5_log_step, s5_b_re, s5_b_im, s5_c_re, s5_c_im, s5_d, s5_w_glu, router_w, router_b, expert_w_gu, expert_b_gu, expert_w_down, expert_b_down):
    bsz, seq, d = x.shape
    assert bsz == 1, "single-sequence block"
    n_ctx = ctx.shape[1]
    depth = w_mod.shape[0]
    n_exp = router_w.shape[2]
    assert n_ctx % 256 == 0 and seq % 256 == 0

    xj = jnp.concatenate([ctx[0], x[0]], axis=0)
    cs = jnp.zeros((SUBLANES, d), F32).at[0].set(c_ctx).at[1].set(c[0])
    b_mod3 = b_mod[:, None, :]
    b_gu4 = expert_b_gu[:, :, None, :]
    b_dn4 = expert_b_down[:, :, None, :]
    wr_pad = jnp.pad(router_w, ((0, 0), (0, 0), (0, LANES - n_exp)))
    wr_hi = wr_pad.astype(BF16)
    wr_pad = jnp.stack([wr_hi, (wr_pad - wr_hi.astype(F32)).astype(BF16)], axis=1)
    br_pad = jnp.pad(router_b, ((0, 0), (0, LANES - n_exp)), constant_values=NEG_INF)[:, None, :]
    final_row = final_g[None, :]

    for i in range(depth):
        last = i == depth - 1
        j = i // 2
        mod = _adaln_mod(cs, w_mod, b_mod3, i)
        if i % 2 == 0:
            sizes = (A_HEADS * 2 * A_QK_DIM, A_HEADS * 2 * A_QK_DIM, A_HEADS * A_V_DIM,
                     B_Q_HEADS * B_HEAD_DIM, B_KV_HEADS * B_HEAD_DIM, B_KV_HEADS * B_HEAD_DIM)
            offs = np.concatenate([[0], np.cumsum(sizes)])
            types = [0, 1, ROPE_NONE, 2, 3, ROPE_NONE]
            ttype = jnp.asarray(np.concatenate([np.full(s // 256, t) for s, t in zip(sizes, types)]), jnp.int32)
            cos_t, sin_t = _rope_tables(n_ctx, seq)
            qkv = _norm_mod_matmul(xj, norm1_g[i][None, :], mod, 0, 1, attn_w_qkv[j].astype(BF16), n_ctx, BF16,
                                   rope=(ttype, cos_t, sin_t))
            qt = qkv[:, offs[0]:offs[1]].T
            ka = qkv[:, offs[1]:offs[2]]
            vt = qkv[:, offs[2]:offs[3]].T.reshape(A_HEADS, A_V_DIM, -1)
            vt = jnp.concatenate([vt, jnp.ones((A_HEADS, V_AUG - A_V_DIM, vt.shape[2]), BF16)], axis=1)
            vt = vt.reshape(A_HEADS * V_AUG, -1)
            f32 = F32
            li = _lambda_init(i)
            lam = (jnp.exp(jnp.sum(lambda_q1[j].astype(f32) * lambda_k1[j].astype(f32)))
                   - jnp.exp(jnp.sum(lambda_q2[j].astype(f32) * lambda_k2[j].astype(f32))) + li)
            attn_args = (lam.reshape(1), subln_g[j][:, None], 1.0 - li)
            ya = jnp.concatenate([_diff_attention(qt[:, :n_ctx], ka[:n_ctx], vt[:, :n_ctx], *attn_args),
                                  _diff_attention(qt[:, n_ctx:], ka, vt, *attn_args)], axis=0)
            yb = _window_attention(qkv, sink_logit[j], n_ctx, int(offs[3]) // LANES, int(offs[4]) // LANES,
                                   int(offs[5]) // LANES)
            w_o = attn_w_o[j].astype(BF16)
            na = A_HEADS * A_V_DIM
            xj, h2, logits = _mixer_out(_attn_out_kernel, [ya, yb], xj, [w_o[:na], w_o[na:]], mod, (2, 3, 4),
                                        norm2_g[i][None, :], wr_pad[i], br_pad[i], n_ctx, "attn_out_router")
        else:
            u = _norm_mod_matmul(xj, norm1_g[i][None, :], mod, 0, 1, s5_w_in[j].astype(BF16), n_ctx, F32)
            tables = _s5_tables(s5_a_re[j], s5_a_im[j], s5_log_step[j], s5_b_re[j], s5_b_im[j],
                                s5_c_re[j], s5_c_im[j], s5_d[j])
            y = _s5_mixer_scan(u, tables, n_ctx)
            w_glu = s5_w_glu[j].astype(BF16)
            xj, h2, logits = _mixer_out(_glu_out_kernel, [y], xj, [w_glu[:, :d], w_glu[:, d:]], mod, (2, 3, 4),
                                        norm2_g[i][None, :], wr_pad[i], br_pad[i], n_ctx, "glu_out_router")
        xj = _moe(h2, logits, xj, mod, 5, expert_w_gu, b_gu4, expert_w_down, b_dn4, i, n_ctx, final_row, last)
    return xj[n_ctx:][None]
```

```python
import functools
import math

import jax
import jax.numpy as jnp
import numpy as np
from jax import lax
from jax.experimental import pallas as pl
from jax.experimental.pallas import tpu as pltpu
from jax.experimental.pallas import tpu_sc as plsc

F32 = jnp.float32
BF16 = jnp.bfloat16
HIGHEST = lax.Precision.HIGHEST

V7X_VMEM_BYTES = 64 * 1024 * 1024
VMEM_LIMIT = V7X_VMEM_BYTES - 8 * 1024 * 1024
LANES = 128
SUBLANES = 8

GRID_W = 64
N_MOD = 6
EPS = 1e-6
NEG_INF = -1e30
ROPE_THETA = 10000.0
A_HEADS = 8
A_QK_DIM = 64
A_V_DIM = 128
B_Q_HEADS = 8
B_KV_HEADS = 2
B_GROUP = 4
B_HEAD_DIM = 128
WINDOW = 128
S5_GROUP = 16
S5_STATE = 64
S5_TC = 16
TOP_K = 4
SWIGLU_LIMIT = 7.0
SWIGLU_ALPHA = 1.702
EXPERT_BLOCK = 512
DOWN_BLOCK = 256
GATE_UP_TILE = 1024
LOG2E = 1.4426950408889634


def _pick(n, cands):
    for c in cands:
        if n % c == 0:
            return c
    raise ValueError(f"no tile for {n} in {cands}")


def _cparams(*sem):
    return pltpu.CompilerParams(dimension_semantics=sem, vmem_limit_bytes=VMEM_LIMIT)


def _mod_kernel(c_ref, w_ref, b_ref, o_ref):
    cv = c_ref[...]
    s = cv * jax.nn.sigmoid(cv)
    o_ref[...] = jnp.dot(s, w_ref[...], preferred_element_type=F32, precision=HIGHEST) + b_ref[...]


def _adaln_mod(cs, w_mod, b_mod, layer):
    d, n = w_mod.shape[1], w_mod.shape[2]
    tn = _pick(n, (1024, 512, 256, 128))
    return pl.pallas_call(
        _mod_kernel,
        grid=(n // tn,),
        in_specs=[pl.BlockSpec((SUBLANES, d), lambda j: (0, 0)),
                  pl.BlockSpec((None, d, tn), lambda j: (layer, 0, j)),
                  pl.BlockSpec((None, 1, tn), lambda j: (layer, 0, j))],
        out_specs=pl.BlockSpec((SUBLANES, tn), lambda j: (0, j)),
        out_shape=jax.ShapeDtypeStruct((SUBLANES, n), F32),
        compiler_params=_cparams("parallel"),
        name="adaln_mod",
    )(cs, w_mod, b_mod)


def _norm_mod(x, g, sh2, sc2, row0, n_ctx):
    ms = jnp.mean(x * x, axis=-1, keepdims=True)
    y = x * lax.rsqrt(ms + EPS) * g
    row = row0 + lax.broadcasted_iota(jnp.int32, (x.shape[0], 1), 0)
    is_ctx = row < n_ctx
    sc = jnp.where(is_ctx, sc2[0:1, :], sc2[1:2, :])
    sh = jnp.where(is_ctx, sh2[0:1, :], sh2[1:2, :])
    return y * (1.0 + sc) + sh


ROPE_NONE = 4


def _rope_store(acc, cos_ref, sin_ref, o_ref, shift):
    cosv, sinv = cos_ref[...], sin_ref[...]
    for cgrp in range(acc.shape[1] // LANES):
        a = acc[:, cgrp * LANES:(cgrp + 1) * LANES]
        lane = lax.broadcasted_iota(jnp.int32, a.shape, 1)
        in_second = (lane & (2 * shift - 1)) >= shift
        sw = jnp.where(in_second, pltpu.roll(a, shift, 1), pltpu.roll(a, LANES - shift, 1))
        o_ref[:, cgrp * LANES:(cgrp + 1) * LANES] = (a * cosv + sw * sinv).astype(o_ref.dtype)


NORM_ROWS = 256


def _norm_mod_to_scratch(x_ref, g_ref, sh_ref, sc_ref, h_scr, row0, n_ctx):
    def body(r, carry):
        off = pl.multiple_of(r * NORM_ROWS, NORM_ROWS)
        h = _norm_mod(x_ref[pl.ds(off, NORM_ROWS), :], g_ref[...], sh_ref[...], sc_ref[...], row0 + off, n_ctx)
        h_scr[pl.ds(off, NORM_ROWS), :] = h.astype(h_scr.dtype)
        return carry

    lax.fori_loop(0, x_ref.shape[0] // NORM_ROWS, body, 0)


def _nmm_rope_kernel(tt_ref, x_ref, g_ref, sh_ref, sc_ref, w_ref, cos_ref, sin_ref, o_ref, h_scr,
                     *, n_ctx, tm):
    i = pl.program_id(0)
    j = pl.program_id(1)

    @pl.when(j == 0)
    def _():
        _norm_mod_to_scratch(x_ref, g_ref, sh_ref, sc_ref, h_scr, i * tm, n_ctx)

    acc = jnp.dot(h_scr[...], w_ref[...], preferred_element_type=F32)
    t = tt_ref[j]

    @pl.when(t == ROPE_NONE)
    def _():
        o_ref[...] = acc.astype(o_ref.dtype)

    @pl.when(t < 2)
    def _():
        _rope_store(acc, cos_ref, sin_ref, o_ref, A_QK_DIM // 4)

    @pl.when(jnp.logical_and(t >= 2, t < ROPE_NONE))
    def _():
        _rope_store(acc, cos_ref, sin_ref, o_ref, B_HEAD_DIM // 4)


def _nmm_plain_kernel(x_ref, g_ref, sh_ref, sc_ref, w_ref, o_ref, h_scr, *, n_ctx, tm):
    i = pl.program_id(0)
    j = pl.program_id(1)

    @pl.when(j == 0)
    def _():
        _norm_mod_to_scratch(x_ref, g_ref, sh_ref, sc_ref, h_scr, i * tm, n_ctx)

    o_ref[...] = jnp.dot(h_scr[...], w_ref[...], preferred_element_type=F32).astype(o_ref.dtype)


def _norm_mod_matmul(x, g, mod, sh_col, sc_col, w, n_ctx, out_dtype, rope=None):
    t_rows, d = x.shape
    n = w.shape[1]
    tm = _pick(t_rows, (1280, 1024, 512, 256, 128))
    tn = 256
    grid = (t_rows // tm, n // tn)
    kern_kw = dict(n_ctx=n_ctx, tm=tm)
    scratch = [pltpu.VMEM((tm, d), BF16)]
    out_shape = jax.ShapeDtypeStruct((t_rows, n), out_dtype)
    if rope is None:
        return pl.pallas_call(
            functools.partial(_nmm_plain_kernel, **kern_kw),
            grid=grid,
            in_specs=[pl.BlockSpec((tm, d), lambda i, j: (i, 0)),
                      pl.BlockSpec((1, d), lambda i, j: (0, 0)),
                      pl.BlockSpec((SUBLANES, d), lambda i, j: (0, sh_col)),
                      pl.BlockSpec((SUBLANES, d), lambda i, j: (0, sc_col)),
                      pl.BlockSpec((d, tn), lambda i, j: (0, j))],
            out_specs=pl.BlockSpec((tm, tn), lambda i, j: (i, j)),
            out_shape=out_shape,
            scratch_shapes=scratch,
            compiler_params=_cparams("parallel", "arbitrary"),
            name="norm_mod_matmul",
        )(x, g, mod, mod, w)
    ttype, cos_t, sin_t = rope
    return pl.pallas_call(
        functools.partial(_nmm_rope_kernel, **kern_kw),
        grid_spec=pltpu.PrefetchScalarGridSpec(
            num_scalar_prefetch=1,
            grid=grid,
            in_specs=[pl.BlockSpec((tm, d), lambda i, j, tt: (i, 0)),
                      pl.BlockSpec((1, d), lambda i, j, tt: (0, 0)),
                      pl.BlockSpec((SUBLANES, d), lambda i, j, tt: (0, sh_col)),
                      pl.BlockSpec((SUBLANES, d), lambda i, j, tt: (0, sc_col)),
                      pl.BlockSpec((d, tn), lambda i, j, tt: (0, j)),
                      pl.BlockSpec((None, tm, LANES), lambda i, j, tt: (tt[j], i, 0)),
                      pl.BlockSpec((None, tm, LANES), lambda i, j, tt: (tt[j], i, 0))],
            out_specs=pl.BlockSpec((tm, tn), lambda i, j, tt: (i, j)),
            scratch_shapes=scratch),
        out_shape=out_shape,
        compiler_params=_cparams("parallel", "arbitrary"),
        name="norm_mod_qkv_rope",
    )(ttype, x, g, mod, mod, w, cos_t, sin_t)


def _rope_tables(n_ctx, seq):
    pos = jnp.arange(seq)
    rows = (pos // GRID_W).astype(F32)
    cols = (pos % GRID_W).astype(F32)

    def tab(dim):
        quarter = dim // 4
        freqs = ROPE_THETA ** (-jnp.arange(quarter, dtype=F32) / quarter)
        ar, ac = rows[:, None] * freqs, cols[:, None] * freqs
        cosv = jnp.concatenate([jnp.cos(ar), jnp.cos(ar), jnp.cos(ac), jnp.cos(ac)], axis=1)
        sinv = jnp.concatenate([-jnp.sin(ar), jnp.sin(ar), -jnp.sin(ac), jnp.sin(ac)], axis=1)
        reps = LANES // dim
        cosv, sinv = jnp.tile(cosv, (1, reps)), jnp.tile(sinv, (1, reps))
        cosv = jnp.concatenate([jnp.ones((n_ctx, LANES), F32), cosv], axis=0)
        sinv = jnp.concatenate([jnp.zeros((n_ctx, LANES), F32), sinv], axis=0)
        return cosv, sinv

    ca, sa = tab(A_QK_DIM)
    cb, sb = tab(B_HEAD_DIM)
    qa = (A_QK_DIM ** -0.5) * LOG2E
    qb = B_HEAD_DIM ** -0.5
    one, zero = jnp.ones_like(ca), jnp.zeros_like(ca)
    cos_t = jnp.stack([ca * qa, ca, cb * qb, cb, one])
    sin_t = jnp.stack([sa * qa, sa, sb * qb, sb, zero])
    return cos_t, sin_t


V_AUG = A_V_DIM + 16


def _flash_kernel(lam_ref, qt_ref, k_ref, vt_ref, g_ref, o_ref, s_a, s_b, acc,
                  *, tq, tk, t_rows, out_scale):
    qt = qt_ref[...]
    row = lax.broadcasted_iota(jnp.int32, qt.shape, 0)
    zero = jnp.zeros_like(qt)
    qm = (jnp.where(row < A_QK_DIM, qt, zero), jnp.where(row >= A_QK_DIM, qt, zero))

    def start(tile, size):
        return tile * size if isinstance(tile, int) else pl.multiple_of(tile * size, size)

    def scores(tile, size, dst):
        kt = k_ref[pl.ds(start(tile, size), size), :]
        for m in range(2):
            dst[m, 0:size, :] = jnp.dot(kt, qm[m], preferred_element_type=F32).astype(BF16)

    def soft_pv(tile, size, src, ms):
        vt = vt_ref[:, pl.ds(start(tile, size), size)]
        new_ms = []
        for m in range(2):
            s = src[m, 0:size, :]
            mx = jnp.maximum(ms[m], jnp.max(s, axis=0, keepdims=True).astype(F32))
            alpha = jnp.exp2(ms[m] - mx)
            p = jnp.exp2(s - mx.astype(BF16))
            acc[m] = alpha * acc[m] + jnp.dot(vt, p, preferred_element_type=F32)
            new_ms.append(mx)
        return tuple(new_ms)

    def finish(ms):
        o1 = acc[0, 0:A_V_DIM, :] / acc[0, A_V_DIM:A_V_DIM + 1, :]
        o2 = acc[1, 0:A_V_DIM, :] / acc[1, A_V_DIM:A_V_DIM + 1, :]
        o = o1 - lam_ref[0] * o2
        var = jnp.mean(o * o, axis=0, keepdims=True)
        o = o * lax.rsqrt(var + EPS) * (g_ref[...] * out_scale)
        o_ref[...] = o.T.astype(o_ref.dtype)

    m0 = jnp.full((1, tq), NEG_INF, F32)
    nk = t_rows // tk
    n_pairs = (nk - 1) // 2
    acc[...] = jnp.zeros_like(acc)
    scores(0, tk, s_a)

    def pair(pp, carry):
        scores(2 * pp + 1, tk, s_b)
        carry = soft_pv(2 * pp, tk, s_a, carry)
        scores(2 * pp + 2, tk, s_a)
        return soft_pv(2 * pp + 1, tk, s_b, carry)

    carry = lax.fori_loop(0, n_pairs, pair, (m0, m0))
    done = 2 * n_pairs
    if nk - done == 2:
        scores(done + 1, tk, s_b)
        carry = soft_pv(done, tk, s_a, carry)
        carry = soft_pv(done + 1, tk, s_b, carry)
    else:
        carry = soft_pv(done, tk, s_a, carry)
    finish(carry)


def _diff_attention(qt, k, vt, lam, subln_col, out_scale):
    t_rows = k.shape[0]
    n_q = qt.shape[1]
    tq = _pick(n_q, (1024, 512, 256))
    tk = _pick(t_rows, (1664, 1280, 1024, 512, 256))
    return pl.pallas_call(
        functools.partial(_flash_kernel, tq=tq, tk=tk, t_rows=t_rows, out_scale=out_scale),
        grid=(A_HEADS, n_q // tq),
        in_specs=[pl.BlockSpec(memory_space=pltpu.SMEM),
                  pl.BlockSpec((LANES, tq), lambda h, i: (h, i)),
                  pl.BlockSpec((t_rows, LANES), lambda h, i: (0, h)),
                  pl.BlockSpec((V_AUG, t_rows), lambda h, i: (h, 0)),
                  pl.BlockSpec((LANES, 1), lambda h, i: (0, 0))],
        out_specs=pl.BlockSpec((tq, LANES), lambda h, i: (i, h)),
        out_shape=jax.ShapeDtypeStruct((n_q, A_HEADS * A_V_DIM), BF16),
        scratch_shapes=[pltpu.VMEM((2, tk, tq), BF16), pltpu.VMEM((2, tk, tq), BF16),
                        pltpu.VMEM((2, V_AUG, tq), F32)],
        compiler_params=_cparams("parallel", "arbitrary"),
        name="diff_attention",
    )(lam, qt, k, vt, subln_col)


def _window_kernel(sink_ref, q_ref, kp_ref, ko_ref, kn_ref, vp_ref, vo_ref, vn_ref, kc_ref, vc_ref, o_ref,
                   *, nb, nb_ctx):
    n = pl.program_id(0)
    qi = lax.broadcasted_iota(jnp.int32, (WINDOW, WINDOW), 0)
    kk = lax.broadcasted_iota(jnp.int32, (WINDOW, WINDOW), 1)
    own_ok = n >= nb_ctx
    prev_ok = n >= nb_ctx + 1
    next_ok = jnp.logical_and(own_ok, n <= nb - 2)
    m_prev = jnp.logical_and(kk >= qi, prev_ok)
    m_own = jnp.logical_and(kk >= 0, own_ok)
    m_next = jnp.logical_and(kk <= qi, next_ok)
    dn = (((1,), (1,)), ((), ()))
    for h in range(B_Q_HEADS):
        kv = slice((h // B_GROUP) * B_HEAD_DIM, (h // B_GROUP + 1) * B_HEAD_DIM)
        q = q_ref[:, h * B_HEAD_DIM:(h + 1) * B_HEAD_DIM]
        s_p = jnp.where(m_prev, lax.dot_general(q, kp_ref[:, kv], dn, preferred_element_type=F32), NEG_INF)
        s_o = jnp.where(m_own, lax.dot_general(q, ko_ref[:, kv], dn, preferred_element_type=F32), NEG_INF)
        s_n = jnp.where(m_next, lax.dot_general(q, kn_ref[:, kv], dn, preferred_element_type=F32), NEG_INF)
        s_c = lax.dot_general(q, kc_ref[:, kv], dn, preferred_element_type=F32)
        sink = sink_ref[h]
        mx = jnp.maximum(jnp.maximum(jnp.max(s_p, axis=1, keepdims=True), jnp.max(s_o, axis=1, keepdims=True)),
                         jnp.maximum(jnp.max(s_n, axis=1, keepdims=True), jnp.max(s_c, axis=1, keepdims=True)))
        mx = jnp.maximum(mx, sink)
        p_p, p_o, p_n, p_c = jnp.exp(s_p - mx), jnp.exp(s_o - mx), jnp.exp(s_n - mx), jnp.exp(s_c - mx)
        den = (jnp.sum(p_p, axis=1, keepdims=True) + jnp.sum(p_o, axis=1, keepdims=True)
               + jnp.sum(p_n, axis=1, keepdims=True) + jnp.sum(p_c, axis=1, keepdims=True)
               + jnp.exp(sink - mx))
        o = (jnp.dot(p_p.astype(BF16), vp_ref[:, kv], preferred_element_type=F32)
             + jnp.dot(p_o.astype(BF16), vo_ref[:, kv], preferred_element_type=F32)
             + jnp.dot(p_n.astype(BF16), vn_ref[:, kv], preferred_element_type=F32)
             + jnp.dot(p_c.astype(BF16), vc_ref[:, kv], preferred_element_type=F32))
        o_ref[:, h * B_HEAD_DIM:(h + 1) * B_HEAD_DIM] = (o / den).astype(o_ref.dtype)


def _window_attention(qkv, sink, n_ctx, col_q, col_k, col_v):
    t_rows = qkv.shape[0]
    nb = t_rows // WINDOW
    nb_ctx = n_ctx // WINDOW
    q_w = B_Q_HEADS * B_HEAD_DIM
    kv_w = B_KV_HEADS * B_HEAD_DIM
    assert (col_q * LANES) % q_w == 0 and (col_k * LANES) % kv_w == 0 and (col_v * LANES) % kv_w == 0

    def kv_spec(col, shift):
        def imap(n):
            return (jnp.clip(n + shift, 0, nb - 1), col * LANES // kv_w)
        return pl.BlockSpec((WINDOW, kv_w), imap)

    return pl.pallas_call(
        functools.partial(_window_kernel, nb=nb, nb_ctx=nb_ctx),
        grid=(nb,),
        in_specs=[pl.BlockSpec(memory_space=pltpu.SMEM),
                  pl.BlockSpec((WINDOW, q_w), lambda n: (n, col_q * LANES // q_w)),
                  kv_spec(col_k, -1), kv_spec(col_k, 0), kv_spec(col_k, 1),
                  kv_spec(col_v, -1), kv_spec(col_v, 0), kv_spec(col_v, 1),
                  pl.BlockSpec((n_ctx, kv_w), lambda n: (0, col_k * LANES // kv_w)),
                  pl.BlockSpec((n_ctx, kv_w), lambda n: (0, col_v * LANES // kv_w))],
        out_specs=pl.BlockSpec((WINDOW, q_w), lambda n: (n, 0)),
        out_shape=jax.ShapeDtypeStruct((t_rows, q_w), BF16),
        compiler_params=_cparams("parallel"),
        name="window_attention",
    )(sink, qkv, qkv, qkv, qkv, qkv, qkv, qkv, qkv, qkv)


ROW_PIECES = 8


def _store_packed_rows(ref, v):
    m, half = v.shape[0], v.shape[1] // 2
    lo = pltpu.bitcast(v[:, :half].astype(BF16).astype(F32), jnp.uint32) >> 16
    hi = pltpu.bitcast(v[:, half:].astype(BF16).astype(F32), jnp.uint32) & jnp.uint32(0xFFFF0000)
    w = lo | hi
    for j in range(ROW_PIECES):
        ref[pl.ds(j, m, stride=ROW_PIECES), :] = w[:, j * LANES:(j + 1) * LANES]


def _load_packed_rows(ref, m, r0=0):
    w = jnp.concatenate([ref[pl.ds(r0 * ROW_PIECES + j, m, stride=ROW_PIECES), :] for j in range(ROW_PIECES)],
                        axis=1)
    lo = pltpu.bitcast(w << 16, F32)
    hi = pltpu.bitcast(w & jnp.uint32(0xFFFF0000), F32)
    return lo, hi


def _post_mixer(y, x, g1_ref, g_ref, sh_ref, sc_ref, wr_ref, br_ref, xo_ref, h_ref, lg_ref, row0, n_ctx):
    row = row0 + lax.broadcasted_iota(jnp.int32, (x.shape[0], 1), 0)
    g1 = jnp.where(row < n_ctx, g1_ref[0:1, :], g1_ref[1:2, :])
    xn = x + g1 * y
    xo_ref[...] = xn
    h = _norm_mod(xn, g_ref[...], sh_ref[...], sc_ref[...], row0, n_ctx)
    _store_packed_rows(h_ref, h)
    h_hi = h.astype(BF16)
    h_lo = (h - h_hi.astype(F32)).astype(BF16)
    lg_ref[...] = (jnp.dot(h_hi, wr_ref[0], preferred_element_type=F32)
                   + jnp.dot(h_lo, wr_ref[0], preferred_element_type=F32)
                   + jnp.dot(h_hi, wr_ref[1], preferred_element_type=F32) + br_ref[...])


def _attn_out_kernel(ya_ref, yb_ref, x_ref, woa_ref, wob_ref, g1_ref, g_ref, sh_ref, sc_ref, wr_ref, br_ref,
                     xo_ref, h_ref, lg_ref, *, n_ctx, tm):
    y = (jnp.dot(ya_ref[...], woa_ref[...], preferred_element_type=F32)
         + jnp.dot(yb_ref[...], wob_ref[...], preferred_element_type=F32))
    _post_mixer(y, x_ref[...], g1_ref, g_ref, sh_ref, sc_ref, wr_ref, br_ref, xo_ref, h_ref, lg_ref,
                pl.program_id(0) * tm, n_ctx)


def _glu_out_kernel(y_ref, x_ref, wv_ref, wg_ref, g1_ref, g_ref, sh_ref, sc_ref, wr_ref, br_ref,
                    xo_ref, h_ref, lg_ref, *, n_ctx, tm):
    a = jax.nn.gelu(y_ref[...], approximate=True).astype(BF16)
    val = jnp.dot(a, wv_ref[...], preferred_element_type=F32)
    gate = jnp.dot(a, wg_ref[...], preferred_element_type=F32)
    _post_mixer(val * jax.nn.sigmoid(gate), x_ref[...], g1_ref, g_ref, sh_ref, sc_ref, wr_ref, br_ref,
                xo_ref, h_ref, lg_ref, pl.program_id(0) * tm, n_ctx)


def _mixer_out(kernel_fn, acts, x, weights, mod, cols, g2row, wr, br, n_ctx, name):
    t_rows, d = x.shape
    tm = 256
    row = lambda i: (i, 0)
    const = lambda i: (0, 0)
    in_specs = ([pl.BlockSpec((tm, a.shape[1]), row) for a in acts]
                + [pl.BlockSpec((tm, d), row)]
                + [pl.BlockSpec(w.shape, const) for w in weights]
                + [pl.BlockSpec((SUBLANES, d), lambda i, c=c: (0, c)) for c in cols[:1]]
                + [pl.BlockSpec((1, d), const)]
                + [pl.BlockSpec((SUBLANES, d), lambda i, c=c: (0, c)) for c in cols[1:]]
                + [pl.BlockSpec(wr.shape, lambda i: (0, 0, 0)), pl.BlockSpec(br.shape, const)])
    return pl.pallas_call(
        functools.partial(kernel_fn, n_ctx=n_ctx, tm=tm),
        grid=(t_rows // tm,),
        in_specs=in_specs,
        out_specs=[pl.BlockSpec((tm, d), row), pl.BlockSpec((tm * ROW_PIECES, LANES), row),
                   pl.BlockSpec((tm, LANES), row)],
        out_shape=[jax.ShapeDtypeStruct((t_rows, d), F32),
                   jax.ShapeDtypeStruct((t_rows * ROW_PIECES, LANES), jnp.uint32),
                   jax.ShapeDtypeStruct((t_rows, LANES), F32)],
        compiler_params=_cparams("parallel"),
        name=name,
    )(*acts, x, *weights, mod, g2row, mod, mod, wr, br)


def _router_kernel(lg_ref, idx_ref, gate_ref, cnt_ref, *, n_exp):
    @pl.when(pl.program_id(0) == 0)
    def _():
        cnt_ref[...] = jnp.zeros_like(cnt_ref)

    lt = lg_ref[...].T[0:n_exp, :]
    eid = lax.broadcasted_iota(jnp.int32, lt.shape, 0).astype(F32)
    vals, idxs = [], []
    hist = jnp.zeros(lt.shape, F32)
    for _ in range(TOP_K):
        mv = jnp.max(lt, axis=0, keepdims=True)
        ix = jnp.min(jnp.where(lt == mv, eid, float(n_exp)), axis=0, keepdims=True)
        sel = eid == ix
        hist = hist + sel.astype(F32)
        lt = jnp.where(sel, -jnp.inf, lt)
        vals.append(mv)
        idxs.append(ix)
    es = [jnp.exp(v - vals[0]) for v in vals]
    den = es[0] + es[1] + es[2] + es[3]
    pad_f = jnp.zeros((SUBLANES - TOP_K, lt.shape[1]), F32)
    idx_ref[...] = jnp.concatenate(idxs + [pad_f], axis=0).astype(jnp.int32)
    gate_ref[...] = jnp.concatenate([e / den for e in es] + [pad_f], axis=0)
    cnt_ref[...] += jnp.sum(hist, axis=1, keepdims=True)


def _router(logits, n_exp):
    t_rows = logits.shape[0]
    tm = 256
    return pl.pallas_call(
        functools.partial(_router_kernel, n_exp=n_exp),
        grid=(t_rows // tm,),
        in_specs=[pl.BlockSpec((tm, LANES), lambda i: (i, 0))],
        out_specs=[pl.BlockSpec((SUBLANES, tm), lambda i: (0, i)),
                   pl.BlockSpec((SUBLANES, tm), lambda i: (0, i)),
                   pl.BlockSpec((n_exp, 1), lambda i: (0, 0))],
        out_shape=[jax.ShapeDtypeStruct((SUBLANES, t_rows), jnp.int32),
                   jax.ShapeDtypeStruct((SUBLANES, t_rows), F32),
                   jax.ShapeDtypeStruct((n_exp, 1), F32)],
        compiler_params=_cparams("arbitrary"),
        name="router_topk",
    )(logits)


def _dest_kernel(idx_ref, start_ref, dest_ref, carry, *, n_exp, tm):
    @pl.when(pl.program_id(0) == 0)
    def _():
        carry[...] = start_ref[...]

    eid = lax.broadcasted_iota(jnp.int32, (n_exp, tm), 0)
    idx = idx_ref[...]
    sels = [eid == idx[k:k + 1, :] for k in range(TOP_K)]
    total = sels[0].astype(F32) + sels[1].astype(F32) + sels[2].astype(F32) + sels[3].astype(F32)
    rr = lax.broadcasted_iota(jnp.int32, (tm, tm), 0)
    cc = lax.broadcasted_iota(jnp.int32, (tm, tm), 1)
    upper = jnp.where(rr < cc, 1.0, 0.0).astype(BF16)
    before = jnp.dot(total.astype(BF16), upper, preferred_element_type=F32) + carry[...]
    rows = [jnp.sum(jnp.where(sels[k], before, 0.0), axis=0, keepdims=True) for k in range(TOP_K)]
    pad = jnp.zeros((SUBLANES - TOP_K, tm), F32)
    dest_ref[...] = jnp.concatenate(rows + [pad], axis=0).astype(jnp.int32)
    carry[...] += jnp.sum(total, axis=1, keepdims=True)


def _dest_rows(idx, starts, n_exp):
    t_rows = idx.shape[1]
    tm = 256
    return pl.pallas_call(
        functools.partial(_dest_kernel, n_exp=n_exp, tm=tm),
        grid=(t_rows // tm,),
        in_specs=[pl.BlockSpec((SUBLANES, tm), lambda i: (0, i)),
                  pl.BlockSpec((n_exp, 1), lambda i: (0, 0))],
        out_specs=pl.BlockSpec((SUBLANES, tm), lambda i: (0, i)),
        out_shape=jax.ShapeDtypeStruct((SUBLANES, t_rows), jnp.int32),
        scratch_shapes=[pltpu.VMEM((n_exp, 1), F32)],
        compiler_params=_cparams("arbitrary"),
        name="moe_dest_rows",
    )(idx, starts)


def _expert_gu_kernel(be_ref, bv_ref, bn_ref, x_ref, wg_ref, wu_ref, bg_ref, bu_ref, o_ref, wg_s, wu_s, *, blk):
    b = pl.program_id(1)
    nvalid = bv_ref[b]

    @pl.when(bn_ref[b] == 1)
    def _():
        wg_s[...] = wg_ref[...].astype(BF16)
        wu_s[...] = wu_ref[...].astype(BF16)

    @pl.when(nvalid > 0)
    def _():
        rows = lax.broadcasted_iota(jnp.int32, (blk, 1), 0)
        lo, hi = _load_packed_rows(x_ref, blk)
        x = jnp.concatenate([lo.astype(BF16), hi.astype(BF16)], axis=1)
        x = jnp.where(rows < nvalid, x, jnp.zeros_like(x))
        gate = jnp.dot(x, wg_s[...], preferred_element_type=F32) + bg_ref[...]
        up = jnp.dot(x, wu_s[...], preferred_element_type=F32) + bu_ref[...]
        gate = jnp.minimum(gate, SWIGLU_LIMIT)
        up = jnp.clip(up, -SWIGLU_LIMIT, SWIGLU_LIMIT)
        act = (up + 1.0) * (gate * jax.nn.sigmoid(SWIGLU_ALPHA * gate))
        o_ref[...] = act.astype(o_ref.dtype)

    @pl.when(nvalid == 0)
    def _():
        o_ref[...] = jnp.zeros_like(o_ref)


def _expert_dn_kernel(be_ref, bv_ref, bn_ref, a_ref, wd_ref, bd_ref, o_ref, wd_s, *, blk):
    b = pl.program_id(0)

    @pl.when(bn_ref[b] == 1)
    def _():
        wd_s[...] = wd_ref[...].astype(BF16)

    @pl.when(bv_ref[b] > 0)
    def _():
        y = jnp.dot(a_ref[...], wd_s[...], preferred_element_type=F32) + bd_ref[...]
        _store_packed_rows(o_ref, y)

    @pl.when(bv_ref[b] == 0)
    def _():
        o_ref[...] = jnp.zeros_like(o_ref)


def _expert_matmul(xs, blk_e, blk_valid, blk_new, w_gu, b_gu, w_dn, b_dn, layer):
    d, f_dim = w_dn.shape[3], w_dn.shape[2]
    n_rows = xs.shape[0] // ROW_PIECES
    blk = EXPERT_BLOCK
    tf = GATE_UP_TILE
    nf = f_dim // tf
    n_blocks = n_rows // blk
    packed_rows = pl.BlockSpec((blk * ROW_PIECES, LANES), lambda f, b, be, bv, bn: (b, 0))
    act = pl.pallas_call(
        functools.partial(_expert_gu_kernel, blk=blk),
        grid_spec=pltpu.PrefetchScalarGridSpec(
            num_scalar_prefetch=3,
            grid=(nf, n_blocks),
            in_specs=[packed_rows,
                      pl.BlockSpec((None, None, d, tf), lambda f, b, be, bv, bn: (layer, be[b], 0, f)),
                      pl.BlockSpec((None, None, d, tf), lambda f, b, be, bv, bn: (layer, be[b], 0, nf + f)),
                      pl.BlockSpec((None, None, 1, tf), lambda f, b, be, bv, bn: (layer, be[b], 0, f)),
                      pl.BlockSpec((None, None, 1, tf), lambda f, b, be, bv, bn: (layer, be[b], 0, nf + f))],
            out_specs=pl.BlockSpec((blk, tf), lambda f, b, be, bv, bn: (b, f)),
            scratch_shapes=[pltpu.VMEM((d, tf), BF16), pltpu.VMEM((d, tf), BF16)]),
        out_shape=jax.ShapeDtypeStruct((n_rows, f_dim), BF16),
        compiler_params=_cparams("arbitrary", "arbitrary"),
        name="expert_gate_up",
    )(blk_e, blk_valid, blk_new, xs, w_gu, w_gu, b_gu, b_gu)
    assert d == 2 * ROW_PIECES * LANES
    sub = blk // DOWN_BLOCK
    part = jnp.arange(n_blocks * sub, dtype=jnp.int32) % sub
    dn_e = jnp.repeat(blk_e, sub)
    dn_valid = jnp.clip(jnp.repeat(blk_valid, sub) - part * DOWN_BLOCK, 0, DOWN_BLOCK)
    dn_new = jnp.where(part == 0, jnp.repeat(blk_new, sub), 0)
    return pl.pallas_call(
        functools.partial(_expert_dn_kernel, blk=DOWN_BLOCK),
        grid_spec=pltpu.PrefetchScalarGridSpec(
            num_scalar_prefetch=3,
            grid=(n_blocks * sub,),
            in_specs=[pl.BlockSpec((DOWN_BLOCK, f_dim), lambda b, be, bv, bn: (b, 0)),
                      pl.BlockSpec((None, None, f_dim, d), lambda b, be, bv, bn: (layer, be[b], 0, 0)),
                      pl.BlockSpec((None, None, 1, d), lambda b, be, bv, bn: (layer, be[b], 0, 0))],
            out_specs=pl.BlockSpec((DOWN_BLOCK * ROW_PIECES, LANES), lambda b, be, bv, bn: (b, 0)),
            scratch_shapes=[pltpu.VMEM((f_dim, d), BF16)]),
        out_shape=jax.ShapeDtypeStruct((n_rows * ROW_PIECES, LANES), jnp.uint32),
        compiler_params=_cparams("arbitrary"),
        name="expert_down",
    )(dn_e, dn_valid, dn_new, act, w_dn, b_dn)


def _combine_kernel(y0_ref, y1_ref, y2_ref, y3_ref, gate_ref, x_ref, g2_ref, fg_ref, o_ref,
                    *, n_ctx, tm, final, skip):
    gates = gate_ref[...]
    f_lo, f_hi = None, None
    for k, y_ref in enumerate((y0_ref, y1_ref, y2_ref, y3_ref)):
        lo, hi = _load_packed_rows(y_ref, tm)
        gk = gates[:, k:k + 1]
        f_lo = lo * gk if f_lo is None else f_lo + lo * gk
        f_hi = hi * gk if f_hi is None else f_hi + hi * gk
    f = jnp.concatenate([f_lo, f_hi], axis=1)
    row = (pl.program_id(0) + skip) * tm + lax.broadcasted_iota(jnp.int32, (tm, 1), 0)
    g2 = jnp.where(row < n_ctx, g2_ref[0:1, :], g2_ref[1:2, :])
    xn = x_ref[...] + g2 * f
    if final:
        ms = jnp.mean(xn * xn, axis=-1, keepdims=True)
        xn = xn * lax.rsqrt(ms + EPS) * fg_ref[...]
    o_ref[...] = xn


def _combine(yk, gates_t, x, mod, g2_col, final_g, n_ctx, final):
    t_rows, d = x.shape
    tm = 256
    nt = t_rows // tm
    skip = n_ctx // tm if final else 0
    assert n_ctx % tm == 0
    return pl.pallas_call(
        functools.partial(_combine_kernel, n_ctx=n_ctx, tm=tm, final=final, skip=skip),
        grid=(nt - skip,),
        in_specs=[pl.BlockSpec((tm * ROW_PIECES, LANES), lambda i, k=k: (k * nt + i + skip, 0))
                  for k in range(TOP_K)]
                 + [pl.BlockSpec((tm, SUBLANES), lambda i: (i + skip, 0)),
                  pl.BlockSpec((tm, d), lambda i: (i + skip, 0)),
                  pl.BlockSpec((SUBLANES, d), lambda i: (0, g2_col)),
                  pl.BlockSpec((1, d), lambda i: (0, 0))],
        out_specs=pl.BlockSpec((tm, d), lambda i: (i, 0)),
        out_shape=jax.ShapeDtypeStruct((t_rows - skip * tm, d), F32),
        compiler_params=_cparams("parallel"),
        name="moe_combine",
    )(yk, yk, yk, yk, gates_t, x, mod, final_g)


SC_WINDOW = 128


def _sc_mesh():
    return plsc.VectorSubcoreMesh(core_axis_name="core", subcore_axis_name="subcore")


def _sc_scatter_rows(x, dest, n_out):
    n_src_blocks = x.shape[0] // SC_WINDOW

    @functools.partial(pl.kernel, out_type=jax.ShapeDtypeStruct((n_out, LANES), x.dtype), mesh=_sc_mesh(),
                       scratch_types=[], name="sc_dispatch_rows")
    def run(x_hbm, i_hbm, o_hbm):
        def body(x_vmem, i_vmem):
            pltpu.sync_copy(x_vmem, o_hbm.at[i_vmem.at[0]])

        pltpu.emit_pipeline(
            body, grid=(dest.shape[1] // SC_WINDOW,),
            in_specs=[pl.BlockSpec((SC_WINDOW, LANES), lambda i: (i % n_src_blocks, 0)),
                      pl.BlockSpec((1, SC_WINDOW), lambda i: (0, i))],
            out_specs=[], core_axis_name=("core", "subcore"),
            dimension_semantics=(pltpu.PARALLEL,))(x_hbm, i_hbm)

    return run(x, dest)


def _sc_gather_rows(y, idx):
    n = idx.shape[1]

    @functools.partial(pl.kernel, out_type=jax.ShapeDtypeStruct((n, LANES), y.dtype), mesh=_sc_mesh(),
                       scratch_types=[], name="sc_combine_rows")
    def run(y_hbm, i_hbm, o_hbm):
        def body(i_vmem, o_vmem):
            pltpu.sync_copy(y_hbm.at[i_vmem.at[0]], o_vmem)

        pltpu.emit_pipeline(
            body, grid=(n // SC_WINDOW,),
            in_specs=[pl.BlockSpec((1, SC_WINDOW), lambda i: (0, i))],
            out_specs=[pl.BlockSpec((SC_WINDOW, LANES), lambda i: (i, 0))],
            core_axis_name=("core", "subcore"),
            dimension_semantics=(pltpu.PARALLEL,))(i_hbm, o_hbm)

    return run(y, idx)


def _moe(h, logits, x, mod, g2_col, w_gu, b_gu, w_dn, b_dn, layer, n_ctx, final_g, final):
    t_rows = x.shape[0]
    n_exp = w_gu.shape[1]
    idx, gates, counts = _router(logits, n_exp)
    counts = counts[:, 0].astype(jnp.int32)
    blk = EXPERT_BLOCK
    padded = (counts + blk - 1) // blk * blk
    pends = jnp.cumsum(padded)
    pstarts = pends - padded
    dest = _dest_rows(idx, pstarts.astype(F32)[:, None], n_exp)[:TOP_K]
    n_blocks = -(-(t_rows * TOP_K) // blk) + n_exp
    n_rows = n_blocks * blk
    blk_start = jnp.arange(n_blocks, dtype=jnp.int32) * blk
    blk_e = jnp.minimum(jnp.sum(blk_start[:, None] >= pends[None, :], axis=1), n_exp - 1).astype(jnp.int32)
    blk_valid = jnp.clip(pstarts[blk_e] + counts[blk_e] - blk_start, 0, blk).astype(jnp.int32)
    blk_valid = jnp.where(blk_start < pends[-1], blk_valid, 0)
    blk_new = jnp.concatenate([jnp.ones((1,), jnp.int32), (blk_e[1:] != blk_e[:-1]).astype(jnp.int32)])
    dest8 = (dest[:, :, None] * ROW_PIECES + jnp.arange(ROW_PIECES, dtype=jnp.int32)).reshape(1, -1)
    xs = _sc_scatter_rows(h, dest8, n_rows * ROW_PIECES)
    ys = _expert_matmul(xs, blk_e, blk_valid, blk_new, w_gu, b_gu, w_dn, b_dn, layer)
    yk = _sc_gather_rows(ys, dest8)
    return _combine(yk, gates.T, x, mod, g2_col, final_g, n_ctx, final)


def _s5_tables(a_re, a_im, log_step, b_re, b_im, c_re, c_im, d_skip):
    tc = S5_TC
    n_grp, n_st = a_re.shape[1], a_re.shape[2]
    lr = jnp.minimum(a_re.astype(F32), -1e-4)
    li = a_im.astype(F32)
    dt = jnp.exp(log_step.astype(F32))[..., None]
    dd = jnp.arange(tc + 1, dtype=F32)[:, None, None, None]
    mag = jnp.exp(lr * dt * dd)
    pr, pi = mag * jnp.cos(li * dt * dd), mag * jnp.sin(li * dt * dd)
    ar, ai = pr[1], pi[1]
    den = lr * lr + li * li
    nr = ar - 1.0
    zr = (nr * lr + ai * li) / den
    zi = (ai * lr - nr * li) / den
    br, bi = b_re.astype(F32), b_im.astype(F32)
    bbr = zr[..., None] * br - zi[..., None] * bi
    bbi = zr[..., None] * bi + zi[..., None] * br
    abr = pr[:tc, ..., None] * bbr - pi[:tc, ..., None] * bbi
    abi = pr[:tc, ..., None] * bbi + pi[:tc, ..., None] * bbr
    cr, ci = c_re.astype(F32), c_im.astype(F32)
    kern = (jnp.einsum('xgip,dxgpj->dxgij', cr, abr, precision=HIGHEST)
            - jnp.einsum('xgip,dxgpj->dxgij', ci, abi, precision=HIGHEST))
    kdim = tc * S5_GROUP
    k_t = kern.transpose(1, 2, 4, 0, 3)
    zeros = jnp.zeros((n_grp, S5_GROUP, kdim), F32)
    ext_f = jnp.concatenate([zeros, k_t[0].reshape(n_grp, S5_GROUP, kdim)], axis=-1)
    ext_b = jnp.concatenate([k_t[1, :, :, ::-1].reshape(n_grp, S5_GROUP, kdim), zeros], axis=-1)
    m_tot = jnp.stack([ext_f[:, :, (tc - s) * S5_GROUP:(tc - s) * S5_GROUP + kdim]
                       + ext_b[:, :, (tc - 1 - s) * S5_GROUP:(tc - 1 - s) * S5_GROUP + kdim]
                       for s in range(tc)], axis=1).reshape(n_grp, kdim, kdim)
    dsk = jnp.tile(d_skip.astype(F32).reshape(n_grp, S5_GROUP), (1, tc))
    m_tot = m_tot + jnp.eye(kdim, dtype=F32)[None] * dsk[:, None, :]
    pw_f = tc - 1 - jnp.arange(tc)
    pw_b = jnp.arange(tc)

    def b_cols(part, pw, x):
        return part[pw, x].transpose(1, 0, 3, 2).reshape(n_grp, tc * S5_GROUP, n_st)

    b_mat = jnp.concatenate([b_cols(abr, pw_f, 0), b_cols(abi, pw_f, 0),
                             b_cols(abr, pw_b, 1), b_cols(abi, pw_b, 1)], axis=-1)
    pcf = 1 + jnp.arange(tc)
    pcb = tc - jnp.arange(tc)

    def c_rows(pw, x):
        prx, pix = pr[pw, x], pi[pw, x]
        re_c = cr[x][None] * prx[:, :, None, :] - ci[x][None] * pix[:, :, None, :]
        im_c = -(cr[x][None] * pix[:, :, None, :] + ci[x][None] * prx[:, :, None, :])
        to_rows = lambda z: z.transpose(1, 3, 0, 2).reshape(n_grp, n_st, tc * S5_GROUP)
        return to_rows(re_c), to_rows(im_c)

    c_mat = jnp.concatenate(list(c_rows(pcf, 0)) + list(c_rows(pcb, 1)), axis=1)
    prt, pit = pr[tc], pi[tc]
    a1 = jnp.concatenate([prt[0], prt[0], prt[1], prt[1]], axis=-1)
    a2 = jnp.concatenate([-pit[0], pit[0], -pit[1], pit[1]], axis=-1)
    return m_tot.astype(BF16), b_mat.astype(BF16), c_mat.astype(BF16), a1, a2


S5_LANE_GROUPS = LANES // S5_GROUP


def _s5_pack(u_ref, c0, cc):
    xs = [u_ref[pl.ds(c0 * S5_TC + tau, cc, stride=S5_TC), :] for tau in range(S5_TC)]
    return [jnp.concatenate([xs[tau][:, g * S5_GROUP:(g + 1) * S5_GROUP] for tau in range(S5_TC)], axis=1)
            for g in range(S5_LANE_GROUPS)]


def _s5_in_kernel(u_ref, b_ref, o_ref, ug_ref, *, cc):
    def body(ci, carry):
        c0 = pl.multiple_of(ci * cc, SUBLANES)
        ugs = _s5_pack(u_ref, c0, cc)
        for g in range(S5_LANE_GROUPS):
            ug_ref[g, pl.ds(c0, cc), :] = ugs[g]
            o_ref[pl.ds(c0, cc), g, :] = jnp.dot(ugs[g].astype(BF16), b_ref[g], preferred_element_type=F32)
        return carry

    lax.fori_loop(0, u_ref.shape[0] // (S5_TC * cc), body, 0)


def _s5_out_kernel(ug_ref, z_ref, m_ref, c_ref, o_ref, *, cc):
    def body(ci, carry):
        c0 = pl.multiple_of(ci * cc, SUBLANES)
        ys = [jnp.dot(ug_ref[g, pl.ds(c0, cc), :].astype(BF16), m_ref[g], preferred_element_type=F32)
              + jnp.dot(z_ref[pl.ds(c0, cc), g, :].astype(BF16), c_ref[g], preferred_element_type=F32)
              for g in range(S5_LANE_GROUPS)]
        for tau in range(S5_TC):
            row = jnp.concatenate([y[:, tau * S5_GROUP:(tau + 1) * S5_GROUP] for y in ys], axis=1)
            o_ref[pl.ds(c0 * S5_TC + tau, cc, stride=S5_TC), :] = row
        return carry

    lax.fori_loop(0, o_ref.shape[0] // (S5_TC * cc), body, 0)


def _s5_scan_kernel(h_ref, a1_ref, a2_ref, z_ref, *, n_chunks, n_ctx_chunks):
    half = 2 * S5_STATE
    a1 = a1_ref[...]
    a2 = a2_ref[...]
    a1f, a1b = a1[:, :half], a1[:, half:]
    a2f, a2b = a2[:, :half], a2[:, half:]
    zero = jnp.zeros((a1.shape[0], half), F32)

    def swap(v):
        return pltpu.roll(v, S5_STATE, 1)

    def step(t, carry):
        rf, rfs, rb, rbs = carry
        cb = jnp.where(t < n_ctx_chunks, n_ctx_chunks - 1 - t, n_chunks - 1 - (t - n_ctx_chunks))
        z_ref[t, :, 0:half] = rf
        z_ref[cb, :, half:2 * half] = rb
        hf = h_ref[t, :, 0:half]
        hb = h_ref[cb, :, half:2 * half]
        nrf = a1f * rf + a2f * rfs + hf
        nrfs = a1f * rfs - a2f * rf + swap(hf)
        nrb = a1b * rb + a2b * rbs + hb
        nrbs = a1b * rbs - a2b * rb + swap(hb)
        return nrf, nrfs, nrb, nrbs

    lax.fori_loop(0, n_chunks, step, (zero, zero, zero, zero), unroll=8 if n_chunks % 8 == 0 else 1)


def _s5_mixer_scan(u, tables, n_ctx):
    m_tot, b_mat, c_mat, a1, a2 = tables
    t_rows, width = u.shape
    tc = S5_TC
    n_grp = width // S5_GROUP
    n_chunks = t_rows // tc
    kdim = tc * S5_GROUP
    sdim = 4 * S5_STATE
    gb = S5_LANE_GROUPS
    gs = SUBLANES
    n_split = 2
    half_chunks = n_chunks // n_split
    cc = _pick(half_chunks, (104, 80, 40, 8))
    u_spec = pl.BlockSpec((t_rows // n_split, LANES), lambda j, r: (r, j))
    tab_spec = lambda a, b: pl.BlockSpec((gb, a, b), lambda j, r: (j, 0, 0))
    st_spec = pl.BlockSpec((half_chunks, gb, sdim), lambda j, r: (r, j, 0))
    grid = (width // LANES, n_split)
    ug_spec = pl.BlockSpec((gb, half_chunks, kdim), lambda j, r: (j, r, 0))
    hin, ug = pl.pallas_call(
        functools.partial(_s5_in_kernel, cc=cc),
        grid=grid,
        in_specs=[u_spec, tab_spec(kdim, sdim)],
        out_specs=[st_spec, ug_spec],
        out_shape=[jax.ShapeDtypeStruct((n_chunks, n_grp, sdim), F32),
                   jax.ShapeDtypeStruct((n_grp, n_chunks, kdim), F32)],
        compiler_params=_cparams("parallel", "arbitrary"),
        name="s5_chunk_inputs",
    )(u, b_mat)
    z = pl.pallas_call(
        functools.partial(_s5_scan_kernel, n_chunks=n_chunks, n_ctx_chunks=n_ctx // tc),
        grid=(n_grp // gs,),
        in_specs=[pl.BlockSpec((n_chunks, gs, sdim), lambda g: (0, g, 0)),
                  pl.BlockSpec((gs, sdim), lambda g: (g, 0)),
                  pl.BlockSpec((gs, sdim), lambda g: (g, 0))],
        out_specs=pl.BlockSpec((n_chunks, gs, sdim), lambda g: (0, g, 0)),
        out_shape=jax.ShapeDtypeStruct((n_chunks, n_grp, sdim), F32),
        compiler_params=_cparams("parallel"),
        name="s5_chunk_scan",
    )(hin, a1, a2)
    return pl.pallas_call(
        functools.partial(_s5_out_kernel, cc=cc),
        grid=grid,
        in_specs=[ug_spec, st_spec, tab_spec(kdim, kdim), tab_spec(sdim, kdim)],
        out_specs=u_spec,
        out_shape=jax.ShapeDtypeStruct((t_rows, width), F32),
        compiler_params=_cparams("parallel", "arbitrary"),
        name="s5_chunk_outputs",
    )(ug, z, m_tot, c_mat)


def _lambda_init(layer):
    return 0.8 - 0.6 * math.exp(-0.3 * layer)


def kernel(x, c, ctx, c_ctx, w_mod, b_mod, norm1_g, norm2_g, final_g, attn_w_qkv, attn_w_o, lambda_q1, lambda_k1, lambda_q2, lambda_k2, subln_g, sink_logit, s5_w_in, s5_a_re, s5_a_im, s5_log_step, s5_b_re, s5_b_im, s5_c_re, s5_c_im, s5_d, s5_w_glu, router_w, router_b, expert_w_gu, expert_b_gu, expert_w_down, expert_b_down):
    bsz, seq, d = x.shape
    assert bsz == 1, "single-sequence block"
    n_ctx = ctx.shape[1]
    depth = w_mod.shape[0]
    n_exp = router_w.shape[2]
    assert n_ctx % 256 == 0 and seq % 256 == 0

    xj = jnp.concatenate([ctx[0], x[0]], axis=0)
    cs = jnp.zeros((SUBLANES, d), F32).at[0].set(c_ctx).at[1].set(c[0])
    b_mod3 = b_mod[:, None, :]
    b_gu4 = expert_b_gu[:, :, None, :]
    b_dn4 = expert_b_down[:, :, None, :]
    wr_pad = jnp.pad(router_w, ((0, 0), (0, 0), (0, LANES - n_exp)))
    wr_hi = wr_pad.astype(BF16)
    wr_pad = jnp.stack([wr_hi, (wr_pad - wr_hi.astype(F32)).astype(BF16)], axis=1)
    br_pad = jnp.pad(router_b, ((0, 0), (0, LANES - n_exp)), constant_values=NEG_INF)[:, None, :]
    final_row = final_g[None, :]

    for i in range(depth):
        last = i == depth - 1
        j = i // 2
        mod = _adaln_mod(cs, w_mod, b_mod3, i)
        if i % 2 == 0:
            sizes = (A_HEADS * 2 * A_QK_DIM, A_HEADS * 2 * A_QK_DIM, A_HEADS * A_V_DIM,
                     B_Q_HEADS * B_HEAD_DIM, B_KV_HEADS * B_HEAD_DIM, B_KV_HEADS * B_HEAD_DIM)
            offs = np.concatenate([[0], np.cumsum(sizes)])
            types = [0, 1, ROPE_NONE, 2, 3, ROPE_NONE]
            ttype = jnp.asarray(np.concatenate([np.full(s // 256, t) for s, t in zip(sizes, types)]), jnp.int32)
            cos_t, sin_t = _rope_tables(n_ctx, seq)
            qkv = _norm_mod_matmul(xj, norm1_g[i][None, :], mod, 0, 1, attn_w_qkv[j].astype(BF16), n_ctx, BF16,
                                   rope=(ttype, cos_t, sin_t))
            qt = qkv[:, offs[0]:offs[1]].T
            ka = qkv[:, offs[1]:offs[2]]
            vt = qkv[:, offs[2]:offs[3]].T.reshape(A_HEADS, A_V_DIM, -1)
            vt = jnp.concatenate([vt, jnp.ones((A_HEADS, V_AUG - A_V_DIM, vt.shape[2]), BF16)], axis=1)
            vt = vt.reshape(A_HEADS * V_AUG, -1)
            f32 = F32
            li = _lambda_init(i)
            lam = (jnp.exp(jnp.sum(lambda_q1[j].astype(f32) * lambda_k1[j].astype(f32)))
                   - jnp.exp(jnp.sum(lambda_q2[j].astype(f32) * lambda_k2[j].astype(f32))) + li)
            attn_args = (lam.reshape(1), subln_g[j][:, None], 1.0 - li)
            ya = jnp.concatenate([_diff_attention(qt[:, :n_ctx], ka[:n_ctx], vt[:, :n_ctx], *attn_args),
                                  _diff_attention(qt[:, n_ctx:], ka, vt, *attn_args)], axis=0)
            yb = _window_attention(qkv, sink_logit[j], n_ctx, int(offs[3]) // LANES, int(offs[4]) // LANES,
                                   int(offs[5]) // LANES)
            w_o = attn_w_o[j].astype(BF16)
            na = A_HEADS * A_V_DIM
            xj, h2, logits = _mixer_out(_attn_out_kernel, [ya, yb], xj, [w_o[:na], w_o[na:]], mod, (2, 3, 4),
                                        norm2_g[i][None, :], wr_pad[i], br_pad[i], n_ctx, "attn_out_router")
        else:
            u = _norm_mod_matmul(xj, norm1_g[i][None, :], mod, 0, 1, s5_w_in[j].astype(BF16), n_ctx, F32)
            tables = _s5_tables(s5_a_re[j], s5_a_im[j], s5_log_step[j], s5_b_re[j], s5_b_im[j],
                                s5_c_re[j], s5_c_im[j], s5_d[j])
            y = _s5_mixer_scan(u, tables, n_ctx)
            w_glu = s5_w_glu[j].astype(BF16)
            xj, h2, logits = _mixer_out(_glu_out_kernel, [y], xj, [w_glu[:, :d], w_glu[:, d:]], mod, (2, 3, 4),
                                        norm2_g[i][None, :], wr_pad[i], br_pad[i], n_ctx, "glu_out_router")
        xj = _moe(h2, logits, xj, mod, 5, expert_w_gu, b_gu4, expert_w_down, b_dn4, i, n_ctx, final_row, last)
    return xj[None]
```

```python
import functools
import math

import jax
import jax.numpy as jnp
import numpy as np
from jax import lax
from jax.experimental import pallas as pl
from jax.experimental.pallas import tpu as pltpu
from jax.experimental.pallas import tpu_sc as plsc

F32 = jnp.float32
BF16 = jnp.bfloat16
F8 = jnp.float8_e4m3fn
HIGHEST = lax.Precision.HIGHEST

V7X_VMEM_BYTES = 64 * 1024 * 1024
VMEM_LIMIT = V7X_VMEM_BYTES - 8 * 1024 * 1024
LANES = 128
SUBLANES = 8

GRID_W = 64
N_MOD = 6
EPS = 1e-6
NEG_INF = -1e30
ROPE_THETA = 10000.0
A_HEADS = 8
A_QK_DIM = 64
A_V_DIM = 128
B_Q_HEADS = 8
B_KV_HEADS = 2
B_GROUP = 4
B_HEAD_DIM = 128
WINDOW = 128
S5_GROUP = 16
S5_STATE = 64
S5_TC = 16
TOP_K = 4
SWIGLU_LIMIT = 7.0
SWIGLU_ALPHA = 1.702
EXPERT_BLOCK = 512
DOWN_BLOCK = 256
GATE_UP_TILE = 1024
LOG2E = 1.4426950408889634


def _pick(n, cands):
    for c in cands:
        if n % c == 0:
            return c
    raise ValueError(f"no tile for {n} in {cands}")


def _cparams(*sem):
    return pltpu.CompilerParams(dimension_semantics=sem, vmem_limit_bytes=VMEM_LIMIT)


def _mod_kernel(c_ref, w_ref, b_ref, o_ref):
    cv = c_ref[...]
    s = cv * jax.nn.sigmoid(cv)
    o_ref[...] = jnp.dot(s, w_ref[...], preferred_element_type=F32, precision=HIGHEST) + b_ref[...]


def _adaln_mod(cs, w_mod, b_mod, layer):
    d, n = w_mod.shape[1], w_mod.shape[2]
    tn = _pick(n, (1024, 512, 256, 128))
    return pl.pallas_call(
        _mod_kernel,
        grid=(n // tn,),
        in_specs=[pl.BlockSpec((SUBLANES, d), lambda j: (0, 0)),
                  pl.BlockSpec((None, d, tn), lambda j: (layer, 0, j)),
                  pl.BlockSpec((None, 1, tn), lambda j: (layer, 0, j))],
        out_specs=pl.BlockSpec((SUBLANES, tn), lambda j: (0, j)),
        out_shape=jax.ShapeDtypeStruct((SUBLANES, n), F32),
        compiler_params=_cparams("parallel"),
        name="adaln_mod",
    )(cs, w_mod, b_mod)


def _norm_mod(x, g, sh2, sc2, row0, n_ctx):
    ms = jnp.mean(x * x, axis=-1, keepdims=True)
    y = x * lax.rsqrt(ms + EPS) * g
    row = row0 + lax.broadcasted_iota(jnp.int32, (x.shape[0], 1), 0)
    is_ctx = row < n_ctx
    sc = jnp.where(is_ctx, sc2[0:1, :], sc2[1:2, :])
    sh = jnp.where(is_ctx, sh2[0:1, :], sh2[1:2, :])
    return y * (1.0 + sc) + sh


ROPE_NONE = 4


def _rope_store(acc, cos_ref, sin_ref, o_ref, shift):
    cosv, sinv = cos_ref[...], sin_ref[...]
    for cgrp in range(acc.shape[1] // LANES):
        a = acc[:, cgrp * LANES:(cgrp + 1) * LANES]
        lane = lax.broadcasted_iota(jnp.int32, a.shape, 1)
        in_second = (lane & (2 * shift - 1)) >= shift
        sw = jnp.where(in_second, pltpu.roll(a, shift, 1), pltpu.roll(a, LANES - shift, 1))
        o_ref[:, cgrp * LANES:(cgrp + 1) * LANES] = (a * cosv + sw * sinv).astype(o_ref.dtype)


NORM_ROWS = 256


def _norm_mod_to_scratch(x_ref, g_ref, sh_ref, sc_ref, h_scr, row0, n_ctx):
    def body(r, carry):
        off = pl.multiple_of(r * NORM_ROWS, NORM_ROWS)
        h = _norm_mod(x_ref[pl.ds(off, NORM_ROWS), :], g_ref[...], sh_ref[...], sc_ref[...], row0 + off, n_ctx)
        h_scr[pl.ds(off, NORM_ROWS), :] = h.astype(h_scr.dtype)
        return carry

    lax.fori_loop(0, x_ref.shape[0] // NORM_ROWS, body, 0)


def _nmm_rope_kernel(tt_ref, x_ref, g_ref, sh_ref, sc_ref, w_ref, cos_ref, sin_ref, o_ref, h_scr,
                     *, n_ctx, tm):
    i = pl.program_id(0)
    j = pl.program_id(1)

    @pl.when(j == 0)
    def _():
        _norm_mod_to_scratch(x_ref, g_ref, sh_ref, sc_ref, h_scr, i * tm, n_ctx)

    acc = jnp.dot(h_scr[...], w_ref[...], preferred_element_type=F32)
    t = tt_ref[j]

    @pl.when(t == ROPE_NONE)
    def _():
        o_ref[...] = acc.astype(o_ref.dtype)

    @pl.when(t < 2)
    def _():
        _rope_store(acc, cos_ref, sin_ref, o_ref, A_QK_DIM // 4)

    @pl.when(jnp.logical_and(t >= 2, t < ROPE_NONE))
    def _():
        _rope_store(acc, cos_ref, sin_ref, o_ref, B_HEAD_DIM // 4)


def _nmm_plain_kernel(x_ref, g_ref, sh_ref, sc_ref, w_ref, o_ref, h_scr, *, n_ctx, tm):
    i = pl.program_id(0)
    j = pl.program_id(1)

    @pl.when(j == 0)
    def _():
        _norm_mod_to_scratch(x_ref, g_ref, sh_ref, sc_ref, h_scr, i * tm, n_ctx)

    o_ref[...] = jnp.dot(h_scr[...], w_ref[...], preferred_element_type=F32).astype(o_ref.dtype)


def _norm_mod_matmul(x, g, mod, sh_col, sc_col, w, n_ctx, out_dtype, rope=None):
    t_rows, d = x.shape
    n = w.shape[1]
    tm = _pick(t_rows, (1280, 1024, 512, 256, 128))
    tn = 256
    grid = (t_rows // tm, n // tn)
    kern_kw = dict(n_ctx=n_ctx, tm=tm)
    scratch = [pltpu.VMEM((tm, d), BF16)]
    out_shape = jax.ShapeDtypeStruct((t_rows, n), out_dtype)
    if rope is None:
        return pl.pallas_call(
            functools.partial(_nmm_plain_kernel, **kern_kw),
            grid=grid,
            in_specs=[pl.BlockSpec((tm, d), lambda i, j: (i, 0)),
                      pl.BlockSpec((1, d), lambda i, j: (0, 0)),
                      pl.BlockSpec((SUBLANES, d), lambda i, j: (0, sh_col)),
                      pl.BlockSpec((SUBLANES, d), lambda i, j: (0, sc_col)),
                      pl.BlockSpec((d, tn), lambda i, j: (0, j))],
            out_specs=pl.BlockSpec((tm, tn), lambda i, j: (i, j)),
            out_shape=out_shape,
            scratch_shapes=scratch,
            compiler_params=_cparams("parallel", "arbitrary"),
            name="norm_mod_matmul",
        )(x, g, mod, mod, w)
    ttype, cos_t, sin_t = rope
    return pl.pallas_call(
        functools.partial(_nmm_rope_kernel, **kern_kw),
        grid_spec=pltpu.PrefetchScalarGridSpec(
            num_scalar_prefetch=1,
            grid=grid,
            in_specs=[pl.BlockSpec((tm, d), lambda i, j, tt: (i, 0)),
                      pl.BlockSpec((1, d), lambda i, j, tt: (0, 0)),
                      pl.BlockSpec((SUBLANES, d), lambda i, j, tt: (0, sh_col)),
                      pl.BlockSpec((SUBLANES, d), lambda i, j, tt: (0, sc_col)),
                      pl.BlockSpec((d, tn), lambda i, j, tt: (0, j)),
                      pl.BlockSpec((None, tm, LANES), lambda i, j, tt: (tt[j], i, 0)),
                      pl.BlockSpec((None, tm, LANES), lambda i, j, tt: (tt[j], i, 0))],
            out_specs=pl.BlockSpec((tm, tn), lambda i, j, tt: (i, j)),
            scratch_shapes=scratch),
        out_shape=out_shape,
        compiler_params=_cparams("parallel", "arbitrary"),
        name="norm_mod_qkv_rope",
    )(ttype, x, g, mod, mod, w, cos_t, sin_t)


def _rope_tables(n_ctx, seq):
    pos = jnp.arange(seq)
    rows = (pos // GRID_W).astype(F32)
    cols = (pos % GRID_W).astype(F32)

    def tab(dim):
        quarter = dim // 4
        freqs = ROPE_THETA ** (-jnp.arange(quarter, dtype=F32) / quarter)
        ar, ac = rows[:, None] * freqs, cols[:, None] * freqs
        cosv = jnp.concatenate([jnp.cos(ar), jnp.cos(ar), jnp.cos(ac), jnp.cos(ac)], axis=1)
        sinv = jnp.concatenate([-jnp.sin(ar), jnp.sin(ar), -jnp.sin(ac), jnp.sin(ac)], axis=1)
        reps = LANES // dim
        cosv, sinv = jnp.tile(cosv, (1, reps)), jnp.tile(sinv, (1, reps))
        cosv = jnp.concatenate([jnp.ones((n_ctx, LANES), F32), cosv], axis=0)
        sinv = jnp.concatenate([jnp.zeros((n_ctx, LANES), F32), sinv], axis=0)
        return cosv, sinv

    ca, sa = tab(A_QK_DIM)
    cb, sb = tab(B_HEAD_DIM)
    qa = (A_QK_DIM ** -0.5) * LOG2E
    qb = B_HEAD_DIM ** -0.5
    one, zero = jnp.ones_like(ca), jnp.zeros_like(ca)
    cos_t = jnp.stack([ca * qa, ca, cb * qb, cb, one])
    sin_t = jnp.stack([sa * qa, sa, sb * qb, sb, zero])
    return cos_t, sin_t


V_AUG = A_V_DIM + 16


def _flash_kernel(lam_ref, qt_ref, k_ref, vt_ref, g_ref, o_ref, s_a, s_b, acc,
                  *, tq, tk, t_rows, out_scale):
    qt = qt_ref[...]
    row = lax.broadcasted_iota(jnp.int32, qt.shape, 0)
    zero = jnp.zeros_like(qt)
    qm = (jnp.where(row < A_QK_DIM, qt, zero).astype(F8), jnp.where(row >= A_QK_DIM, qt, zero).astype(F8))

    def start(tile, size):
        return tile * size if isinstance(tile, int) else pl.multiple_of(tile * size, size)

    def scores(tile, size, dst):
        kt = k_ref[pl.ds(start(tile, size), size), :]
        for m in range(2):
            dst[m, 0:size, :] = jnp.dot(kt, qm[m], preferred_element_type=F32).astype(BF16)

    def soft_pv(tile, size, src, ms):
        vt = vt_ref[:, pl.ds(start(tile, size), size)].astype(F8)
        new_ms = []
        for m in range(2):
            s = src[m, 0:size, :]
            mx = jnp.maximum(ms[m], jnp.max(s, axis=0, keepdims=True).astype(F32))
            alpha = jnp.exp2(ms[m] - mx)
            p = jnp.exp2(s - mx.astype(BF16)).astype(F8)
            acc[m] = alpha * acc[m] + jnp.dot(vt, p, preferred_element_type=F32)
            new_ms.append(mx)
        return tuple(new_ms)

    def finish(ms):
        o1 = acc[0, 0:A_V_DIM, :] / acc[0, A_V_DIM:A_V_DIM + 1, :]
        o2 = acc[1, 0:A_V_DIM, :] / acc[1, A_V_DIM:A_V_DIM + 1, :]
        o = o1 - lam_ref[0] * o2
        var = jnp.mean(o * o, axis=0, keepdims=True)
        o = o * lax.rsqrt(var + EPS) * (g_ref[...] * out_scale)
        o_ref[...] = o.T.astype(o_ref.dtype)

    m0 = jnp.full((1, tq), NEG_INF, F32)
    nk = t_rows // tk
    n_pairs = (nk - 1) // 2
    acc[...] = jnp.zeros_like(acc)
    scores(0, tk, s_a)

    def pair(pp, carry):
        scores(2 * pp + 1, tk, s_b)
        carry = soft_pv(2 * pp, tk, s_a, carry)
        scores(2 * pp + 2, tk, s_a)
        return soft_pv(2 * pp + 1, tk, s_b, carry)

    carry = lax.fori_loop(0, n_pairs, pair, (m0, m0))
    done = 2 * n_pairs
    if nk - done == 2:
        scores(done + 1, tk, s_b)
        carry = soft_pv(done, tk, s_a, carry)
        carry = soft_pv(done + 1, tk, s_b, carry)
    else:
        carry = soft_pv(done, tk, s_a, carry)
    finish(carry)


def _diff_attention(qt, k, vt, lam, subln_col, out_scale):
    t_rows = k.shape[0]
    n_q = qt.shape[1]
    tq = _pick(n_q, (1024, 512, 256))
    tk = _pick(t_rows, (1664, 1280, 1024, 512, 256))
    return pl.pallas_call(
        functools.partial(_flash_kernel, tq=tq, tk=tk, t_rows=t_rows, out_scale=out_scale),
        grid=(A_HEADS, n_q // tq),
        in_specs=[pl.BlockSpec(memory_space=pltpu.SMEM),
                  pl.BlockSpec((LANES, tq), lambda h, i: (h, i)),
                  pl.BlockSpec((t_rows, LANES), lambda h, i: (0, h)),
                  pl.BlockSpec((V_AUG, t_rows), lambda h, i: (h, 0)),
                  pl.BlockSpec((LANES, 1), lambda h, i: (0, 0))],
        out_specs=pl.BlockSpec((tq, LANES), lambda h, i: (i, h)),
        out_shape=jax.ShapeDtypeStruct((n_q, A_HEADS * A_V_DIM), BF16),
        scratch_shapes=[pltpu.VMEM((2, tk, tq), BF16), pltpu.VMEM((2, tk, tq), BF16),
                        pltpu.VMEM((2, V_AUG, tq), F32)],
        compiler_params=_cparams("parallel", "arbitrary"),
        name="diff_attention",
    )(lam, qt, k, vt, subln_col)


def _window_kernel(sink_ref, q_ref, kp_ref, ko_ref, kn_ref, vp_ref, vo_ref, vn_ref, kc_ref, vc_ref, o_ref,
                   *, nb, nb_ctx):
    n = pl.program_id(0)
    qi = lax.broadcasted_iota(jnp.int32, (WINDOW, WINDOW), 0)
    kk = lax.broadcasted_iota(jnp.int32, (WINDOW, WINDOW), 1)
    own_ok = n >= nb_ctx
    prev_ok = n >= nb_ctx + 1
    next_ok = jnp.logical_and(own_ok, n <= nb - 2)
    m_prev = jnp.logical_and(kk >= qi, prev_ok)
    m_own = jnp.logical_and(kk >= 0, own_ok)
    m_next = jnp.logical_and(kk <= qi, next_ok)
    dn = (((1,), (1,)), ((), ()))
    for h in range(B_Q_HEADS):
        kv = slice((h // B_GROUP) * B_HEAD_DIM, (h // B_GROUP + 1) * B_HEAD_DIM)
        q = q_ref[:, h * B_HEAD_DIM:(h + 1) * B_HEAD_DIM]
        s_p = jnp.where(m_prev, lax.dot_general(q, kp_ref[:, kv], dn, preferred_element_type=F32), NEG_INF)
        s_o = jnp.where(m_own, lax.dot_general(q, ko_ref[:, kv], dn, preferred_element_type=F32), NEG_INF)
        s_n = jnp.where(m_next, lax.dot_general(q, kn_ref[:, kv], dn, preferred_element_type=F32), NEG_INF)
        s_c = lax.dot_general(q, kc_ref[:, kv], dn, preferred_element_type=F32)
        sink = sink_ref[h]
        mx = jnp.maximum(jnp.maximum(jnp.max(s_p, axis=1, keepdims=True), jnp.max(s_o, axis=1, keepdims=True)),
                         jnp.maximum(jnp.max(s_n, axis=1, keepdims=True), jnp.max(s_c, axis=1, keepdims=True)))
        mx = jnp.maximum(mx, sink)
        p_p, p_o, p_n, p_c = jnp.exp(s_p - mx), jnp.exp(s_o - mx), jnp.exp(s_n - mx), jnp.exp(s_c - mx)
        den = (jnp.sum(p_p, axis=1, keepdims=True) + jnp.sum(p_o, axis=1, keepdims=True)
               + jnp.sum(p_n, axis=1, keepdims=True) + jnp.sum(p_c, axis=1, keepdims=True)
               + jnp.exp(sink - mx))
        o = (jnp.dot(p_p.astype(BF16), vp_ref[:, kv], preferred_element_type=F32)
             + jnp.dot(p_o.astype(BF16), vo_ref[:, kv], preferred_element_type=F32)
             + jnp.dot(p_n.astype(BF16), vn_ref[:, kv], preferred_element_type=F32)
             + jnp.dot(p_c.astype(BF16), vc_ref[:, kv], preferred_element_type=F32))
        o_ref[:, h * B_HEAD_DIM:(h + 1) * B_HEAD_DIM] = (o / den).astype(o_ref.dtype)


def _window_attention(qkv, sink, n_ctx, col_q, col_k, col_v):
    t_rows = qkv.shape[0]
    nb = t_rows // WINDOW
    nb_ctx = n_ctx // WINDOW
    q_w = B_Q_HEADS * B_HEAD_DIM
    kv_w = B_KV_HEADS * B_HEAD_DIM
    assert (col_q * LANES) % q_w == 0 and (col_k * LANES) % kv_w == 0 and (col_v * LANES) % kv_w == 0

    def kv_spec(col, shift):
        def imap(n):
            return (jnp.clip(n + shift, 0, nb - 1), col * LANES // kv_w)
        return pl.BlockSpec((WINDOW, kv_w), imap)

    return pl.pallas_call(
        functools.partial(_window_kernel, nb=nb, nb_ctx=nb_ctx),
        grid=(nb,),
        in_specs=[pl.BlockSpec(memory_space=pltpu.SMEM),
                  pl.BlockSpec((WINDOW, q_w), lambda n: (n, col_q * LANES // q_w)),
                  kv_spec(col_k, -1), kv_spec(col_k, 0), kv_spec(col_k, 1),
                  kv_spec(col_v, -1), kv_spec(col_v, 0), kv_spec(col_v, 1),
                  pl.BlockSpec((n_ctx, kv_w), lambda n: (0, col_k * LANES // kv_w)),
                  pl.BlockSpec((n_ctx, kv_w), lambda n: (0, col_v * LANES // kv_w))],
        out_specs=pl.BlockSpec((WINDOW, q_w), lambda n: (n, 0)),
        out_shape=jax.ShapeDtypeStruct((t_rows, q_w), BF16),
        compiler_params=_cparams("parallel"),
        name="window_attention",
    )(sink, qkv, qkv, qkv, qkv, qkv, qkv, qkv, qkv, qkv)


ROW_PIECES = 8


def _store_packed_rows(ref, v):
    m, half = v.shape[0], v.shape[1] // 2
    lo = pltpu.bitcast(v[:, :half].astype(BF16).astype(F32), jnp.uint32) >> 16
    hi = pltpu.bitcast(v[:, half:].astype(BF16).astype(F32), jnp.uint32) & jnp.uint32(0xFFFF0000)
    w = lo | hi
    for j in range(ROW_PIECES):
        ref[pl.ds(j, m, stride=ROW_PIECES), :] = w[:, j * LANES:(j + 1) * LANES]


def _load_packed_rows(ref, m, r0=0):
    w = jnp.concatenate([ref[pl.ds(r0 * ROW_PIECES + j, m, stride=ROW_PIECES), :] for j in range(ROW_PIECES)],
                        axis=1)
    lo = pltpu.bitcast(w << 16, F32)
    hi = pltpu.bitcast(w & jnp.uint32(0xFFFF0000), F32)
    return lo, hi


def _post_mixer(y, x, g1_ref, g_ref, sh_ref, sc_ref, wr_ref, br_ref, xo_ref, h_ref, lg_ref, row0, n_ctx):
    row = row0 + lax.broadcasted_iota(jnp.int32, (x.shape[0], 1), 0)
    g1 = jnp.where(row < n_ctx, g1_ref[0:1, :], g1_ref[1:2, :])
    xn = x + g1 * y
    xo_ref[...] = xn
    h = _norm_mod(xn, g_ref[...], sh_ref[...], sc_ref[...], row0, n_ctx)
    _store_packed_rows(h_ref, h)
    h_hi = h.astype(BF16)
    h_lo = (h - h_hi.astype(F32)).astype(BF16)
    lg_ref[...] = (jnp.dot(h_hi, wr_ref[0], preferred_element_type=F32)
                   + jnp.dot(h_lo, wr_ref[0], preferred_element_type=F32)
                   + jnp.dot(h_hi, wr_ref[1], preferred_element_type=F32) + br_ref[...])


def _attn_out_kernel(ya_ref, yb_ref, x_ref, woa_ref, wob_ref, g1_ref, g_ref, sh_ref, sc_ref, wr_ref, br_ref,
                     xo_ref, h_ref, lg_ref, *, n_ctx, tm):
    y = (jnp.dot(ya_ref[...], woa_ref[...], preferred_element_type=F32)
         + jnp.dot(yb_ref[...], wob_ref[...], preferred_element_type=F32))
    _post_mixer(y, x_ref[...], g1_ref, g_ref, sh_ref, sc_ref, wr_ref, br_ref, xo_ref, h_ref, lg_ref,
                pl.program_id(0) * tm, n_ctx)


def _glu_out_kernel(y_ref, x_ref, wv_ref, wg_ref, g1_ref, g_ref, sh_ref, sc_ref, wr_ref, br_ref,
                    xo_ref, h_ref, lg_ref, *, n_ctx, tm):
    a = jax.nn.gelu(y_ref[...], approximate=True).astype(BF16)
    val = jnp.dot(a, wv_ref[...], preferred_element_type=F32)
    gate = jnp.dot(a, wg_ref[...], preferred_element_type=F32)
    _post_mixer(val * jax.nn.sigmoid(gate), x_ref[...], g1_ref, g_ref, sh_ref, sc_ref, wr_ref, br_ref,
                xo_ref, h_ref, lg_ref, pl.program_id(0) * tm, n_ctx)


def _mixer_out(kernel_fn, acts, x, weights, mod, cols, g2row, wr, br, n_ctx, name):
    t_rows, d = x.shape
    tm = 256
    row = lambda i: (i, 0)
    const = lambda i: (0, 0)
    in_specs = ([pl.BlockSpec((tm, a.shape[1]), row) for a in acts]
                + [pl.BlockSpec((tm, d), row)]
                + [pl.BlockSpec(w.shape, const) for w in weights]
                + [pl.BlockSpec((SUBLANES, d), lambda i, c=c: (0, c)) for c in cols[:1]]
                + [pl.BlockSpec((1, d), const)]
                + [pl.BlockSpec((SUBLANES, d), lambda i, c=c: (0, c)) for c in cols[1:]]
                + [pl.BlockSpec(wr.shape, lambda i: (0, 0, 0)), pl.BlockSpec(br.shape, const)])
    return pl.pallas_call(
        functools.partial(kernel_fn, n_ctx=n_ctx, tm=tm),
        grid=(t_rows // tm,),
        in_specs=in_specs,
        out_specs=[pl.BlockSpec((tm, d), row), pl.BlockSpec((tm * ROW_PIECES, LANES), row),
                   pl.BlockSpec((tm, LANES), row)],
        out_shape=[jax.ShapeDtypeStruct((t_rows, d), F32),
                   jax.ShapeDtypeStruct((t_rows * ROW_PIECES, LANES), jnp.uint32),
                   jax.ShapeDtypeStruct((t_rows, LANES), F32)],
        compiler_params=_cparams("parallel"),
        name=name,
    )(*acts, x, *weights, mod, g2row, mod, mod, wr, br)


def _router_kernel(lg_ref, idx_ref, gate_ref, cnt_ref, *, n_exp):
    @pl.when(pl.program_id(0) == 0)
    def _():
        cnt_ref[...] = jnp.zeros_like(cnt_ref)

    lt = lg_ref[...].T[0:n_exp, :]
    eid = lax.broadcasted_iota(jnp.int32, lt.shape, 0).astype(F32)
    vals, idxs = [], []
    hist = jnp.zeros(lt.shape, F32)
    for _ in range(TOP_K):
        mv = jnp.max(lt, axis=0, keepdims=True)
        ix = jnp.min(jnp.where(lt == mv, eid, float(n_exp)), axis=0, keepdims=True)
        sel = eid == ix
        hist = hist + sel.astype(F32)
        lt = jnp.where(sel, -jnp.inf, lt)
        vals.append(mv)
        idxs.append(ix)
    es = [jnp.exp(v - vals[0]) for v in vals]
    den = es[0] + es[1] + es[2] + es[3]
    pad_f = jnp.zeros((SUBLANES - TOP_K, lt.shape[1]), F32)
    idx_ref[...] = jnp.concatenate(idxs + [pad_f], axis=0).astype(jnp.int32)
    gate_ref[...] = jnp.concatenate([e / den for e in es] + [pad_f], axis=0)
    cnt_ref[...] += jnp.sum(hist, axis=1, keepdims=True)


def _router(logits, n_exp):
    t_rows = logits.shape[0]
    tm = 256
    return pl.pallas_call(
        functools.partial(_router_kernel, n_exp=n_exp),
        grid=(t_rows // tm,),
        in_specs=[pl.BlockSpec((tm, LANES), lambda i: (i, 0))],
        out_specs=[pl.BlockSpec((SUBLANES, tm), lambda i: (0, i)),
                   pl.BlockSpec((SUBLANES, tm), lambda i: (0, i)),
                   pl.BlockSpec((n_exp, 1), lambda i: (0, 0))],
        out_shape=[jax.ShapeDtypeStruct((SUBLANES, t_rows), jnp.int32),
                   jax.ShapeDtypeStruct((SUBLANES, t_rows), F32),
                   jax.ShapeDtypeStruct((n_exp, 1), F32)],
        compiler_params=_cparams("arbitrary"),
        name="router_topk",
    )(logits)


def _dest_kernel(idx_ref, start_ref, dest_ref, carry, *, n_exp, tm):
    @pl.when(pl.program_id(0) == 0)
    def _():
        carry[...] = start_ref[...]

    eid = lax.broadcasted_iota(jnp.int32, (n_exp, tm), 0)
    idx = idx_ref[...]
    sels = [eid == idx[k:k + 1, :] for k in range(TOP_K)]
    total = sels[0].astype(F32) + sels[1].astype(F32) + sels[2].astype(F32) + sels[3].astype(F32)
    rr = lax.broadcasted_iota(jnp.int32, (tm, tm), 0)
    cc = lax.broadcasted_iota(jnp.int32, (tm, tm), 1)
    upper = jnp.where(rr < cc, 1.0, 0.0).astype(BF16)
    before = jnp.dot(total.astype(BF16), upper, preferred_element_type=F32) + carry[...]
    rows = [jnp.sum(jnp.where(sels[k], before, 0.0), axis=0, keepdims=True) for k in range(TOP_K)]
    pad = jnp.zeros((SUBLANES - TOP_K, tm), F32)
    dest_ref[...] = jnp.concatenate(rows + [pad], axis=0).astype(jnp.int32)
    carry[...] += jnp.sum(total, axis=1, keepdims=True)


def _dest_rows(idx, starts, n_exp):
    t_rows = idx.shape[1]
    tm = 256
    return pl.pallas_call(
        functools.partial(_dest_kernel, n_exp=n_exp, tm=tm),
        grid=(t_rows // tm,),
        in_specs=[pl.BlockSpec((SUBLANES, tm), lambda i: (0, i)),
                  pl.BlockSpec((n_exp, 1), lambda i: (0, 0))],
        out_specs=pl.BlockSpec((SUBLANES, tm), lambda i: (0, i)),
        out_shape=jax.ShapeDtypeStruct((SUBLANES, t_rows), jnp.int32),
        scratch_shapes=[pltpu.VMEM((n_exp, 1), F32)],
        compiler_params=_cparams("arbitrary"),
        name="moe_dest_rows",
    )(idx, starts)


def _expert_gu_kernel(be_ref, bv_ref, bn_ref, x_ref, wg_ref, wu_ref, bg_ref, bu_ref, o_ref, wg_s, wu_s, *, blk):
    b = pl.program_id(1)
    nvalid = bv_ref[b]

    @pl.when(bn_ref[b] == 1)
    def _():
        wg_s[...] = wg_ref[...].astype(BF16)
        wu_s[...] = wu_ref[...].astype(BF16)

    @pl.when(nvalid > 0)
    def _():
        rows = lax.broadcasted_iota(jnp.int32, (blk, 1), 0)
        lo, hi = _load_packed_rows(x_ref, blk)
        x = jnp.concatenate([lo.astype(BF16), hi.astype(BF16)], axis=1)
        x = jnp.where(rows < nvalid, x, jnp.zeros_like(x))
        gate = jnp.dot(x, wg_s[...], preferred_element_type=F32) + bg_ref[...]
        up = jnp.dot(x, wu_s[...], preferred_element_type=F32) + bu_ref[...]
        gate = jnp.minimum(gate, SWIGLU_LIMIT)
        up = jnp.clip(up, -SWIGLU_LIMIT, SWIGLU_LIMIT)
        act = (up + 1.0) * (gate * jax.nn.sigmoid(SWIGLU_ALPHA * gate))
        o_ref[...] = act.astype(o_ref.dtype)

    @pl.when(nvalid == 0)
    def _():
        o_ref[...] = jnp.zeros_like(o_ref)


def _expert_dn_kernel(be_ref, bv_ref, bn_ref, a_ref, wd_ref, bd_ref, o_ref, wd_s, *, blk):
    b = pl.program_id(0)

    @pl.when(bn_ref[b] == 1)
    def _():
        wd_s[...] = wd_ref[...].astype(BF16)

    @pl.when(bv_ref[b] > 0)
    def _():
        y = jnp.dot(a_ref[...], wd_s[...], preferred_element_type=F32) + bd_ref[...]
        _store_packed_rows(o_ref, y)

    @pl.when(bv_ref[b] == 0)
    def _():
        o_ref[...] = jnp.zeros_like(o_ref)


def _expert_matmul(xs, blk_e, blk_valid, blk_new, w_gu, b_gu, w_dn, b_dn, layer):
    d, f_dim = w_dn.shape[3], w_dn.shape[2]
    n_rows = xs.shape[0] // ROW_PIECES
    blk = EXPERT_BLOCK
    tf = GATE_UP_TILE
    nf = f_dim // tf
    n_blocks = n_rows // blk
    packed_rows = pl.BlockSpec((blk * ROW_PIECES, LANES), lambda f, b, be, bv, bn: (b, 0))
    act = pl.pallas_call(
        functools.partial(_expert_gu_kernel, blk=blk),
        grid_spec=pltpu.PrefetchScalarGridSpec(
            num_scalar_prefetch=3,
            grid=(nf, n_blocks),
            in_specs=[packed_rows,
                      pl.BlockSpec((None, None, d, tf), lambda f, b, be, bv, bn: (layer, be[b], 0, f)),
                      pl.BlockSpec((None, None, d, tf), lambda f, b, be, bv, bn: (layer, be[b], 0, nf + f)),
                      pl.BlockSpec((None, None, 1, tf), lambda f, b, be, bv, bn: (layer, be[b], 0, f)),
                      pl.BlockSpec((None, None, 1, tf), lambda f, b, be, bv, bn: (layer, be[b], 0, nf + f))],
            out_specs=pl.BlockSpec((blk, tf), lambda f, b, be, bv, bn: (b, f)),
            scratch_shapes=[pltpu.VMEM((d, tf), BF16), pltpu.VMEM((d, tf), BF16)]),
        out_shape=jax.ShapeDtypeStruct((n_rows, f_dim), BF16),
        compiler_params=_cparams("arbitrary", "arbitrary"),
        name="expert_gate_up",
    )(blk_e, blk_valid, blk_new, xs, w_gu, w_gu, b_gu, b_gu)
    assert d == 2 * ROW_PIECES * LANES
    sub = blk // DOWN_BLOCK
    part = jnp.arange(n_blocks * sub, dtype=jnp.int32) % sub
    dn_e = jnp.repeat(blk_e, sub)
    dn_valid = jnp.clip(jnp.repeat(blk_valid, sub) - part * DOWN_BLOCK, 0, DOWN_BLOCK)
    dn_new = jnp.where(part == 0, jnp.repeat(blk_new, sub), 0)
    return pl.pallas_call(
        functools.partial(_expert_dn_kernel, blk=DOWN_BLOCK),
        grid_spec=pltpu.PrefetchScalarGridSpec(
            num_scalar_prefetch=3,
            grid=(n_blocks * sub,),
            in_specs=[pl.BlockSpec((DOWN_BLOCK, f_dim), lambda b, be, bv, bn: (b, 0)),
                      pl.BlockSpec((None, None, f_dim, d), lambda b, be, bv, bn: (layer, be[b], 0, 0)),
                      pl.BlockSpec((None, None, 1, d), lambda b, be, bv, bn: (layer, be[b], 0, 0))],
            out_specs=pl.BlockSpec((DOWN_BLOCK * ROW_PIECES, LANES), lambda b, be, bv, bn: (b, 0)),
            scratch_shapes=[pltpu.VMEM((f_dim, d), BF16)]),
        out_shape=jax.ShapeDtypeStruct((n_rows * ROW_PIECES, LANES), jnp.uint32),
        compiler_params=_cparams("arbitrary"),
        name="expert_down",
    )(dn_e, dn_valid, dn_new, act, w_dn, b_dn)


def _combine_kernel(y0_ref, y1_ref, y2_ref, y3_ref, gate_ref, x_ref, g2_ref, fg_ref, o_ref,
                    *, n_ctx, tm, final, skip):
    gates = gate_ref[...]
    f_lo, f_hi = None, None
    for k, y_ref in enumerate((y0_ref, y1_ref, y2_ref, y3_ref)):
        lo, hi = _load_packed_rows(y_ref, tm)
        gk = gates[:, k:k + 1]
        f_lo = lo * gk if f_lo is None else f_lo + lo * gk
        f_hi = hi * gk if f_hi is None else f_hi + hi * gk
    f = jnp.concatenate([f_lo, f_hi], axis=1)
    row = (pl.program_id(0) + skip) * tm + lax.broadcasted_iota(jnp.int32, (tm, 1), 0)
    g2 = jnp.where(row < n_ctx, g2_ref[0:1, :], g2_ref[1:2, :])
    xn = x_ref[...] + g2 * f
    if final:
        ms = jnp.mean(xn * xn, axis=-1, keepdims=True)
        xn = xn * lax.rsqrt(ms + EPS) * fg_ref[...]
    o_ref[...] = xn


def _combine(yk, gates_t, x, mod, g2_col, final_g, n_ctx, final):
    t_rows, d = x.shape
    tm = 256
    nt = t_rows // tm
    skip = n_ctx // tm if final else 0
    assert n_ctx % tm == 0
    return pl.pallas_call(
        functools.partial(_combine_kernel, n_ctx=n_ctx, tm=tm, final=final, skip=skip),
        grid=(nt - skip,),
        in_specs=[pl.BlockSpec((tm * ROW_PIECES, LANES), lambda i, k=k: (k * nt + i + skip, 0))
                  for k in range(TOP_K)]
                 + [pl.BlockSpec((tm, SUBLANES), lambda i: (i + skip, 0)),
                  pl.BlockSpec((tm, d), lambda i: (i + skip, 0)),
                  pl.BlockSpec((SUBLANES, d), lambda i: (0, g2_col)),
                  pl.BlockSpec((1, d), lambda i: (0, 0))],
        out_specs=pl.BlockSpec((tm, d), lambda i: (i, 0)),
        out_shape=jax.ShapeDtypeStruct((t_rows - skip * tm, d), F32),
        compiler_params=_cparams("parallel"),
        name="moe_combine",
    )(yk, yk, yk, yk, gates_t, x, mod, final_g)


SC_WINDOW = 128


def _sc_mesh():
    return plsc.VectorSubcoreMesh(core_axis_name="core", subcore_axis_name="subcore")


def _sc_scatter_rows(x, dest, n_out):
    n_src_blocks = x.shape[0] // SC_WINDOW

    @functools.partial(pl.kernel, out_type=jax.ShapeDtypeStruct((n_out, LANES), x.dtype), mesh=_sc_mesh(),
                       scratch_types=[], name="sc_dispatch_rows")
    def run(x_hbm, i_hbm, o_hbm):
        def body(x_vmem, i_vmem):
            pltpu.sync_copy(x_vmem, o_hbm.at[i_vmem.at[0]])

        pltpu.emit_pipeline(
            body, grid=(dest.shape[1] // SC_WINDOW,),
            in_specs=[pl.BlockSpec((SC_WINDOW, LANES), lambda i: (i % n_src_blocks, 0)),
                      pl.BlockSpec((1, SC_WINDOW), lambda i: (0, i))],
            out_specs=[], core_axis_name=("core", "subcore"),
            dimension_semantics=(pltpu.PARALLEL,))(x_hbm, i_hbm)

    return run(x, dest)


def _sc_gather_rows(y, idx):
    n = idx.shape[1]

    @functools.partial(pl.kernel, out_type=jax.ShapeDtypeStruct((n, LANES), y.dtype), mesh=_sc_mesh(),
                       scratch_types=[], name="sc_combine_rows")
    def run(y_hbm, i_hbm, o_hbm):
        def body(i_vmem, o_vmem):
            pltpu.sync_copy(y_hbm.at[i_vmem.at[0]], o_vmem)

        pltpu.emit_pipeline(
            body, grid=(n // SC_WINDOW,),
            in_specs=[pl.BlockSpec((1, SC_WINDOW), lambda i: (0, i))],
            out_specs=[pl.BlockSpec((SC_WINDOW, LANES), lambda i: (i, 0))],
            core_axis_name=("core", "subcore"),
            dimension_semantics=(pltpu.PARALLEL,))(i_hbm, o_hbm)

    return run(y, idx)


def _moe(h, logits, x, mod, g2_col, w_gu, b_gu, w_dn, b_dn, layer, n_ctx, final_g, final):
    t_rows = x.shape[0]
    n_exp = w_gu.shape[1]
    idx, gates, counts = _router(logits, n_exp)
    counts = counts[:, 0].astype(jnp.int32)
    blk = EXPERT_BLOCK
    padded = (counts + blk - 1) // blk * blk
    pends = jnp.cumsum(padded)
    pstarts = pends - padded
    dest = _dest_rows(idx, pstarts.astype(F32)[:, None], n_exp)[:TOP_K]
    n_blocks = -(-(t_rows * TOP_K) // blk) + n_exp
    n_rows = n_blocks * blk
    blk_start = jnp.arange(n_blocks, dtype=jnp.int32) * blk
    blk_e = jnp.minimum(jnp.sum(blk_start[:, None] >= pends[None, :], axis=1), n_exp - 1).astype(jnp.int32)
    blk_valid = jnp.clip(pstarts[blk_e] + counts[blk_e] - blk_start, 0, blk).astype(jnp.int32)
    blk_valid = jnp.where(blk_start < pends[-1], blk_valid, 0)
    blk_new = jnp.concatenate([jnp.ones((1,), jnp.int32), (blk_e[1:] != blk_e[:-1]).astype(jnp.int32)])
    dest8 = (dest[:, :, None] * ROW_PIECES + jnp.arange(ROW_PIECES, dtype=jnp.int32)).reshape(1, -1)
    xs = _sc_scatter_rows(h, dest8, n_rows * ROW_PIECES)
    ys = _expert_matmul(xs, blk_e, blk_valid, blk_new, w_gu, b_gu, w_dn, b_dn, layer)
    yk = _sc_gather_rows(ys, dest8)
    return _combine(yk, gates.T, x, mod, g2_col, final_g, n_ctx, final)


def _s5_tables(a_re, a_im, log_step, b_re, b_im, c_re, c_im, d_skip):
    tc = S5_TC
    n_grp, n_st = a_re.shape[1], a_re.shape[2]
    lr = jnp.minimum(a_re.astype(F32), -1e-4)
    li = a_im.astype(F32)
    dt = jnp.exp(log_step.astype(F32))[..., None]
    dd = jnp.arange(tc + 1, dtype=F32)[:, None, None, None]
    mag = jnp.exp(lr * dt * dd)
    pr, pi = mag * jnp.cos(li * dt * dd), mag * jnp.sin(li * dt * dd)
    ar, ai = pr[1], pi[1]
    den = lr * lr + li * li
    nr = ar - 1.0
    zr = (nr * lr + ai * li) / den
    zi = (ai * lr - nr * li) / den
    br, bi = b_re.astype(F32), b_im.astype(F32)
    bbr = zr[..., None] * br - zi[..., None] * bi
    bbi = zr[..., None] * bi + zi[..., None] * br
    abr = pr[:tc, ..., None] * bbr - pi[:tc, ..., None] * bbi
    abi = pr[:tc, ..., None] * bbi + pi[:tc, ..., None] * bbr
    cr, ci = c_re.astype(F32), c_im.astype(F32)
    kern = (jnp.einsum('xgip,dxgpj->dxgij', cr, abr, precision=HIGHEST)
            - jnp.einsum('xgip,dxgpj->dxgij', ci, abi, precision=HIGHEST))
    kdim = tc * S5_GROUP
    k_t = kern.transpose(1, 2, 4, 0, 3)
    zeros = jnp.zeros((n_grp, S5_GROUP, kdim), F32)
    ext_f = jnp.concatenate([zeros, k_t[0].reshape(n_grp, S5_GROUP, kdim)], axis=-1)
    ext_b = jnp.concatenate([k_t[1, :, :, ::-1].reshape(n_grp, S5_GROUP, kdim), zeros], axis=-1)
    m_tot = jnp.stack([ext_f[:, :, (tc - s) * S5_GROUP:(tc - s) * S5_GROUP + kdim]
                       + ext_b[:, :, (tc - 1 - s) * S5_GROUP:(tc - 1 - s) * S5_GROUP + kdim]
                       for s in range(tc)], axis=1).reshape(n_grp, kdim, kdim)
    dsk = jnp.tile(d_skip.astype(F32).reshape(n_grp, S5_GROUP), (1, tc))
    m_tot = m_tot + jnp.eye(kdim, dtype=F32)[None] * dsk[:, None, :]
    pw_f = tc - 1 - jnp.arange(tc)
    pw_b = jnp.arange(tc)

    def b_cols(part, pw, x):
        return part[pw, x].transpose(1, 0, 3, 2).reshape(n_grp, tc * S5_GROUP, n_st)

    b_mat = jnp.concatenate([b_cols(abr, pw_f, 0), b_cols(abi, pw_f, 0),
                             b_cols(abr, pw_b, 1), b_cols(abi, pw_b, 1)], axis=-1)
    pcf = 1 + jnp.arange(tc)
    pcb = tc - jnp.arange(tc)

    def c_rows(pw, x):
        prx, pix = pr[pw, x], pi[pw, x]
        re_c = cr[x][None] * prx[:, :, None, :] - ci[x][None] * pix[:, :, None, :]
        im_c = -(cr[x][None] * pix[:, :, None, :] + ci[x][None] * prx[:, :, None, :])
        to_rows = lambda z: z.transpose(1, 3, 0, 2).reshape(n_grp, n_st, tc * S5_GROUP)
        return to_rows(re_c), to_rows(im_c)

    c_mat = jnp.concatenate(list(c_rows(pcf, 0)) + list(c_rows(pcb, 1)), axis=1)
    prt, pit = pr[tc], pi[tc]
    a1 = jnp.concatenate([prt[0], prt[0], prt[1], prt[1]], axis=-1)
    a2 = jnp.concatenate([-pit[0], pit[0], -pit[1], pit[1]], axis=-1)
    return m_tot.astype(BF16), b_mat.astype(BF16), c_mat.astype(BF16), a1, a2


S5_LANE_GROUPS = LANES // S5_GROUP


def _s5_pack(u_ref, c0, cc):
    xs = [u_ref[pl.ds(c0 * S5_TC + tau, cc, stride=S5_TC), :] for tau in range(S5_TC)]
    return [jnp.concatenate([xs[tau][:, g * S5_GROUP:(g + 1) * S5_GROUP] for tau in range(S5_TC)], axis=1)
            for g in range(S5_LANE_GROUPS)]


def _s5_in_kernel(u_ref, b_ref, o_ref, ug_ref, *, cc):
    def body(ci, carry):
        c0 = pl.multiple_of(ci * cc, SUBLANES)
        ugs = _s5_pack(u_ref, c0, cc)
        for g in range(S5_LANE_GROUPS):
            ug_ref[g, pl.ds(c0, cc), :] = ugs[g]
            o_ref[pl.ds(c0, cc), g, :] = jnp.dot(ugs[g].astype(BF16), b_ref[g], preferred_element_type=F32)
        return carry

    lax.fori_loop(0, u_ref.shape[0] // (S5_TC * cc), body, 0)


def _s5_out_kernel(ug_ref, z_ref, m_ref, c_ref, o_ref, *, cc):
    def body(ci, carry):
        c0 = pl.multiple_of(ci * cc, SUBLANES)
        ys = [jnp.dot(ug_ref[g, pl.ds(c0, cc), :].astype(BF16), m_ref[g], preferred_element_type=F32)
              + jnp.dot(z_ref[pl.ds(c0, cc), g, :].astype(BF16), c_ref[g], preferred_element_type=F32)
              for g in range(S5_LANE_GROUPS)]
        for tau in range(S5_TC):
            row = jnp.concatenate([y[:, tau * S5_GROUP:(tau + 1) * S5_GROUP] for y in ys], axis=1)
            o_ref[pl.ds(c0 * S5_TC + tau, cc, stride=S5_TC), :] = row
        return carry

    lax.fori_loop(0, o_ref.shape[0] // (S5_TC * cc), body, 0)


def _s5_scan_kernel(h_ref, a1_ref, a2_ref, z_ref, *, n_chunks, n_ctx_chunks):
    half = 2 * S5_STATE
    a1 = a1_ref[...]
    a2 = a2_ref[...]
    a1f, a1b = a1[:, :half], a1[:, half:]
    a2f, a2b = a2[:, :half], a2[:, half:]
    zero = jnp.zeros((a1.shape[0], half), F32)

    def swap(v):
        return pltpu.roll(v, S5_STATE, 1)

    def step(t, carry):
        rf, rfs, rb, rbs = carry
        cb = jnp.where(t < n_ctx_chunks, n_ctx_chunks - 1 - t, n_chunks - 1 - (t - n_ctx_chunks))
        z_ref[t, :, 0:half] = rf
        z_ref[cb, :, half:2 * half] = rb
        hf = h_ref[t, :, 0:half]
        hb = h_ref[cb, :, half:2 * half]
        nrf = a1f * rf + a2f * rfs + hf
        nrfs = a1f * rfs - a2f * rf + swap(hf)
        nrb = a1b * rb + a2b * rbs + hb
        nrbs = a1b * rbs - a2b * rb + swap(hb)
        return nrf, nrfs, nrb, nrbs

    lax.fori_loop(0, n_chunks, step, (zero, zero, zero, zero), unroll=8 if n_chunks % 8 == 0 else 1)


def _s5_mixer_scan(u, tables, n_ctx):
    m_tot, b_mat, c_mat, a1, a2 = tables
    t_rows, width = u.shape
    tc = S5_TC
    n_grp = width // S5_GROUP
    n_chunks = t_rows // tc
    kdim = tc * S5_GROUP
    sdim = 4 * S5_STATE
    gb = S5_LANE_GROUPS
    gs = SUBLANES
    n_split = 2
    half_chunks = n_chunks // n_split
    cc = _pick(half_chunks, (104, 80, 40, 8))
    u_spec = pl.BlockSpec((t_rows // n_split, LANES), lambda j, r: (r, j))
    tab_spec = lambda a, b: pl.BlockSpec((gb, a, b), lambda j, r: (j, 0, 0))
    st_spec = pl.BlockSpec((half_chunks, gb, sdim), lambda j, r: (r, j, 0))
    grid = (width // LANES, n_split)
    ug_spec = pl.BlockSpec((gb, half_chunks, kdim), lambda j, r: (j, r, 0))
    hin, ug = pl.pallas_call(
        functools.partial(_s5_in_kernel, cc=cc),
        grid=grid,
        in_specs=[u_spec, tab_spec(kdim, sdim)],
        out_specs=[st_spec, ug_spec],
        out_shape=[jax.ShapeDtypeStruct((n_chunks, n_grp, sdim), F32),
                   jax.ShapeDtypeStruct((n_grp, n_chunks, kdim), F32)],
        compiler_params=_cparams("parallel", "arbitrary"),
        name="s5_chunk_inputs",
    )(u, b_mat)
    z = pl.pallas_call(
        functools.partial(_s5_scan_kernel, n_chunks=n_chunks, n_ctx_chunks=n_ctx // tc),
        grid=(n_grp // gs,),
        in_specs=[pl.BlockSpec((n_chunks, gs, sdim), lambda g: (0, g, 0)),
                  pl.BlockSpec((gs, sdim), lambda g: (g, 0)),
                  pl.BlockSpec((gs, sdim), lambda g: (g, 0))],
        out_specs=pl.BlockSpec((n_chunks, gs, sdim), lambda g: (0, g, 0)),
        out_shape=jax.ShapeDtypeStruct((n_chunks, n_grp, sdim), F32),
        compiler_params=_cparams("parallel"),
        name="s5_chunk_scan",
    )(hin, a1, a2)
    return pl.pallas_call(
        functools.partial(_s5_out_kernel, cc=cc),
        grid=grid,
        in_specs=[ug_spec, st_spec, tab_spec(kdim, kdim), tab_spec(sdim, kdim)],
        out_specs=u_spec,
        out_shape=jax.ShapeDtypeStruct((t_rows, width), F32),
        compiler_params=_cparams("parallel", "arbitrary"),
        name="s5_chunk_outputs",
    )(ug, z, m_tot, c_mat)


def _lambda_init(layer):
    return 0.8 - 0.6 * math.exp(-0.3 * layer)


def kernel(x, c, ctx, c_ctx, w_mod, b_mod, norm1_g, norm2_g, final_g, attn_w_qkv, attn_w_o, lambda_q1, lambda_k1, lambda_q2, lambda_k2, subln_g, sink_logit, s5_w_in, s5_a_re, s5_a_im, s5_log_step, s5_b_re, s5_b_im, s5_c_re, s5_c_im, s5_d, s5_w_glu, router_w, router_b, expert_w_gu, expert_b_gu, expert_w_down, expert_b_down):
    bsz, seq, d = x.shape
    assert bsz == 1, "single-sequence block"
    n_ctx = ctx.shape[1]
    depth = w_mod.shape[0]
    n_exp = router_w.shape[2]
    assert n_ctx % 256 == 0 and seq % 256 == 0

    xj = jnp.concatenate([ctx[0], x[0]], axis=0)
    cs = jnp.zeros((SUBLANES, d), F32).at[0].set(c_ctx).at[1].set(c[0])
    b_mod3 = b_mod[:, None, :]
    b_gu4 = expert_b_gu[:, :, None, :]
    b_dn4 = expert_b_down[:, :, None, :]
    wr_pad = jnp.pad(router_w, ((0, 0), (0, 0), (0, LANES - n_exp)))
    wr_hi = wr_pad.astype(BF16)
    wr_pad = jnp.stack([wr_hi, (wr_pad - wr_hi.astype(F32)).astype(BF16)], axis=1)
    br_pad = jnp.pad(router_b, ((0, 0), (0, LANES - n_exp)), constant_values=NEG_INF)[:, None, :]
    final_row = final_g[None, :]

    for i in range(depth):
        last = i == depth - 1
        j = i // 2
        mod = _adaln_mod(cs, w_mod, b_mod3, i)
        if i % 2 == 0:
            sizes = (A_HEADS * 2 * A_QK_DIM, A_HEADS * 2 * A_QK_DIM, A_HEADS * A_V_DIM,
                     B_Q_HEADS * B_HEAD_DIM, B_KV_HEADS * B_HEAD_DIM, B_KV_HEADS * B_HEAD_DIM)
            offs = np.concatenate([[0], np.cumsum(sizes)])
            types = [0, 1, ROPE_NONE, 2, 3, ROPE_NONE]
            ttype = jnp.asarray(np.concatenate([np.full(s // 256, t) for s, t in zip(sizes, types)]), jnp.int32)
            cos_t, sin_t = _rope_tables(n_ctx, seq)
            qkv = _norm_mod_matmul(xj, norm1_g[i][None, :], mod, 0, 1, attn_w_qkv[j].astype(BF16), n_ctx, BF16,
                                   rope=(ttype, cos_t, sin_t))
            qt = qkv[:, offs[0]:offs[1]].T
            ka = qkv[:, offs[1]:offs[2]]
            vt = qkv[:, offs[2]:offs[3]].T.reshape(A_HEADS, A_V_DIM, -1)
            vt = jnp.concatenate([vt, jnp.ones((A_HEADS, V_AUG - A_V_DIM, vt.shape[2]), BF16)], axis=1)
            vt = vt.reshape(A_HEADS * V_AUG, -1)
            f32 = F32
            li = _lambda_init(i)
            lam = (jnp.exp(jnp.sum(lambda_q1[j].astype(f32) * lambda_k1[j].astype(f32)))
                   - jnp.exp(jnp.sum(lambda_q2[j].astype(f32) * lambda_k2[j].astype(f32))) + li)
            attn_args = (lam.reshape(1), subln_g[j][:, None], 1.0 - li)
            ka = ka.astype(F8)
            ya = jnp.concatenate([_diff_attention(qt[:, :n_ctx], ka[:n_ctx], vt[:, :n_ctx], *attn_args),
                                  _diff_attention(qt[:, n_ctx:], ka, vt, *attn_args)], axis=0)
            yb = _window_attention(qkv, sink_logit[j], n_ctx, int(offs[3]) // LANES, int(offs[4]) // LANES,
                                   int(offs[5]) // LANES)
            w_o = attn_w_o[j].astype(BF16)
            na = A_HEADS * A_V_DIM
            xj, h2, logits = _mixer_out(_attn_out_kernel, [ya, yb], xj, [w_o[:na], w_o[na:]], mod, (2, 3, 4),
                                        norm2_g[i][None, :], wr_pad[i], br_pad[i], n_ctx, "attn_out_router")
        else:
            u = _norm_mod_matmul(xj, norm1_g[i][None, :], mod, 0, 1, s5_w_in[j].astype(BF16), n_ctx, F32)
            tables = _s5_tables(s5_a_re[j], s5_a_im[j], s5_log_step[j], s5_b_re[j], s5_b_im[j],
                                s5_c_re[j], s5_c_im[j], s5_d[j])
            y = _s5_mixer_scan(u, tables, n_ctx)
            w_glu = s5_w_glu[j].astype(BF16)
            xj, h2, logits = _mixer_out(_glu_out_kernel, [y], xj, [w_glu[:, :d], w_glu[:, d:]], mod, (2, 3, 4),
                                        norm2_g[i][None, :], wr_pad[i], br_pad[i], n_ctx, "glu_out_router")
        xj = _moe(h2, logits, xj, mod, 5, expert_w_gu, b_gu4, expert_w_down, b_dn4, i, n_ctx, final_row, last)
    return xj[None]
```

```python
import functools
import math

import jax
import jax.numpy as jnp
import numpy as np
from jax import lax
from jax.experimental import pallas as pl
from jax.experimental.pallas import tpu as pltpu
from jax.experimental.pallas import tpu_sc as plsc

F32 = jnp.float32
BF16 = jnp.bfloat16
F8 = jnp.float8_e4m3fn
HIGHEST = lax.Precision.HIGHEST

V7X_VMEM_BYTES = 64 * 1024 * 1024
VMEM_LIMIT = V7X_VMEM_BYTES - 8 * 1024 * 1024
LANES = 128
SUBLANES = 8

GRID_W = 64
N_MOD = 6
EPS = 1e-6
NEG_INF = -1e30
ROPE_THETA = 10000.0
A_HEADS = 8
A_QK_DIM = 64
A_V_DIM = 128
B_Q_HEADS = 8
B_KV_HEADS = 2
B_GROUP = 4
B_HEAD_DIM = 128
WINDOW = 128
S5_GROUP = 16
S5_STATE = 64
S5_TC = 16
TOP_K = 4
SWIGLU_LIMIT = 7.0
SWIGLU_ALPHA = 1.702
EXPERT_BLOCK = 512
DOWN_BLOCK = 256
GATE_UP_TILE = 1024
FP8_MAX = 448.0
FP8_TARGET = 256.0
LOG2E = 1.4426950408889634


def _pick(n, cands):
    for c in cands:
        if n % c == 0:
            return c
    raise ValueError(f"no tile for {n} in {cands}")


def _cparams(*sem):
    return pltpu.CompilerParams(dimension_semantics=sem, vmem_limit_bytes=VMEM_LIMIT)


def _mod_kernel(c_ref, w_ref, b_ref, o_ref):
    cv = c_ref[...]
    s = cv * jax.nn.sigmoid(cv)
    o_ref[...] = jnp.dot(s, w_ref[...], preferred_element_type=F32, precision=HIGHEST) + b_ref[...]


def _adaln_mod(cs, w_mod, b_mod, layer):
    d, n = w_mod.shape[1], w_mod.shape[2]
    tn = _pick(n, (1024, 512, 256, 128))
    return pl.pallas_call(
        _mod_kernel,
        grid=(n // tn,),
        in_specs=[pl.BlockSpec((SUBLANES, d), lambda j: (0, 0)),
                  pl.BlockSpec((None, d, tn), lambda j: (layer, 0, j)),
                  pl.BlockSpec((None, 1, tn), lambda j: (layer, 0, j))],
        out_specs=pl.BlockSpec((SUBLANES, tn), lambda j: (0, j)),
        out_shape=jax.ShapeDtypeStruct((SUBLANES, n), F32),
        compiler_params=_cparams("parallel"),
        name="adaln_mod",
    )(cs, w_mod, b_mod)


def _norm_mod(x, g, sh2, sc2, row0, n_ctx):
    ms = jnp.mean(x * x, axis=-1, keepdims=True)
    y = x * lax.rsqrt(ms + EPS) * g
    row = row0 + lax.broadcasted_iota(jnp.int32, (x.shape[0], 1), 0)
    is_ctx = row < n_ctx
    sc = jnp.where(is_ctx, sc2[0:1, :], sc2[1:2, :])
    sh = jnp.where(is_ctx, sh2[0:1, :], sh2[1:2, :])
    return y * (1.0 + sc) + sh


ROPE_NONE = 4


def _rope_store(acc, cos_ref, sin_ref, o_ref, shift):
    cosv, sinv = cos_ref[...], sin_ref[...]
    for cgrp in range(acc.shape[1] // LANES):
        a = acc[:, cgrp * LANES:(cgrp + 1) * LANES]
        lane = lax.broadcasted_iota(jnp.int32, a.shape, 1)
        in_second = (lane & (2 * shift - 1)) >= shift
        sw = jnp.where(in_second, pltpu.roll(a, shift, 1), pltpu.roll(a, LANES - shift, 1))
        o_ref[:, cgrp * LANES:(cgrp + 1) * LANES] = (a * cosv + sw * sinv).astype(o_ref.dtype)


NORM_ROWS = 256


def _norm_mod_to_scratch(x_ref, g_ref, sh_ref, sc_ref, h_scr, row0, n_ctx):
    def body(r, carry):
        off = pl.multiple_of(r * NORM_ROWS, NORM_ROWS)
        h = _norm_mod(x_ref[pl.ds(off, NORM_ROWS), :], g_ref[...], sh_ref[...], sc_ref[...], row0 + off, n_ctx)
        h_scr[pl.ds(off, NORM_ROWS), :] = h.astype(h_scr.dtype)
        return carry

    lax.fori_loop(0, x_ref.shape[0] // NORM_ROWS, body, 0)


def _nmm_rope_kernel(tt_ref, x_ref, g_ref, sh_ref, sc_ref, w_ref, cos_ref, sin_ref, o_ref, h_scr,
                     *, n_ctx, tm):
    i = pl.program_id(0)
    j = pl.program_id(1)

    @pl.when(j == 0)
    def _():
        _norm_mod_to_scratch(x_ref, g_ref, sh_ref, sc_ref, h_scr, i * tm, n_ctx)

    acc = jnp.dot(h_scr[...], w_ref[...], preferred_element_type=F32)
    t = tt_ref[j]

    @pl.when(t == ROPE_NONE)
    def _():
        o_ref[...] = acc.astype(o_ref.dtype)

    @pl.when(t < 2)
    def _():
        _rope_store(acc, cos_ref, sin_ref, o_ref, A_QK_DIM // 4)

    @pl.when(jnp.logical_and(t >= 2, t < ROPE_NONE))
    def _():
        _rope_store(acc, cos_ref, sin_ref, o_ref, B_HEAD_DIM // 4)


def _nmm_plain_kernel(x_ref, g_ref, sh_ref, sc_ref, w_ref, o_ref, h_scr, *, n_ctx, tm):
    i = pl.program_id(0)
    j = pl.program_id(1)

    @pl.when(j == 0)
    def _():
        _norm_mod_to_scratch(x_ref, g_ref, sh_ref, sc_ref, h_scr, i * tm, n_ctx)

    o_ref[...] = jnp.dot(h_scr[...], w_ref[...], preferred_element_type=F32).astype(o_ref.dtype)


def _norm_mod_matmul(x, g, mod, sh_col, sc_col, w, n_ctx, out_dtype, rope=None):
    t_rows, d = x.shape
    n = w.shape[1]
    tm = _pick(t_rows, (1280, 1024, 512, 256, 128))
    tn = 256
    grid = (t_rows // tm, n // tn)
    kern_kw = dict(n_ctx=n_ctx, tm=tm)
    scratch = [pltpu.VMEM((tm, d), BF16)]
    out_shape = jax.ShapeDtypeStruct((t_rows, n), out_dtype)
    if rope is None:
        return pl.pallas_call(
            functools.partial(_nmm_plain_kernel, **kern_kw),
            grid=grid,
            in_specs=[pl.BlockSpec((tm, d), lambda i, j: (i, 0)),
                      pl.BlockSpec((1, d), lambda i, j: (0, 0)),
                      pl.BlockSpec((SUBLANES, d), lambda i, j: (0, sh_col)),
                      pl.BlockSpec((SUBLANES, d), lambda i, j: (0, sc_col)),
                      pl.BlockSpec((d, tn), lambda i, j: (0, j))],
            out_specs=pl.BlockSpec((tm, tn), lambda i, j: (i, j)),
            out_shape=out_shape,
            scratch_shapes=scratch,
            compiler_params=_cparams("parallel", "arbitrary"),
            name="norm_mod_matmul",
        )(x, g, mod, mod, w)
    ttype, cos_t, sin_t = rope
    return pl.pallas_call(
        functools.partial(_nmm_rope_kernel, **kern_kw),
        grid_spec=pltpu.PrefetchScalarGridSpec(
            num_scalar_prefetch=1,
            grid=grid,
            in_specs=[pl.BlockSpec((tm, d), lambda i, j, tt: (i, 0)),
                      pl.BlockSpec((1, d), lambda i, j, tt: (0, 0)),
                      pl.BlockSpec((SUBLANES, d), lambda i, j, tt: (0, sh_col)),
                      pl.BlockSpec((SUBLANES, d), lambda i, j, tt: (0, sc_col)),
                      pl.BlockSpec((d, tn), lambda i, j, tt: (0, j)),
                      pl.BlockSpec((None, tm, LANES), lambda i, j, tt: (tt[j], i, 0)),
                      pl.BlockSpec((None, tm, LANES), lambda i, j, tt: (tt[j], i, 0))],
            out_specs=pl.BlockSpec((tm, tn), lambda i, j, tt: (i, j)),
            scratch_shapes=scratch),
        out_shape=out_shape,
        compiler_params=_cparams("parallel", "arbitrary"),
        name="norm_mod_qkv_rope",
    )(ttype, x, g, mod, mod, w, cos_t, sin_t)


def _rope_tables(n_ctx, seq):
    pos = jnp.arange(seq)
    rows = (pos // GRID_W).astype(F32)
    cols = (pos % GRID_W).astype(F32)

    def tab(dim):
        quarter = dim // 4
        freqs = ROPE_THETA ** (-jnp.arange(quarter, dtype=F32) / quarter)
        ar, ac = rows[:, None] * freqs, cols[:, None] * freqs
        cosv = jnp.concatenate([jnp.cos(ar), jnp.cos(ar), jnp.cos(ac), jnp.cos(ac)], axis=1)
        sinv = jnp.concatenate([-jnp.sin(ar), jnp.sin(ar), -jnp.sin(ac), jnp.sin(ac)], axis=1)
        reps = LANES // dim
        cosv, sinv = jnp.tile(cosv, (1, reps)), jnp.tile(sinv, (1, reps))
        cosv = jnp.concatenate([jnp.ones((n_ctx, LANES), F32), cosv], axis=0)
        sinv = jnp.concatenate([jnp.zeros((n_ctx, LANES), F32), sinv], axis=0)
        return cosv, sinv

    ca, sa = tab(A_QK_DIM)
    cb, sb = tab(B_HEAD_DIM)
    qa = (A_QK_DIM ** -0.5) * LOG2E
    qb = B_HEAD_DIM ** -0.5
    one, zero = jnp.ones_like(ca), jnp.zeros_like(ca)
    cos_t = jnp.stack([ca * qa, ca, cb * qb, cb, one])
    sin_t = jnp.stack([sa * qa, sa, sb * qb, sb, zero])
    return cos_t, sin_t


V_AUG = A_V_DIM + 16


def _flash_kernel(lam_ref, qt_ref, k_ref, vt_ref, g_ref, o_ref, s_a, s_b, acc,
                  *, tq, tk, t_rows, out_scale):
    qt = qt_ref[...]
    row = lax.broadcasted_iota(jnp.int32, qt.shape, 0)
    zero = jnp.zeros_like(qt)
    qt = jnp.clip(qt, -FP8_MAX, FP8_MAX)
    qm = (jnp.where(row < A_QK_DIM, qt, zero).astype(F8), jnp.where(row >= A_QK_DIM, qt, zero).astype(F8))

    def start(tile, size):
        return tile * size if isinstance(tile, int) else pl.multiple_of(tile * size, size)

    def scores(tile, size, dst):
        kt = k_ref[pl.ds(start(tile, size), size), :]
        for m in range(2):
            dst[m, 0:size, :] = jnp.dot(kt, qm[m], preferred_element_type=F32).astype(BF16)

    def soft_pv(tile, size, src, ms):
        vt = vt_ref[:, pl.ds(start(tile, size), size)].astype(F8)
        new_ms = []
        for m in range(2):
            s = src[m, 0:size, :]
            mx = jnp.maximum(ms[m], jnp.max(s, axis=0, keepdims=True).astype(F32))
            alpha = jnp.exp2(ms[m] - mx)
            p = jnp.exp2(s - mx.astype(BF16)).astype(F8)
            acc[m] = alpha * acc[m] + jnp.dot(vt, p, preferred_element_type=F32)
            new_ms.append(mx)
        return tuple(new_ms)

    def finish(ms):
        o1 = acc[0, 0:A_V_DIM, :] / acc[0, A_V_DIM:A_V_DIM + 1, :]
        o2 = acc[1, 0:A_V_DIM, :] / acc[1, A_V_DIM:A_V_DIM + 1, :]
        o = (o1 - lam_ref[0] * o2) * lam_ref[1 + pl.program_id(0)]
        var = jnp.mean(o * o, axis=0, keepdims=True)
        o = o * lax.rsqrt(var + EPS) * (g_ref[...] * out_scale)
        o_ref[...] = o.T.astype(o_ref.dtype)

    m0 = jnp.full((1, tq), NEG_INF, F32)
    nk = t_rows // tk
    n_pairs = (nk - 1) // 2
    acc[...] = jnp.zeros_like(acc)
    scores(0, tk, s_a)

    def pair(pp, carry):
        scores(2 * pp + 1, tk, s_b)
        carry = soft_pv(2 * pp, tk, s_a, carry)
        scores(2 * pp + 2, tk, s_a)
        return soft_pv(2 * pp + 1, tk, s_b, carry)

    carry = lax.fori_loop(0, n_pairs, pair, (m0, m0))
    done = 2 * n_pairs
    if nk - done == 2:
        scores(done + 1, tk, s_b)
        carry = soft_pv(done, tk, s_a, carry)
        carry = soft_pv(done + 1, tk, s_b, carry)
    else:
        carry = soft_pv(done, tk, s_a, carry)
    finish(carry)


def _diff_attention(qt, k, vt, lam, subln_col, out_scale):
    t_rows = k.shape[0]
    n_q = qt.shape[1]
    tq = _pick(n_q, (2048, 1024, 512, 256))
    tk = _pick(t_rows, (1664, 1280, 1024, 512, 256))
    return pl.pallas_call(
        functools.partial(_flash_kernel, tq=tq, tk=tk, t_rows=t_rows, out_scale=out_scale),
        grid=(A_HEADS, n_q // tq),
        in_specs=[pl.BlockSpec(memory_space=pltpu.SMEM),
                  pl.BlockSpec((LANES, tq), lambda h, i: (h, i)),
                  pl.BlockSpec((t_rows, LANES), lambda h, i: (0, h)),
                  pl.BlockSpec((V_AUG, t_rows), lambda h, i: (h, 0)),
                  pl.BlockSpec((LANES, 1), lambda h, i: (0, 0))],
        out_specs=pl.BlockSpec((tq, LANES), lambda h, i: (i, h)),
        out_shape=jax.ShapeDtypeStruct((n_q, A_HEADS * A_V_DIM), BF16),
        scratch_shapes=[pltpu.VMEM((2, tk, tq), BF16), pltpu.VMEM((2, tk, tq), BF16),
                        pltpu.VMEM((2, V_AUG, tq), F32)],
        compiler_params=_cparams("parallel", "arbitrary"),
        name="diff_attention",
    )(lam, qt, k, vt, subln_col)


def _window_kernel(sink_ref, q_ref, kp_ref, ko_ref, kn_ref, vp_ref, vo_ref, vn_ref, kc_ref, vc_ref, o_ref,
                   *, nb, nb_ctx):
    n = pl.program_id(0)
    qi = lax.broadcasted_iota(jnp.int32, (WINDOW, WINDOW), 0)
    kk = lax.broadcasted_iota(jnp.int32, (WINDOW, WINDOW), 1)
    own_ok = n >= nb_ctx
    prev_ok = n >= nb_ctx + 1
    next_ok = jnp.logical_and(own_ok, n <= nb - 2)
    m_prev = jnp.logical_and(kk >= qi, prev_ok)
    m_own = jnp.logical_and(kk >= 0, own_ok)
    m_next = jnp.logical_and(kk <= qi, next_ok)
    dn = (((1,), (1,)), ((), ()))
    for h in range(B_Q_HEADS):
        kv = slice((h // B_GROUP) * B_HEAD_DIM, (h // B_GROUP + 1) * B_HEAD_DIM)
        q = q_ref[:, h * B_HEAD_DIM:(h + 1) * B_HEAD_DIM]
        s_p = jnp.where(m_prev, lax.dot_general(q, kp_ref[:, kv], dn, preferred_element_type=F32), NEG_INF)
        s_o = jnp.where(m_own, lax.dot_general(q, ko_ref[:, kv], dn, preferred_element_type=F32), NEG_INF)
        s_n = jnp.where(m_next, lax.dot_general(q, kn_ref[:, kv], dn, preferred_element_type=F32), NEG_INF)
        s_c = lax.dot_general(q, kc_ref[:, kv], dn, preferred_element_type=F32)
        sink = sink_ref[h]
        mx = jnp.maximum(jnp.maximum(jnp.max(s_p, axis=1, keepdims=True), jnp.max(s_o, axis=1, keepdims=True)),
                         jnp.maximum(jnp.max(s_n, axis=1, keepdims=True), jnp.max(s_c, axis=1, keepdims=True)))
        mx = jnp.maximum(mx, sink)
        p_p, p_o, p_n, p_c = jnp.exp(s_p - mx), jnp.exp(s_o - mx), jnp.exp(s_n - mx), jnp.exp(s_c - mx)
        den = (jnp.sum(p_p, axis=1, keepdims=True) + jnp.sum(p_o, axis=1, keepdims=True)
               + jnp.sum(p_n, axis=1, keepdims=True) + jnp.sum(p_c, axis=1, keepdims=True)
               + jnp.exp(sink - mx))
        o = (jnp.dot(p_p.astype(BF16), vp_ref[:, kv], preferred_element_type=F32)
             + jnp.dot(p_o.astype(BF16), vo_ref[:, kv], preferred_element_type=F32)
             + jnp.dot(p_n.astype(BF16), vn_ref[:, kv], preferred_element_type=F32)
             + jnp.dot(p_c.astype(BF16), vc_ref[:, kv], preferred_element_type=F32))
        o_ref[:, h * B_HEAD_DIM:(h + 1) * B_HEAD_DIM] = (o / den).astype(o_ref.dtype)


def _window_attention(qkv, sink, n_ctx, col_q, col_k, col_v):
    t_rows = qkv.shape[0]
    nb = t_rows // WINDOW
    nb_ctx = n_ctx // WINDOW
    q_w = B_Q_HEADS * B_HEAD_DIM
    kv_w = B_KV_HEADS * B_HEAD_DIM
    assert (col_q * LANES) % q_w == 0 and (col_k * LANES) % kv_w == 0 and (col_v * LANES) % kv_w == 0

    def kv_spec(col, shift):
        def imap(n):
            return (jnp.clip(n + shift, 0, nb - 1), col * LANES // kv_w)
        return pl.BlockSpec((WINDOW, kv_w), imap)

    return pl.pallas_call(
        functools.partial(_window_kernel, nb=nb, nb_ctx=nb_ctx),
        grid=(nb,),
        in_specs=[pl.BlockSpec(memory_space=pltpu.SMEM),
                  pl.BlockSpec((WINDOW, q_w), lambda n: (n, col_q * LANES // q_w)),
                  kv_spec(col_k, -1), kv_spec(col_k, 0), kv_spec(col_k, 1),
                  kv_spec(col_v, -1), kv_spec(col_v, 0), kv_spec(col_v, 1),
                  pl.BlockSpec((n_ctx, kv_w), lambda n: (0, col_k * LANES // kv_w)),
                  pl.BlockSpec((n_ctx, kv_w), lambda n: (0, col_v * LANES // kv_w))],
        out_specs=pl.BlockSpec((WINDOW, q_w), lambda n: (n, 0)),
        out_shape=jax.ShapeDtypeStruct((t_rows, q_w), BF16),
        compiler_params=_cparams("parallel"),
        name="window_attention",
    )(sink, qkv, qkv, qkv, qkv, qkv, qkv, qkv, qkv, qkv)


ROW_PIECES = 8


def _store_packed_rows(ref, v):
    m, half = v.shape[0], v.shape[1] // 2
    lo = pltpu.bitcast(v[:, :half].astype(BF16).astype(F32), jnp.uint32) >> 16
    hi = pltpu.bitcast(v[:, half:].astype(BF16).astype(F32), jnp.uint32) & jnp.uint32(0xFFFF0000)
    w = lo | hi
    for j in range(ROW_PIECES):
        ref[pl.ds(j, m, stride=ROW_PIECES), :] = w[:, j * LANES:(j + 1) * LANES]


def _load_packed_rows(ref, m, r0=0):
    w = jnp.concatenate([ref[pl.ds(r0 * ROW_PIECES + j, m, stride=ROW_PIECES), :] for j in range(ROW_PIECES)],
                        axis=1)
    lo = pltpu.bitcast(w << 16, F32)
    hi = pltpu.bitcast(w & jnp.uint32(0xFFFF0000), F32)
    return lo, hi


def _post_mixer(y, x, g1_ref, g_ref, sh_ref, sc_ref, wr_ref, br_ref, xo_ref, h_ref, lg_ref, row0, n_ctx):
    row = row0 + lax.broadcasted_iota(jnp.int32, (x.shape[0], 1), 0)
    g1 = jnp.where(row < n_ctx, g1_ref[0:1, :], g1_ref[1:2, :])
    xn = x + g1 * y
    xo_ref[...] = xn
    h = _norm_mod(xn, g_ref[...], sh_ref[...], sc_ref[...], row0, n_ctx)
    _store_packed_rows(h_ref, h)
    h_hi = h.astype(BF16)
    h_lo = (h - h_hi.astype(F32)).astype(BF16)
    lg_ref[...] = (jnp.dot(h_hi, wr_ref[0], preferred_element_type=F32)
                   + jnp.dot(h_lo, wr_ref[0], preferred_element_type=F32)
                   + jnp.dot(h_hi, wr_ref[1], preferred_element_type=F32) + br_ref[...])


def _attn_out_kernel(ya_ref, yb_ref, x_ref, woa_ref, wob_ref, g1_ref, g_ref, sh_ref, sc_ref, wr_ref, br_ref,
                     xo_ref, h_ref, lg_ref, *, n_ctx, tm):
    y = (jnp.dot(ya_ref[...], woa_ref[...], preferred_element_type=F32)
         + jnp.dot(yb_ref[...], wob_ref[...], preferred_element_type=F32))
    _post_mixer(y, x_ref[...], g1_ref, g_ref, sh_ref, sc_ref, wr_ref, br_ref, xo_ref, h_ref, lg_ref,
                pl.program_id(0) * tm, n_ctx)


def _glu_out_kernel(y_ref, x_ref, wv_ref, wg_ref, g1_ref, g_ref, sh_ref, sc_ref, wr_ref, br_ref,
                    xo_ref, h_ref, lg_ref, *, n_ctx, tm):
    a = jax.nn.gelu(y_ref[...], approximate=True).astype(BF16)
    val = jnp.dot(a, wv_ref[...], preferred_element_type=F32)
    gate = jnp.dot(a, wg_ref[...], preferred_element_type=F32)
    _post_mixer(val * jax.nn.sigmoid(gate), x_ref[...], g1_ref, g_ref, sh_ref, sc_ref, wr_ref, br_ref,
                xo_ref, h_ref, lg_ref, pl.program_id(0) * tm, n_ctx)


def _mixer_out(kernel_fn, acts, x, weights, mod, cols, g2row, wr, br, n_ctx, name):
    t_rows, d = x.shape
    tm = 256
    row = lambda i: (i, 0)
    const = lambda i: (0, 0)
    in_specs = ([pl.BlockSpec((tm, a.shape[1]), row) for a in acts]
                + [pl.BlockSpec((tm, d), row)]
                + [pl.BlockSpec(w.shape, const) for w in weights]
                + [pl.BlockSpec((SUBLANES, d), lambda i, c=c: (0, c)) for c in cols[:1]]
                + [pl.BlockSpec((1, d), const)]
                + [pl.BlockSpec((SUBLANES, d), lambda i, c=c: (0, c)) for c in cols[1:]]
                + [pl.BlockSpec(wr.shape, lambda i: (0, 0, 0)), pl.BlockSpec(br.shape, const)])
    return pl.pallas_call(
        functools.partial(kernel_fn, n_ctx=n_ctx, tm=tm),
        grid=(t_rows // tm,),
        in_specs=in_specs,
        out_specs=[pl.BlockSpec((tm, d), row), pl.BlockSpec((tm * ROW_PIECES, LANES), row),
                   pl.BlockSpec((tm, LANES), row)],
        out_shape=[jax.ShapeDtypeStruct((t_rows, d), F32),
                   jax.ShapeDtypeStruct((t_rows * ROW_PIECES, LANES), jnp.uint32),
                   jax.ShapeDtypeStruct((t_rows, LANES), F32)],
        compiler_params=_cparams("parallel"),
        name=name,
    )(*acts, x, *weights, mod, g2row, mod, mod, wr, br)


def _router_kernel(lg_ref, idx_ref, gate_ref, cnt_ref, *, n_exp):
    @pl.when(pl.program_id(0) == 0)
    def _():
        cnt_ref[...] = jnp.zeros_like(cnt_ref)

    lt = lg_ref[...].T[0:n_exp, :]
    eid = lax.broadcasted_iota(jnp.int32, lt.shape, 0).astype(F32)
    vals, idxs = [], []
    hist = jnp.zeros(lt.shape, F32)
    for _ in range(TOP_K):
        mv = jnp.max(lt, axis=0, keepdims=True)
        ix = jnp.min(jnp.where(lt == mv, eid, float(n_exp)), axis=0, keepdims=True)
        sel = eid == ix
        hist = hist + sel.astype(F32)
        lt = jnp.where(sel, -jnp.inf, lt)
        vals.append(mv)
        idxs.append(ix)
    es = [jnp.exp(v - vals[0]) for v in vals]
    den = es[0] + es[1] + es[2] + es[3]
    pad_f = jnp.zeros((SUBLANES - TOP_K, lt.shape[1]), F32)
    idx_ref[...] = jnp.concatenate(idxs + [pad_f], axis=0).astype(jnp.int32)
    gate_ref[...] = jnp.concatenate([e / den for e in es] + [pad_f], axis=0)
    cnt_ref[...] += jnp.sum(hist, axis=1, keepdims=True)


def _router(logits, n_exp):
    t_rows = logits.shape[0]
    tm = 256
    return pl.pallas_call(
        functools.partial(_router_kernel, n_exp=n_exp),
        grid=(t_rows // tm,),
        in_specs=[pl.BlockSpec((tm, LANES), lambda i: (i, 0))],
        out_specs=[pl.BlockSpec((SUBLANES, tm), lambda i: (0, i)),
                   pl.BlockSpec((SUBLANES, tm), lambda i: (0, i)),
                   pl.BlockSpec((n_exp, 1), lambda i: (0, 0))],
        out_shape=[jax.ShapeDtypeStruct((SUBLANES, t_rows), jnp.int32),
                   jax.ShapeDtypeStruct((SUBLANES, t_rows), F32),
                   jax.ShapeDtypeStruct((n_exp, 1), F32)],
        compiler_params=_cparams("arbitrary"),
        name="router_topk",
    )(logits)


def _dest_kernel(idx_ref, start_ref, dest_ref, carry, *, n_exp, tm):
    @pl.when(pl.program_id(0) == 0)
    def _():
        carry[...] = start_ref[...]

    eid = lax.broadcasted_iota(jnp.int32, (n_exp, tm), 0)
    idx = idx_ref[...]
    sels = [eid == idx[k:k + 1, :] for k in range(TOP_K)]
    total = sels[0].astype(F32) + sels[1].astype(F32) + sels[2].astype(F32) + sels[3].astype(F32)
    rr = lax.broadcasted_iota(jnp.int32, (tm, tm), 0)
    cc = lax.broadcasted_iota(jnp.int32, (tm, tm), 1)
    upper = jnp.where(rr < cc, 1.0, 0.0).astype(BF16)
    before = jnp.dot(total.astype(BF16), upper, preferred_element_type=F32) + carry[...]
    rows = [jnp.sum(jnp.where(sels[k], before, 0.0), axis=0, keepdims=True) for k in range(TOP_K)]
    pad = jnp.zeros((SUBLANES - TOP_K, tm), F32)
    dest_ref[...] = jnp.concatenate(rows + [pad], axis=0).astype(jnp.int32)
    carry[...] += jnp.sum(total, axis=1, keepdims=True)


def _dest_rows(idx, starts, n_exp):
    t_rows = idx.shape[1]
    tm = 256
    return pl.pallas_call(
        functools.partial(_dest_kernel, n_exp=n_exp, tm=tm),
        grid=(t_rows // tm,),
        in_specs=[pl.BlockSpec((SUBLANES, tm), lambda i: (0, i)),
                  pl.BlockSpec((n_exp, 1), lambda i: (0, 0))],
        out_specs=pl.BlockSpec((SUBLANES, tm), lambda i: (0, i)),
        out_shape=jax.ShapeDtypeStruct((SUBLANES, t_rows), jnp.int32),
        scratch_shapes=[pltpu.VMEM((n_exp, 1), F32)],
        compiler_params=_cparams("arbitrary"),
        name="moe_dest_rows",
    )(idx, starts)


def _to_fp8(v):
    amax = jnp.max(jnp.abs(v), axis=(0, 1), keepdims=True)
    scale = FP8_TARGET / jnp.maximum(amax, 1e-30)
    return (v * scale).astype(F8), 1.0 / scale


def _expert_gu_kernel(be_ref, bv_ref, bn_ref, x_ref, wg_ref, wu_ref, bg_ref, bu_ref, o_ref,
                      wg_s, wu_s, sg_s, su_s, *, blk):
    b = pl.program_id(1)
    nvalid = bv_ref[b]

    @pl.when(bn_ref[b] == 1)
    def _():
        wg_s[...], inv_g = _to_fp8(wg_ref[...])
        wu_s[...], inv_u = _to_fp8(wu_ref[...])
        sg_s[...] = jnp.broadcast_to(inv_g, sg_s.shape)
        su_s[...] = jnp.broadcast_to(inv_u, su_s.shape)

    @pl.when(nvalid > 0)
    def _():
        rows = lax.broadcasted_iota(jnp.int32, (blk, 1), 0)
        lo, hi = _load_packed_rows(x_ref, blk)
        x = jnp.concatenate([lo, hi], axis=1)
        x = jnp.where(rows < nvalid, x, jnp.zeros_like(x))
        x, inv_x = _to_fp8(x)
        gate = jnp.dot(x, wg_s[...], preferred_element_type=F32) * (sg_s[...] * inv_x) + bg_ref[...]
        up = jnp.dot(x, wu_s[...], preferred_element_type=F32) * (su_s[...] * inv_x) + bu_ref[...]
        gate = jnp.minimum(gate, SWIGLU_LIMIT)
        up = jnp.clip(up, -SWIGLU_LIMIT, SWIGLU_LIMIT)
        act = (up + 1.0) * (gate * jax.nn.sigmoid(SWIGLU_ALPHA * gate))
        o_ref[...] = act.astype(o_ref.dtype)

    @pl.when(nvalid == 0)
    def _():
        o_ref[...] = jnp.zeros_like(o_ref)


def _expert_dn_kernel(be_ref, bv_ref, bn_ref, a_ref, wd_ref, bd_ref, o_ref, wd_s, sd_s, *, blk):
    b = pl.program_id(0)

    @pl.when(bn_ref[b] == 1)
    def _():
        wd_s[...], inv_d = _to_fp8(wd_ref[...])
        sd_s[...] = jnp.broadcast_to(inv_d, sd_s.shape)

    @pl.when(bv_ref[b] > 0)
    def _():
        y = jnp.dot(a_ref[...].astype(F8), wd_s[...], preferred_element_type=F32) * sd_s[...] + bd_ref[...]
        _store_packed_rows(o_ref, y)

    @pl.when(bv_ref[b] == 0)
    def _():
        o_ref[...] = jnp.zeros_like(o_ref)


def _expert_matmul(xs, blk_e, blk_valid, blk_new, w_gu, b_gu, w_dn, b_dn, layer):
    d, f_dim = w_dn.shape[3], w_dn.shape[2]
    n_rows = xs.shape[0] // ROW_PIECES
    blk = EXPERT_BLOCK
    tf = GATE_UP_TILE
    nf = f_dim // tf
    n_blocks = n_rows // blk
    packed_rows = pl.BlockSpec((blk * ROW_PIECES, LANES), lambda f, b, be, bv, bn: (b, 0))
    act = pl.pallas_call(
        functools.partial(_expert_gu_kernel, blk=blk),
        grid_spec=pltpu.PrefetchScalarGridSpec(
            num_scalar_prefetch=3,
            grid=(nf, n_blocks),
            in_specs=[packed_rows,
                      pl.BlockSpec((None, None, d, tf), lambda f, b, be, bv, bn: (layer, be[b], 0, f)),
                      pl.BlockSpec((None, None, d, tf), lambda f, b, be, bv, bn: (layer, be[b], 0, nf + f)),
                      pl.BlockSpec((None, None, 1, tf), lambda f, b, be, bv, bn: (layer, be[b], 0, f)),
                      pl.BlockSpec((None, None, 1, tf), lambda f, b, be, bv, bn: (layer, be[b], 0, nf + f))],
            out_specs=pl.BlockSpec((blk, tf), lambda f, b, be, bv, bn: (b, f)),
            scratch_shapes=[pltpu.VMEM((d, tf), F8), pltpu.VMEM((d, tf), F8),
                            pltpu.VMEM((1, tf), F32), pltpu.VMEM((1, tf), F32)]),
        out_shape=jax.ShapeDtypeStruct((n_rows, f_dim), BF16),
        compiler_params=_cparams("arbitrary", "arbitrary"),
        name="expert_gate_up",
    )(blk_e, blk_valid, blk_new, xs, w_gu, w_gu, b_gu, b_gu)
    assert d == 2 * ROW_PIECES * LANES
    sub = blk // DOWN_BLOCK
    part = jnp.arange(n_blocks * sub, dtype=jnp.int32) % sub
    dn_e = jnp.repeat(blk_e, sub)
    dn_valid = jnp.clip(jnp.repeat(blk_valid, sub) - part * DOWN_BLOCK, 0, DOWN_BLOCK)
    dn_new = jnp.where(part == 0, jnp.repeat(blk_new, sub), 0)
    return pl.pallas_call(
        functools.partial(_expert_dn_kernel, blk=DOWN_BLOCK),
        grid_spec=pltpu.PrefetchScalarGridSpec(
            num_scalar_prefetch=3,
            grid=(n_blocks * sub,),
            in_specs=[pl.BlockSpec((DOWN_BLOCK, f_dim), lambda b, be, bv, bn: (b, 0)),
                      pl.BlockSpec((None, None, f_dim, d), lambda b, be, bv, bn: (layer, be[b], 0, 0)),
                      pl.BlockSpec((None, None, 1, d), lambda b, be, bv, bn: (layer, be[b], 0, 0))],
            out_specs=pl.BlockSpec((DOWN_BLOCK * ROW_PIECES, LANES), lambda b, be, bv, bn: (b, 0)),
            scratch_shapes=[pltpu.VMEM((f_dim, d), F8), pltpu.VMEM((1, d), F32)]),
        out_shape=jax.ShapeDtypeStruct((n_rows * ROW_PIECES, LANES), jnp.uint32),
        compiler_params=_cparams("arbitrary"),
        name="expert_down",
    )(dn_e, dn_valid, dn_new, act, w_dn, b_dn)


def _combine_kernel(y0_ref, y1_ref, y2_ref, y3_ref, gate_ref, x_ref, g2_ref, fg_ref, o_ref,
                    *, n_ctx, tm, final, skip):
    gates = gate_ref[...]
    f_lo, f_hi = None, None
    for k, y_ref in enumerate((y0_ref, y1_ref, y2_ref, y3_ref)):
        lo, hi = _load_packed_rows(y_ref, tm)
        gk = gates[:, k:k + 1]
        f_lo = lo * gk if f_lo is None else f_lo + lo * gk
        f_hi = hi * gk if f_hi is None else f_hi + hi * gk
    f = jnp.concatenate([f_lo, f_hi], axis=1)
    row = (pl.program_id(0) + skip) * tm + lax.broadcasted_iota(jnp.int32, (tm, 1), 0)
    g2 = jnp.where(row < n_ctx, g2_ref[0:1, :], g2_ref[1:2, :])
    xn = x_ref[...] + g2 * f
    if final:
        ms = jnp.mean(xn * xn, axis=-1, keepdims=True)
        xn = xn * lax.rsqrt(ms + EPS) * fg_ref[...]
    o_ref[...] = xn


def _combine(yk, gates_t, x, mod, g2_col, final_g, n_ctx, final):
    t_rows, d = x.shape
    tm = 256
    nt = t_rows // tm
    skip = n_ctx // tm if final else 0
    assert n_ctx % tm == 0
    return pl.pallas_call(
        functools.partial(_combine_kernel, n_ctx=n_ctx, tm=tm, final=final, skip=skip),
        grid=(nt - skip,),
        in_specs=[pl.BlockSpec((tm * ROW_PIECES, LANES), lambda i, k=k: (k * nt + i + skip, 0))
                  for k in range(TOP_K)]
                 + [pl.BlockSpec((tm, SUBLANES), lambda i: (i + skip, 0)),
                  pl.BlockSpec((tm, d), lambda i: (i + skip, 0)),
                  pl.BlockSpec((SUBLANES, d), lambda i: (0, g2_col)),
                  pl.BlockSpec((1, d), lambda i: (0, 0))],
        out_specs=pl.BlockSpec((tm, d), lambda i: (i, 0)),
        out_shape=jax.ShapeDtypeStruct((t_rows - skip * tm, d), F32),
        compiler_params=_cparams("parallel"),
        name="moe_combine",
    )(yk, yk, yk, yk, gates_t, x, mod, final_g)


SC_WINDOW = 128


def _sc_mesh():
    return plsc.VectorSubcoreMesh(core_axis_name="core", subcore_axis_name="subcore")


def _sc_scatter_rows(x, dest, n_out):
    n_src_blocks = x.shape[0] // SC_WINDOW

    @functools.partial(pl.kernel, out_type=jax.ShapeDtypeStruct((n_out, LANES), x.dtype), mesh=_sc_mesh(),
                       scratch_types=[], name="sc_dispatch_rows")
    def run(x_hbm, i_hbm, o_hbm):
        def body(x_vmem, i_vmem):
            pltpu.sync_copy(x_vmem, o_hbm.at[i_vmem.at[0]])

        pltpu.emit_pipeline(
            body, grid=(dest.shape[1] // SC_WINDOW,),
            in_specs=[pl.BlockSpec((SC_WINDOW, LANES), lambda i: (i % n_src_blocks, 0)),
                      pl.BlockSpec((1, SC_WINDOW), lambda i: (0, i))],
            out_specs=[], core_axis_name=("core", "subcore"),
            dimension_semantics=(pltpu.PARALLEL,))(x_hbm, i_hbm)

    return run(x, dest)


def _sc_gather_rows(y, idx):
    n = idx.shape[1]

    @functools.partial(pl.kernel, out_type=jax.ShapeDtypeStruct((n, LANES), y.dtype), mesh=_sc_mesh(),
                       scratch_types=[], name="sc_combine_rows")
    def run(y_hbm, i_hbm, o_hbm):
        def body(i_vmem, o_vmem):
            pltpu.sync_copy(y_hbm.at[i_vmem.at[0]], o_vmem)

        pltpu.emit_pipeline(
            body, grid=(n // SC_WINDOW,),
            in_specs=[pl.BlockSpec((1, SC_WINDOW), lambda i: (0, i))],
            out_specs=[pl.BlockSpec((SC_WINDOW, LANES), lambda i: (i, 0))],
            core_axis_name=("core", "subcore"),
            dimension_semantics=(pltpu.PARALLEL,))(i_hbm, o_hbm)

    return run(y, idx)


def _moe(h, logits, x, mod, g2_col, w_gu, b_gu, w_dn, b_dn, layer, n_ctx, final_g, final):
    t_rows = x.shape[0]
    n_exp = w_gu.shape[1]
    idx, gates, counts = _router(logits, n_exp)
    counts = counts[:, 0].astype(jnp.int32)
    blk = EXPERT_BLOCK
    padded = (counts + blk - 1) // blk * blk
    pends = jnp.cumsum(padded)
    pstarts = pends - padded
    dest = _dest_rows(idx, pstarts.astype(F32)[:, None], n_exp)[:TOP_K]
    n_blocks = -(-(t_rows * TOP_K) // blk) + n_exp
    n_rows = n_blocks * blk
    blk_start = jnp.arange(n_blocks, dtype=jnp.int32) * blk
    blk_e = jnp.minimum(jnp.sum(blk_start[:, None] >= pends[None, :], axis=1), n_exp - 1).astype(jnp.int32)
    blk_valid = jnp.clip(pstarts[blk_e] + counts[blk_e] - blk_start, 0, blk).astype(jnp.int32)
    blk_valid = jnp.where(blk_start < pends[-1], blk_valid, 0)
    blk_new = jnp.concatenate([jnp.ones((1,), jnp.int32), (blk_e[1:] != blk_e[:-1]).astype(jnp.int32)])
    dest8 = (dest[:, :, None] * ROW_PIECES + jnp.arange(ROW_PIECES, dtype=jnp.int32)).reshape(1, -1)
    xs = _sc_scatter_rows(h, dest8, n_rows * ROW_PIECES)
    ys = _expert_matmul(xs, blk_e, blk_valid, blk_new, w_gu, b_gu, w_dn, b_dn, layer)
    yk = _sc_gather_rows(ys, dest8)
    return _combine(yk, gates.T, x, mod, g2_col, final_g, n_ctx, final)


def _s5_tables(a_re, a_im, log_step, b_re, b_im, c_re, c_im, d_skip):
    tc = S5_TC
    n_grp, n_st = a_re.shape[1], a_re.shape[2]
    lr = jnp.minimum(a_re.astype(F32), -1e-4)
    li = a_im.astype(F32)
    dt = jnp.exp(log_step.astype(F32))[..., None]
    dd = jnp.arange(tc + 1, dtype=F32)[:, None, None, None]
    mag = jnp.exp(lr * dt * dd)
    pr, pi = mag * jnp.cos(li * dt * dd), mag * jnp.sin(li * dt * dd)
    ar, ai = pr[1], pi[1]
    den = lr * lr + li * li
    nr = ar - 1.0
    zr = (nr * lr + ai * li) / den
    zi = (ai * lr - nr * li) / den
    br, bi = b_re.astype(F32), b_im.astype(F32)
    bbr = zr[..., None] * br - zi[..., None] * bi
    bbi = zr[..., None] * bi + zi[..., None] * br
    abr = pr[:tc, ..., None] * bbr - pi[:tc, ..., None] * bbi
    abi = pr[:tc, ..., None] * bbi + pi[:tc, ..., None] * bbr
    cr, ci = c_re.astype(F32), c_im.astype(F32)
    kern = (jnp.einsum('xgip,dxgpj->dxgij', cr, abr, precision=HIGHEST)
            - jnp.einsum('xgip,dxgpj->dxgij', ci, abi, precision=HIGHEST))
    kdim = tc * S5_GROUP
    k_t = kern.transpose(1, 2, 4, 0, 3)
    zeros = jnp.zeros((n_grp, S5_GROUP, kdim), F32)
    ext_f = jnp.concatenate([zeros, k_t[0].reshape(n_grp, S5_GROUP, kdim)], axis=-1)
    ext_b = jnp.concatenate([k_t[1, :, :, ::-1].reshape(n_grp, S5_GROUP, kdim), zeros], axis=-1)
    m_tot = jnp.stack([ext_f[:, :, (tc - s) * S5_GROUP:(tc - s) * S5_GROUP + kdim]
                       + ext_b[:, :, (tc - 1 - s) * S5_GROUP:(tc - 1 - s) * S5_GROUP + kdim]
                       for s in range(tc)], axis=1).reshape(n_grp, kdim, kdim)
    dsk = jnp.tile(d_skip.astype(F32).reshape(n_grp, S5_GROUP), (1, tc))
    m_tot = m_tot + jnp.eye(kdim, dtype=F32)[None] * dsk[:, None, :]
    pw_f = tc - 1 - jnp.arange(tc)
    pw_b = jnp.arange(tc)

    def b_cols(part, pw, x):
        return part[pw, x].transpose(1, 0, 3, 2).reshape(n_grp, tc * S5_GROUP, n_st)

    b_mat = jnp.concatenate([b_cols(abr, pw_f, 0), b_cols(abi, pw_f, 0),
                             b_cols(abr, pw_b, 1), b_cols(abi, pw_b, 1)], axis=-1)
    pcf = 1 + jnp.arange(tc)
    pcb = tc - jnp.arange(tc)

    def c_rows(pw, x):
        prx, pix = pr[pw, x], pi[pw, x]
        re_c = cr[x][None] * prx[:, :, None, :] - ci[x][None] * pix[:, :, None, :]
        im_c = -(cr[x][None] * pix[:, :, None, :] + ci[x][None] * prx[:, :, None, :])
        to_rows = lambda z: z.transpose(1, 3, 0, 2).reshape(n_grp, n_st, tc * S5_GROUP)
        return to_rows(re_c), to_rows(im_c)

    c_mat = jnp.concatenate(list(c_rows(pcf, 0)) + list(c_rows(pcb, 1)), axis=1)
    prt, pit = pr[tc], pi[tc]
    a1 = jnp.concatenate([prt[0], prt[0], prt[1], prt[1]], axis=-1)
    a2 = jnp.concatenate([-pit[0], pit[0], -pit[1], pit[1]], axis=-1)
    return m_tot.astype(BF16), b_mat.astype(BF16), c_mat.astype(BF16), a1, a2


S5_LANE_GROUPS = LANES // S5_GROUP


def _s5_pack(u_ref, c0, cc):
    xs = [u_ref[pl.ds(c0 * S5_TC + tau, cc, stride=S5_TC), :] for tau in range(S5_TC)]
    return [jnp.concatenate([xs[tau][:, g * S5_GROUP:(g + 1) * S5_GROUP] for tau in range(S5_TC)], axis=1)
            for g in range(S5_LANE_GROUPS)]


def _s5_in_kernel(u_ref, b_ref, o_ref, ug_ref, *, cc):
    def body(ci, carry):
        c0 = pl.multiple_of(ci * cc, SUBLANES)
        ugs = _s5_pack(u_ref, c0, cc)
        for g in range(S5_LANE_GROUPS):
            ug_ref[g, pl.ds(c0, cc), :] = ugs[g]
            o_ref[pl.ds(c0, cc), g, :] = jnp.dot(ugs[g].astype(BF16), b_ref[g], preferred_element_type=F32)
        return carry

    lax.fori_loop(0, u_ref.shape[0] // (S5_TC * cc), body, 0)


def _s5_out_kernel(ug_ref, z_ref, m_ref, c_ref, o_ref, *, cc):
    def body(ci, carry):
        c0 = pl.multiple_of(ci * cc, SUBLANES)
        ys = [jnp.dot(ug_ref[g, pl.ds(c0, cc), :].astype(BF16), m_ref[g], preferred_element_type=F32)
              + jnp.dot(z_ref[pl.ds(c0, cc), g, :].astype(BF16), c_ref[g], preferred_element_type=F32)
              for g in range(S5_LANE_GROUPS)]
        for tau in range(S5_TC):
            row = jnp.concatenate([y[:, tau * S5_GROUP:(tau + 1) * S5_GROUP] for y in ys], axis=1)
            o_ref[pl.ds(c0 * S5_TC + tau, cc, stride=S5_TC), :] = row
        return carry

    lax.fori_loop(0, o_ref.shape[0] // (S5_TC * cc), body, 0)


def _s5_scan_kernel(h_ref, a1_ref, a2_ref, z_ref, *, n_chunks, n_ctx_chunks):
    half = 2 * S5_STATE
    a1 = a1_ref[...]
    a2 = a2_ref[...]
    a1f, a1b = a1[:, :half], a1[:, half:]
    a2f, a2b = a2[:, :half], a2[:, half:]
    zero = jnp.zeros((a1.shape[0], half), F32)

    def swap(v):
        return pltpu.roll(v, S5_STATE, 1)

    def step(t, carry):
        rf, rfs, rb, rbs = carry
        cb = jnp.where(t < n_ctx_chunks, n_ctx_chunks - 1 - t, n_chunks - 1 - (t - n_ctx_chunks))
        z_ref[t, :, 0:half] = rf
        z_ref[cb, :, half:2 * half] = rb
        hf = h_ref[t, :, 0:half]
        hb = h_ref[cb, :, half:2 * half]
        nrf = a1f * rf + a2f * rfs + hf
        nrfs = a1f * rfs - a2f * rf + swap(hf)
        nrb = a1b * rb + a2b * rbs + hb
        nrbs = a1b * rbs - a2b * rb + swap(hb)
        return nrf, nrfs, nrb, nrbs

    lax.fori_loop(0, n_chunks, step, (zero, zero, zero, zero), unroll=8 if n_chunks % 8 == 0 else 1)


def _s5_mixer_scan(u, tables, n_ctx):
    m_tot, b_mat, c_mat, a1, a2 = tables
    t_rows, width = u.shape
    tc = S5_TC
    n_grp = width // S5_GROUP
    n_chunks = t_rows // tc
    kdim = tc * S5_GROUP
    sdim = 4 * S5_STATE
    gb = S5_LANE_GROUPS
    gs = SUBLANES
    n_split = 2
    half_chunks = n_chunks // n_split
    cc = _pick(half_chunks, (104, 80, 40, 8))
    u_spec = pl.BlockSpec((t_rows // n_split, LANES), lambda j, r: (r, j))
    tab_spec = lambda a, b: pl.BlockSpec((gb, a, b), lambda j, r: (j, 0, 0))
    st_spec = pl.BlockSpec((half_chunks, gb, sdim), lambda j, r: (r, j, 0))
    grid = (width // LANES, n_split)
    ug_spec = pl.BlockSpec((gb, half_chunks, kdim), lambda j, r: (j, r, 0))
    hin, ug = pl.pallas_call(
        functools.partial(_s5_in_kernel, cc=cc),
        grid=grid,
        in_specs=[u_spec, tab_spec(kdim, sdim)],
        out_specs=[st_spec, ug_spec],
        out_shape=[jax.ShapeDtypeStruct((n_chunks, n_grp, sdim), F32),
                   jax.ShapeDtypeStruct((n_grp, n_chunks, kdim), F32)],
        compiler_params=_cparams("parallel", "arbitrary"),
        name="s5_chunk_inputs",
    )(u, b_mat)
    z = pl.pallas_call(
        functools.partial(_s5_scan_kernel, n_chunks=n_chunks, n_ctx_chunks=n_ctx // tc),
        grid=(n_grp // gs,),
        in_specs=[pl.BlockSpec((n_chunks, gs, sdim), lambda g: (0, g, 0)),
                  pl.BlockSpec((gs, sdim), lambda g: (g, 0)),
                  pl.BlockSpec((gs, sdim), lambda g: (g, 0))],
        out_specs=pl.BlockSpec((n_chunks, gs, sdim), lambda g: (0, g, 0)),
        out_shape=jax.ShapeDtypeStruct((n_chunks, n_grp, sdim), F32),
        compiler_params=_cparams("parallel"),
        name="s5_chunk_scan",
    )(hin, a1, a2)
    return pl.pallas_call(
        functools.partial(_s5_out_kernel, cc=cc),
        grid=grid,
        in_specs=[ug_spec, st_spec, tab_spec(kdim, kdim), tab_spec(sdim, kdim)],
        out_specs=u_spec,
        out_shape=jax.ShapeDtypeStruct((t_rows, width), F32),
        compiler_params=_cparams("parallel", "arbitrary"),
        name="s5_chunk_outputs",
    )(ug, z, m_tot, c_mat)


def _pow2_scale(amax):
    return jnp.exp2(jnp.floor(jnp.log2(FP8_TARGET / jnp.maximum(amax, 1e-30))))


def _lambda_init(layer):
    return 0.8 - 0.6 * math.exp(-0.3 * layer)


def kernel(x, c, ctx, c_ctx, w_mod, b_mod, norm1_g, norm2_g, final_g, attn_w_qkv, attn_w_o, lambda_q1, lambda_k1, lambda_q2, lambda_k2, subln_g, sink_logit, s5_w_in, s5_a_re, s5_a_im, s5_log_step, s5_b_re, s5_b_im, s5_c_re, s5_c_im, s5_d, s5_w_glu, router_w, router_b, expert_w_gu, expert_b_gu, expert_w_down, expert_b_down):
    bsz, seq, d = x.shape
    assert bsz == 1, "single-sequence block"
    n_ctx = ctx.shape[1]
    depth = w_mod.shape[0]
    n_exp = router_w.shape[2]
    assert n_ctx % 256 == 0 and seq % 256 == 0

    xj = jnp.concatenate([ctx[0], x[0]], axis=0)
    cs = jnp.zeros((SUBLANES, d), F32).at[0].set(c_ctx).at[1].set(c[0])
    b_mod3 = b_mod[:, None, :]
    b_gu4 = expert_b_gu[:, :, None, :]
    b_dn4 = expert_b_down[:, :, None, :]
    wr_pad = jnp.pad(router_w, ((0, 0), (0, 0), (0, LANES - n_exp)))
    wr_hi = wr_pad.astype(BF16)
    wr_pad = jnp.stack([wr_hi, (wr_pad - wr_hi.astype(F32)).astype(BF16)], axis=1)
    br_pad = jnp.pad(router_b, ((0, 0), (0, LANES - n_exp)), constant_values=NEG_INF)[:, None, :]
    final_row = final_g[None, :]

    for i in range(depth):
        last = i == depth - 1
        j = i // 2
        mod = _adaln_mod(cs, w_mod, b_mod3, i)
        if i % 2 == 0:
            sizes = (A_HEADS * 2 * A_QK_DIM, A_HEADS * 2 * A_QK_DIM, A_HEADS * A_V_DIM,
                     B_Q_HEADS * B_HEAD_DIM, B_KV_HEADS * B_HEAD_DIM, B_KV_HEADS * B_HEAD_DIM)
            offs = np.concatenate([[0], np.cumsum(sizes)])
            types = [0, 1, ROPE_NONE, 2, 3, ROPE_NONE]
            ttype = jnp.asarray(np.concatenate([np.full(s // 256, t) for s, t in zip(sizes, types)]), jnp.int32)
            cos_t, sin_t = _rope_tables(n_ctx, seq)
            qkv = _norm_mod_matmul(xj, norm1_g[i][None, :], mod, 0, 1, attn_w_qkv[j].astype(BF16), n_ctx, BF16,
                                   rope=(ttype, cos_t, sin_t))
            qt = qkv[:, offs[0]:offs[1]].T
            ka = qkv[:, offs[1]:offs[2]]
            ka3 = ka.reshape(-1, A_HEADS, 2 * A_QK_DIM).astype(F32)
            ck = _pow2_scale(jnp.max(jnp.abs(ka3), axis=(0, 2)))
            ka = (ka3 * ck[None, :, None]).astype(F8).reshape(ka.shape)
            qt = (qt.reshape(A_HEADS, 2 * A_QK_DIM, -1).astype(F32) / ck[:, None, None]).astype(BF16).reshape(qt.shape)
            vt = qkv[:, offs[2]:offs[3]].T.reshape(A_HEADS, A_V_DIM, -1)
            cv = _pow2_scale(jnp.max(jnp.abs(vt.astype(F32)), axis=(1, 2)))
            vt = (vt.astype(F32) * cv[:, None, None]).astype(BF16)
            vt = jnp.concatenate([vt, jnp.ones((A_HEADS, V_AUG - A_V_DIM, vt.shape[2]), BF16)], axis=1)
            vt = vt.reshape(A_HEADS * V_AUG, -1)
            f32 = F32
            li = _lambda_init(i)
            lam = (jnp.exp(jnp.sum(lambda_q1[j].astype(f32) * lambda_k1[j].astype(f32)))
                   - jnp.exp(jnp.sum(lambda_q2[j].astype(f32) * lambda_k2[j].astype(f32))) + li)
            attn_args = (jnp.concatenate([lam.reshape(1), 1.0 / cv]), subln_g[j][:, None], 1.0 - li)
            ya = jnp.concatenate([_diff_attention(qt[:, :n_ctx], ka[:n_ctx], vt[:, :n_ctx], *attn_args),
                                  _diff_attention(qt[:, n_ctx:], ka, vt, *attn_args)], axis=0)
            yb = _window_attention(qkv, sink_logit[j], n_ctx, int(offs[3]) // LANES, int(offs[4]) // LANES,
                                   int(offs[5]) // LANES)
            w_o = attn_w_o[j].astype(BF16)
            na = A_HEADS * A_V_DIM
            xj, h2, logits = _mixer_out(_attn_out_kernel, [ya, yb], xj, [w_o[:na], w_o[na:]], mod, (2, 3, 4),
                                        norm2_g[i][None, :], wr_pad[i], br_pad[i], n_ctx, "attn_out_router")
        else:
            u = _norm_mod_matmul(xj, norm1_g[i][None, :], mod, 0, 1, s5_w_in[j].astype(BF16), n_ctx, F32)
            tables = _s5_tables(s5_a_re[j], s5_a_im[j], s5_log_step[j], s5_b_re[j], s5_b_im[j],
                                s5_c_re[j], s5_c_im[j], s5_d[j])
            y = _s5_mixer_scan(u, tables, n_ctx)
            w_glu = s5_w_glu[j].astype(BF16)
            xj, h2, logits = _mixer_out(_glu_out_kernel, [y], xj, [w_glu[:, :d], w_glu[:, d:]], mod, (2, 3, 4),
                                        norm2_g[i][None, :], wr_pad[i], br_pad[i], n_ctx, "glu_out_router")
        xj = _moe(h2, logits, xj, mod, 5, expert_w_gu, b_gu4, expert_w_down, b_dn4, i, n_ctx, final_row, last)
    return xj[None]
```

```python
import functools
import math

import jax
import jax.numpy as jnp
import numpy as np
from jax import lax
from jax.experimental import pallas as pl
from jax.experimental.pallas import tpu as pltpu
from jax.experimental.pallas import tpu_sc as plsc

F32 = jnp.float32
BF16 = jnp.bfloat16
F8 = jnp.float8_e4m3fn
HIGHEST = lax.Precision.HIGHEST

V7X_VMEM_BYTES = 64 * 1024 * 1024
VMEM_LIMIT = V7X_VMEM_BYTES - 8 * 1024 * 1024
LANES = 128
SUBLANES = 8

GRID_W = 64
EPS = 1e-6
NEG_INF = -1e30
ROPE_THETA = 10000.0
A_HEADS = 8
A_QK_DIM = 64
A_V_DIM = 128
B_Q_HEADS = 8
B_KV_HEADS = 2
B_GROUP = 4
B_HEAD_DIM = 128
WINDOW = 128
S5_GROUP = 16
S5_STATE = 64
S5_TC = 16
TOP_K = 4
SWIGLU_LIMIT = 7.0
SWIGLU_ALPHA = 1.702
EXPERT_BLOCK = 512
DOWN_BLOCK = 256
GATE_UP_TILE = 1024
FP8_MAX = 448.0
FP8_TARGET = 256.0
LOG2E = 1.4426950408889634


def _pick(n, cands):
    for c in cands:
        if n % c == 0:
            return c
    raise ValueError(f"no tile for {n} in {cands}")


def _cparams(*sem):
    return pltpu.CompilerParams(dimension_semantics=sem, vmem_limit_bytes=VMEM_LIMIT)


def _mod_kernel(c_ref, w_ref, b_ref, o_ref):
    cv = c_ref[...]
    s = cv * jax.nn.sigmoid(cv)
    o_ref[...] = jnp.dot(s, w_ref[...], preferred_element_type=F32, precision=HIGHEST) + b_ref[...]


def _adaln_mod(cs, w_mod, b_mod, layer):
    d, n = w_mod.shape[1], w_mod.shape[2]
    tn = _pick(n, (1024, 512, 256, 128))
    return pl.pallas_call(
        _mod_kernel,
        grid=(n // tn,),
        in_specs=[pl.BlockSpec((SUBLANES, d), lambda j: (0, 0)),
                  pl.BlockSpec((None, d, tn), lambda j: (layer, 0, j)),
                  pl.BlockSpec((None, 1, tn), lambda j: (layer, 0, j))],
        out_specs=pl.BlockSpec((SUBLANES, tn), lambda j: (0, j)),
        out_shape=jax.ShapeDtypeStruct((SUBLANES, n), F32),
        compiler_params=_cparams("parallel"),
        name="adaln_mod",
    )(cs, w_mod, b_mod)


def _norm_mod(x, g, sh2, sc2, row0, n_ctx):
    ms = jnp.mean(x * x, axis=-1, keepdims=True)
    y = x * lax.rsqrt(ms + EPS) * g
    row = row0 + lax.broadcasted_iota(jnp.int32, (x.shape[0], 1), 0)
    is_ctx = row < n_ctx
    sc = jnp.where(is_ctx, sc2[0:1, :], sc2[1:2, :])
    sh = jnp.where(is_ctx, sh2[0:1, :], sh2[1:2, :])
    return y * (1.0 + sc) + sh


ROPE_NONE = 4


def _rope_store(acc, cos_ref, sin_ref, o_ref, shift):
    cosv, sinv = cos_ref[...], sin_ref[...]
    for cgrp in range(acc.shape[1] // LANES):
        a = acc[:, cgrp * LANES:(cgrp + 1) * LANES]
        lane = lax.broadcasted_iota(jnp.int32, a.shape, 1)
        in_second = (lane & (2 * shift - 1)) >= shift
        sw = jnp.where(in_second, pltpu.roll(a, shift, 1), pltpu.roll(a, LANES - shift, 1))
        o_ref[:, cgrp * LANES:(cgrp + 1) * LANES] = (a * cosv + sw * sinv).astype(o_ref.dtype)


NORM_ROWS = 256


def _norm_mod_to_scratch(x_ref, g_ref, sh_ref, sc_ref, h_scr, row0, n_ctx):
    def body(r, carry):
        off = pl.multiple_of(r * NORM_ROWS, NORM_ROWS)
        h = _norm_mod(x_ref[pl.ds(off, NORM_ROWS), :], g_ref[...], sh_ref[...], sc_ref[...], row0 + off, n_ctx)
        h_scr[pl.ds(off, NORM_ROWS), :] = h.astype(h_scr.dtype)
        return carry

    lax.fori_loop(0, x_ref.shape[0] // NORM_ROWS, body, 0)


def _nmm_rope_kernel(tt_ref, x_ref, g_ref, sh_ref, sc_ref, w_ref, cos_ref, sin_ref, o_ref, h_scr,
                     *, n_ctx, tm):
    i = pl.program_id(0)
    j = pl.program_id(1)

    @pl.when(j == 0)
    def _():
        _norm_mod_to_scratch(x_ref, g_ref, sh_ref, sc_ref, h_scr, i * tm, n_ctx)

    acc = jnp.dot(h_scr[...], w_ref[...], preferred_element_type=F32)
    t = tt_ref[j]

    @pl.when(t == ROPE_NONE)
    def _():
        o_ref[...] = acc.astype(o_ref.dtype)

    @pl.when(t < 2)
    def _():
        _rope_store(acc, cos_ref, sin_ref, o_ref, A_QK_DIM // 4)

    @pl.when(jnp.logical_and(t >= 2, t < ROPE_NONE))
    def _():
        _rope_store(acc, cos_ref, sin_ref, o_ref, B_HEAD_DIM // 4)


def _nmm_plain_kernel(x_ref, g_ref, sh_ref, sc_ref, w_ref, o_ref, h_scr, *, n_ctx, tm):
    i = pl.program_id(0)
    j = pl.program_id(1)

    @pl.when(j == 0)
    def _():
        _norm_mod_to_scratch(x_ref, g_ref, sh_ref, sc_ref, h_scr, i * tm, n_ctx)

    o_ref[...] = jnp.dot(h_scr[...], w_ref[...], preferred_element_type=F32).astype(o_ref.dtype)


def _norm_mod_matmul(x, g, mod, sh_col, sc_col, w, n_ctx, out_dtype, rope=None):
    t_rows, d = x.shape
    n = w.shape[1]
    tm = _pick(t_rows, (1280, 1024, 512, 256, 128))
    tn = 256
    grid = (t_rows // tm, n // tn)
    kern_kw = dict(n_ctx=n_ctx, tm=tm)
    scratch = [pltpu.VMEM((tm, d), BF16)]
    out_shape = jax.ShapeDtypeStruct((t_rows, n), out_dtype)
    if rope is None:
        return pl.pallas_call(
            functools.partial(_nmm_plain_kernel, **kern_kw),
            grid=grid,
            in_specs=[pl.BlockSpec((tm, d), lambda i, j: (i, 0)),
                      pl.BlockSpec((1, d), lambda i, j: (0, 0)),
                      pl.BlockSpec((SUBLANES, d), lambda i, j: (0, sh_col)),
                      pl.BlockSpec((SUBLANES, d), lambda i, j: (0, sc_col)),
                      pl.BlockSpec((d, tn), lambda i, j: (0, j))],
            out_specs=pl.BlockSpec((tm, tn), lambda i, j: (i, j)),
            out_shape=out_shape,
            scratch_shapes=scratch,
            compiler_params=_cparams("parallel", "arbitrary"),
            name="norm_mod_matmul",
        )(x, g, mod, mod, w)
    ttype, cos_t, sin_t = rope
    return pl.pallas_call(
        functools.partial(_nmm_rope_kernel, **kern_kw),
        grid_spec=pltpu.PrefetchScalarGridSpec(
            num_scalar_prefetch=1,
            grid=grid,
            in_specs=[pl.BlockSpec((tm, d), lambda i, j, tt: (i, 0)),
                      pl.BlockSpec((1, d), lambda i, j, tt: (0, 0)),
                      pl.BlockSpec((SUBLANES, d), lambda i, j, tt: (0, sh_col)),
                      pl.BlockSpec((SUBLANES, d), lambda i, j, tt: (0, sc_col)),
                      pl.BlockSpec((d, tn), lambda i, j, tt: (0, j)),
                      pl.BlockSpec((None, tm, LANES), lambda i, j, tt: (tt[j], i, 0)),
                      pl.BlockSpec((None, tm, LANES), lambda i, j, tt: (tt[j], i, 0))],
            out_specs=pl.BlockSpec((tm, tn), lambda i, j, tt: (i, j)),
            scratch_shapes=scratch),
        out_shape=out_shape,
        compiler_params=_cparams("parallel", "arbitrary"),
        name="norm_mod_qkv_rope",
    )(ttype, x, g, mod, mod, w, cos_t, sin_t)


def _rope_tables(n_ctx, seq):
    pos = jnp.arange(seq)
    rows = (pos // GRID_W).astype(F32)
    cols = (pos % GRID_W).astype(F32)

    def tab(dim):
        quarter = dim // 4
        freqs = ROPE_THETA ** (-jnp.arange(quarter, dtype=F32) / quarter)
        ar, ac = rows[:, None] * freqs, cols[:, None] * freqs
        cosv = jnp.concatenate([jnp.cos(ar), jnp.cos(ar), jnp.cos(ac), jnp.cos(ac)], axis=1)
        sinv = jnp.concatenate([-jnp.sin(ar), jnp.sin(ar), -jnp.sin(ac), jnp.sin(ac)], axis=1)
        reps = LANES // dim
        cosv, sinv = jnp.tile(cosv, (1, reps)), jnp.tile(sinv, (1, reps))
        cosv = jnp.concatenate([jnp.ones((n_ctx, LANES), F32), cosv], axis=0)
        sinv = jnp.concatenate([jnp.zeros((n_ctx, LANES), F32), sinv], axis=0)
        return cosv, sinv

    ca, sa = tab(A_QK_DIM)
    cb, sb = tab(B_HEAD_DIM)
    qa = (A_QK_DIM ** -0.5) * LOG2E
    qb = B_HEAD_DIM ** -0.5
    one, zero = jnp.ones_like(ca), jnp.zeros_like(ca)
    cos_t = jnp.stack([ca * qa, ca, cb * qb, cb, one])
    sin_t = jnp.stack([sa * qa, sa, sb * qb, sb, zero])
    return cos_t, sin_t


V_AUG = A_V_DIM + 16


def _flash_kernel(lam_ref, qt_ref, k_ref, vt_ref, g_ref, o_ref, s_a, s_b, acc,
                  *, tq, tk, t_rows, out_scale):
    qt = qt_ref[...]
    row = lax.broadcasted_iota(jnp.int32, qt.shape, 0)
    zero = jnp.zeros_like(qt)
    qt = jnp.clip(qt, -FP8_MAX, FP8_MAX)
    qm = (jnp.where(row < A_QK_DIM, qt, zero).astype(F8), jnp.where(row >= A_QK_DIM, qt, zero).astype(F8))

    def start(tile, size):
        return tile * size if isinstance(tile, int) else pl.multiple_of(tile * size, size)

    def scores(tile, size, dst):
        kt = k_ref[pl.ds(start(tile, size), size), :]
        for m in range(2):
            dst[m, 0:size, :] = jnp.dot(kt, qm[m], preferred_element_type=F32).astype(BF16)

    def soft_pv(tile, size, src, ms):
        vt = vt_ref[:, pl.ds(start(tile, size), size)].astype(F8)
        new_ms = []
        for m in range(2):
            s = src[m, 0:size, :]
            mx = jnp.maximum(ms[m], jnp.max(s, axis=0, keepdims=True).astype(F32))
            alpha = jnp.exp2(ms[m] - mx)
            p = jnp.exp2(s - mx.astype(BF16)).astype(F8)
            acc[m] = alpha * acc[m] + jnp.dot(vt, p, preferred_element_type=F32)
            new_ms.append(mx)
        return tuple(new_ms)

    def finish():
        o1 = acc[0, 0:A_V_DIM, :] / acc[0, A_V_DIM:A_V_DIM + 1, :]
        o2 = acc[1, 0:A_V_DIM, :] / acc[1, A_V_DIM:A_V_DIM + 1, :]
        o = (o1 - lam_ref[0] * o2) * lam_ref[1 + pl.program_id(0)]
        var = jnp.mean(o * o, axis=0, keepdims=True)
        o = o * lax.rsqrt(var + EPS) * (g_ref[...] * out_scale)
        o_ref[...] = o.T.astype(o_ref.dtype)

    m0 = jnp.full((1, tq), NEG_INF, F32)
    nk = t_rows // tk
    n_pairs = (nk - 1) // 2
    acc[...] = jnp.zeros_like(acc)
    scores(0, tk, s_a)

    def pair(pp, carry):
        scores(2 * pp + 1, tk, s_b)
        carry = soft_pv(2 * pp, tk, s_a, carry)
        scores(2 * pp + 2, tk, s_a)
        return soft_pv(2 * pp + 1, tk, s_b, carry)

    carry = lax.fori_loop(0, n_pairs, pair, (m0, m0))
    done = 2 * n_pairs
    if nk - done == 2:
        scores(done + 1, tk, s_b)
        carry = soft_pv(done, tk, s_a, carry)
        carry = soft_pv(done + 1, tk, s_b, carry)
    else:
        carry = soft_pv(done, tk, s_a, carry)
    finish()


def _diff_attention(qt, k, vt, lam, subln_col, out_scale):
    t_rows = k.shape[0]
    n_q = qt.shape[1]
    tq = _pick(n_q, (2048, 1024, 512, 256))
    tk = _pick(t_rows, (1664, 1280, 1024, 512, 256))
    return pl.pallas_call(
        functools.partial(_flash_kernel, tq=tq, tk=tk, t_rows=t_rows, out_scale=out_scale),
        grid=(A_HEADS, n_q // tq),
        in_specs=[pl.BlockSpec(memory_space=pltpu.SMEM),
                  pl.BlockSpec((LANES, tq), lambda h, i: (h, i)),
                  pl.BlockSpec((t_rows, LANES), lambda h, i: (0, h)),
                  pl.BlockSpec((V_AUG, t_rows), lambda h, i: (h, 0)),
                  pl.BlockSpec((LANES, 1), lambda h, i: (0, 0))],
        out_specs=pl.BlockSpec((tq, LANES), lambda h, i: (i, h)),
        out_shape=jax.ShapeDtypeStruct((n_q, A_HEADS * A_V_DIM), BF16),
        scratch_shapes=[pltpu.VMEM((2, tk, tq), BF16), pltpu.VMEM((2, tk, tq), BF16),
                        pltpu.VMEM((2, V_AUG, tq), F32)],
        compiler_params=_cparams("parallel", "arbitrary"),
        name="diff_attention",
    )(lam, qt, k, vt, subln_col)


def _window_kernel(sink_ref, q_ref, kp_ref, ko_ref, kn_ref, vp_ref, vo_ref, vn_ref, kc_ref, vc_ref, o_ref,
                   *, nb, nb_ctx):
    n = pl.program_id(0)
    qi = lax.broadcasted_iota(jnp.int32, (WINDOW, WINDOW), 0)
    kk = lax.broadcasted_iota(jnp.int32, (WINDOW, WINDOW), 1)
    own_ok = n >= nb_ctx
    prev_ok = n >= nb_ctx + 1
    next_ok = jnp.logical_and(own_ok, n <= nb - 2)
    m_prev = jnp.logical_and(kk >= qi, prev_ok)
    m_own = jnp.logical_and(kk >= 0, own_ok)
    m_next = jnp.logical_and(kk <= qi, next_ok)
    dn = (((1,), (1,)), ((), ()))
    for h in range(B_Q_HEADS):
        kv = slice((h // B_GROUP) * B_HEAD_DIM, (h // B_GROUP + 1) * B_HEAD_DIM)
        q = q_ref[:, h * B_HEAD_DIM:(h + 1) * B_HEAD_DIM]
        s_p = jnp.where(m_prev, lax.dot_general(q, kp_ref[:, kv], dn, preferred_element_type=F32), NEG_INF)
        s_o = jnp.where(m_own, lax.dot_general(q, ko_ref[:, kv], dn, preferred_element_type=F32), NEG_INF)
        s_n = jnp.where(m_next, lax.dot_general(q, kn_ref[:, kv], dn, preferred_element_type=F32), NEG_INF)
        s_c = lax.dot_general(q, kc_ref[:, kv], dn, preferred_element_type=F32)
        sink = sink_ref[h]
        mx = jnp.maximum(jnp.maximum(jnp.max(s_p, axis=1, keepdims=True), jnp.max(s_o, axis=1, keepdims=True)),
                         jnp.maximum(jnp.max(s_n, axis=1, keepdims=True), jnp.max(s_c, axis=1, keepdims=True)))
        mx = jnp.maximum(mx, sink)
        p_p, p_o, p_n, p_c = jnp.exp(s_p - mx), jnp.exp(s_o - mx), jnp.exp(s_n - mx), jnp.exp(s_c - mx)
        den = (jnp.sum(p_p, axis=1, keepdims=True) + jnp.sum(p_o, axis=1, keepdims=True)
               + jnp.sum(p_n, axis=1, keepdims=True) + jnp.sum(p_c, axis=1, keepdims=True)
               + jnp.exp(sink - mx))
        o = (jnp.dot(p_p.astype(BF16), vp_ref[:, kv], preferred_element_type=F32)
             + jnp.dot(p_o.astype(BF16), vo_ref[:, kv], preferred_element_type=F32)
             + jnp.dot(p_n.astype(BF16), vn_ref[:, kv], preferred_element_type=F32)
             + jnp.dot(p_c.astype(BF16), vc_ref[:, kv], preferred_element_type=F32))
        o_ref[:, h * B_HEAD_DIM:(h + 1) * B_HEAD_DIM] = (o / den).astype(o_ref.dtype)


def _window_attention(qkv, sink, n_ctx, col_q, col_k, col_v):
    t_rows = qkv.shape[0]
    nb = t_rows // WINDOW
    nb_ctx = n_ctx // WINDOW
    q_w = B_Q_HEADS * B_HEAD_DIM
    kv_w = B_KV_HEADS * B_HEAD_DIM
    assert (col_q * LANES) % q_w == 0 and (col_k * LANES) % kv_w == 0 and (col_v * LANES) % kv_w == 0

    def kv_spec(col, shift):
        def imap(n):
            return (jnp.clip(n + shift, 0, nb - 1), col * LANES // kv_w)
        return pl.BlockSpec((WINDOW, kv_w), imap)

    return pl.pallas_call(
        functools.partial(_window_kernel, nb=nb, nb_ctx=nb_ctx),
        grid=(nb,),
        in_specs=[pl.BlockSpec(memory_space=pltpu.SMEM),
                  pl.BlockSpec((WINDOW, q_w), lambda n: (n, col_q * LANES // q_w)),
                  kv_spec(col_k, -1), kv_spec(col_k, 0), kv_spec(col_k, 1),
                  kv_spec(col_v, -1), kv_spec(col_v, 0), kv_spec(col_v, 1),
                  pl.BlockSpec((n_ctx, kv_w), lambda n: (0, col_k * LANES // kv_w)),
                  pl.BlockSpec((n_ctx, kv_w), lambda n: (0, col_v * LANES // kv_w))],
        out_specs=pl.BlockSpec((WINDOW, q_w), lambda n: (n, 0)),
        out_shape=jax.ShapeDtypeStruct((t_rows, q_w), BF16),
        compiler_params=_cparams("parallel"),
        name="window_attention",
    )(sink, qkv, qkv, qkv, qkv, qkv, qkv, qkv, qkv, qkv)


ROW_PIECES = 8


def _store_packed_rows(ref, v):
    m, half = v.shape[0], v.shape[1] // 2
    lo = pltpu.bitcast(v[:, :half].astype(BF16).astype(F32), jnp.uint32) >> 16
    hi = pltpu.bitcast(v[:, half:].astype(BF16).astype(F32), jnp.uint32) & jnp.uint32(0xFFFF0000)
    w = lo | hi
    for j in range(ROW_PIECES):
        ref[pl.ds(j, m, stride=ROW_PIECES), :] = w[:, j * LANES:(j + 1) * LANES]


def _load_packed_rows(ref, m, r0=0):
    w = jnp.concatenate([ref[pl.ds(r0 * ROW_PIECES + j, m, stride=ROW_PIECES), :] for j in range(ROW_PIECES)],
                        axis=1)
    lo = pltpu.bitcast(w << 16, F32)
    hi = pltpu.bitcast(w & jnp.uint32(0xFFFF0000), F32)
    return lo, hi


def _post_mixer(y, x, g1_ref, g_ref, sh_ref, sc_ref, wr_ref, br_ref, xo_ref, h_ref, lg_ref, row0, n_ctx):
    row = row0 + lax.broadcasted_iota(jnp.int32, (x.shape[0], 1), 0)
    g1 = jnp.where(row < n_ctx, g1_ref[0:1, :], g1_ref[1:2, :])
    xn = x + g1 * y
    xo_ref[...] = xn
    h = _norm_mod(xn, g_ref[...], sh_ref[...], sc_ref[...], row0, n_ctx)
    _store_packed_rows(h_ref, h)
    h_hi = h.astype(BF16)
    h_lo = (h - h_hi.astype(F32)).astype(BF16)
    lg_ref[...] = (jnp.dot(h_hi, wr_ref[0], preferred_element_type=F32)
                   + jnp.dot(h_lo, wr_ref[0], preferred_element_type=F32)
                   + jnp.dot(h_hi, wr_ref[1], preferred_element_type=F32) + br_ref[...])


def _attn_out_kernel(ya_ref, yb_ref, x_ref, woa_ref, wob_ref, g1_ref, g_ref, sh_ref, sc_ref, wr_ref, br_ref,
                     xo_ref, h_ref, lg_ref, *, n_ctx, tm):
    y = (jnp.dot(ya_ref[...], woa_ref[...], preferred_element_type=F32)
         + jnp.dot(yb_ref[...], wob_ref[...], preferred_element_type=F32))
    _post_mixer(y, x_ref[...], g1_ref, g_ref, sh_ref, sc_ref, wr_ref, br_ref, xo_ref, h_ref, lg_ref,
                pl.program_id(0) * tm, n_ctx)


def _glu_out_kernel(y_ref, x_ref, wv_ref, wg_ref, g1_ref, g_ref, sh_ref, sc_ref, wr_ref, br_ref,
                    xo_ref, h_ref, lg_ref, *, n_ctx, tm):
    a = jax.nn.gelu(y_ref[...], approximate=True).astype(BF16)
    val = jnp.dot(a, wv_ref[...], preferred_element_type=F32)
    gate = jnp.dot(a, wg_ref[...], preferred_element_type=F32)
    _post_mixer(val * jax.nn.sigmoid(gate), x_ref[...], g1_ref, g_ref, sh_ref, sc_ref, wr_ref, br_ref,
                xo_ref, h_ref, lg_ref, pl.program_id(0) * tm, n_ctx)


def _mixer_out(kernel_fn, acts, x, weights, mod, cols, norm_g, wr, br, n_ctx, name):
    t_rows, d = x.shape
    tm = 256
    row = lambda i: (i, 0)
    const = lambda i: (0, 0)
    in_specs = ([pl.BlockSpec((tm, a.shape[1]), row) for a in acts]
                + [pl.BlockSpec((tm, d), row)]
                + [pl.BlockSpec(w.shape, const) for w in weights]
                + [pl.BlockSpec((SUBLANES, d), lambda i, c=c: (0, c)) for c in cols[:1]]
                + [pl.BlockSpec((1, d), const)]
                + [pl.BlockSpec((SUBLANES, d), lambda i, c=c: (0, c)) for c in cols[1:]]
                + [pl.BlockSpec(wr.shape, lambda i: (0, 0, 0)), pl.BlockSpec(br.shape, const)])
    return pl.pallas_call(
        functools.partial(kernel_fn, n_ctx=n_ctx, tm=tm),
        grid=(t_rows // tm,),
        in_specs=in_specs,
        out_specs=[pl.BlockSpec((tm, d), row), pl.BlockSpec((tm * ROW_PIECES, LANES), row),
                   pl.BlockSpec((tm, LANES), row)],
        out_shape=[jax.ShapeDtypeStruct((t_rows, d), F32),
                   jax.ShapeDtypeStruct((t_rows * ROW_PIECES, LANES), jnp.uint32),
                   jax.ShapeDtypeStruct((t_rows, LANES), F32)],
        compiler_params=_cparams("parallel"),
        name=name,
    )(*acts, x, *weights, mod, norm_g, mod, mod, wr, br)


def _router_kernel(lg_ref, idx_ref, gate_ref, cnt_ref, *, n_exp):
    @pl.when(pl.program_id(0) == 0)
    def _():
        cnt_ref[...] = jnp.zeros_like(cnt_ref)

    lt = lg_ref[...].T[0:n_exp, :]
    eid = lax.broadcasted_iota(jnp.int32, lt.shape, 0).astype(F32)
    vals, idxs = [], []
    hist = jnp.zeros(lt.shape, F32)
    for _ in range(TOP_K):
        mv = jnp.max(lt, axis=0, keepdims=True)
        ix = jnp.min(jnp.where(lt == mv, eid, float(n_exp)), axis=0, keepdims=True)
        sel = eid == ix
        hist = hist + sel.astype(F32)
        lt = jnp.where(sel, -jnp.inf, lt)
        vals.append(mv)
        idxs.append(ix)
    es = [jnp.exp(v - vals[0]) for v in vals]
    den = es[0] + es[1] + es[2] + es[3]
    pad_f = jnp.zeros((SUBLANES - TOP_K, lt.shape[1]), F32)
    idx_ref[...] = jnp.concatenate(idxs + [pad_f], axis=0).astype(jnp.int32)
    gate_ref[...] = jnp.concatenate([e / den for e in es] + [pad_f], axis=0)
    cnt_ref[...] += jnp.sum(hist, axis=1, keepdims=True)


def _router(logits, n_exp):
    t_rows = logits.shape[0]
    tm = 256
    return pl.pallas_call(
        functools.partial(_router_kernel, n_exp=n_exp),
        grid=(t_rows // tm,),
        in_specs=[pl.BlockSpec((tm, LANES), lambda i: (i, 0))],
        out_specs=[pl.BlockSpec((SUBLANES, tm), lambda i: (0, i)),
                   pl.BlockSpec((SUBLANES, tm), lambda i: (0, i)),
                   pl.BlockSpec((n_exp, 1), lambda i: (0, 0))],
        out_shape=[jax.ShapeDtypeStruct((SUBLANES, t_rows), jnp.int32),
                   jax.ShapeDtypeStruct((SUBLANES, t_rows), F32),
                   jax.ShapeDtypeStruct((n_exp, 1), F32)],
        compiler_params=_cparams("arbitrary"),
        name="router_topk",
    )(logits)


def _dest_kernel(idx_ref, start_ref, dest_ref, carry, *, n_exp, tm):
    @pl.when(pl.program_id(0) == 0)
    def _():
        carry[...] = start_ref[...]

    eid = lax.broadcasted_iota(jnp.int32, (n_exp, tm), 0)
    idx = idx_ref[...]
    sels = [eid == idx[k:k + 1, :] for k in range(TOP_K)]
    total = sels[0].astype(F32) + sels[1].astype(F32) + sels[2].astype(F32) + sels[3].astype(F32)
    rr = lax.broadcasted_iota(jnp.int32, (tm, tm), 0)
    cc = lax.broadcasted_iota(jnp.int32, (tm, tm), 1)
    upper = jnp.where(rr < cc, 1.0, 0.0).astype(BF16)
    before = jnp.dot(total.astype(BF16), upper, preferred_element_type=F32) + carry[...]
    rows = [jnp.sum(jnp.where(sels[k], before, 0.0), axis=0, keepdims=True) for k in range(TOP_K)]
    pad = jnp.zeros((SUBLANES - TOP_K, tm), F32)
    dest_ref[...] = jnp.concatenate(rows + [pad], axis=0).astype(jnp.int32)
    carry[...] += jnp.sum(total, axis=1, keepdims=True)


def _dest_rows(idx, starts, n_exp):
    t_rows = idx.shape[1]
    tm = 256
    return pl.pallas_call(
        functools.partial(_dest_kernel, n_exp=n_exp, tm=tm),
        grid=(t_rows // tm,),
        in_specs=[pl.BlockSpec((SUBLANES, tm), lambda i: (0, i)),
                  pl.BlockSpec((n_exp, 1), lambda i: (0, 0))],
        out_specs=pl.BlockSpec((SUBLANES, tm), lambda i: (0, i)),
        out_shape=jax.ShapeDtypeStruct((SUBLANES, t_rows), jnp.int32),
        scratch_shapes=[pltpu.VMEM((n_exp, 1), F32)],
        compiler_params=_cparams("arbitrary"),
        name="moe_dest_rows",
    )(idx, starts)


def _to_fp8(v):
    amax = jnp.max(jnp.abs(v), axis=(0, 1), keepdims=True)
    scale = FP8_TARGET / jnp.maximum(amax, 1e-30)
    return (v * scale).astype(F8), 1.0 / scale


def _expert_gu_kernel(be_ref, bv_ref, bn_ref, x_ref, wg_ref, wu_ref, bg_ref, bu_ref, o_ref,
                      wg_s, wu_s, sg_s, su_s, *, blk):
    b = pl.program_id(1)
    nvalid = bv_ref[b]

    @pl.when(bn_ref[b] == 1)
    def _():
        wg_s[...], inv_g = _to_fp8(wg_ref[...])
        wu_s[...], inv_u = _to_fp8(wu_ref[...])
        sg_s[...] = jnp.broadcast_to(inv_g, sg_s.shape)
        su_s[...] = jnp.broadcast_to(inv_u, su_s.shape)

    @pl.when(nvalid > 0)
    def _():
        rows = lax.broadcasted_iota(jnp.int32, (blk, 1), 0)
        lo, hi = _load_packed_rows(x_ref, blk)
        x = jnp.concatenate([lo, hi], axis=1)
        x = jnp.where(rows < nvalid, x, jnp.zeros_like(x))
        x, inv_x = _to_fp8(x)
        gate = jnp.dot(x, wg_s[...], preferred_element_type=F32) * (sg_s[...] * inv_x) + bg_ref[...]
        up = jnp.dot(x, wu_s[...], preferred_element_type=F32) * (su_s[...] * inv_x) + bu_ref[...]
        gate = jnp.minimum(gate, SWIGLU_LIMIT)
        up = jnp.clip(up, -SWIGLU_LIMIT, SWIGLU_LIMIT)
        act = (up + 1.0) * (gate * jax.nn.sigmoid(SWIGLU_ALPHA * gate))
        o_ref[...] = act.astype(o_ref.dtype)

    @pl.when(nvalid == 0)
    def _():
        o_ref[...] = jnp.zeros_like(o_ref)


def _expert_dn_kernel(be_ref, bv_ref, bn_ref, a_ref, wd_ref, bd_ref, o_ref, wd_s, sd_s):
    b = pl.program_id(0)

    @pl.when(bn_ref[b] == 1)
    def _():
        wd_s[...], inv_d = _to_fp8(wd_ref[...])
        sd_s[...] = jnp.broadcast_to(inv_d, sd_s.shape)

    @pl.when(bv_ref[b] > 0)
    def _():
        y = jnp.dot(a_ref[...].astype(F8), wd_s[...], preferred_element_type=F32) * sd_s[...] + bd_ref[...]
        _store_packed_rows(o_ref, y)

    @pl.when(bv_ref[b] == 0)
    def _():
        o_ref[...] = jnp.zeros_like(o_ref)


def _expert_matmul(xs, blk_e, blk_valid, blk_new, w_gu, b_gu, w_dn, b_dn, layer):
    d, f_dim = w_dn.shape[3], w_dn.shape[2]
    n_rows = xs.shape[0] // ROW_PIECES
    blk = EXPERT_BLOCK
    tf = GATE_UP_TILE
    nf = f_dim // tf
    n_blocks = n_rows // blk
    packed_rows = pl.BlockSpec((blk * ROW_PIECES, LANES), lambda f, b, be, bv, bn: (b, 0))
    act = pl.pallas_call(
        functools.partial(_expert_gu_kernel, blk=blk),
        grid_spec=pltpu.PrefetchScalarGridSpec(
            num_scalar_prefetch=3,
            grid=(nf, n_blocks),
            in_specs=[packed_rows,
                      pl.BlockSpec((None, None, d, tf), lambda f, b, be, bv, bn: (layer, be[b], 0, f)),
                      pl.BlockSpec((None, None, d, tf), lambda f, b, be, bv, bn: (layer, be[b], 0, nf + f)),
                      pl.BlockSpec((None, None, 1, tf), lambda f, b, be, bv, bn: (layer, be[b], 0, f)),
                      pl.BlockSpec((None, None, 1, tf), lambda f, b, be, bv, bn: (layer, be[b], 0, nf + f))],
            out_specs=pl.BlockSpec((blk, tf), lambda f, b, be, bv, bn: (b, f)),
            scratch_shapes=[pltpu.VMEM((d, tf), F8), pltpu.VMEM((d, tf), F8),
                            pltpu.VMEM((1, tf), F32), pltpu.VMEM((1, tf), F32)]),
        out_shape=jax.ShapeDtypeStruct((n_rows, f_dim), BF16),
        compiler_params=_cparams("arbitrary", "arbitrary"),
        name="expert_gate_up",
    )(blk_e, blk_valid, blk_new, xs, w_gu, w_gu, b_gu, b_gu)
    assert d == 2 * ROW_PIECES * LANES
    sub = blk // DOWN_BLOCK
    part = jnp.arange(n_blocks * sub, dtype=jnp.int32) % sub
    dn_e = jnp.repeat(blk_e, sub)
    dn_valid = jnp.clip(jnp.repeat(blk_valid, sub) - part * DOWN_BLOCK, 0, DOWN_BLOCK)
    dn_new = jnp.where(part == 0, jnp.repeat(blk_new, sub), 0)
    return pl.pallas_call(
        _expert_dn_kernel,
        grid_spec=pltpu.PrefetchScalarGridSpec(
            num_scalar_prefetch=3,
            grid=(n_blocks * sub,),
            in_specs=[pl.BlockSpec((DOWN_BLOCK, f_dim), lambda b, be, bv, bn: (b, 0)),
                      pl.BlockSpec((None, None, f_dim, d), lambda b, be, bv, bn: (layer, be[b], 0, 0)),
                      pl.BlockSpec((None, None, 1, d), lambda b, be, bv, bn: (layer, be[b], 0, 0))],
            out_specs=pl.BlockSpec((DOWN_BLOCK * ROW_PIECES, LANES), lambda b, be, bv, bn: (b, 0)),
            scratch_shapes=[pltpu.VMEM((f_dim, d), F8), pltpu.VMEM((1, d), F32)]),
        out_shape=jax.ShapeDtypeStruct((n_rows * ROW_PIECES, LANES), jnp.uint32),
        compiler_params=_cparams("arbitrary"),
        name="expert_down",
    )(dn_e, dn_valid, dn_new, act, w_dn, b_dn)


def _combine_kernel(y0_ref, y1_ref, y2_ref, y3_ref, gate_ref, x_ref, g2_ref, fg_ref, o_ref,
                    *, n_ctx, tm, final, skip):
    gates = gate_ref[...]
    f_lo, f_hi = None, None
    for k, y_ref in enumerate((y0_ref, y1_ref, y2_ref, y3_ref)):
        lo, hi = _load_packed_rows(y_ref, tm)
        gk = gates[:, k:k + 1]
        f_lo = lo * gk if f_lo is None else f_lo + lo * gk
        f_hi = hi * gk if f_hi is None else f_hi + hi * gk
    f = jnp.concatenate([f_lo, f_hi], axis=1)
    row = (pl.program_id(0) + skip) * tm + lax.broadcasted_iota(jnp.int32, (tm, 1), 0)
    g2 = jnp.where(row < n_ctx, g2_ref[0:1, :], g2_ref[1:2, :])
    xn = x_ref[...] + g2 * f
    if final:
        ms = jnp.mean(xn * xn, axis=-1, keepdims=True)
        xn = xn * lax.rsqrt(ms + EPS) * fg_ref[...]
    o_ref[...] = xn


def _combine(yk, gates_t, x, mod, g2_col, final_g, n_ctx, final):
    t_rows, d = x.shape
    tm = 256
    nt = t_rows // tm
    skip = n_ctx // tm if final else 0
    assert n_ctx % tm == 0
    return pl.pallas_call(
        functools.partial(_combine_kernel, n_ctx=n_ctx, tm=tm, final=final, skip=skip),
        grid=(nt - skip,),
        in_specs=[pl.BlockSpec((tm * ROW_PIECES, LANES), lambda i, k=k: (k * nt + i + skip, 0))
                  for k in range(TOP_K)]
                 + [pl.BlockSpec((tm, SUBLANES), lambda i: (i + skip, 0)),
                  pl.BlockSpec((tm, d), lambda i: (i + skip, 0)),
                  pl.BlockSpec((SUBLANES, d), lambda i: (0, g2_col)),
                  pl.BlockSpec((1, d), lambda i: (0, 0))],
        out_specs=pl.BlockSpec((tm, d), lambda i: (i, 0)),
        out_shape=jax.ShapeDtypeStruct((t_rows - skip * tm, d), F32),
        compiler_params=_cparams("parallel"),
        name="moe_combine",
    )(yk, yk, yk, yk, gates_t, x, mod, final_g)


SC_WINDOW = 128


def _sc_mesh():
    return plsc.VectorSubcoreMesh(core_axis_name="core", subcore_axis_name="subcore")


def _sc_scatter_rows(x, dest, n_out):
    n_src_blocks = x.shape[0] // SC_WINDOW

    @functools.partial(pl.kernel, out_type=jax.ShapeDtypeStruct((n_out, LANES), x.dtype), mesh=_sc_mesh(),
                       scratch_types=[], name="sc_dispatch_rows")
    def run(x_hbm, i_hbm, o_hbm):
        def body(x_vmem, i_vmem):
            pltpu.sync_copy(x_vmem, o_hbm.at[i_vmem.at[0]])

        pltpu.emit_pipeline(
            body, grid=(dest.shape[1] // SC_WINDOW,),
            in_specs=[pl.BlockSpec((SC_WINDOW, LANES), lambda i: (i % n_src_blocks, 0)),
                      pl.BlockSpec((1, SC_WINDOW), lambda i: (0, i))],
            out_specs=[], core_axis_name=("core", "subcore"),
            dimension_semantics=(pltpu.PARALLEL,))(x_hbm, i_hbm)

    return run(x, dest)


def _sc_gather_rows(y, idx):
    n = idx.shape[1]

    @functools.partial(pl.kernel, out_type=jax.ShapeDtypeStruct((n, LANES), y.dtype), mesh=_sc_mesh(),
                       scratch_types=[], name="sc_combine_rows")
    def run(y_hbm, i_hbm, o_hbm):
        def body(i_vmem, o_vmem):
            pltpu.sync_copy(y_hbm.at[i_vmem.at[0]], o_vmem)

        pltpu.emit_pipeline(
            body, grid=(n // SC_WINDOW,),
            in_specs=[pl.BlockSpec((1, SC_WINDOW), lambda i: (0, i))],
            out_specs=[pl.BlockSpec((SC_WINDOW, LANES), lambda i: (i, 0))],
            core_axis_name=("core", "subcore"),
            dimension_semantics=(pltpu.PARALLEL,))(i_hbm, o_hbm)

    return run(y, idx)


def _moe(h, logits, x, mod, g2_col, w_gu, b_gu, w_dn, b_dn, layer, n_ctx, final_g, final):
    t_rows = x.shape[0]
    n_exp = w_gu.shape[1]
    idx, gates, counts = _router(logits, n_exp)
    counts = counts[:, 0].astype(jnp.int32)
    blk = EXPERT_BLOCK
    padded = (counts + blk - 1) // blk * blk
    pends = jnp.cumsum(padded)
    pstarts = pends - padded
    dest = _dest_rows(idx, pstarts.astype(F32)[:, None], n_exp)[:TOP_K]
    n_blocks = -(-(t_rows * TOP_K) // blk) + n_exp
    n_rows = n_blocks * blk
    blk_start = jnp.arange(n_blocks, dtype=jnp.int32) * blk
    blk_e = jnp.minimum(jnp.sum(blk_start[:, None] >= pends[None, :], axis=1), n_exp - 1).astype(jnp.int32)
    blk_valid = jnp.clip(pstarts[blk_e] + counts[blk_e] - blk_start, 0, blk).astype(jnp.int32)
    blk_valid = jnp.where(blk_start < pends[-1], blk_valid, 0)
    blk_new = jnp.concatenate([jnp.ones((1,), jnp.int32), (blk_e[1:] != blk_e[:-1]).astype(jnp.int32)])
    dest8 = (dest[:, :, None] * ROW_PIECES + jnp.arange(ROW_PIECES, dtype=jnp.int32)).reshape(1, -1)
    xs = _sc_scatter_rows(h, dest8, n_rows * ROW_PIECES)
    ys = _expert_matmul(xs, blk_e, blk_valid, blk_new, w_gu, b_gu, w_dn, b_dn, layer)
    yk = _sc_gather_rows(ys, dest8)
    return _combine(yk, gates.T, x, mod, g2_col, final_g, n_ctx, final)


def _s5_tables(a_re, a_im, log_step, b_re, b_im, c_re, c_im, d_skip):
    tc = S5_TC
    n_grp, n_st = a_re.shape[1], a_re.shape[2]
    lr = jnp.minimum(a_re.astype(F32), -1e-4)
    li = a_im.astype(F32)
    dt = jnp.exp(log_step.astype(F32))[..., None]
    dd = jnp.arange(tc + 1, dtype=F32)[:, None, None, None]
    mag = jnp.exp(lr * dt * dd)
    pr, pi = mag * jnp.cos(li * dt * dd), mag * jnp.sin(li * dt * dd)
    ar, ai = pr[1], pi[1]
    den = lr * lr + li * li
    nr = ar - 1.0
    zr = (nr * lr + ai * li) / den
    zi = (ai * lr - nr * li) / den
    br, bi = b_re.astype(F32), b_im.astype(F32)
    bbr = zr[..., None] * br - zi[..., None] * bi
    bbi = zr[..., None] * bi + zi[..., None] * br
    abr = pr[:tc, ..., None] * bbr - pi[:tc, ..., None] * bbi
    abi = pr[:tc, ..., None] * bbi + pi[:tc, ..., None] * bbr
    cr, ci = c_re.astype(F32), c_im.astype(F32)
    kern = (jnp.einsum('xgip,dxgpj->dxgij', cr, abr, precision=HIGHEST)
            - jnp.einsum('xgip,dxgpj->dxgij', ci, abi, precision=HIGHEST))
    kdim = tc * S5_GROUP
    k_t = kern.transpose(1, 2, 4, 0, 3)
    zeros = jnp.zeros((n_grp, S5_GROUP, kdim), F32)
    ext_f = jnp.concatenate([zeros, k_t[0].reshape(n_grp, S5_GROUP, kdim)], axis=-1)
    ext_b = jnp.concatenate([k_t[1, :, :, ::-1].reshape(n_grp, S5_GROUP, kdim), zeros], axis=-1)
    m_tot = jnp.stack([ext_f[:, :, (tc - s) * S5_GROUP:(tc - s) * S5_GROUP + kdim]
                       + ext_b[:, :, (tc - 1 - s) * S5_GROUP:(tc - 1 - s) * S5_GROUP + kdim]
                       for s in range(tc)], axis=1).reshape(n_grp, kdim, kdim)
    dsk = jnp.tile(d_skip.astype(F32).reshape(n_grp, S5_GROUP), (1, tc))
    m_tot = m_tot + jnp.eye(kdim, dtype=F32)[None] * dsk[:, None, :]
    pw_f = tc - 1 - jnp.arange(tc)
    pw_b = jnp.arange(tc)

    def b_cols(part, pw, x):
        return part[pw, x].transpose(1, 0, 3, 2).reshape(n_grp, tc * S5_GROUP, n_st)

    b_mat = jnp.concatenate([b_cols(abr, pw_f, 0), b_cols(abi, pw_f, 0),
                             b_cols(abr, pw_b, 1), b_cols(abi, pw_b, 1)], axis=-1)
    pcf = 1 + jnp.arange(tc)
    pcb = tc - jnp.arange(tc)

    def c_rows(pw, x):
        prx, pix = pr[pw, x], pi[pw, x]
        re_c = cr[x][None] * prx[:, :, None, :] - ci[x][None] * pix[:, :, None, :]
        im_c = -(cr[x][None] * pix[:, :, None, :] + ci[x][None] * prx[:, :, None, :])
        to_rows = lambda z: z.transpose(1, 3, 0, 2).reshape(n_grp, n_st, tc * S5_GROUP)
        return to_rows(re_c), to_rows(im_c)

    c_mat = jnp.concatenate(list(c_rows(pcf, 0)) + list(c_rows(pcb, 1)), axis=1)
    prt, pit = pr[tc], pi[tc]
    a1 = jnp.concatenate([prt[0], prt[0], prt[1], prt[1]], axis=-1)
    a2 = jnp.concatenate([-pit[0], pit[0], -pit[1], pit[1]], axis=-1)
    return m_tot.astype(BF16), b_mat.astype(BF16), c_mat.astype(BF16), a1, a2


S5_LANE_GROUPS = LANES // S5_GROUP


def _s5_pack(u_ref, c0, cc):
    xs = [u_ref[pl.ds(c0 * S5_TC + tau, cc, stride=S5_TC), :] for tau in range(S5_TC)]
    return [jnp.concatenate([xs[tau][:, g * S5_GROUP:(g + 1) * S5_GROUP] for tau in range(S5_TC)], axis=1)
            for g in range(S5_LANE_GROUPS)]


def _s5_in_kernel(u_ref, b_ref, o_ref, ug_ref, *, cc):
    def body(ci, carry):
        c0 = pl.multiple_of(ci * cc, SUBLANES)
        ugs = _s5_pack(u_ref, c0, cc)
        for g in range(S5_LANE_GROUPS):
            ug_ref[g, pl.ds(c0, cc), :] = ugs[g]
            o_ref[pl.ds(c0, cc), g, :] = jnp.dot(ugs[g].astype(BF16), b_ref[g], preferred_element_type=F32)
        return carry

    lax.fori_loop(0, u_ref.shape[0] // (S5_TC * cc), body, 0)


def _s5_out_kernel(ug_ref, z_ref, m_ref, c_ref, o_ref, *, cc):
    def body(ci, carry):
        c0 = pl.multiple_of(ci * cc, SUBLANES)
        ys = [jnp.dot(ug_ref[g, pl.ds(c0, cc), :].astype(BF16), m_ref[g], preferred_element_type=F32)
              + jnp.dot(z_ref[pl.ds(c0, cc), g, :].astype(BF16), c_ref[g], preferred_element_type=F32)
              for g in range(S5_LANE_GROUPS)]
        for tau in range(S5_TC):
            row = jnp.concatenate([y[:, tau * S5_GROUP:(tau + 1) * S5_GROUP] for y in ys], axis=1)
            o_ref[pl.ds(c0 * S5_TC + tau, cc, stride=S5_TC), :] = row
        return carry

    lax.fori_loop(0, o_ref.shape[0] // (S5_TC * cc), body, 0)


def _s5_scan_kernel(h_ref, a1_ref, a2_ref, z_ref, *, n_chunks, n_ctx_chunks):
    half = 2 * S5_STATE
    a1 = a1_ref[...]
    a2 = a2_ref[...]
    a1f, a1b = a1[:, :half], a1[:, half:]
    a2f, a2b = a2[:, :half], a2[:, half:]
    zero = jnp.zeros((a1.shape[0], half), F32)

    def swap(v):
        return pltpu.roll(v, S5_STATE, 1)

    def step(t, carry):
        rf, rfs, rb, rbs = carry
        cb = jnp.where(t < n_ctx_chunks, n_ctx_chunks - 1 - t, n_chunks - 1 - (t - n_ctx_chunks))
        z_ref[t, :, 0:half] = rf
        z_ref[cb, :, half:2 * half] = rb
        hf = h_ref[t, :, 0:half]
        hb = h_ref[cb, :, half:2 * half]
        nrf = a1f * rf + a2f * rfs + hf
        nrfs = a1f * rfs - a2f * rf + swap(hf)
        nrb = a1b * rb + a2b * rbs + hb
        nrbs = a1b * rbs - a2b * rb + swap(hb)
        return nrf, nrfs, nrb, nrbs

    lax.fori_loop(0, n_chunks, step, (zero, zero, zero, zero), unroll=8 if n_chunks % 8 == 0 else 1)


def _s5_mixer_scan(u, tables, n_ctx):
    m_tot, b_mat, c_mat, a1, a2 = tables
    t_rows, width = u.shape
    tc = S5_TC
    n_grp = width // S5_GROUP
    n_chunks = t_rows // tc
    kdim = tc * S5_GROUP
    sdim = 4 * S5_STATE
    gb = S5_LANE_GROUPS
    gs = SUBLANES
    n_split = 2
    half_chunks = n_chunks // n_split
    cc = _pick(half_chunks, (104, 80, 40, 8))
    u_spec = pl.BlockSpec((t_rows // n_split, LANES), lambda j, r: (r, j))
    tab_spec = lambda a, b: pl.BlockSpec((gb, a, b), lambda j, r: (j, 0, 0))
    st_spec = pl.BlockSpec((half_chunks, gb, sdim), lambda j, r: (r, j, 0))
    grid = (width // LANES, n_split)
    ug_spec = pl.BlockSpec((gb, half_chunks, kdim), lambda j, r: (j, r, 0))
    hin, ug = pl.pallas_call(
        functools.partial(_s5_in_kernel, cc=cc),
        grid=grid,
        in_specs=[u_spec, tab_spec(kdim, sdim)],
        out_specs=[st_spec, ug_spec],
        out_shape=[jax.ShapeDtypeStruct((n_chunks, n_grp, sdim), F32),
                   jax.ShapeDtypeStruct((n_grp, n_chunks, kdim), F32)],
        compiler_params=_cparams("parallel", "arbitrary"),
        name="s5_chunk_inputs",
    )(u, b_mat)
    z = pl.pallas_call(
        functools.partial(_s5_scan_kernel, n_chunks=n_chunks, n_ctx_chunks=n_ctx // tc),
        grid=(n_grp // gs,),
        in_specs=[pl.BlockSpec((n_chunks, gs, sdim), lambda g: (0, g, 0)),
                  pl.BlockSpec((gs, sdim), lambda g: (g, 0)),
                  pl.BlockSpec((gs, sdim), lambda g: (g, 0))],
        out_specs=pl.BlockSpec((n_chunks, gs, sdim), lambda g: (0, g, 0)),
        out_shape=jax.ShapeDtypeStruct((n_chunks, n_grp, sdim), F32),
        compiler_params=_cparams("parallel"),
        name="s5_chunk_scan",
    )(hin, a1, a2)
    return pl.pallas_call(
        functools.partial(_s5_out_kernel, cc=cc),
        grid=grid,
        in_specs=[ug_spec, st_spec, tab_spec(kdim, kdim), tab_spec(sdim, kdim)],
        out_specs=u_spec,
        out_shape=jax.ShapeDtypeStruct((t_rows, width), F32),
        compiler_params=_cparams("parallel", "arbitrary"),
        name="s5_chunk_outputs",
    )(ug, z, m_tot, c_mat)


def _pow2_scale(amax):
    return jnp.exp2(jnp.floor(jnp.log2(FP8_TARGET / jnp.maximum(amax, 1e-30))))


def _lambda_init(layer):
    return 0.8 - 0.6 * math.exp(-0.3 * layer)


def kernel(x, c, ctx, c_ctx, w_mod, b_mod, norm1_g, norm2_g, final_g, attn_w_qkv, attn_w_o, lambda_q1, lambda_k1, lambda_q2, lambda_k2, subln_g, sink_logit, s5_w_in, s5_a_re, s5_a_im, s5_log_step, s5_b_re, s5_b_im, s5_c_re, s5_c_im, s5_d, s5_w_glu, router_w, router_b, expert_w_gu, expert_b_gu, expert_w_down, expert_b_down):
    bsz, seq, d = x.shape
    assert bsz == 1, "single-sequence block"
    n_ctx = ctx.shape[1]
    depth = w_mod.shape[0]
    n_exp = router_w.shape[2]
    assert n_ctx % 256 == 0 and seq % 256 == 0

    xj = jnp.concatenate([ctx[0], x[0]], axis=0)
    cs = jnp.zeros((SUBLANES, d), F32).at[0].set(c_ctx).at[1].set(c[0])
    b_mod3 = b_mod[:, None, :]
    b_gu4 = expert_b_gu[:, :, None, :]
    b_dn4 = expert_b_down[:, :, None, :]
    wr_pad = jnp.pad(router_w, ((0, 0), (0, 0), (0, LANES - n_exp)))
    wr_hi = wr_pad.astype(BF16)
    wr_pad = jnp.stack([wr_hi, (wr_pad - wr_hi.astype(F32)).astype(BF16)], axis=1)
    br_pad = jnp.pad(router_b, ((0, 0), (0, LANES - n_exp)), constant_values=NEG_INF)[:, None, :]
    final_row = final_g[None, :]

    for i in range(depth):
        last = i == depth - 1
        j = i // 2
        mod = _adaln_mod(cs, w_mod, b_mod3, i)
        if i % 2 == 0:
            sizes = (A_HEADS * 2 * A_QK_DIM, A_HEADS * 2 * A_QK_DIM, A_HEADS * A_V_DIM,
                     B_Q_HEADS * B_HEAD_DIM, B_KV_HEADS * B_HEAD_DIM, B_KV_HEADS * B_HEAD_DIM)
            offs = np.concatenate([[0], np.cumsum(sizes)])
            types = [0, 1, ROPE_NONE, 2, 3, ROPE_NONE]
            ttype = jnp.asarray(np.concatenate([np.full(s // 256, t) for s, t in zip(sizes, types)]), jnp.int32)
            cos_t, sin_t = _rope_tables(n_ctx, seq)
            qkv = _norm_mod_matmul(xj, norm1_g[i][None, :], mod, 0, 1, attn_w_qkv[j].astype(BF16), n_ctx, BF16,
                                   rope=(ttype, cos_t, sin_t))
            qt = qkv[:, offs[0]:offs[1]].T
            ka = qkv[:, offs[1]:offs[2]]
            ka3 = ka.reshape(-1, A_HEADS, 2 * A_QK_DIM).astype(F32)
            ck = _pow2_scale(jnp.max(jnp.abs(ka3), axis=(0, 2)))
            ka = (ka3 * ck[None, :, None]).astype(F8).reshape(ka.shape)
            qt = (qt.reshape(A_HEADS, 2 * A_QK_DIM, -1).astype(F32) / ck[:, None, None]).astype(BF16).reshape(qt.shape)
            vt = qkv[:, offs[2]:offs[3]].T.reshape(A_HEADS, A_V_DIM, -1)
            cv = _pow2_scale(jnp.max(jnp.abs(vt.astype(F32)), axis=(1, 2)))
            vt = (vt.astype(F32) * cv[:, None, None]).astype(BF16)
            vt = jnp.concatenate([vt, jnp.ones((A_HEADS, V_AUG - A_V_DIM, vt.shape[2]), BF16)], axis=1)
            vt = vt.reshape(A_HEADS * V_AUG, -1)
            f32 = F32
            li = _lambda_init(i)
            lam = (jnp.exp(jnp.sum(lambda_q1[j].astype(f32) * lambda_k1[j].astype(f32)))
                   - jnp.exp(jnp.sum(lambda_q2[j].astype(f32) * lambda_k2[j].astype(f32))) + li)
            attn_args = (jnp.concatenate([lam.reshape(1), 1.0 / cv]), subln_g[j][:, None], 1.0 - li)
            ya = jnp.concatenate([_diff_attention(qt[:, :n_ctx], ka[:n_ctx], vt[:, :n_ctx], *attn_args),
                                  _diff_attention(qt[:, n_ctx:], ka, vt, *attn_args)], axis=0)
            yb = _window_attention(qkv, sink_logit[j], n_ctx, int(offs[3]) // LANES, int(offs[4]) // LANES,
                                   int(offs[5]) // LANES)
            w_o = attn_w_o[j].astype(BF16)
            na = A_HEADS * A_V_DIM
            xj, h2, logits = _mixer_out(_attn_out_kernel, [ya, yb], xj, [w_o[:na], w_o[na:]], mod, (2, 3, 4),
                                        norm2_g[i][None, :], wr_pad[i], br_pad[i], n_ctx, "attn_out_router")
        else:
            u = _norm_mod_matmul(xj, norm1_g[i][None, :], mod, 0, 1, s5_w_in[j].astype(BF16), n_ctx, F32)
            tables = _s5_tables(s5_a_re[j], s5_a_im[j], s5_log_step[j], s5_b_re[j], s5_b_im[j],
                                s5_c_re[j], s5_c_im[j], s5_d[j])
            y = _s5_mixer_scan(u, tables, n_ctx)
            w_glu = s5_w_glu[j].astype(BF16)
            xj, h2, logits = _mixer_out(_glu_out_kernel, [y], xj, [w_glu[:, :d], w_glu[:, d:]], mod, (2, 3, 4),
                                        norm2_g[i][None, :], wr_pad[i], br_pad[i], n_ctx, "glu_out_router")
        xj = _moe(h2, logits, xj, mod, 5, expert_w_gu, b_gu4, expert_w_down, b_dn4, i, n_ctx, final_row, last)
    return xj[None]
```
